```python
import math
import jax, jax.numpy as jnp
from jax import lax
import numpy as np

D_MODEL = 2048
BATCH = 4
SEQ = 2048
DEPTH = 1

HEAD_DIM = 128
ROPE_DIM = HEAD_DIM // 4
ROPE_THETA = 500000.0
N_TOTAL_HEADS = D_MODEL // HEAD_DIM
N_NSA_HEADS = N_TOTAL_HEADS // 2
N_NSA_KV = 2
NSA_GROUP = N_NSA_HEADS // N_NSA_KV
CMP_LEN = 32
CMP_STRIDE = 16
CMP_HIDDEN = 256
SLC_LEN = 64
SLC_TOP = 16
WINDOW = 512
N_DIFF_HEADS = N_TOTAL_HEADS // 4
D_FF = 5632
CONV_W = 3
PLE_DIM = 256
Q_BLOCK = 128
SLC_Q_BLOCK = 64
EPS = 1e-6

NSA_WIDTH = N_NSA_HEADS * HEAD_DIM
NSA_KV_WIDTH = N_NSA_KV * HEAD_DIM
DIFF_QK_WIDTH = N_DIFF_HEADS * 2 * HEAD_DIM
DIFF_V_WIDTH = N_DIFF_HEADS * 2 * HEAD_DIM
MIX_WIDTH = NSA_WIDTH + DIFF_V_WIDTH
IN_SPLITS = (NSA_WIDTH,) + (NSA_KV_WIDTH,) * 6 + (3 * N_NSA_HEADS, DIFF_QK_WIDTH, DIFF_QK_WIDTH, DIFF_V_WIDTH)
IN_WIDTH = sum(IN_SPLITS)

kernel_name = 'hybrid_nsa_diffattn_convffn_ple'


def rmsnorm(x, g):
    xf = x.astype(jnp.float32)
    y = xf * lax.rsqrt(jnp.mean(xf * xf, axis=-1, keepdims=True) + EPS)
    return (y * g.astype(jnp.float32)).astype(x.dtype)


def rope_tables(seq):
    inv = 1.0 / (ROPE_THETA ** (jnp.arange(0, ROPE_DIM, 2, dtype=jnp.float32) / ROPE_DIM))
    ang = jnp.arange(seq, dtype=jnp.float32)[:, None] * inv[None, :]
    return jnp.cos(ang), jnp.sin(ang)


def partial_rope(x, cos, sin):
    c = cos.astype(x.dtype)
    s = sin.astype(x.dtype)
    half = ROPE_DIM // 2
    x1, x2, xp = x[..., :half], x[..., half:ROPE_DIM], x[..., ROPE_DIM:]
    return jnp.concatenate([x1 * c - x2 * s, x2 * c + x1 * s, xp], axis=-1)


def masked_softmax(s, mask):
    s = jnp.where(mask, s.astype(jnp.float32), -1e30)
    p = jax.nn.softmax(s, axis=-1)
    return jnp.where(mask, p, 0.0)


def compress_blocks(blocks, pos, w1, w2):
    b = (blocks + pos.astype(blocks.dtype)).reshape(*blocks.shape[:-2], CMP_LEN * HEAD_DIM)
    return jax.nn.gelu(b @ w1) @ w2


def nsa_attention(q, kc, vc, ks, vs, kw, vw, gates, cmp_k_pos, cmp_k_w1, cmp_k_w2,
                  cmp_v_pos, cmp_v_w1, cmp_v_w2, cos, sin):
    B, S = q.shape[0], q.shape[1]
    scale = HEAD_DIM ** -0.5
    q = partial_rope(q.reshape(B, S, N_NSA_KV, NSA_GROUP, HEAD_DIM).transpose(0, 2, 3, 1, 4), cos, sin)

    def kv_layout(t):
        return t.reshape(B, S, N_NSA_KV, HEAD_DIM).transpose(0, 2, 1, 3)

    kc, ks, kw = (partial_rope(kv_layout(t), cos, sin) for t in (kc, ks, kw))
    vc, vs, vw = (kv_layout(t) for t in (vc, vs, vw))
    t_pos = jnp.arange(S, dtype=jnp.int32)

    n_cmp = (S - CMP_LEN) // CMP_STRIDE + 1
    cmp_starts = np.arange(n_cmp, dtype=np.int32) * CMP_STRIDE
    blk_idx = cmp_starts[:, None] + np.arange(CMP_LEN, dtype=np.int32)[None, :]
    k_cmp = compress_blocks(kc[:, :, blk_idx], cmp_k_pos, cmp_k_w1, cmp_k_w2)
    v_cmp = compress_blocks(vc[:, :, blk_idx], cmp_v_pos, cmp_v_w1, cmp_v_w2)
    cmp_mask = jnp.asarray(cmp_starts + CMP_LEN - 1)[None, :] <= t_pos[:, None]
    s_cmp = jnp.einsum('bhgqd,bhcd->bhgqc', q, k_cmp) * scale
    p_cmp = masked_softmax(s_cmp, cmp_mask)
    o_cmp = jnp.einsum('bhgqc,bhcd->bhgqd', p_cmp.astype(v_cmp.dtype), v_cmp)

    n_sel = S // SLC_LEN
    n_top = min(SLC_TOP, n_sel)
    sel_starts = np.arange(n_sel, dtype=np.int32) * SLC_LEN
    overlap = np.clip(np.minimum(cmp_starts[:, None] + CMP_LEN, sel_starts[None, :] + SLC_LEN)
                      - np.maximum(cmp_starts[:, None], sel_starts[None, :]), 0, None).astype(np.float32) / CMP_LEN
    p_slc = jnp.einsum('bhgqc,cn->bhqn', p_cmp, jnp.asarray(overlap))
    cur = t_pos // SLC_LEN
    j = jnp.arange(n_sel, dtype=jnp.int32)
    forced = (j[None, :] == 0) | (j[None, :] == cur[:, None]) | (j[None, :] == cur[:, None] - 1)
    future = j[None, :] > cur[:, None]
    sel_score = jnp.where(forced, 1e4, jnp.where(future, -1e4, p_slc))
    _, sel_idx = lax.top_k(sel_score, n_top)

    ks_blk = ks.reshape(B, N_NSA_KV, n_sel, SLC_LEN, HEAD_DIM)
    vs_blk = vs.reshape(B, N_NSA_KV, n_sel, SLC_LEN, HEAD_DIM)
    nqs = S // SLC_Q_BLOCK
    q_sb = q.reshape(B, N_NSA_KV, NSA_GROUP, nqs, SLC_Q_BLOCK, HEAD_DIM).transpose(3, 0, 1, 2, 4, 5)
    idx_sb = sel_idx.reshape(B, N_NSA_KV, nqs, SLC_Q_BLOCK, n_top).transpose(2, 0, 1, 3, 4)
    slc_starts = jnp.arange(nqs, dtype=jnp.int32) * SLC_Q_BLOCK
    bi = jnp.arange(B)[:, None, None, None]
    hi = jnp.arange(N_NSA_KV)[None, :, None, None]
    n_keys = n_top * SLC_LEN

    def slc_block(args):
        qb, ib, start = args
        kg = ks_blk[bi, hi, ib].reshape(B, N_NSA_KV, SLC_Q_BLOCK, n_keys, HEAD_DIM)
        vg = vs_blk[bi, hi, ib].reshape(B, N_NSA_KV, SLC_Q_BLOCK, n_keys, HEAD_DIM)
        s = jnp.einsum('bhgqd,bhqkd->bhgqk', qb, kg) * scale
        kpos = (ib[..., None] * SLC_LEN + jnp.arange(SLC_LEN, dtype=jnp.int32)).reshape(B, N_NSA_KV, SLC_Q_BLOCK, n_keys)
        qpos = start + jnp.arange(SLC_Q_BLOCK, dtype=jnp.int32)
        mask = (kpos <= qpos[None, None, :, None])[:, :, None]
        p = masked_softmax(s, mask)
        return jnp.einsum('bhgqk,bhqkd->bhgqd', p.astype(vg.dtype), vg)

    o_slc = lax.map(slc_block, (q_sb, idx_sb, slc_starts))
    o_slc = o_slc.transpose(1, 2, 3, 0, 4, 5).reshape(B, N_NSA_KV, NSA_GROUP, S, HEAD_DIM)

    span = WINDOW + Q_BLOCK
    kw_pad = jnp.pad(kw, ((0, 0), (0, 0), (WINDOW, 0), (0, 0)))
    vw_pad = jnp.pad(vw, ((0, 0), (0, 0), (WINDOW, 0), (0, 0)))
    nqw = S // Q_BLOCK
    q_wb = q.reshape(B, N_NSA_KV, NSA_GROUP, nqw, Q_BLOCK, HEAD_DIM).transpose(3, 0, 1, 2, 4, 5)
    win_starts = jnp.arange(nqw, dtype=jnp.int32) * Q_BLOCK

    def win_block(args):
        qb, start = args
        kb = lax.dynamic_slice_in_dim(kw_pad, start, span, axis=2)
        vb = lax.dynamic_slice_in_dim(vw_pad, start, span, axis=2)
        s = jnp.einsum('bhgqd,bhkd->bhgqk', qb, kb) * scale
        qpos = start + jnp.arange(Q_BLOCK, dtype=jnp.int32)
        kpos = start - WINDOW + jnp.arange(span, dtype=jnp.int32)
        dist = qpos[:, None] - kpos[None, :]
        mask = (dist >= 0) & (dist < WINDOW) & (kpos[None, :] >= 0)
        p = masked_softmax(s, mask)
        return jnp.einsum('bhgqk,bhkd->bhgqd', p.astype(vb.dtype), vb)

    o_win = lax.map(win_block, (q_wb, win_starts))
    o_win = o_win.transpose(1, 2, 3, 0, 4, 5).reshape(B, N_NSA_KV, NSA_GROUP, S, HEAD_DIM)

    g = jax.nn.sigmoid(gates.astype(jnp.float32)).reshape(B, S, 3, N_NSA_KV, NSA_GROUP).transpose(2, 0, 3, 4, 1)[..., None]
    o = g[0] * o_cmp + g[1] * o_slc + g[2] * o_win
    return o.transpose(0, 3, 1, 2, 4).reshape(B, S, NSA_WIDTH).astype(q.dtype)


def diff_attention(q, k, v, lq1, lk1, lq2, lk2, subln, lambda_init, cos, sin):
    B, S = q.shape[0], q.shape[1]
    scale = HEAD_DIM ** -0.5
    q = partial_rope(q.reshape(B, S, N_DIFF_HEADS, 2, HEAD_DIM).transpose(0, 2, 3, 1, 4), cos, sin)
    k = partial_rope(k.reshape(B, S, N_DIFF_HEADS, 2, HEAD_DIM).transpose(0, 2, 3, 1, 4), cos, sin)
    v = v.reshape(B, S, N_DIFF_HEADS, 2 * HEAD_DIM).transpose(0, 2, 1, 3)
    f32 = jnp.float32
    lam = (jnp.exp(jnp.sum(lq1.astype(f32) * lk1.astype(f32)))
           - jnp.exp(jnp.sum(lq2.astype(f32) * lk2.astype(f32))) + lambda_init)
    nqb = S // Q_BLOCK
    q_b = q.reshape(B, N_DIFF_HEADS, 2, nqb, Q_BLOCK, HEAD_DIM).transpose(3, 0, 1, 2, 4, 5)
    starts = jnp.arange(nqb, dtype=jnp.int32) * Q_BLOCK
    kpos = jnp.arange(S, dtype=jnp.int32)

    def blk(args):
        qb, start = args
        s = jnp.einsum('bhmqd,bhmkd->bhmqk', qb, k) * scale
        mask = kpos[None, :] <= (start + jnp.arange(Q_BLOCK, dtype=jnp.int32))[:, None]
        p = masked_softmax(s, mask)
        a = p[:, :, 0] - lam * p[:, :, 1]
        return jnp.einsum('bhqk,bhkd->bhqd', a.astype(v.dtype), v)

    o = lax.map(blk, (q_b, starts))
    o = o.transpose(1, 0, 3, 2, 4).reshape(B, S, N_DIFF_HEADS, 2 * HEAD_DIM)
    o = rmsnorm(o, subln) * (1.0 - lambda_init)
    return o.reshape(B, S, DIFF_V_WIDTH)


def conv_ffn(x, w_up, conv_w, conv_b, w_down):
    h = x @ w_up
    C = h.shape[-1]
    rhs = conv_w.astype(h.dtype).reshape(CONV_W, 1, C)
    h = lax.conv_general_dilated(h, rhs, window_strides=(1,), padding=[(CONV_W - 1, 0)],
                                 dimension_numbers=('NWC', 'WIO', 'NWC'), feature_group_count=C)
    h = h + conv_b
    u, g = jnp.split(h, 2, axis=-1)
    return (jax.nn.silu(g) * u) @ w_down


def setup_inputs(seed: int = 0) -> dict:
    key = jax.random.key(seed)
    ks = jax.random.split(key, 26)
    f = jnp.float32
    L = DEPTH

    def nrm(k, shape, scale):
        return jax.random.normal(k, shape, f) * scale

    def gain(k, shape):
        return 1.0 + 0.01 * jax.random.normal(k, shape, f)

    return {
        'x': nrm(ks[0], (BATCH, SEQ, D_MODEL), 1.0),
        'p': nrm(ks[1], (DEPTH, BATCH, SEQ, PLE_DIM), 1.0),
        'attn_norm': gain(ks[2], (L, D_MODEL)),
        'w_in': nrm(ks[3], (L, D_MODEL, IN_WIDTH), D_MODEL ** -0.5),
        'cmp_k_pos': nrm(ks[4], (L, CMP_LEN, HEAD_DIM), 0.02),
        'cmp_k_w1': nrm(ks[5], (L, CMP_LEN * HEAD_DIM, CMP_HIDDEN), (CMP_LEN * HEAD_DIM) ** -0.5),
        'cmp_k_w2': nrm(ks[6], (L, CMP_HIDDEN, HEAD_DIM), CMP_HIDDEN ** -0.5),
        'cmp_v_pos': nrm(ks[7], (L, CMP_LEN, HEAD_DIM), 0.02),
        'cmp_v_w1': nrm(ks[8], (L, CMP_LEN * HEAD_DIM, CMP_HIDDEN), (CMP_LEN * HEAD_DIM) ** -0.5),
        'cmp_v_w2': nrm(ks[9], (L, CMP_HIDDEN, HEAD_DIM), CMP_HIDDEN ** -0.5),
        'nsa_out_norm': gain(ks[10], (L, NSA_WIDTH)),
        'diff_lq1': nrm(ks[11], (L, HEAD_DIM), 0.1),
        'diff_lk1': nrm(ks[12], (L, HEAD_DIM), 0.1),
        'diff_lq2': nrm(ks[13], (L, HEAD_DIM), 0.1),
        'diff_lk2': nrm(ks[14], (L, HEAD_DIM), 0.1),
        'diff_subln': gain(ks[15], (L, 2 * HEAD_DIM)),
        'w_o': nrm(ks[16], (L, MIX_WIDTH, D_MODEL), MIX_WIDTH ** -0.5),
        'ffn_norm': gain(ks[17], (L, D_MODEL)),
        'w_up': nrm(ks[18], (L, D_MODEL, 2 * D_FF), D_MODEL ** -0.5),
        'conv_w': nrm(ks[19], (L, CONV_W, 2 * D_FF), CONV_W ** -0.5),
        'conv_b': nrm(ks[20], (L, 2 * D_FF), 0.01),
        'w_down': nrm(ks[21], (L, D_FF, D_MODEL), D_FF ** -0.5),
        'ple_norm': gain(ks[22], (L, D_MODEL)),
        'w_ple_gate': nrm(ks[23], (L, D_MODEL, D_MODEL), D_MODEL ** -0.5),
        'w_ple_proj': nrm(ks[24], (L, PLE_DIM, D_MODEL), PLE_DIM ** -0.5),
        'final_norm': gain(ks[25], (D_MODEL,)),
    }


def reference(x, p, attn_norm, w_in, cmp_k_pos, cmp_k_w1, cmp_k_w2, cmp_v_pos, cmp_v_w1, cmp_v_w2,
              nsa_out_norm, diff_lq1, diff_lk1, diff_lq2, diff_lk2, diff_subln, w_o, ffn_norm,
              w_up, conv_w, conv_b, w_down, ple_norm, w_ple_gate, w_ple_proj, final_norm):
    S = x.shape[1]
    cos, sin = rope_tables(S)
    split_at = np.cumsum(IN_SPLITS)[:-1].tolist()
    h = x
    for i in range(DEPTH):
        lambda_init = 0.8 - 0.6 * math.exp(-0.3 * i)
        proj = rmsnorm(h, attn_norm[i]) @ w_in[i]
        (nq, nkc, nvc, nks, nvs, nkw, nvw, ngate, dq, dk, dv) = jnp.split(proj, split_at, axis=-1)
        y_nsa = nsa_attention(nq, nkc, nvc, nks, nvs, nkw, nvw, ngate,
                              cmp_k_pos[i], cmp_k_w1[i], cmp_k_w2[i],
                              cmp_v_pos[i], cmp_v_w1[i], cmp_v_w2[i], cos, sin)
        y_nsa = rmsnorm(y_nsa, nsa_out_norm[i])
        y_diff = diff_attention(dq, dk, dv, diff_lq1[i], diff_lk1[i], diff_lq2[i], diff_lk2[i],
                                diff_subln[i], lambda_init, cos, sin)
        h = h + jnp.concatenate([y_nsa, y_diff], axis=-1) @ w_o[i]
        h = h + conv_ffn(rmsnorm(h, ffn_norm[i]), w_up[i], conv_w[i], conv_b[i], w_down[i])
        gate = jax.nn.sigmoid(rmsnorm(h, ple_norm[i]) @ w_ple_gate[i])
        h = h + gate * (p[i] @ w_ple_proj[i])
    return rmsnorm(h, final_norm)
```

```python
import functools
import math

import numpy as np
import jax
import jax.numpy as jnp
from jax import lax
from jax.experimental import pallas as pl
from jax.experimental.pallas import tpu as pltpu

D_MODEL = 2048
SEQ = 2048
HEAD_DIM = 128
ROPE_DIM = HEAD_DIM // 4
ROPE_THETA = 500000.0
N_NSA_HEADS = 8
N_NSA_KV = 2
NSA_GROUP = N_NSA_HEADS // N_NSA_KV
CMP_LEN = 32
CMP_STRIDE = 16
CMP_HIDDEN = 256
SLC_LEN = 64
SLC_TOP = 16
WINDOW = 512
N_DIFF_HEADS = 4
D_FF = 5632
CONV_W = 3
PLE_DIM = 256
EPS = 1e-6

NSA_WIDTH = N_NSA_HEADS * HEAD_DIM
NSA_KV_WIDTH = N_NSA_KV * HEAD_DIM
DIFF_WIDTH = N_DIFF_HEADS * 2 * HEAD_DIM
N_GATES = 3 * N_NSA_HEADS
N_CMP = (SEQ - CMP_LEN) // CMP_STRIDE + 1
N_CMP_PAD = SEQ // CMP_STRIDE
N_SEL = SEQ // SLC_LEN
GATE_STRIDE = 16

COL_NQ = 0
COL_KC = COL_NQ + NSA_WIDTH
COL_KS = COL_KC + NSA_KV_WIDTH
COL_KW = COL_KS + NSA_KV_WIDTH
COL_DQ = COL_KW + NSA_KV_WIDTH
COL_DK = COL_DQ + DIFF_WIDTH
ROPE_COLS = COL_DK + DIFF_WIDTH
COL_VC = ROPE_COLS
COL_VS = COL_VC + NSA_KV_WIDTH
COL_VW = COL_VS + NSA_KV_WIDTH
COL_DV = COL_VW + NSA_KV_WIDTH
COL_GATE = COL_DV + DIFF_WIDTH
PROJ_COLS = COL_GATE + HEAD_DIM

V7X_LANES = 128
V7X_VMEM_REQUEST_CAP = 56 * 1024 * 1024
NEG = -1e30

_NT = (((1,), (1,)), ((), ()))
_TN = (((0,), (0,)), ((), ()))
BF16 = jnp.bfloat16
F32 = jnp.float32


def _params(semantics, vmem_estimate_bytes):
    limit = min(max(int(vmem_estimate_bytes), 32 * 1024 * 1024), V7X_VMEM_REQUEST_CAP)
    return pltpu.CompilerParams(dimension_semantics=semantics, vmem_limit_bytes=limit)


def _rmsnorm(x, g):
    return x * lax.rsqrt(jnp.mean(x * x, axis=-1, keepdims=True) + EPS) * g


def _inproj_kernel(x_ref, g_ref, w_ref, cos_ref, sin_ref, o_ref, xn_ref, *, n_rope_tiles):
    j = pl.program_id(1)

    @pl.when(j == 0)
    def _():
        xn_ref[...] = _rmsnorm(x_ref[...], g_ref[...]).astype(BF16)

    acc = jnp.dot(xn_ref[...], w_ref[...], preferred_element_type=F32)
    n_heads = acc.shape[1] // HEAD_DIM

    @pl.when(j < n_rope_tiles)
    def _():
        c = cos_ref[...]
        s = sin_ref[...]
        first_half = lax.broadcasted_iota(jnp.int32, c.shape, 1) < ROPE_DIM // 2
        for h in range(n_heads):
            a = acc[:, h * HEAD_DIM:(h + 1) * HEAD_DIM]
            partner = jnp.where(first_half,
                                pltpu.roll(a, HEAD_DIM - ROPE_DIM // 2, 1),
                                pltpu.roll(a, ROPE_DIM // 2, 1))
            o_ref[:, h * HEAD_DIM:(h + 1) * HEAD_DIM] = (a * c + partner * s).astype(o_ref.dtype)

    @pl.when(j >= n_rope_tiles)
    def _():
        o_ref[...] = acc.astype(o_ref.dtype)


def _in_proj(x2, g, w_p, cos_t, sin_t, *, tm=512, tn=1920):
    t = x2.shape[0]
    assert t % tm == 0 and SEQ % tm == 0 and PROJ_COLS % tn == 0 and ROPE_COLS % tn == 0
    seq_tiles = SEQ // tm
    est = 2 * tm * D_MODEL * 4 + 2 * D_MODEL * tn * 2 + 2 * tm * tn * 2 + tm * D_MODEL * 2 + 2 * tm * tn * 4
    return pl.pallas_call(
        functools.partial(_inproj_kernel, n_rope_tiles=ROPE_COLS // tn),
        grid=(t // tm, PROJ_COLS // tn),
        in_specs=[
            pl.BlockSpec((tm, D_MODEL), lambda i, j: (i, 0)),
            pl.BlockSpec((1, D_MODEL), lambda i, j: (0, 0)),
            pl.BlockSpec((D_MODEL, tn), lambda i, j: (0, j)),
            pl.BlockSpec((tm, HEAD_DIM), lambda i, j: (i % seq_tiles, 0)),
            pl.BlockSpec((tm, HEAD_DIM), lambda i, j: (i % seq_tiles, 0)),
        ],
        out_specs=pl.BlockSpec((tm, tn), lambda i, j: (i, j)),
        out_shape=jax.ShapeDtypeStruct((t, PROJ_COLS), BF16),
        scratch_shapes=[pltpu.VMEM((tm, D_MODEL), BF16)],
        compiler_params=_params(("parallel", "arbitrary"), est),
        name="in_proj",
    )(x2, g, w_p, cos_t, sin_t)


def _compress_kernel(h_ref, w1_ref, pos_ref, w2_ref, o_ref):
    half = CMP_LEN * HEAD_DIM // 2
    h = h_ref[0]
    top = jnp.dot(h, w1_ref[0, :half, :], preferred_element_type=F32)
    bot = jnp.dot(h, w1_ref[0, half:, :], preferred_element_type=F32)
    pos_bias = jnp.dot(pos_ref[0], w1_ref[0], preferred_element_type=F32)[0:1]
    pre = top + pltpu.roll(bot, bot.shape[0] - 1, 0) + pos_bias
    act = jax.nn.gelu(pre)
    o_ref[0] = jnp.dot(act.astype(BF16), w2_ref[0], preferred_element_type=F32).astype(o_ref.dtype)


def _compress(hkv, w1, pos, w2):
    rows = hkv.shape[1]
    kdim = CMP_LEN * HEAD_DIM
    est = 2 * (rows * kdim // 2 * 2 + kdim * CMP_HIDDEN * 2) + 6 * rows * CMP_HIDDEN * 4
    return pl.pallas_call(
        _compress_kernel,
        grid=(2,),
        in_specs=[
            pl.BlockSpec((1, rows, kdim // 2), lambda i: (i, 0, 0)),
            pl.BlockSpec((1, kdim, CMP_HIDDEN), lambda i: (i, 0, 0)),
            pl.BlockSpec((1, 8, kdim), lambda i: (i, 0, 0)),
            pl.BlockSpec((1, CMP_HIDDEN, HEAD_DIM), lambda i: (i, 0, 0)),
        ],
        out_specs=pl.BlockSpec((1, rows, HEAD_DIM), lambda i: (i, 0, 0)),
        out_shape=jax.ShapeDtypeStruct((2, rows, HEAD_DIM), BF16),
        compiler_params=_params(("parallel",), est),
        name="compress",
    )(hkv, w1, pos, w2)


def _online_softmax_step(s, v, m, l, acc):
    m_new = jnp.maximum(m, jnp.max(s, axis=0, keepdims=True))
    alpha = jnp.exp(m - m_new)
    p = jnp.exp(s - m_new)
    l_new = alpha * l + jnp.sum(p, axis=0, keepdims=True)
    pv = lax.dot_general(v, p.astype(BF16), _TN, preferred_element_type=F32)
    return m_new, l_new, alpha * acc + pv


def _nsa_kernel(q_ref, kc_ref, vc_ref, ks_ref, vs_ref, kw_ref, vw_ref, gate_ref, ovt_ref,
                o_ref, thr_ref, *, tq, tk_s, tk_w):
    kv = pl.program_id(1)
    qi = pl.program_id(2)
    start = qi * tq
    scale = HEAD_DIM ** -0.5
    nl = NSA_GROUP * tq

    q4 = q_ref[...]
    qs = jnp.concatenate([q4[:, g * HEAD_DIM:(g + 1) * HEAD_DIM] for g in range(NSA_GROUP)], axis=0)
    t_lane = start + (lax.broadcasted_iota(jnp.int32, (1, nl), 1) & (tq - 1))

    s = lax.dot_general(kc_ref[0], qs, _NT, preferred_element_type=F32) * scale
    c_end = lax.broadcasted_iota(jnp.int32, (N_CMP_PAD, nl), 0) * CMP_STRIDE + (CMP_LEN - 1)
    cmask = c_end <= t_lane
    s = jnp.where(cmask, s, NEG)
    m = jnp.max(s, axis=0, keepdims=True)
    e = jnp.where(cmask, jnp.exp(s - m), 0.0)
    l = jnp.sum(e, axis=0, keepdims=True)
    p_cmp = e * jnp.where(l > 0.0, 1.0 / l, 0.0)
    o_cmp = lax.dot_general(vc_ref[0], p_cmp.astype(BF16), _TN, preferred_element_type=F32)

    p_sum = p_cmp[:, 0:tq]
    for g in range(1, NSA_GROUP):
        p_sum = p_sum + p_cmp[:, g * tq:(g + 1) * tq]
    p_hi = p_sum.astype(BF16)
    p_lo = (p_sum - p_hi.astype(F32)).astype(BF16)
    ovt = ovt_ref[...]
    p_slc = (jnp.dot(ovt, p_hi, preferred_element_type=F32)
             + jnp.dot(ovt, p_lo, preferred_element_type=F32))

    j_idx = lax.broadcasted_iota(jnp.int32, (N_SEL, tq), 0)
    t_q = start + lax.broadcasted_iota(jnp.int32, (N_SEL, tq), 1)
    cur = t_q // SLC_LEN
    forced = (j_idx == 0) | (j_idx == cur) | (j_idx == cur - 1)
    score = jnp.where(forced, 1e4, jnp.where(j_idx > cur, -1e4, p_slc))
    rank = jnp.zeros((N_SEL, tq), jnp.int32)
    for i in range(N_SEL):
        row = score[i:i + 1, :]
        tie = jnp.where(j_idx > i, 1, 0)
        rank = rank + jnp.where(row > score, 1, jnp.where(row == score, tie, 0))
    thr = jnp.where(rank < SLC_TOP, t_q, -1)
    thr_ref[...] = jnp.concatenate([thr] * NSA_GROUP, axis=1)

    m0 = jnp.full((1, nl), NEG, F32)
    l0 = jnp.zeros((1, nl), F32)
    acc0 = jnp.zeros((HEAD_DIM, nl), F32)

    blocks_per_tile = tk_s // SLC_LEN

    def slc_body(kt, carry):
        k0 = pl.multiple_of(kt * tk_s, tk_s)
        k = ks_ref[pl.ds(k0, tk_s), :]
        v = vs_ref[pl.ds(k0, tk_s), :]
        s = lax.dot_general(k, qs, _NT, preferred_element_type=F32) * scale
        parts = []
        for c in range(blocks_per_tile):
            thr_c = thr_ref[pl.ds(kt * blocks_per_tile + c, 1), :]
            kpos = k0 + c * SLC_LEN + lax.broadcasted_iota(jnp.int32, (SLC_LEN, nl), 0)
            parts.append(jnp.where(kpos <= thr_c, s[c * SLC_LEN:(c + 1) * SLC_LEN, :], NEG))
        s = jnp.concatenate(parts, axis=0)
        return _online_softmax_step(s, v, *carry)

    n_kt_s = (start + tq + tk_s - 1) // tk_s
    _, l_s, acc_s = lax.fori_loop(0, n_kt_s, slc_body, (m0, l0, acc0))

    def win_body(kt, carry):
        k0 = pl.multiple_of(kt * tk_w, tk_w)
        k = kw_ref[pl.ds(k0, tk_w), :]
        v = vw_ref[pl.ds(k0, tk_w), :]
        s = lax.dot_general(k, qs, _NT, preferred_element_type=F32) * scale
        kpos = k0 + lax.broadcasted_iota(jnp.int32, (tk_w, nl), 0)
        mask = (kpos <= t_lane) & (kpos > t_lane - WINDOW)
        return _online_softmax_step(jnp.where(mask, s, NEG), v, *carry)

    kt_lo = jnp.maximum(start - WINDOW, 0) // tk_w
    kt_hi = (start + tq + tk_w - 1) // tk_w
    _, l_w, acc_w = lax.fori_loop(kt_lo, kt_hi, win_body, (m0, l0, acc0))

    o_slc = acc_s * (1.0 / l_s)
    o_win = acc_w * (1.0 / l_w)
    gates_t = jax.nn.sigmoid(gate_ref[...].astype(F32)).T
    gts = jnp.where(kv == 0, gates_t[0:GATE_STRIDE], gates_t[GATE_STRIDE:2 * GATE_STRIDE])
    for g in range(NSA_GROUP):
        sl = slice(g * tq, (g + 1) * tq)
        o_t = (gts[g:g + 1, :] * o_cmp[:, sl]
               + gts[NSA_GROUP + g:NSA_GROUP + g + 1, :] * o_slc[:, sl]
               + gts[2 * NSA_GROUP + g:2 * NSA_GROUP + g + 1, :] * o_win[:, sl])
        o_ref[:, g * HEAD_DIM:(g + 1) * HEAD_DIM] = o_t.T


def _nsa_attention(proj, cmp_kv, ovt, batch, *, tq=128, tk_s=256, tk_w=128):
    t = proj.shape[0]
    nq = SEQ // tq
    rows_per_kv = N_CMP_PAD
    blk = HEAD_DIM
    est = (4 * 2 * SEQ * HEAD_DIM * 2 + 2 * tq * NSA_GROUP * HEAD_DIM * (2 + 4)
           + 24 * tk_s * NSA_GROUP * tq * 4)

    def kv_spec(col):
        return pl.BlockSpec((SEQ, HEAD_DIM), lambda b, h, i: (b, col // blk + h))

    def cmp_spec(which):
        return pl.BlockSpec((1, rows_per_kv, HEAD_DIM), lambda b, h, i: (which, b * N_NSA_KV + h, 0))

    return pl.pallas_call(
        functools.partial(_nsa_kernel, tq=tq, tk_s=tk_s, tk_w=tk_w),
        grid=(batch, N_NSA_KV, nq),
        in_specs=[
            pl.BlockSpec((tq, NSA_GROUP * HEAD_DIM), lambda b, h, i: (b * nq + i, h)),
            cmp_spec(0), cmp_spec(1),
            kv_spec(COL_KS), kv_spec(COL_VS), kv_spec(COL_KW), kv_spec(COL_VW),
            pl.BlockSpec((tq, HEAD_DIM), lambda b, h, i: (b * nq + i, COL_GATE // blk)),
            pl.BlockSpec((N_SEL, N_CMP_PAD), lambda b, h, i: (0, 0)),
        ],
        out_specs=pl.BlockSpec((tq, NSA_GROUP * HEAD_DIM), lambda b, h, i: (b * nq + i, h)),
        out_shape=jax.ShapeDtypeStruct((t, NSA_WIDTH), F32),
        scratch_shapes=[pltpu.VMEM((N_SEL, NSA_GROUP * tq), jnp.int32)],
        compiler_params=_params(("parallel", "parallel", "arbitrary"), est),
        name="nsa_attn",
    )(proj, cmp_kv, cmp_kv, proj, proj, proj, proj, proj, ovt)


def _diff_kernel(q_ref, k_ref, v_ref, lq1_ref, lk1_ref, lq2_ref, lk2_ref, sub_ref, o_ref,
                 *, tq, tk, lambda_init):
    qi = pl.program_id(2)
    start = qi * tq
    scale = HEAD_DIM ** -0.5
    q = q_ref[...]
    q1 = q[:, :HEAD_DIM]
    q2 = q[:, HEAD_DIM:]
    lam = (jnp.exp(jnp.sum(lq1_ref[...] * lk1_ref[...], axis=-1, keepdims=True))
           - jnp.exp(jnp.sum(lq2_ref[...] * lk2_ref[...], axis=-1, keepdims=True)) + lambda_init)
    t_lane = start + lax.broadcasted_iota(jnp.int32, (1, tq), 1)

    def body(kt, carry):
        m1, l1, m2, l2, acc = carry
        k0 = pl.multiple_of(kt * tk, tk)
        kk = k_ref[pl.ds(k0, tk), :]
        v = v_ref[pl.ds(k0, tk), :]
        mask = k0 + lax.broadcasted_iota(jnp.int32, (tk, tq), 0) <= t_lane
        s1 = jnp.where(mask, lax.dot_general(kk[:, :HEAD_DIM], q1, _NT, preferred_element_type=F32) * scale, NEG)
        s2 = jnp.where(mask, lax.dot_general(kk[:, HEAD_DIM:], q2, _NT, preferred_element_type=F32) * scale, NEG)
        m1n = jnp.maximum(m1, jnp.max(s1, axis=0, keepdims=True))
        m2n = jnp.maximum(m2, jnp.max(s2, axis=0, keepdims=True))
        a1 = jnp.exp(m1 - m1n)
        a2 = jnp.exp(m2 - m2n)
        p1 = jnp.exp(s1 - m1n)
        p2 = jnp.exp(s2 - m2n)
        l1n = a1 * l1 + jnp.sum(p1, axis=0, keepdims=True)
        l2n = a2 * l2 + jnp.sum(p2, axis=0, keepdims=True)
        p = jnp.concatenate([p1, p2], axis=1).astype(BF16)
        alpha = jnp.concatenate([a1, a2], axis=1)
        acc = alpha * acc + lax.dot_general(v, p, _TN, preferred_element_type=F32)
        return m1n, l1n, m2n, l2n, acc

    m0 = jnp.full((1, tq), NEG, F32)
    l0 = jnp.zeros((1, tq), F32)
    acc0 = jnp.zeros((2 * HEAD_DIM, 2 * tq), F32)
    n_kt = (start + tq + tk - 1) // tk
    _, l1, _, l2, acc = lax.fori_loop(0, n_kt, body, (m0, l0, m0, l0, acc0))
    o_t = acc[:, :tq] * (1.0 / l1) - lam * (acc[:, tq:] * (1.0 / l2))
    o = _rmsnorm(o_t.T, sub_ref[...]) * (1.0 - lambda_init)
    o_ref[...] = o.astype(o_ref.dtype)


def _diff_attention(proj, lq1, lk1, lq2, lk2, subln, batch, lambda_init, *, tq=256, tk=256):
    t = proj.shape[0]
    nq = SEQ // tq
    wide = 2 * HEAD_DIM
    est = 2 * 2 * SEQ * wide * 2 + 4 * tq * wide * 4 + 24 * tk * tq * 4
    vec = pl.BlockSpec((1, HEAD_DIM), lambda b, h, i: (0, 0))
    return pl.pallas_call(
        functools.partial(_diff_kernel, tq=tq, tk=tk, lambda_init=lambda_init),
        grid=(batch, N_DIFF_HEADS, nq),
        in_specs=[
            pl.BlockSpec((tq, wide), lambda b, h, i: (b * nq + i, COL_DQ // wide + h)),
            pl.BlockSpec((SEQ, wide), lambda b, h, i: (b, COL_DK // wide + h)),
            pl.BlockSpec((SEQ, wide), lambda b, h, i: (b, COL_DV // wide + h)),
            vec, vec, vec, vec,
            pl.BlockSpec((1, wide), lambda b, h, i: (0, 0)),
        ],
        out_specs=pl.BlockSpec((tq, wide), lambda b, h, i: (b * nq + i, h)),
        out_shape=jax.ShapeDtypeStruct((t, DIFF_WIDTH), BF16),
        compiler_params=_params(("parallel", "parallel", "arbitrary"), est),
        name="diff_attn",
    )(proj, proj, proj, lq1, lk1, lq2, lk2, subln)


def _oproj_kernel(yn_ref, g_ref, yd_ref, wo_ref, x_ref, o_ref):
    yn = _rmsnorm(yn_ref[...], g_ref[...]).astype(BF16)
    acc = jnp.dot(yn, wo_ref[:NSA_WIDTH, :], preferred_element_type=F32)
    acc = acc + jnp.dot(yd_ref[...], wo_ref[NSA_WIDTH:, :], preferred_element_type=F32)
    o_ref[...] = x_ref[...] + acc


def _out_proj(y_nsa, g, y_diff, w_o, x2, *, tm=512):
    t = x2.shape[0]
    est = (2 * (NSA_WIDTH + DIFF_WIDTH) * D_MODEL * 2 + 2 * tm * NSA_WIDTH * 4 + 2 * tm * DIFF_WIDTH * 2
           + 5 * tm * D_MODEL * 4)
    return pl.pallas_call(
        _oproj_kernel,
        grid=(t // tm,),
        in_specs=[
            pl.BlockSpec((tm, NSA_WIDTH), lambda i: (i, 0)),
            pl.BlockSpec((1, NSA_WIDTH), lambda i: (0, 0)),
            pl.BlockSpec((tm, DIFF_WIDTH), lambda i: (i, 0)),
            pl.BlockSpec((NSA_WIDTH + DIFF_WIDTH, D_MODEL), lambda i: (0, 0)),
            pl.BlockSpec((tm, D_MODEL), lambda i: (i, 0)),
        ],
        out_specs=pl.BlockSpec((tm, D_MODEL), lambda i: (i, 0)),
        out_shape=jax.ShapeDtypeStruct((t, D_MODEL), F32),
        compiler_params=_params(("parallel",), est),
        name="out_proj",
    )(y_nsa, g, y_diff, w_o, x2)


HALO = 8


def _ffn_kernel(h_ref, halo_ref, g_ref, wu_ref, wg_ref, cwu_ref, cwg_ref, cbu_ref, cbg_ref, wd_ref,
                o_ref, xn_ref, acc_ref, *, tm):
    i = pl.program_id(0)
    j = pl.program_id(1)

    @pl.when(j == 0)
    def _():
        keep = jnp.where((i * tm) % SEQ == 0, 0.0, 1.0)
        xn_ref[0:HALO, :] = (_rmsnorm(halo_ref[...], g_ref[...]) * keep).astype(BF16)
        xn_ref[HALO:, :] = _rmsnorm(h_ref[...], g_ref[...]).astype(BF16)
        acc_ref[...] = jnp.zeros_like(acc_ref)

    xn = xn_ref[...]

    def conv(w_ref, cw_ref, cb_ref):
        h = jnp.dot(xn, w_ref[...], preferred_element_type=F32)
        cw = cw_ref[...]
        out = cw[CONV_W - 1:CONV_W, :] * h[HALO:, :] + cb_ref[...]
        for back in range(1, CONV_W):
            out = out + cw[CONV_W - 1 - back:CONV_W - back, :] * pltpu.roll(h, back, 0)[HALO:, :]
        return out

    u = conv(wu_ref, cwu_ref, cbu_ref)
    gate = conv(wg_ref, cwg_ref, cbg_ref)
    act = (jax.nn.silu(gate) * u).astype(BF16)
    acc_ref[...] += jnp.dot(act, wd_ref[...], preferred_element_type=F32)

    @pl.when(j == pl.num_programs(1) - 1)
    def _():
        o_ref[...] = h_ref[...] + acc_ref[...]


def _conv_ffn(h1, g, w_up, conv_w, conv_b, w_down, *, tm=512, tf=512):
    t = h1.shape[0]
    assert t % tm == 0 and SEQ % tm == 0 and D_FF % tf == 0
    nf = D_FF // tf
    est = (4 * tm * D_MODEL * 4 + 2 * 3 * D_MODEL * tf * 2 + (tm + HALO) * D_MODEL * 2 + tm * D_MODEL * 4
           + 8 * (tm + HALO) * tf * 4)
    return pl.pallas_call(
        functools.partial(_ffn_kernel, tm=tm),
        grid=(t // tm, nf),
        in_specs=[
            pl.BlockSpec((tm, D_MODEL), lambda i, j: (i, 0)),
            pl.BlockSpec((HALO, D_MODEL), lambda i, j: (jnp.maximum(i * (tm // HALO) - 1, 0), 0)),
            pl.BlockSpec((1, D_MODEL), lambda i, j: (0, 0)),
            pl.BlockSpec((D_MODEL, tf), lambda i, j: (0, j)),
            pl.BlockSpec((D_MODEL, tf), lambda i, j: (0, nf + j)),
            pl.BlockSpec((CONV_W, tf), lambda i, j: (0, j)),
            pl.BlockSpec((CONV_W, tf), lambda i, j: (0, nf + j)),
            pl.BlockSpec((1, tf), lambda i, j: (0, j)),
            pl.BlockSpec((1, tf), lambda i, j: (0, nf + j)),
            pl.BlockSpec((tf, D_MODEL), lambda i, j: (j, 0)),
        ],
        out_specs=pl.BlockSpec((tm, D_MODEL), lambda i, j: (i, 0)),
        out_shape=jax.ShapeDtypeStruct((t, D_MODEL), F32),
        scratch_shapes=[pltpu.VMEM((tm + HALO, D_MODEL), BF16), pltpu.VMEM((tm, D_MODEL), F32)],
        compiler_params=_params(("parallel", "arbitrary"), est),
        name="conv_ffn",
    )(h1, h1, g, w_up, w_up, conv_w, conv_w, conv_b, conv_b, w_down)


def _ple_kernel(h_ref, gp_ref, wg_ref, p_ref, wp_ref, gf_ref, o_ref):
    h = h_ref[...]
    hn = _rmsnorm(h, gp_ref[...]).astype(BF16)
    gate = jax.nn.sigmoid(jnp.dot(hn, wg_ref[...], preferred_element_type=F32))
    emb = jnp.dot(p_ref[...].astype(BF16), wp_ref[...], preferred_element_type=F32)
    o_ref[...] = _rmsnorm(h + gate * emb, gf_ref[...])


def _ple_out(h2, g_ple, w_gate, p2, w_proj, g_final, *, tm=256):
    t = h2.shape[0]
    est = 2 * D_MODEL * D_MODEL * 2 + 2 * PLE_DIM * D_MODEL * 2 + 4 * tm * D_MODEL * 4 + 6 * tm * D_MODEL * 4
    return pl.pallas_call(
        _ple_kernel,
        grid=(t // tm,),
        in_specs=[
            pl.BlockSpec((tm, D_MODEL), lambda i: (i, 0)),
            pl.BlockSpec((1, D_MODEL), lambda i: (0, 0)),
            pl.BlockSpec((D_MODEL, D_MODEL), lambda i: (0, 0)),
            pl.BlockSpec((tm, PLE_DIM), lambda i: (i, 0)),
            pl.BlockSpec((PLE_DIM, D_MODEL), lambda i: (0, 0)),
            pl.BlockSpec((1, D_MODEL), lambda i: (0, 0)),
        ],
        out_specs=pl.BlockSpec((tm, D_MODEL), lambda i: (i, 0)),
        out_shape=jax.ShapeDtypeStruct((t, D_MODEL), F32),
        compiler_params=_params(("parallel",), est),
        name="ple_out",
    )(h2, g_ple, w_gate, p2, w_proj, g_final)


def _permute_w_in(w):
    sizes = (NSA_WIDTH,) + (NSA_KV_WIDTH,) * 6 + (N_GATES, DIFF_WIDTH, DIFF_WIDTH, DIFF_WIDTH)
    offs = np.concatenate([[0], np.cumsum(sizes)])
    nq, kc, vc, ks, vs, kw, vw, gate, dq, dk, dv = (w[:, offs[n]:offs[n + 1]] for n in range(len(sizes)))
    gate = gate.reshape(D_MODEL, 3, N_NSA_KV, NSA_GROUP).transpose(0, 2, 1, 3).reshape(D_MODEL, N_NSA_KV, 3 * NSA_GROUP)
    gate = jnp.pad(gate, ((0, 0), (0, 0), (0, GATE_STRIDE - 3 * NSA_GROUP))).reshape(D_MODEL, N_NSA_KV * GATE_STRIDE)
    gate = jnp.pad(gate, ((0, 0), (0, HEAD_DIM - N_NSA_KV * GATE_STRIDE)))
    return jnp.concatenate([nq, kc, ks, kw, dq, dk, vc, vs, vw, dv, gate], axis=1).astype(BF16)


def _rope_tables():
    inv = 1.0 / (ROPE_THETA ** (jnp.arange(0, ROPE_DIM, 2, dtype=F32) / ROPE_DIM))
    ang = jnp.arange(SEQ, dtype=F32)[:, None] * inv[None, :]
    cos, sin = jnp.cos(ang), jnp.sin(ang)
    rest = HEAD_DIM - ROPE_DIM
    cos_t = jnp.concatenate([cos, cos, jnp.ones((SEQ, rest), F32)], axis=1)
    sin_t = jnp.concatenate([-sin, sin, jnp.zeros((SEQ, rest), F32)], axis=1)
    return cos_t, sin_t


def _overlap_t():
    cmp_starts = np.arange(N_CMP) * CMP_STRIDE
    sel_starts = np.arange(N_SEL) * SLC_LEN
    ov = np.clip(np.minimum(cmp_starts[:, None] + CMP_LEN, sel_starts[None, :] + SLC_LEN)
                 - np.maximum(cmp_starts[:, None], sel_starts[None, :]), 0, None).astype(np.float32) / CMP_LEN
    ovt = np.zeros((N_SEL, N_CMP_PAD), np.float32)
    ovt[:, :N_CMP] = ov.T
    return jnp.asarray(ovt, BF16)


def _half_blocks(proj, col, batch):
    half = CMP_STRIDE
    a = proj[:, col:col + NSA_KV_WIDTH].reshape(batch, SEQ // half, half, N_NSA_KV, HEAD_DIM)
    return a.transpose(0, 3, 1, 2, 4).reshape(batch * N_NSA_KV * (SEQ // half), half * HEAD_DIM)


def kernel(x, p, attn_norm, w_in, cmp_k_pos, cmp_k_w1, cmp_k_w2, cmp_v_pos, cmp_v_w1, cmp_v_w2, nsa_out_norm, diff_lq1, diff_lk1, diff_lq2, diff_lk2, diff_subln, w_o, ffn_norm, w_up, conv_w, conv_b, w_down, ple_norm, w_ple_gate, w_ple_proj, final_norm):
    batch, seq, _ = x.shape
    assert seq == SEQ and p.shape[0] == 1
    t = batch * seq
    layer = 0
    lambda_init = 0.8 - 0.6 * math.exp(-0.3 * layer)
    x2 = x.reshape(t, D_MODEL)
    cos_t, sin_t = _rope_tables()

    proj = _in_proj(x2, attn_norm[layer][None], _permute_w_in(w_in[layer]), cos_t, sin_t)

    hkv = jnp.stack([_half_blocks(proj, COL_KC, batch), _half_blocks(proj, COL_VC, batch)])
    w1 = jnp.stack([cmp_k_w1[layer], cmp_v_w1[layer]]).astype(BF16)
    w2 = jnp.stack([cmp_k_w2[layer], cmp_v_w2[layer]]).astype(BF16)
    pos = jnp.stack([cmp_k_pos[layer], cmp_v_pos[layer]]).reshape(2, 1, CMP_LEN * HEAD_DIM)
    pos = jnp.broadcast_to(pos, (2, 8, CMP_LEN * HEAD_DIM)).astype(BF16)
    cmp_kv = _compress(hkv, w1, pos, w2)

    y_nsa = _nsa_attention(proj, cmp_kv, _overlap_t(), batch)
    y_diff = _diff_attention(proj, diff_lq1[layer][None], diff_lk1[layer][None], diff_lq2[layer][None],
                             diff_lk2[layer][None], diff_subln[layer][None], batch, lambda_init)
    h1 = _out_proj(y_nsa, nsa_out_norm[layer][None], y_diff, w_o[layer].astype(BF16), x2)
    h2 = _conv_ffn(h1, ffn_norm[layer][None], w_up[layer].astype(BF16), conv_w[layer], conv_b[layer][None],
                   w_down[layer].astype(BF16))
    out = _ple_out(h2, ple_norm[layer][None], w_ple_gate[layer].astype(BF16), p[layer].reshape(t, PLE_DIM),
                   w_ple_proj[layer].astype(BF16), final_norm[None])
    return out.reshape(batch, seq, D_MODEL)
```

```python
import functools
import math

import numpy as np
import jax
import jax.numpy as jnp
from jax import lax
from jax.experimental import pallas as pl
from jax.experimental.pallas import tpu as pltpu

D_MODEL = 2048
SEQ = 2048
HEAD_DIM = 128
ROPE_DIM = HEAD_DIM // 4
ROPE_THETA = 500000.0
N_NSA_HEADS = 8
N_NSA_KV = 2
NSA_GROUP = N_NSA_HEADS // N_NSA_KV
CMP_LEN = 32
CMP_STRIDE = 16
CMP_HIDDEN = 256
SLC_LEN = 64
SLC_TOP = 16
WINDOW = 512
N_DIFF_HEADS = 4
D_FF = 5632
CONV_W = 3
PLE_DIM = 256
EPS = 1e-6

NSA_WIDTH = N_NSA_HEADS * HEAD_DIM
NSA_KV_WIDTH = N_NSA_KV * HEAD_DIM
DIFF_WIDTH = N_DIFF_HEADS * 2 * HEAD_DIM
N_GATES = 3 * N_NSA_HEADS
N_CMP = (SEQ - CMP_LEN) // CMP_STRIDE + 1
N_CMP_PAD = SEQ // CMP_STRIDE
N_SEL = SEQ // SLC_LEN
GATE_STRIDE = 16

COL_NQ = 0
COL_KC = COL_NQ + NSA_WIDTH
COL_KS = COL_KC + NSA_KV_WIDTH
COL_KW = COL_KS + NSA_KV_WIDTH
COL_DQ = COL_KW + NSA_KV_WIDTH
COL_DK = COL_DQ + DIFF_WIDTH
ROPE_COLS = COL_DK + DIFF_WIDTH
COL_VC = ROPE_COLS
COL_VS = COL_VC + NSA_KV_WIDTH
COL_VW = COL_VS + NSA_KV_WIDTH
COL_DV = COL_VW + NSA_KV_WIDTH
COL_GATE = COL_DV + DIFF_WIDTH
PROJ_COLS = COL_GATE + HEAD_DIM

V7X_LANES = 128
V7X_VMEM_REQUEST_CAP = 56 * 1024 * 1024
NEG = -1e30
QSCALE = HEAD_DIM ** -0.5 * math.log2(math.e)

_NT = (((1,), (1,)), ((), ()))
_TN = (((0,), (0,)), ((), ()))
BF16 = jnp.bfloat16
F32 = jnp.float32


def _params(semantics, vmem_estimate_bytes):
    limit = min(max(int(vmem_estimate_bytes), 32 * 1024 * 1024), V7X_VMEM_REQUEST_CAP)
    return pltpu.CompilerParams(dimension_semantics=semantics, vmem_limit_bytes=limit)


def _rmsnorm(x, g):
    return x * lax.rsqrt(jnp.mean(x * x, axis=-1, keepdims=True) + EPS) * g


INPROJ_CHUNK = 1024


def _is_query_col(col):
    return col < COL_KC or COL_DQ <= col < COL_DK


def _inproj_kernel(x_ref, g_ref, w_ref, cos_ref, sin_ref, o_ref):
    xn = _rmsnorm(x_ref[...], g_ref[...]).astype(BF16)
    c = cos_ref[...]
    s = sin_ref[...]
    cq = c * QSCALE
    sq = s * QSCALE
    first_half = lax.broadcasted_iota(jnp.int32, c.shape, 1) < ROPE_DIM // 2
    for lo in range(0, PROJ_COLS, INPROJ_CHUNK):
        hi = min(lo + INPROJ_CHUNK, PROJ_COLS)
        acc = jnp.dot(xn, w_ref[:, lo:hi], preferred_element_type=F32)
        for col in range(lo, hi, HEAD_DIM):
            a = acc[:, col - lo:col - lo + HEAD_DIM]
            if col < ROPE_COLS:
                partner = jnp.where(first_half,
                                    pltpu.roll(a, HEAD_DIM - ROPE_DIM // 2, 1),
                                    pltpu.roll(a, ROPE_DIM // 2, 1))
                a = a * cq + partner * sq if _is_query_col(col) else a * c + partner * s
            o_ref[:, col:col + HEAD_DIM] = a.astype(o_ref.dtype)


def _in_proj(x2, g, w_p, cos_t, sin_t, *, tm=256):
    t = x2.shape[0]
    assert t % tm == 0 and SEQ % tm == 0
    seq_tiles = SEQ // tm
    est = (D_MODEL * PROJ_COLS * 2 + 2 * tm * D_MODEL * 4 + 2 * tm * PROJ_COLS * 2 + tm * D_MODEL * 2
           + 3 * tm * INPROJ_CHUNK * 4)
    return pl.pallas_call(
        _inproj_kernel,
        grid=(t // tm,),
        in_specs=[
            pl.BlockSpec((tm, D_MODEL), lambda i: (i, 0)),
            pl.BlockSpec((1, D_MODEL), lambda i: (0, 0)),
            pl.BlockSpec((D_MODEL, PROJ_COLS), lambda i: (0, 0), pipeline_mode=pl.Buffered(1)),
            pl.BlockSpec((tm, HEAD_DIM), lambda i: (i % seq_tiles, 0)),
            pl.BlockSpec((tm, HEAD_DIM), lambda i: (i % seq_tiles, 0)),
        ],
        out_specs=pl.BlockSpec((tm, PROJ_COLS), lambda i: (i, 0)),
        out_shape=jax.ShapeDtypeStruct((t, PROJ_COLS), BF16),
        compiler_params=_params(("parallel",), est),
        name="in_proj",
    )(x2, g, w_p, cos_t, sin_t)


def _compress_kernel(h_ref, w1_ref, pos_ref, w2_ref, o_ref):
    half = CMP_LEN * HEAD_DIM // 2
    h = h_ref[0]
    top = jnp.dot(h, w1_ref[0, :half, :], preferred_element_type=F32)
    bot = jnp.dot(h, w1_ref[0, half:, :], preferred_element_type=F32)
    pos_bias = jnp.dot(pos_ref[0], w1_ref[0], preferred_element_type=F32)[0:1]
    pre = top + pltpu.roll(bot, bot.shape[0] - 1, 0) + pos_bias
    act = jax.nn.gelu(pre)
    o_ref[0] = jnp.dot(act.astype(BF16), w2_ref[0], preferred_element_type=F32).astype(o_ref.dtype)


def _compress(hkv, w1, pos, w2):
    rows = hkv.shape[1]
    kdim = CMP_LEN * HEAD_DIM
    est = 2 * (rows * kdim // 2 * 2 + kdim * CMP_HIDDEN * 2) + 6 * rows * CMP_HIDDEN * 4
    return pl.pallas_call(
        _compress_kernel,
        grid=(2,),
        in_specs=[
            pl.BlockSpec((1, rows, kdim // 2), lambda i: (i, 0, 0)),
            pl.BlockSpec((1, kdim, CMP_HIDDEN), lambda i: (i, 0, 0)),
            pl.BlockSpec((1, 8, kdim), lambda i: (i, 0, 0)),
            pl.BlockSpec((1, CMP_HIDDEN, HEAD_DIM), lambda i: (i, 0, 0)),
        ],
        out_specs=pl.BlockSpec((1, rows, HEAD_DIM), lambda i: (i, 0, 0)),
        out_shape=jax.ShapeDtypeStruct((2, rows, HEAD_DIM), BF16),
        compiler_params=_params(("parallel",), est),
        name="compress",
    )(hkv, w1, pos, w2)


def _online_softmax_step(s, v, m, l, acc):
    m_new = jnp.maximum(m, jnp.max(s, axis=0, keepdims=True))
    alpha = jnp.exp2(m - m_new)
    p = jnp.exp2(s - m_new)
    l_new = alpha * l + jnp.sum(p, axis=0, keepdims=True)
    pv = lax.dot_general(v, p.astype(BF16), _TN, preferred_element_type=F32)
    return m_new, l_new, alpha * acc + pv


def _nsa_kernel(q_ref, kc_ref, vc_ref, ks_ref, vs_ref, kw_ref, vw_ref, gate_ref, ovt_ref,
                o_ref, bias_ref, *, tq):
    kv = pl.program_id(1)
    qi = pl.program_id(2)
    start = qi * tq
    nl = NSA_GROUP * tq

    q4 = q_ref[...]
    qs = jnp.concatenate([q4[:, g * HEAD_DIM:(g + 1) * HEAD_DIM] for g in range(NSA_GROUP)], axis=0)
    q_local = lax.broadcasted_iota(jnp.int32, (1, nl), 1) & (tq - 1)
    t_lane = start + q_local
    k_local = lax.broadcasted_iota(jnp.int32, (tq, nl), 0)
    causal = k_local <= q_local

    s = lax.dot_general(kc_ref[0], qs, _NT, preferred_element_type=F32)
    c_end = lax.broadcasted_iota(jnp.int32, (N_CMP_PAD, nl), 0) * CMP_STRIDE + (CMP_LEN - 1)
    cmask = c_end <= t_lane
    s = jnp.where(cmask, s, NEG)
    m = jnp.max(s, axis=0, keepdims=True)
    e = jnp.where(cmask, jnp.exp2(s - m), 0.0)
    l = jnp.sum(e, axis=0, keepdims=True)
    p_cmp = e * jnp.where(l > 0.0, 1.0 / l, 0.0)
    o_cmp = lax.dot_general(vc_ref[0], p_cmp.astype(BF16), _TN, preferred_element_type=F32)

    p_sum = p_cmp[:, 0:tq]
    for g in range(1, NSA_GROUP):
        p_sum = p_sum + p_cmp[:, g * tq:(g + 1) * tq]
    p_hi = p_sum.astype(BF16)
    p_lo = (p_sum - p_hi.astype(F32)).astype(BF16)
    ovt = ovt_ref[...]
    p_slc = (jnp.dot(ovt, p_hi, preferred_element_type=F32)
             + jnp.dot(ovt, p_lo, preferred_element_type=F32))

    j_idx = lax.broadcasted_iota(jnp.int32, (N_SEL, tq), 0)
    t_q = start + lax.broadcasted_iota(jnp.int32, (N_SEL, tq), 1)
    cur = t_q // SLC_LEN
    forced = (j_idx == 0) | (j_idx == cur) | (j_idx == cur - 1)
    score = jnp.where(forced, 1e4, jnp.where(j_idx > cur, -1e4, p_slc))
    rank = jnp.zeros((N_SEL, tq), jnp.int32)
    for i in range(N_SEL):
        row = score[i:i + 1, :]
        tie = jnp.where(j_idx > i, 1, 0)
        rank = rank + jnp.where(row > score, 1, jnp.where(row == score, tie, 0))
    bias = jnp.where((rank < SLC_TOP) & (j_idx <= cur), 0.0, NEG)
    bias_ref[...] = jnp.concatenate([bias] * NSA_GROUP, axis=1)
    blocks_per_tile = tq // SLC_LEN

    def slc_scores(kt):
        k0 = pl.multiple_of(kt * tq, tq)
        s = lax.dot_general(ks_ref[pl.ds(k0, tq), :], qs, _NT, preferred_element_type=F32)
        parts = [s[c * SLC_LEN:(c + 1) * SLC_LEN, :] + bias_ref[pl.ds(kt * blocks_per_tile + c, 1), :]
                 for c in range(blocks_per_tile)]
        return jnp.concatenate(parts, axis=0), vs_ref[pl.ds(k0, tq), :]

    def slc_body(kt, carry):
        s, v = slc_scores(kt)
        return _online_softmax_step(s, v, *carry)

    m0 = jnp.full((1, nl), NEG, F32)
    l0 = jnp.zeros((1, nl), F32)
    acc0 = jnp.zeros((HEAD_DIM, nl), F32)
    carry = lax.fori_loop(0, qi, slc_body, (m0, l0, acc0))
    s, v = slc_scores(qi)
    _, l_s, acc_s = _online_softmax_step(jnp.where(causal, s, NEG), v, *carry)
    o_slc = acc_s * (1.0 / l_s)

    scores, values = [], []
    for back in range(WINDOW // tq, -1, -1):
        k0 = start - back * tq
        inside = k0 >= 0
        k0 = pl.multiple_of(jnp.maximum(k0, 0), tq)
        s = lax.dot_general(kw_ref[pl.ds(k0, tq), :], qs, _NT, preferred_element_type=F32)
        if back == 0:
            s = jnp.where(causal, s, NEG)
        elif back == WINDOW // tq:
            edge = q_local + jnp.where(inside, 0, tq)
            s = jnp.where(k_local > edge, s, NEG)
        else:
            s = s + jnp.where(inside, 0.0, NEG)
        scores.append(s)
        values.append(vw_ref[pl.ds(k0, tq), :])
    s = jnp.concatenate(scores, axis=0)
    p = jnp.exp2(s - jnp.max(s, axis=0, keepdims=True))
    l_w = jnp.sum(p, axis=0, keepdims=True)
    acc_w = lax.dot_general(jnp.concatenate(values, axis=0), p.astype(BF16), _TN, preferred_element_type=F32)
    o_win = acc_w * (1.0 / l_w)

    gates_t = jax.nn.sigmoid(gate_ref[...].astype(F32)).T
    gts = jnp.where(kv == 0, gates_t[0:GATE_STRIDE], gates_t[GATE_STRIDE:2 * GATE_STRIDE])
    for g in range(NSA_GROUP):
        sl = slice(g * tq, (g + 1) * tq)
        o_t = (gts[g:g + 1, :] * o_cmp[:, sl]
               + gts[NSA_GROUP + g:NSA_GROUP + g + 1, :] * o_slc[:, sl]
               + gts[2 * NSA_GROUP + g:2 * NSA_GROUP + g + 1, :] * o_win[:, sl])
        o_ref[:, g * HEAD_DIM:(g + 1) * HEAD_DIM] = o_t.T


def _nsa_attention(proj, cmp_kv, ovt, batch, *, tq=256):
    t = proj.shape[0]
    assert WINDOW % tq == 0 and tq % SLC_LEN == 0 and SEQ % tq == 0
    nq = SEQ // tq
    rows_per_kv = N_CMP_PAD
    blk = HEAD_DIM
    est = (4 * 2 * SEQ * HEAD_DIM * 2 + 2 * tq * NSA_GROUP * HEAD_DIM * (2 + 4)
           + 10 * (WINDOW + tq) * NSA_GROUP * tq * 4)

    def kv_spec(col):
        return pl.BlockSpec((SEQ, HEAD_DIM), lambda b, h, i: (b, col // blk + h))

    def cmp_spec(which):
        return pl.BlockSpec((1, rows_per_kv, HEAD_DIM), lambda b, h, i: (which, b * N_NSA_KV + h, 0))

    return pl.pallas_call(
        functools.partial(_nsa_kernel, tq=tq),
        grid=(batch, N_NSA_KV, nq),
        in_specs=[
            pl.BlockSpec((tq, NSA_GROUP * HEAD_DIM), lambda b, h, i: (b * nq + i, h)),
            cmp_spec(0), cmp_spec(1),
            kv_spec(COL_KS), kv_spec(COL_VS), kv_spec(COL_KW), kv_spec(COL_VW),
            pl.BlockSpec((tq, HEAD_DIM), lambda b, h, i: (b * nq + i, COL_GATE // blk)),
            pl.BlockSpec((N_SEL, N_CMP_PAD), lambda b, h, i: (0, 0)),
        ],
        out_specs=pl.BlockSpec((tq, NSA_GROUP * HEAD_DIM), lambda b, h, i: (b * nq + i, h)),
        out_shape=jax.ShapeDtypeStruct((t, NSA_WIDTH), F32),
        scratch_shapes=[pltpu.VMEM((N_SEL, NSA_GROUP * tq), F32)],
        compiler_params=_params(("parallel", "parallel", "arbitrary"), est),
        name="nsa_attn",
    )(proj, cmp_kv, cmp_kv, proj, proj, proj, proj, proj, ovt)


def _diff_kernel(q_ref, k_ref, v_ref, lq1_ref, lk1_ref, lq2_ref, lk2_ref, sub_ref, o_ref,
                 *, tq, lambda_init):
    qi = pl.program_id(2)
    q = q_ref[...]
    q1 = q[:, :HEAD_DIM]
    q2 = q[:, HEAD_DIM:]
    lam = (jnp.exp(jnp.sum(lq1_ref[...] * lk1_ref[...], axis=-1, keepdims=True))
           - jnp.exp(jnp.sum(lq2_ref[...] * lk2_ref[...], axis=-1, keepdims=True)) + lambda_init)

    def scores(kt):
        k0 = pl.multiple_of(kt * tq, tq)
        kk = k_ref[pl.ds(k0, tq), :]
        s1 = lax.dot_general(kk[:, :HEAD_DIM], q1, _NT, preferred_element_type=F32)
        s2 = lax.dot_general(kk[:, HEAD_DIM:], q2, _NT, preferred_element_type=F32)
        return jnp.concatenate([s1, s2], axis=1), v_ref[pl.ds(k0, tq), :]

    def body(kt, carry):
        s, v = scores(kt)
        return _online_softmax_step(s, v, *carry)

    m0 = jnp.full((1, 2 * tq), NEG, F32)
    l0 = jnp.zeros((1, 2 * tq), F32)
    acc0 = jnp.zeros((2 * HEAD_DIM, 2 * tq), F32)
    carry = lax.fori_loop(0, qi, body, (m0, l0, acc0))
    s, v = scores(qi)
    k_local = lax.broadcasted_iota(jnp.int32, (tq, 2 * tq), 0)
    q_local = lax.broadcasted_iota(jnp.int32, (1, 2 * tq), 1) & (tq - 1)
    _, l, acc = _online_softmax_step(jnp.where(k_local <= q_local, s, NEG), v, *carry)
    o_n = acc * (1.0 / l)
    o_t = o_n[:, :tq] - lam * o_n[:, tq:]
    o = _rmsnorm(o_t.T, sub_ref[...]) * (1.0 - lambda_init)
    o_ref[...] = o.astype(o_ref.dtype)


def _diff_attention(proj, lq1, lk1, lq2, lk2, subln, batch, lambda_init, *, tq=512):
    t = proj.shape[0]
    nq = SEQ // tq
    wide = 2 * HEAD_DIM
    est = 2 * 2 * SEQ * wide * 2 + 4 * tq * wide * 4 + 16 * tq * 2 * tq * 4
    vec = pl.BlockSpec((1, HEAD_DIM), lambda b, h, i: (0, 0))
    return pl.pallas_call(
        functools.partial(_diff_kernel, tq=tq, lambda_init=lambda_init),
        grid=(batch, N_DIFF_HEADS, nq),
        in_specs=[
            pl.BlockSpec((tq, wide), lambda b, h, i: (b * nq + i, COL_DQ // wide + h)),
            pl.BlockSpec((SEQ, wide), lambda b, h, i: (b, COL_DK // wide + h)),
            pl.BlockSpec((SEQ, wide), lambda b, h, i: (b, COL_DV // wide + h)),
            vec, vec, vec, vec,
            pl.BlockSpec((1, wide), lambda b, h, i: (0, 0)),
        ],
        out_specs=pl.BlockSpec((tq, wide), lambda b, h, i: (b * nq + i, h)),
        out_shape=jax.ShapeDtypeStruct((t, DIFF_WIDTH), BF16),
        compiler_params=_params(("parallel", "parallel", "arbitrary"), est),
        name="diff_attn",
    )(proj, proj, proj, lq1, lk1, lq2, lk2, subln)


def _oproj_kernel(yn_ref, g_ref, yd_ref, wo_ref, x_ref, o_ref):
    yn = _rmsnorm(yn_ref[...], g_ref[...]).astype(BF16)
    acc = jnp.dot(yn, wo_ref[:NSA_WIDTH, :], preferred_element_type=F32)
    acc = acc + jnp.dot(yd_ref[...], wo_ref[NSA_WIDTH:, :], preferred_element_type=F32)
    o_ref[...] = x_ref[...] + acc


def _out_proj(y_nsa, g, y_diff, w_o, x2, *, tm=512):
    t = x2.shape[0]
    est = (2 * (NSA_WIDTH + DIFF_WIDTH) * D_MODEL * 2 + 2 * tm * NSA_WIDTH * 4 + 2 * tm * DIFF_WIDTH * 2
           + 5 * tm * D_MODEL * 4)
    return pl.pallas_call(
        _oproj_kernel,
        grid=(t // tm,),
        in_specs=[
            pl.BlockSpec((tm, NSA_WIDTH), lambda i: (i, 0)),
            pl.BlockSpec((1, NSA_WIDTH), lambda i: (0, 0)),
            pl.BlockSpec((tm, DIFF_WIDTH), lambda i: (i, 0)),
            pl.BlockSpec((NSA_WIDTH + DIFF_WIDTH, D_MODEL), lambda i: (0, 0)),
            pl.BlockSpec((tm, D_MODEL), lambda i: (i, 0)),
        ],
        out_specs=pl.BlockSpec((tm, D_MODEL), lambda i: (i, 0)),
        out_shape=jax.ShapeDtypeStruct((t, D_MODEL), F32),
        compiler_params=_params(("parallel",), est),
        name="out_proj",
    )(y_nsa, g, y_diff, w_o, x2)


HALO = 8
FFN_CHUNK = 256


def _ffn_kernel(h_ref, halo_ref, g_ref, wu_ref, wg_ref, cwu_ref, cwg_ref, cbu_ref, cbg_ref, wd_ref,
                o_ref, xn_ref, acc_ref, *, tm):
    i = pl.program_id(0)
    j = pl.program_id(1)

    @pl.when(j == 0)
    def _():
        keep = jnp.where((i * tm) % SEQ == 0, 0.0, 1.0)
        xn_ref[0:HALO, :] = (_rmsnorm(halo_ref[...], g_ref[...]) * keep).astype(BF16)
        xn_ref[HALO:, :] = _rmsnorm(h_ref[...], g_ref[...]).astype(BF16)
        acc_ref[...] = jnp.zeros_like(acc_ref)

    xn = xn_ref[...]

    def conv(h, cw_ref, cb_ref, cols):
        cw = cw_ref[:, cols]
        out = cw[CONV_W - 1:CONV_W, :] * h[HALO:, :] + cb_ref[:, cols]
        for back in range(1, CONV_W):
            out = out + cw[CONV_W - 1 - back:CONV_W - back, :] * pltpu.roll(h, back, 0)[HALO:, :]
        return out

    chunks = [slice(lo, lo + FFN_CHUNK) for lo in range(0, wu_ref.shape[1], FFN_CHUNK)]
    ups = [(jnp.dot(xn, wu_ref[:, cols], preferred_element_type=F32),
            jnp.dot(xn, wg_ref[:, cols], preferred_element_type=F32)) for cols in chunks]
    down = None
    for cols, (hu, hg) in zip(chunks, ups):
        u = conv(hu, cwu_ref, cbu_ref, cols)
        gate = conv(hg, cwg_ref, cbg_ref, cols)
        act = (jax.nn.silu(gate) * u).astype(BF16)
        part = jnp.dot(act, wd_ref[cols, :], preferred_element_type=F32)
        down = part if down is None else down + part
    acc_ref[...] += down

    @pl.when(j == pl.num_programs(1) - 1)
    def _():
        o_ref[...] = h_ref[...] + acc_ref[...]


def _conv_ffn(h1, g, w_up, conv_w, conv_b, w_down, *, tm=512, tf=512):
    t = h1.shape[0]
    assert t % tm == 0 and SEQ % tm == 0 and D_FF % tf == 0
    nf = D_FF // tf
    est = (4 * tm * D_MODEL * 4 + 2 * 3 * D_MODEL * tf * 2 + (tm + HALO) * D_MODEL * 2 + tm * D_MODEL * 4
           + 8 * (tm + HALO) * tf * 4)
    return pl.pallas_call(
        functools.partial(_ffn_kernel, tm=tm),
        grid=(t // tm, nf),
        in_specs=[
            pl.BlockSpec((tm, D_MODEL), lambda i, j: (i, 0)),
            pl.BlockSpec((HALO, D_MODEL), lambda i, j: (jnp.maximum(i * (tm // HALO) - 1, 0), 0)),
            pl.BlockSpec((1, D_MODEL), lambda i, j: (0, 0)),
            pl.BlockSpec((D_MODEL, tf), lambda i, j: (0, j)),
            pl.BlockSpec((D_MODEL, tf), lambda i, j: (0, nf + j)),
            pl.BlockSpec((CONV_W, tf), lambda i, j: (0, j)),
            pl.BlockSpec((CONV_W, tf), lambda i, j: (0, nf + j)),
            pl.BlockSpec((1, tf), lambda i, j: (0, j)),
            pl.BlockSpec((1, tf), lambda i, j: (0, nf + j)),
            pl.BlockSpec((tf, D_MODEL), lambda i, j: (j, 0)),
        ],
        out_specs=pl.BlockSpec((tm, D_MODEL), lambda i, j: (i, 0)),
        out_shape=jax.ShapeDtypeStruct((t, D_MODEL), F32),
        scratch_shapes=[pltpu.VMEM((tm + HALO, D_MODEL), BF16), pltpu.VMEM((tm, D_MODEL), F32)],
        compiler_params=_params(("parallel", "arbitrary"), est),
        name="conv_ffn",
    )(h1, h1, g, w_up, w_up, conv_w, conv_w, conv_b, conv_b, w_down)


def _ple_kernel(h_ref, gp_ref, wg_ref, p_ref, wp_ref, gf_ref, o_ref):
    h = h_ref[...]
    hn = _rmsnorm(h, gp_ref[...]).astype(BF16)
    gate = jax.nn.sigmoid(jnp.dot(hn, wg_ref[...], preferred_element_type=F32))
    emb = jnp.dot(p_ref[...].astype(BF16), wp_ref[...], preferred_element_type=F32)
    o_ref[...] = _rmsnorm(h + gate * emb, gf_ref[...])


def _ple_out(h2, g_ple, w_gate, p2, w_proj, g_final, *, tm=256):
    t = h2.shape[0]
    est = 2 * D_MODEL * D_MODEL * 2 + 2 * PLE_DIM * D_MODEL * 2 + 4 * tm * D_MODEL * 4 + 6 * tm * D_MODEL * 4
    return pl.pallas_call(
        _ple_kernel,
        grid=(t // tm,),
        in_specs=[
            pl.BlockSpec((tm, D_MODEL), lambda i: (i, 0)),
            pl.BlockSpec((1, D_MODEL), lambda i: (0, 0)),
            pl.BlockSpec((D_MODEL, D_MODEL), lambda i: (0, 0)),
            pl.BlockSpec((tm, PLE_DIM), lambda i: (i, 0)),
            pl.BlockSpec((PLE_DIM, D_MODEL), lambda i: (0, 0)),
            pl.BlockSpec((1, D_MODEL), lambda i: (0, 0)),
        ],
        out_specs=pl.BlockSpec((tm, D_MODEL), lambda i: (i, 0)),
        out_shape=jax.ShapeDtypeStruct((t, D_MODEL), F32),
        compiler_params=_params(("parallel",), est),
        name="ple_out",
    )(h2, g_ple, w_gate, p2, w_proj, g_final)


def _permute_w_in(w):
    sizes = (NSA_WIDTH,) + (NSA_KV_WIDTH,) * 6 + (N_GATES, DIFF_WIDTH, DIFF_WIDTH, DIFF_WIDTH)
    offs = np.concatenate([[0], np.cumsum(sizes)])
    nq, kc, vc, ks, vs, kw, vw, gate, dq, dk, dv = (w[:, offs[n]:offs[n + 1]] for n in range(len(sizes)))
    gate = gate.reshape(D_MODEL, 3, N_NSA_KV, NSA_GROUP).transpose(0, 2, 1, 3).reshape(D_MODEL, N_NSA_KV, 3 * NSA_GROUP)
    gate = jnp.pad(gate, ((0, 0), (0, 0), (0, GATE_STRIDE - 3 * NSA_GROUP))).reshape(D_MODEL, N_NSA_KV * GATE_STRIDE)
    gate = jnp.pad(gate, ((0, 0), (0, HEAD_DIM - N_NSA_KV * GATE_STRIDE)))
    return jnp.concatenate([nq, kc, ks, kw, dq, dk, vc, vs, vw, dv, gate], axis=1).astype(BF16)


def _rope_tables():
    inv = 1.0 / (ROPE_THETA ** (jnp.arange(0, ROPE_DIM, 2, dtype=F32) / ROPE_DIM))
    ang = jnp.arange(SEQ, dtype=F32)[:, None] * inv[None, :]
    cos, sin = jnp.cos(ang), jnp.sin(ang)
    rest = HEAD_DIM - ROPE_DIM
    cos_t = jnp.concatenate([cos, cos, jnp.ones((SEQ, rest), F32)], axis=1)
    sin_t = jnp.concatenate([-sin, sin, jnp.zeros((SEQ, rest), F32)], axis=1)
    return cos_t, sin_t


def _overlap_t():
    cmp_starts = np.arange(N_CMP) * CMP_STRIDE
    sel_starts = np.arange(N_SEL) * SLC_LEN
    ov = np.clip(np.minimum(cmp_starts[:, None] + CMP_LEN, sel_starts[None, :] + SLC_LEN)
                 - np.maximum(cmp_starts[:, None], sel_starts[None, :]), 0, None).astype(np.float32) / CMP_LEN
    ovt = np.zeros((N_SEL, N_CMP_PAD), np.float32)
    ovt[:, :N_CMP] = ov.T
    return jnp.asarray(ovt, BF16)


def _half_blocks(proj, col, batch):
    half = CMP_STRIDE
    a = proj[:, col:col + NSA_KV_WIDTH].reshape(batch, SEQ // half, half, N_NSA_KV, HEAD_DIM)
    return a.transpose(0, 3, 1, 2, 4).reshape(batch * N_NSA_KV * (SEQ // half), half * HEAD_DIM)


def kernel(x, p, attn_norm, w_in, cmp_k_pos, cmp_k_w1, cmp_k_w2, cmp_v_pos, cmp_v_w1, cmp_v_w2, nsa_out_norm, diff_lq1, diff_lk1, diff_lq2, diff_lk2, diff_subln, w_o, ffn_norm, w_up, conv_w, conv_b, w_down, ple_norm, w_ple_gate, w_ple_proj, final_norm):
    batch, seq, _ = x.shape
    assert seq == SEQ and p.shape[0] == 1
    t = batch * seq
    layer = 0
    lambda_init = 0.8 - 0.6 * math.exp(-0.3 * layer)
    x2 = x.reshape(t, D_MODEL)
    cos_t, sin_t = _rope_tables()

    proj = _in_proj(x2, attn_norm[layer][None], _permute_w_in(w_in[layer]), cos_t, sin_t)

    hkv = jnp.stack([_half_blocks(proj, COL_KC, batch), _half_blocks(proj, COL_VC, batch)])
    w1 = jnp.stack([cmp_k_w1[layer], cmp_v_w1[layer]]).astype(BF16)
    w2 = jnp.stack([cmp_k_w2[layer], cmp_v_w2[layer]]).astype(BF16)
    pos = jnp.stack([cmp_k_pos[layer], cmp_v_pos[layer]]).reshape(2, 1, CMP_LEN * HEAD_DIM)
    pos = jnp.broadcast_to(pos, (2, 8, CMP_LEN * HEAD_DIM)).astype(BF16)
    cmp_kv = _compress(hkv, w1, pos, w2)

    y_nsa = _nsa_attention(proj, cmp_kv, _overlap_t(), batch)
    y_diff = _diff_attention(proj, diff_lq1[layer][None], diff_lk1[layer][None], diff_lq2[layer][None],
                             diff_lk2[layer][None], diff_subln[layer][None], batch, lambda_init)
    h1 = _out_proj(y_nsa, nsa_out_norm[layer][None], y_diff, w_o[layer].astype(BF16), x2)
    h2 = _conv_ffn(h1, ffn_norm[layer][None], w_up[layer].astype(BF16), conv_w[layer], conv_b[layer][None],
                   w_down[layer].astype(BF16))
    out = _ple_out(h2, ple_norm[layer][None], w_ple_gate[layer].astype(BF16), p[layer].reshape(t, PLE_DIM),
                   w_ple_proj[layer].astype(BF16), final_norm[None])
    return out.reshape(batch, seq, D_MODEL)
```

```python
import functools
import math

import numpy as np
import jax
import jax.numpy as jnp
from jax import lax
from jax.experimental import pallas as pl
from jax.experimental.pallas import tpu as pltpu

D_MODEL = 2048
SEQ = 2048
HEAD_DIM = 128
ROPE_DIM = HEAD_DIM // 4
ROPE_THETA = 500000.0
N_NSA_HEADS = 8
N_NSA_KV = 2
NSA_GROUP = N_NSA_HEADS // N_NSA_KV
CMP_LEN = 32
CMP_STRIDE = 16
CMP_HIDDEN = 256
SLC_LEN = 64
SLC_TOP = 16
WINDOW = 512
N_DIFF_HEADS = 4
D_FF = 5632
CONV_W = 3
PLE_DIM = 256
EPS = 1e-6

NSA_WIDTH = N_NSA_HEADS * HEAD_DIM
NSA_KV_WIDTH = N_NSA_KV * HEAD_DIM
DIFF_WIDTH = N_DIFF_HEADS * 2 * HEAD_DIM
N_GATES = 3 * N_NSA_HEADS
N_CMP = (SEQ - CMP_LEN) // CMP_STRIDE + 1
N_CMP_PAD = SEQ // CMP_STRIDE
N_SEL = SEQ // SLC_LEN
GATE_STRIDE = 16

COL_NQ = 0
COL_KS = COL_NQ + NSA_WIDTH
COL_KW = COL_KS + NSA_KV_WIDTH
COL_DQ = COL_KW + NSA_KV_WIDTH
COL_DK = COL_DQ + DIFF_WIDTH
COL_VS = COL_DK + DIFF_WIDTH
COL_VW = COL_VS + NSA_KV_WIDTH
COL_DV = COL_VW + NSA_KV_WIDTH
COL_GATE = COL_DV + DIFF_WIDTH
PROJ_COLS = COL_GATE + HEAD_DIM

V7X_LANES = 128
SUBLANES = 8
V7X_VMEM_REQUEST_CAP = 56 * 1024 * 1024
NEG = -1e30
QSCALE = HEAD_DIM ** -0.5 * math.log2(math.e)

_NT = (((1,), (1,)), ((), ()))
_TN = (((0,), (0,)), ((), ()))
BF16 = jnp.bfloat16
F32 = jnp.float32


def _params(semantics, vmem_estimate_bytes, flags=None):
    limit = min(max(int(vmem_estimate_bytes), 32 * 1024 * 1024), V7X_VMEM_REQUEST_CAP)
    return pltpu.CompilerParams(dimension_semantics=semantics, vmem_limit_bytes=limit, flags=flags)


def _rmsnorm(x, g):
    return x * lax.rsqrt(jnp.mean(x * x, axis=-1, keepdims=True) + EPS) * g


INPROJ_CHUNK = 1024
HALF_BLOCK = CMP_STRIDE

_HEADS_A = ([(COL_NQ + h * HEAD_DIM, True, True) for h in range(N_NSA_HEADS)]
            + [(("cmp", 0, h), True, False) for h in range(N_NSA_KV)]
            + [(("cmp", 1, h), False, False) for h in range(N_NSA_KV)]
            + [(COL_KS + h * HEAD_DIM, True, False) for h in range(N_NSA_KV)]
            + [(COL_VS + h * HEAD_DIM, False, False) for h in range(N_NSA_KV)]
            + [(COL_KW + h * HEAD_DIM, True, False) for h in range(N_NSA_KV)]
            + [(COL_VW + h * HEAD_DIM, False, False) for h in range(N_NSA_KV)])
_HEADS_B = ([(COL_DQ + h * HEAD_DIM, True, True) for h in range(2 * N_DIFF_HEADS)]
            + [(COL_DK + h * HEAD_DIM, True, False) for h in range(2 * N_DIFF_HEADS)]
            + [(COL_DV + h * HEAD_DIM, False, False) for h in range(2 * N_DIFF_HEADS)])
_HEADS_GATE = [(COL_GATE, False, False)]


def _inproj_kernel(x_ref, g_ref, wa_ref, wg_ref, wb_ref, cos_ref, sin_ref, o_ref, hkv_ref, stage_ref):
    tm = x_ref.shape[0]
    xn = _rmsnorm(x_ref[...], g_ref[...]).astype(BF16)
    c = cos_ref[...]
    s = sin_ref[...]
    cq = c * QSCALE
    sq = s * QSCALE
    first_half = lax.broadcasted_iota(jnp.int32, c.shape, 1) < ROPE_DIM // 2
    n_staged = 0
    for w_ref, heads in ((wa_ref, _HEADS_A), (wg_ref, _HEADS_GATE), (wb_ref, _HEADS_B)):
        for lo in range(0, w_ref.shape[1], INPROJ_CHUNK):
            hi = min(lo + INPROJ_CHUNK, w_ref.shape[1])
            acc = jnp.dot(xn, w_ref[:, lo:hi], preferred_element_type=F32)
            for src in range(lo, hi, HEAD_DIM):
                dest, rope, is_query = heads[src // HEAD_DIM]
                a = acc[:, src - lo:src - lo + HEAD_DIM]
                if rope:
                    partner = jnp.where(first_half,
                                        pltpu.roll(a, HEAD_DIM - ROPE_DIM // 2, 1),
                                        pltpu.roll(a, ROPE_DIM // 2, 1))
                    a = a * cq + partner * sq if is_query else a * c + partner * s
                if isinstance(dest, tuple):
                    _, which, head = dest
                    stage = stage_ref.at[n_staged]
                    n_staged += 1
                    stage[...] = a
                    for r in range(HALF_BLOCK):
                        rows = stage[pl.ds(r, tm // HALF_BLOCK, stride=HALF_BLOCK), :]
                        hkv_ref[which, head, :, r * HEAD_DIM:(r + 1) * HEAD_DIM] = rows.astype(hkv_ref.dtype)
                else:
                    o_ref[:, dest:dest + HEAD_DIM] = a.astype(o_ref.dtype)


def _in_proj(x2, g, w_a, w_g, w_b, cos_t, sin_t, *, tm=256):
    t = x2.shape[0]
    assert t % tm == 0 and SEQ % tm == 0 and tm % (HALF_BLOCK * 16) == 0
    seq_tiles = SEQ // tm
    n_w = w_a.shape[1] + w_g.shape[1] + w_b.shape[1]
    est = (D_MODEL * n_w * 2 + 2 * tm * D_MODEL * 4 + 2 * tm * n_w * 2 + tm * D_MODEL * 2
           + 3 * tm * INPROJ_CHUNK * 4)
    resident = dict(pipeline_mode=pl.Buffered(1))
    half_cols = HALF_BLOCK * HEAD_DIM
    return pl.pallas_call(
        _inproj_kernel,
        grid=(t // tm,),
        in_specs=[
            pl.BlockSpec((tm, D_MODEL), lambda i: (i, 0)),
            pl.BlockSpec((1, D_MODEL), lambda i: (0, 0)),
            pl.BlockSpec(w_a.shape, lambda i: (0, 0), **resident),
            pl.BlockSpec(w_g.shape, lambda i: (0, 0), **resident),
            pl.BlockSpec(w_b.shape, lambda i: (0, 0), **resident),
            pl.BlockSpec((tm, HEAD_DIM), lambda i: (i % seq_tiles, 0)),
            pl.BlockSpec((tm, HEAD_DIM), lambda i: (i % seq_tiles, 0)),
        ],
        out_specs=[
            pl.BlockSpec((tm, PROJ_COLS), lambda i: (i, 0)),
            pl.BlockSpec((2, N_NSA_KV, tm // HALF_BLOCK, half_cols), lambda i: (0, 0, i, 0)),
        ],
        out_shape=[
            jax.ShapeDtypeStruct((t, PROJ_COLS), BF16),
            jax.ShapeDtypeStruct((2, N_NSA_KV, t // HALF_BLOCK, half_cols), BF16),
        ],
        scratch_shapes=[pltpu.VMEM((2 * N_NSA_KV, tm, HEAD_DIM), F32)],
        compiler_params=_params(("parallel",), est),
        name="in_proj",
    )(x2, g, w_a, w_g, w_b, cos_t, sin_t)


def _compress_kernel(h_ref, w1_ref, pos_ref, w2_ref, o_ref):
    half = CMP_LEN * HEAD_DIM // 2
    h = h_ref[0]
    top = jnp.dot(h, w1_ref[0, :half, :], preferred_element_type=F32)
    bot = jnp.dot(h, w1_ref[0, half:, :], preferred_element_type=F32)
    pos_bias = jnp.dot(pos_ref[0], w1_ref[0], preferred_element_type=F32)[0:1]
    pre = top + pltpu.roll(bot, bot.shape[0] - 1, 0) + pos_bias
    act = jax.nn.gelu(pre)
    o_ref[0] = jnp.dot(act.astype(BF16), w2_ref[0], preferred_element_type=F32).astype(o_ref.dtype)


def _compress(hkv, w1, pos, w2):
    rows = hkv.shape[1]
    kdim = CMP_LEN * HEAD_DIM
    est = 2 * (rows * kdim // 2 * 2 + kdim * CMP_HIDDEN * 2) + 6 * rows * CMP_HIDDEN * 4
    return pl.pallas_call(
        _compress_kernel,
        grid=(2,),
        in_specs=[
            pl.BlockSpec((1, rows, kdim // 2), lambda i: (i, 0, 0)),
            pl.BlockSpec((1, kdim, CMP_HIDDEN), lambda i: (i, 0, 0)),
            pl.BlockSpec((1, 8, kdim), lambda i: (i, 0, 0)),
            pl.BlockSpec((1, CMP_HIDDEN, HEAD_DIM), lambda i: (i, 0, 0)),
        ],
        out_specs=pl.BlockSpec((1, rows, HEAD_DIM), lambda i: (i, 0, 0)),
        out_shape=jax.ShapeDtypeStruct((2, rows, HEAD_DIM), BF16),
        compiler_params=_params(("parallel",), est),
        name="compress",
    )(hkv, w1, pos, w2)


def _online_softmax_step(s, v, m, l, acc):
    m_new = jnp.maximum(m, jnp.max(s, axis=0, keepdims=True))
    alpha = jnp.exp2(m - m_new)
    p = jnp.exp2(s - m_new)
    l_new = alpha * l + jnp.sum(p, axis=0, keepdims=True)
    pv = lax.dot_general(v, p.astype(BF16), _TN, preferred_element_type=F32)
    return m_new, l_new, alpha * acc + pv


def _nsa_kernel(q_ref, kc_ref, vc_ref, ks_ref, vs_ref, kw_ref, vw_ref, gate_ref, ovt_ref,
                o_ref, bias_ref, *, tq):
    kv = pl.program_id(1)
    qi = pl.program_id(2)
    start = qi * tq
    nl = NSA_GROUP * tq

    q4 = q_ref[...]
    qs = jnp.concatenate([q4[:, g * HEAD_DIM:(g + 1) * HEAD_DIM] for g in range(NSA_GROUP)], axis=0)
    q_local = lax.broadcasted_iota(jnp.int32, (1, nl), 1) & (tq - 1)
    t_lane = start + q_local
    k_local = lax.broadcasted_iota(jnp.int32, (tq, nl), 0)
    causal = k_local <= q_local

    s = lax.dot_general(kc_ref[0], qs, _NT, preferred_element_type=F32)
    c_end = lax.broadcasted_iota(jnp.int32, (N_CMP_PAD, nl), 0) * CMP_STRIDE + (CMP_LEN - 1)
    cmask = c_end <= t_lane
    s = jnp.where(cmask, s, NEG)
    m = jnp.max(s, axis=0, keepdims=True)
    e = jnp.where(cmask, jnp.exp2(s - m), 0.0)
    l = jnp.sum(e, axis=0, keepdims=True)
    p_cmp = e * jnp.where(l > 0.0, 1.0 / l, 0.0)
    o_cmp = lax.dot_general(vc_ref[0], p_cmp.astype(BF16), _TN, preferred_element_type=F32)

    p_sum = p_cmp[:, 0:tq]
    for g in range(1, NSA_GROUP):
        p_sum = p_sum + p_cmp[:, g * tq:(g + 1) * tq]
    p_hi = p_sum.astype(BF16)
    p_lo = (p_sum - p_hi.astype(F32)).astype(BF16)
    ovt = ovt_ref[...]
    p_slc = (jnp.dot(ovt, p_hi, preferred_element_type=F32)
             + jnp.dot(ovt, p_lo, preferred_element_type=F32))

    j_idx = lax.broadcasted_iota(jnp.int32, (N_SEL, tq), 0)
    t_q = start + lax.broadcasted_iota(jnp.int32, (N_SEL, tq), 1)
    cur = t_q // SLC_LEN
    forced = (j_idx == 0) | (j_idx == cur) | (j_idx == cur - 1)
    score = jnp.where(forced, 1e4, jnp.where(j_idx > cur, -1e4, p_slc))
    rank = jnp.zeros((N_SEL, tq), jnp.int32)
    for i in range(N_SEL):
        row = score[i:i + 1, :]
        tie = jnp.where(j_idx > i, 1, 0)
        rank = rank + jnp.where(row > score, 1, jnp.where(row == score, tie, 0))
    bias = jnp.where((rank < SLC_TOP) & (j_idx <= cur), 0.0, NEG)
    bias_ref[...] = jnp.concatenate([bias] * NSA_GROUP, axis=1)
    blocks_per_tile = tq // SLC_LEN

    def slc_scores(kt):
        k0 = pl.multiple_of(kt * tq, tq)
        s = lax.dot_general(ks_ref[pl.ds(k0, tq), :], qs, _NT, preferred_element_type=F32)
        parts = [s[c * SLC_LEN:(c + 1) * SLC_LEN, :] + bias_ref[pl.ds(kt * blocks_per_tile + c, 1), :]
                 for c in range(blocks_per_tile)]
        return jnp.concatenate(parts, axis=0), vs_ref[pl.ds(k0, tq), :]

    def slc_body(kt, carry):
        s, v = slc_scores(kt)
        return _online_softmax_step(s, v, *carry)

    m0 = jnp.full((1, nl), NEG, F32)
    l0 = jnp.zeros((1, nl), F32)
    acc0 = jnp.zeros((HEAD_DIM, nl), F32)
    carry = lax.fori_loop(0, qi, slc_body, (m0, l0, acc0))
    s, v = slc_scores(qi)
    _, l_s, acc_s = _online_softmax_step(jnp.where(causal, s, NEG), v, *carry)
    o_slc = acc_s * (1.0 / l_s)

    scores, values = [], []
    for back in range(WINDOW // tq, -1, -1):
        k0 = start - back * tq
        inside = k0 >= 0
        k0 = pl.multiple_of(jnp.maximum(k0, 0), tq)
        s = lax.dot_general(kw_ref[pl.ds(k0, tq), :], qs, _NT, preferred_element_type=F32)
        if back == 0:
            s = jnp.where(causal, s, NEG)
        elif back == WINDOW // tq:
            edge = q_local + jnp.where(inside, 0, tq)
            s = jnp.where(k_local > edge, s, NEG)
        else:
            s = s + jnp.where(inside, 0.0, NEG)
        scores.append(s)
        values.append(vw_ref[pl.ds(k0, tq), :])
    s = jnp.concatenate(scores, axis=0)
    p = jnp.exp2(s - jnp.max(s, axis=0, keepdims=True))
    l_w = jnp.sum(p, axis=0, keepdims=True)
    acc_w = lax.dot_general(jnp.concatenate(values, axis=0), p.astype(BF16), _TN, preferred_element_type=F32)
    o_win = acc_w * (1.0 / l_w)

    gates_t = jax.nn.sigmoid(gate_ref[...].astype(F32)).T
    gts = jnp.where(kv == 0, gates_t[0:GATE_STRIDE], gates_t[GATE_STRIDE:2 * GATE_STRIDE])
    for g in range(NSA_GROUP):
        sl = slice(g * tq, (g + 1) * tq)
        o_t = (gts[g:g + 1, :] * o_cmp[:, sl]
               + gts[NSA_GROUP + g:NSA_GROUP + g + 1, :] * o_slc[:, sl]
               + gts[2 * NSA_GROUP + g:2 * NSA_GROUP + g + 1, :] * o_win[:, sl])
        o_ref[:, g * HEAD_DIM:(g + 1) * HEAD_DIM] = o_t.T


def _nsa_attention(proj, cmp_kv, ovt, batch, *, tq=256):
    t = proj.shape[0]
    assert WINDOW % tq == 0 and tq % SLC_LEN == 0 and SEQ % tq == 0
    nq = SEQ // tq
    rows_per_kv = N_CMP_PAD
    blk = HEAD_DIM
    est = (4 * 2 * SEQ * HEAD_DIM * 2 + 2 * tq * NSA_GROUP * HEAD_DIM * (2 + 4)
           + 10 * (WINDOW + tq) * NSA_GROUP * tq * 4)

    def kv_spec(col):
        return pl.BlockSpec((SEQ, HEAD_DIM), lambda b, h, i: (b, col // blk + h))

    def cmp_spec(which):
        return pl.BlockSpec((1, rows_per_kv, HEAD_DIM), lambda b, h, i: (which, h * batch + b, 0))

    return pl.pallas_call(
        functools.partial(_nsa_kernel, tq=tq),
        grid=(batch, N_NSA_KV, nq),
        in_specs=[
            pl.BlockSpec((tq, NSA_GROUP * HEAD_DIM), lambda b, h, i: (b * nq + i, h)),
            cmp_spec(0), cmp_spec(1),
            kv_spec(COL_KS), kv_spec(COL_VS), kv_spec(COL_KW), kv_spec(COL_VW),
            pl.BlockSpec((tq, HEAD_DIM), lambda b, h, i: (b * nq + i, COL_GATE // blk)),
            pl.BlockSpec((N_SEL, N_CMP_PAD), lambda b, h, i: (0, 0)),
        ],
        out_specs=pl.BlockSpec((tq, NSA_GROUP * HEAD_DIM), lambda b, h, i: (b * nq + i, h)),
        out_shape=jax.ShapeDtypeStruct((t, NSA_WIDTH), F32),
        scratch_shapes=[pltpu.VMEM((N_SEL, NSA_GROUP * tq), F32)],
        compiler_params=_params(("parallel", "parallel", "arbitrary"), est),
        name="nsa_attn",
    )(proj, cmp_kv, cmp_kv, proj, proj, proj, proj, proj, ovt)


def _diff_kernel(q_ref, k_ref, v_ref, lq1_ref, lk1_ref, lq2_ref, lk2_ref, sub_ref, o_ref,
                 *, tq, lambda_init):
    qi = pl.program_id(2)
    q = q_ref[...]
    q1 = q[:, :HEAD_DIM]
    q2 = q[:, HEAD_DIM:]
    lam = (jnp.exp(jnp.sum(lq1_ref[...] * lk1_ref[...], axis=-1, keepdims=True))
           - jnp.exp(jnp.sum(lq2_ref[...] * lk2_ref[...], axis=-1, keepdims=True)) + lambda_init)

    def scores(kt):
        k0 = pl.multiple_of(kt * tq, tq)
        kk = k_ref[pl.ds(k0, tq), :]
        s1 = lax.dot_general(kk[:, :HEAD_DIM], q1, _NT, preferred_element_type=F32)
        s2 = lax.dot_general(kk[:, HEAD_DIM:], q2, _NT, preferred_element_type=F32)
        return jnp.concatenate([s1, s2], axis=1), v_ref[pl.ds(k0, tq), :]

    def body(kt, carry):
        s, v = scores(kt)
        return _online_softmax_step(s, v, *carry)

    m0 = jnp.full((1, 2 * tq), NEG, F32)
    l0 = jnp.zeros((1, 2 * tq), F32)
    acc0 = jnp.zeros((2 * HEAD_DIM, 2 * tq), F32)
    carry = lax.fori_loop(0, qi, body, (m0, l0, acc0))
    s, v = scores(qi)
    k_local = lax.broadcasted_iota(jnp.int32, (tq, 2 * tq), 0)
    q_local = lax.broadcasted_iota(jnp.int32, (1, 2 * tq), 1) & (tq - 1)
    _, l, acc = _online_softmax_step(jnp.where(k_local <= q_local, s, NEG), v, *carry)
    o_n = acc * (1.0 / l)
    o_t = o_n[:, :tq] - lam * o_n[:, tq:]
    o = _rmsnorm(o_t.T, sub_ref[...]) * (1.0 - lambda_init)
    o_ref[...] = o.astype(o_ref.dtype)


def _diff_attention(proj, lq1, lk1, lq2, lk2, subln, batch, lambda_init, *, tq=512):
    t = proj.shape[0]
    nq = SEQ // tq
    wide = 2 * HEAD_DIM
    est = 2 * 2 * SEQ * wide * 2 + 4 * tq * wide * 4 + 16 * tq * 2 * tq * 4
    vec = pl.BlockSpec((1, HEAD_DIM), lambda b, h, i: (0, 0))
    return pl.pallas_call(
        functools.partial(_diff_kernel, tq=tq, lambda_init=lambda_init),
        grid=(batch, N_DIFF_HEADS, nq),
        in_specs=[
            pl.BlockSpec((tq, wide), lambda b, h, i: (b * nq + i, COL_DQ // wide + h)),
            pl.BlockSpec((SEQ, wide), lambda b, h, i: (b, COL_DK // wide + h)),
            pl.BlockSpec((SEQ, wide), lambda b, h, i: (b, COL_DV // wide + h)),
            vec, vec, vec, vec,
            pl.BlockSpec((1, wide), lambda b, h, i: (0, 0)),
        ],
        out_specs=pl.BlockSpec((tq, wide), lambda b, h, i: (b * nq + i, h)),
        out_shape=jax.ShapeDtypeStruct((t, DIFF_WIDTH), BF16),
        compiler_params=_params(("parallel", "parallel", "arbitrary"), est),
        name="diff_attn",
    )(proj, proj, proj, lq1, lk1, lq2, lk2, subln)


def _oproj_kernel(yn_ref, g_ref, yd_ref, wo_ref, x_ref, o_ref):
    yn = _rmsnorm(yn_ref[...], g_ref[...]).astype(BF16)
    acc = jnp.dot(yn, wo_ref[:NSA_WIDTH, :], preferred_element_type=F32)
    acc = acc + jnp.dot(yd_ref[...], wo_ref[NSA_WIDTH:, :], preferred_element_type=F32)
    o_ref[...] = x_ref[...] + acc


def _out_proj(y_nsa, g, y_diff, w_o, x2, *, tm=512):
    t = x2.shape[0]
    est = (2 * (NSA_WIDTH + DIFF_WIDTH) * D_MODEL * 2 + 2 * tm * NSA_WIDTH * 4 + 2 * tm * DIFF_WIDTH * 2
           + 5 * tm * D_MODEL * 4)
    return pl.pallas_call(
        _oproj_kernel,
        grid=(t // tm,),
        in_specs=[
            pl.BlockSpec((tm, NSA_WIDTH), lambda i: (i, 0)),
            pl.BlockSpec((1, NSA_WIDTH), lambda i: (0, 0)),
            pl.BlockSpec((tm, DIFF_WIDTH), lambda i: (i, 0)),
            pl.BlockSpec((NSA_WIDTH + DIFF_WIDTH, D_MODEL), lambda i: (0, 0)),
            pl.BlockSpec((tm, D_MODEL), lambda i: (i, 0)),
        ],
        out_specs=pl.BlockSpec((tm, D_MODEL), lambda i: (i, 0)),
        out_shape=jax.ShapeDtypeStruct((t, D_MODEL), F32),
        compiler_params=_params(("parallel",), est),
        name="out_proj",
    )(y_nsa, g, y_diff, w_o, x2)


HALO = 8
FFN_CHUNK = 256


def _ffn_kernel(h_ref, halo_ref, g_ref, wu_ref, wg_ref, cwu_ref, cwg_ref, cbu_ref, cbg_ref, wd_ref,
                o_ref, xn_ref, act_ref, acc_ref, *, tm):
    i = pl.program_id(0)
    j = pl.program_id(1)

    @pl.when(j == 0)
    def _():
        keep = jnp.where((i * tm) % SEQ == 0, 0.0, 1.0)
        xn_ref[0:HALO, :] = (_rmsnorm(halo_ref[...], g_ref[...]) * keep).astype(BF16)
        xn_ref[HALO:, :] = _rmsnorm(h_ref[...], g_ref[...]).astype(BF16)
        acc_ref[...] = jnp.zeros_like(acc_ref)

    xn = xn_ref[...]

    def conv(h, cw_ref, cb_ref, cols):
        cw = cw_ref[:, cols]
        h3 = h.reshape((HALO + tm) // SUBLANES, SUBLANES, h.shape[1])
        sub = lax.broadcasted_iota(jnp.int32, h3.shape[1:], 0)
        out = cw[CONV_W - 1:CONV_W, :] * h3[1:] + cb_ref[:, cols]
        for back in range(1, CONV_W):
            rolled = pltpu.roll(h3, back, 1)
            shifted = jnp.where(sub < back, rolled[:-1], rolled[1:])
            out = out + cw[CONV_W - 1 - back:CONV_W - back, :] * shifted
        return out.reshape(tm, h.shape[1])

    for lo in range(0, wu_ref.shape[1], FFN_CHUNK):
        cols = slice(lo, lo + FFN_CHUNK)
        u = conv(jnp.dot(xn, wu_ref[:, cols], preferred_element_type=F32), cwu_ref, cbu_ref, cols)
        gate = conv(jnp.dot(xn, wg_ref[:, cols], preferred_element_type=F32), cwg_ref, cbg_ref, cols)
        act_ref[:, cols] = (jax.nn.silu(gate) * u).astype(BF16)
    acc_ref[...] += jnp.dot(act_ref[...], wd_ref[...], preferred_element_type=F32)

    @pl.when(j == pl.num_programs(1) - 1)
    def _():
        o_ref[...] = h_ref[...] + acc_ref[...]


def _conv_ffn(h1, g, w_up, conv_w, conv_b, w_down, *, tm=512, tf=512):
    t = h1.shape[0]
    assert t % tm == 0 and SEQ % tm == 0 and D_FF % tf == 0
    nf = D_FF // tf
    est = (4 * tm * D_MODEL * 4 + 2 * 3 * D_MODEL * tf * 2 + (tm + HALO) * D_MODEL * 2 + tm * D_MODEL * 4
           + 8 * (tm + HALO) * tf * 4)
    return pl.pallas_call(
        functools.partial(_ffn_kernel, tm=tm),
        grid=(t // tm, nf),
        in_specs=[
            pl.BlockSpec((tm, D_MODEL), lambda i, j: (i, 0)),
            pl.BlockSpec((HALO, D_MODEL), lambda i, j: (jnp.maximum(i * (tm // HALO) - 1, 0), 0)),
            pl.BlockSpec((1, D_MODEL), lambda i, j: (0, 0)),
            pl.BlockSpec((D_MODEL, tf), lambda i, j: (0, j)),
            pl.BlockSpec((D_MODEL, tf), lambda i, j: (0, nf + j)),
            pl.BlockSpec((CONV_W, tf), lambda i, j: (0, j)),
            pl.BlockSpec((CONV_W, tf), lambda i, j: (0, nf + j)),
            pl.BlockSpec((1, tf), lambda i, j: (0, j)),
            pl.BlockSpec((1, tf), lambda i, j: (0, nf + j)),
            pl.BlockSpec((tf, D_MODEL), lambda i, j: (j, 0)),
        ],
        out_specs=pl.BlockSpec((tm, D_MODEL), lambda i, j: (i, 0)),
        out_shape=jax.ShapeDtypeStruct((t, D_MODEL), F32),
        scratch_shapes=[pltpu.VMEM((tm + HALO, D_MODEL), BF16), pltpu.VMEM((tm, tf), BF16),
                        pltpu.VMEM((tm, D_MODEL), F32)],
        compiler_params=_params(("parallel", "arbitrary"), est),
        name="conv_ffn",
    )(h1, h1, g, w_up, w_up, conv_w, conv_w, conv_b, conv_b, w_down)


PLE_CHUNK = 512


def _ple_kernel(h_ref, gp_ref, wg_ref, p_ref, wp_ref, gf_ref, o_ref):
    hn = _rmsnorm(h_ref[...], gp_ref[...]).astype(BF16)
    pb = p_ref[...].astype(BF16)
    ssq = None
    for lo in range(0, D_MODEL, PLE_CHUNK):
        cols = slice(lo, lo + PLE_CHUNK)
        gate = jax.nn.sigmoid(jnp.dot(hn, wg_ref[:, cols], preferred_element_type=F32))
        emb = jnp.dot(pb, wp_ref[:, cols], preferred_element_type=F32)
        h3 = h_ref[:, cols] + gate * emb
        o_ref[:, cols] = h3
        part = jnp.sum(h3 * h3, axis=-1, keepdims=True)
        ssq = part if ssq is None else ssq + part
    o_ref[...] = o_ref[...] * lax.rsqrt(ssq * (1.0 / D_MODEL) + EPS) * gf_ref[...]


def _ple_out(h2, g_ple, w_gate, p2, w_proj, g_final, *, tm=256):
    t = h2.shape[0]
    est = 2 * D_MODEL * D_MODEL * 2 + 2 * PLE_DIM * D_MODEL * 2 + 4 * tm * D_MODEL * 4 + 6 * tm * D_MODEL * 4
    return pl.pallas_call(
        _ple_kernel,
        grid=(t // tm,),
        in_specs=[
            pl.BlockSpec((tm, D_MODEL), lambda i: (i, 0)),
            pl.BlockSpec((1, D_MODEL), lambda i: (0, 0)),
            pl.BlockSpec((D_MODEL, D_MODEL), lambda i: (0, 0)),
            pl.BlockSpec((tm, PLE_DIM), lambda i: (i, 0)),
            pl.BlockSpec((PLE_DIM, D_MODEL), lambda i: (0, 0)),
            pl.BlockSpec((1, D_MODEL), lambda i: (0, 0)),
        ],
        out_specs=pl.BlockSpec((tm, D_MODEL), lambda i: (i, 0)),
        out_shape=jax.ShapeDtypeStruct((t, D_MODEL), F32),
        compiler_params=_params(("parallel",), est),
        name="ple_out",
    )(h2, g_ple, w_gate, p2, w_proj, g_final)


def _split_w_in(w):
    gate_lo = NSA_WIDTH + 6 * NSA_KV_WIDTH
    gate = w[:, gate_lo:gate_lo + N_GATES]
    gate = gate.reshape(D_MODEL, 3, N_NSA_KV, NSA_GROUP).transpose(0, 2, 1, 3).reshape(D_MODEL, N_NSA_KV, 3 * NSA_GROUP)
    gate = jnp.pad(gate, ((0, 0), (0, 0), (0, GATE_STRIDE - 3 * NSA_GROUP))).reshape(D_MODEL, N_NSA_KV * GATE_STRIDE)
    gate = jnp.pad(gate, ((0, 0), (0, HEAD_DIM - N_NSA_KV * GATE_STRIDE)))
    return w[:, :gate_lo].astype(BF16), gate.astype(BF16), w[:, gate_lo + N_GATES:].astype(BF16)


def _rope_tables():
    inv = 1.0 / (ROPE_THETA ** (jnp.arange(0, ROPE_DIM, 2, dtype=F32) / ROPE_DIM))
    ang = jnp.arange(SEQ, dtype=F32)[:, None] * inv[None, :]
    cos, sin = jnp.cos(ang), jnp.sin(ang)
    rest = HEAD_DIM - ROPE_DIM
    cos_t = jnp.concatenate([cos, cos, jnp.ones((SEQ, rest), F32)], axis=1)
    sin_t = jnp.concatenate([-sin, sin, jnp.zeros((SEQ, rest), F32)], axis=1)
    return cos_t, sin_t


def _overlap_t():
    cmp_starts = np.arange(N_CMP) * CMP_STRIDE
    sel_starts = np.arange(N_SEL) * SLC_LEN
    ov = np.clip(np.minimum(cmp_starts[:, None] + CMP_LEN, sel_starts[None, :] + SLC_LEN)
                 - np.maximum(cmp_starts[:, None], sel_starts[None, :]), 0, None).astype(np.float32) / CMP_LEN
    ovt = np.zeros((N_SEL, N_CMP_PAD), np.float32)
    ovt[:, :N_CMP] = ov.T
    return jnp.asarray(ovt, BF16)


def kernel(x, p, attn_norm, w_in, cmp_k_pos, cmp_k_w1, cmp_k_w2, cmp_v_pos, cmp_v_w1, cmp_v_w2, nsa_out_norm, diff_lq1, diff_lk1, diff_lq2, diff_lk2, diff_subln, w_o, ffn_norm, w_up, conv_w, conv_b, w_down, ple_norm, w_ple_gate, w_ple_proj, final_norm):
    batch, seq, _ = x.shape
    assert seq == SEQ and p.shape[0] == 1
    t = batch * seq
    layer = 0
    lambda_init = 0.8 - 0.6 * math.exp(-0.3 * layer)
    x2 = x.reshape(t, D_MODEL)
    cos_t, sin_t = _rope_tables()

    proj, hkv = _in_proj(x2, attn_norm[layer][None], *_split_w_in(w_in[layer]), cos_t, sin_t)

    hkv = hkv.reshape(2, N_NSA_KV * t // HALF_BLOCK, HALF_BLOCK * HEAD_DIM)
    w1 = jnp.stack([cmp_k_w1[layer], cmp_v_w1[layer]]).astype(BF16)
    w2 = jnp.stack([cmp_k_w2[layer], cmp_v_w2[layer]]).astype(BF16)
    pos = jnp.stack([cmp_k_pos[layer], cmp_v_pos[layer]]).reshape(2, 1, CMP_LEN * HEAD_DIM)
    pos = jnp.broadcast_to(pos, (2, 8, CMP_LEN * HEAD_DIM)).astype(BF16)
    cmp_kv = _compress(hkv, w1, pos, w2)

    y_nsa = _nsa_attention(proj, cmp_kv, _overlap_t(), batch)
    y_diff = _diff_attention(proj, diff_lq1[layer][None], diff_lk1[layer][None], diff_lq2[layer][None],
                             diff_lk2[layer][None], diff_subln[layer][None], batch, lambda_init)
    h1 = _out_proj(y_nsa, nsa_out_norm[layer][None], y_diff, w_o[layer].astype(BF16), x2)
    h2 = _conv_ffn(h1, ffn_norm[layer][None], w_up[layer].astype(BF16), conv_w[layer], conv_b[layer][None],
                   w_down[layer].astype(BF16))
    out = _ple_out(h2, ple_norm[layer][None], w_ple_gate[layer].astype(BF16), p[layer].reshape(t, PLE_DIM),
                   w_ple_proj[layer].astype(BF16), final_norm[None])
    return out.reshape(batch, seq, D_MODEL)
```

```python
import functools
import math

import numpy as np
import jax
import jax.numpy as jnp
from jax import lax
from jax.experimental import pallas as pl
from jax.experimental.pallas import tpu as pltpu

D_MODEL = 2048
SEQ = 2048
HEAD_DIM = 128
ROPE_DIM = HEAD_DIM // 4
ROPE_THETA = 500000.0
N_NSA_HEADS = 8
N_NSA_KV = 2
NSA_GROUP = N_NSA_HEADS // N_NSA_KV
CMP_LEN = 32
CMP_STRIDE = 16
CMP_HIDDEN = 256
SLC_LEN = 64
SLC_TOP = 16
WINDOW = 512
N_DIFF_HEADS = 4
D_FF = 5632
CONV_W = 3
PLE_DIM = 256
EPS = 1e-6

NSA_WIDTH = N_NSA_HEADS * HEAD_DIM
NSA_KV_WIDTH = N_NSA_KV * HEAD_DIM
DIFF_WIDTH = N_DIFF_HEADS * 2 * HEAD_DIM
N_GATES = 3 * N_NSA_HEADS
N_CMP = (SEQ - CMP_LEN) // CMP_STRIDE + 1
N_CMP_PAD = SEQ // CMP_STRIDE
N_SEL = SEQ // SLC_LEN
GATE_STRIDE = 16

COL_NQ = 0
COL_KS = COL_NQ + NSA_WIDTH
COL_KW = COL_KS + NSA_KV_WIDTH
COL_DQ = COL_KW + NSA_KV_WIDTH
COL_DK = COL_DQ + DIFF_WIDTH
COL_VS = COL_DK + DIFF_WIDTH
COL_VW = COL_VS + NSA_KV_WIDTH
COL_DV = COL_VW + NSA_KV_WIDTH
COL_GATE = COL_DV + DIFF_WIDTH
PROJ_COLS = COL_GATE + HEAD_DIM

V7X_LANES = 128
SUBLANES = 8
V7X_VMEM_REQUEST_CAP = 56 * 1024 * 1024
NEG = -1e30
QSCALE = HEAD_DIM ** -0.5 * math.log2(math.e)

_NT = (((1,), (1,)), ((), ()))
_TN = (((0,), (0,)), ((), ()))
BF16 = jnp.bfloat16
F32 = jnp.float32


def _params(semantics, vmem_estimate_bytes, flags=None):
    limit = min(max(int(vmem_estimate_bytes), 32 * 1024 * 1024), V7X_VMEM_REQUEST_CAP)
    return pltpu.CompilerParams(dimension_semantics=semantics, vmem_limit_bytes=limit, flags=flags)


def _rmsnorm(x, g):
    return x * lax.rsqrt(jnp.mean(x * x, axis=-1, keepdims=True) + EPS) * g


INPROJ_CHUNK = 1024
HALF_BLOCK = CMP_STRIDE

_HEADS_A = ([(COL_NQ + h * HEAD_DIM, True, True) for h in range(N_NSA_HEADS)]
            + [(("cmp", 0, h), True, False) for h in range(N_NSA_KV)]
            + [(("cmp", 1, h), False, False) for h in range(N_NSA_KV)]
            + [(COL_KS + h * HEAD_DIM, True, False) for h in range(N_NSA_KV)]
            + [(COL_VS + h * HEAD_DIM, False, False) for h in range(N_NSA_KV)]
            + [(COL_KW + h * HEAD_DIM, True, False) for h in range(N_NSA_KV)]
            + [(COL_VW + h * HEAD_DIM, False, False) for h in range(N_NSA_KV)])
_HEADS_B = ([(COL_DQ + h * HEAD_DIM, True, True) for h in range(2 * N_DIFF_HEADS)]
            + [(COL_DK + h * HEAD_DIM, True, False) for h in range(2 * N_DIFF_HEADS)]
            + [(COL_DV + h * HEAD_DIM, False, False) for h in range(2 * N_DIFF_HEADS)])
_HEADS_GATE = [(COL_GATE, False, False)]


def _inproj_kernel(x_ref, g_ref, wa_ref, wg_ref, wb_ref, cos_ref, sin_ref, o_ref, hkv_ref, stage_ref):
    tm = x_ref.shape[0]
    xn = _rmsnorm(x_ref[...], g_ref[...]).astype(BF16)
    c = cos_ref[...]
    s = sin_ref[...]
    cq = c * QSCALE
    sq = s * QSCALE
    first_half = lax.broadcasted_iota(jnp.int32, c.shape, 1) < ROPE_DIM // 2
    n_staged = 0
    for w_ref, heads in ((wa_ref, _HEADS_A), (wg_ref, _HEADS_GATE), (wb_ref, _HEADS_B)):
        for lo in range(0, w_ref.shape[1], INPROJ_CHUNK):
            hi = min(lo + INPROJ_CHUNK, w_ref.shape[1])
            acc = jnp.dot(xn, w_ref[:, lo:hi], preferred_element_type=F32)
            for src in range(lo, hi, HEAD_DIM):
                dest, rope, is_query = heads[src // HEAD_DIM]
                a = acc[:, src - lo:src - lo + HEAD_DIM]
                if rope:
                    partner = jnp.where(first_half,
                                        pltpu.roll(a, HEAD_DIM - ROPE_DIM // 2, 1),
                                        pltpu.roll(a, ROPE_DIM // 2, 1))
                    a = a * cq + partner * sq if is_query else a * c + partner * s
                if isinstance(dest, tuple):
                    _, which, head = dest
                    stage = stage_ref.at[n_staged]
                    n_staged += 1
                    stage[...] = a
                    for r in range(HALF_BLOCK):
                        rows = stage[pl.ds(r, tm // HALF_BLOCK, stride=HALF_BLOCK), :]
                        hkv_ref[which, head, :, r * HEAD_DIM:(r + 1) * HEAD_DIM] = rows.astype(hkv_ref.dtype)
                else:
                    o_ref[:, dest:dest + HEAD_DIM] = a.astype(o_ref.dtype)


def _in_proj(x2, g, w_a, w_g, w_b, cos_t, sin_t, *, tm=256):
    t = x2.shape[0]
    assert t % tm == 0 and SEQ % tm == 0 and tm % (HALF_BLOCK * 16) == 0
    seq_tiles = SEQ // tm
    n_w = w_a.shape[1] + w_g.shape[1] + w_b.shape[1]
    est = (D_MODEL * n_w * 2 + 2 * tm * D_MODEL * 4 + 2 * tm * n_w * 2 + tm * D_MODEL * 2
           + 3 * tm * INPROJ_CHUNK * 4)
    resident = dict(pipeline_mode=pl.Buffered(1))
    half_cols = HALF_BLOCK * HEAD_DIM
    return pl.pallas_call(
        _inproj_kernel,
        grid=(t // tm,),
        in_specs=[
            pl.BlockSpec((tm, D_MODEL), lambda i: (i, 0)),
            pl.BlockSpec((1, D_MODEL), lambda i: (0, 0)),
            pl.BlockSpec(w_a.shape, lambda i: (0, 0), **resident),
            pl.BlockSpec(w_g.shape, lambda i: (0, 0), **resident),
            pl.BlockSpec(w_b.shape, lambda i: (0, 0), **resident),
            pl.BlockSpec((tm, HEAD_DIM), lambda i: (i % seq_tiles, 0)),
            pl.BlockSpec((tm, HEAD_DIM), lambda i: (i % seq_tiles, 0)),
        ],
        out_specs=[
            pl.BlockSpec((tm, PROJ_COLS), lambda i: (i, 0)),
            pl.BlockSpec((2, N_NSA_KV, tm // HALF_BLOCK, half_cols), lambda i: (0, 0, i, 0)),
        ],
        out_shape=[
            jax.ShapeDtypeStruct((t, PROJ_COLS), BF16),
            jax.ShapeDtypeStruct((2, N_NSA_KV, t // HALF_BLOCK, half_cols), BF16),
        ],
        scratch_shapes=[pltpu.VMEM((2 * N_NSA_KV, tm, HEAD_DIM), F32)],
        compiler_params=_params(("parallel",), est),
        name="in_proj",
    )(x2, g, w_a, w_g, w_b, cos_t, sin_t)


def _compress_kernel(h_ref, w1_ref, pos_ref, w2_ref, o_ref):
    half = CMP_LEN * HEAD_DIM // 2
    h = h_ref[0]
    top = jnp.dot(h, w1_ref[0, :half, :], preferred_element_type=F32)
    bot = jnp.dot(h, w1_ref[0, half:, :], preferred_element_type=F32)
    pos_bias = jnp.dot(pos_ref[0], w1_ref[0], preferred_element_type=F32)[0:1]
    pre = top + pltpu.roll(bot, bot.shape[0] - 1, 0) + pos_bias
    act = jax.nn.gelu(pre)
    o_ref[0] = jnp.dot(act.astype(BF16), w2_ref[0], preferred_element_type=F32).astype(o_ref.dtype)


def _compress(hkv, w1, pos, w2):
    rows = hkv.shape[1]
    kdim = CMP_LEN * HEAD_DIM
    est = 2 * (rows * kdim // 2 * 2 + kdim * CMP_HIDDEN * 2) + 6 * rows * CMP_HIDDEN * 4
    return pl.pallas_call(
        _compress_kernel,
        grid=(2,),
        in_specs=[
            pl.BlockSpec((1, rows, kdim // 2), lambda i: (i, 0, 0)),
            pl.BlockSpec((1, kdim, CMP_HIDDEN), lambda i: (i, 0, 0)),
            pl.BlockSpec((1, 8, kdim), lambda i: (i, 0, 0)),
            pl.BlockSpec((1, CMP_HIDDEN, HEAD_DIM), lambda i: (i, 0, 0)),
        ],
        out_specs=pl.BlockSpec((1, rows, HEAD_DIM), lambda i: (i, 0, 0)),
        out_shape=jax.ShapeDtypeStruct((2, rows, HEAD_DIM), BF16),
        compiler_params=_params(("parallel",), est),
        name="compress",
    )(hkv, w1, pos, w2)


def _causal_flash(chains, n_before, diag_mask, d, lanes):
    def step(score_fn, value_fn, kt, carry, mask=None):
        m, l, acc = carry
        s = score_fn(kt)
        if mask is not None:
            s = jnp.where(mask, s, NEG)
        m_new = jnp.maximum(m, jnp.max(s, axis=0, keepdims=True))
        alpha = jnp.exp2(m - m_new)
        p = jnp.exp2(s - m_new)
        pv = lax.dot_general(value_fn(kt), p.astype(BF16), _TN, preferred_element_type=F32)
        return m_new, alpha * l + jnp.sum(p, axis=0, keepdims=True), alpha * acc + pv

    def body(kt, carries):
        return tuple(step(sf, vf, kt, c) for (sf, vf), c in zip(chains, carries))

    init = (jnp.full((1, lanes), NEG, F32), jnp.zeros((1, lanes), F32), jnp.zeros((d, lanes), F32))
    carries = lax.fori_loop(0, n_before, body, (init,) * len(chains))
    outs = []
    for (sf, vf), c in zip(chains, carries):
        _, l, acc = step(sf, vf, n_before, c, diag_mask)
        outs.append(acc * (1.0 / l))
    return outs


def _nsa_kernel(q_ref, kc0_ref, vc0_ref, kc1_ref, vc1_ref, ks_ref, vs_ref, kw_ref, vw_ref, gate_ref, ovt_ref,
                o_ref, bias_ref, *, tq):
    qi = pl.program_id(1)
    start = qi * tq
    nl = NSA_GROUP * tq
    kvs = range(N_NSA_KV)
    cmp_refs = ((kc0_ref, vc0_ref), (kc1_ref, vc1_ref))

    def head_cols(kv):
        return slice(kv * HEAD_DIM, (kv + 1) * HEAD_DIM)

    q_all = q_ref[...]
    qs = [jnp.concatenate([q_all[:, (kv * NSA_GROUP + g) * HEAD_DIM:(kv * NSA_GROUP + g + 1) * HEAD_DIM]
                           for g in range(NSA_GROUP)], axis=0) for kv in kvs]
    q_local = lax.broadcasted_iota(jnp.int32, (1, nl), 1) & (tq - 1)
    t_lane = start + q_local
    k_local = lax.broadcasted_iota(jnp.int32, (tq, nl), 0)
    causal = k_local <= q_local
    c_end = lax.broadcasted_iota(jnp.int32, (N_CMP_PAD, nl), 0) * CMP_STRIDE + (CMP_LEN - 1)
    cmask = c_end <= t_lane
    j_idx = lax.broadcasted_iota(jnp.int32, (N_SEL, tq), 0)
    t_q = start + lax.broadcasted_iota(jnp.int32, (N_SEL, tq), 1)
    cur = t_q // SLC_LEN
    forced = (j_idx == 0) | (j_idx == cur) | (j_idx == cur - 1)
    ovt = ovt_ref[...]

    o_cmp = []
    for kv in kvs:
        kc_ref, vc_ref = cmp_refs[kv]
        s = lax.dot_general(kc_ref[0], qs[kv], _NT, preferred_element_type=F32)
        s = jnp.where(cmask, s, NEG)
        m = jnp.max(s, axis=0, keepdims=True)
        e = jnp.where(cmask, jnp.exp2(s - m), 0.0)
        l = jnp.sum(e, axis=0, keepdims=True)
        p_cmp = e * jnp.where(l > 0.0, 1.0 / l, 0.0)
        o_cmp.append(lax.dot_general(vc_ref[0], p_cmp.astype(BF16), _TN, preferred_element_type=F32))

        p_sum = p_cmp[:, 0:tq]
        for g in range(1, NSA_GROUP):
            p_sum = p_sum + p_cmp[:, g * tq:(g + 1) * tq]
        p_hi = p_sum.astype(BF16)
        p_lo = (p_sum - p_hi.astype(F32)).astype(BF16)
        p_slc = (jnp.dot(ovt, p_hi, preferred_element_type=F32)
                 + jnp.dot(ovt, p_lo, preferred_element_type=F32))
        score = jnp.where(forced, 1e4, jnp.where(j_idx > cur, -1e4, p_slc))
        rank = jnp.zeros((N_SEL, tq), jnp.int32)
        for i in range(N_SEL):
            row = score[i:i + 1, :]
            tie = jnp.where(j_idx > i, 1, 0)
            rank = rank + jnp.where(row > score, 1, jnp.where(row == score, tie, 0))
        bias = jnp.where((rank < SLC_TOP) & (j_idx <= cur), 0.0, NEG)
        bias_ref[kv] = jnp.concatenate([bias] * NSA_GROUP, axis=1)

    blocks_per_tile = tq // SLC_LEN

    def slc_chain(kv):
        def scores(kt):
            k0 = pl.multiple_of(kt * tq, tq)
            s = lax.dot_general(ks_ref[pl.ds(k0, tq), head_cols(kv)], qs[kv], _NT, preferred_element_type=F32)
            parts = [s[c * SLC_LEN:(c + 1) * SLC_LEN, :] + bias_ref[kv, pl.ds(kt * blocks_per_tile + c, 1), :]
                     for c in range(blocks_per_tile)]
            return jnp.concatenate(parts, axis=0)

        def values(kt):
            return vs_ref[pl.ds(pl.multiple_of(kt * tq, tq), tq), head_cols(kv)]

        return scores, values

    o_slc = _causal_flash([slc_chain(kv) for kv in kvs], qi, causal, HEAD_DIM, nl)

    gates_t = jax.nn.sigmoid(gate_ref[...].astype(F32)).T
    for kv in kvs:
        scores, values = [], []
        for back in range(WINDOW // tq, -1, -1):
            k0 = start - back * tq
            inside = k0 >= 0
            k0 = pl.multiple_of(jnp.maximum(k0, 0), tq)
            s = lax.dot_general(kw_ref[pl.ds(k0, tq), head_cols(kv)], qs[kv], _NT, preferred_element_type=F32)
            if back == 0:
                s = jnp.where(causal, s, NEG)
            elif back == WINDOW // tq:
                edge = q_local + jnp.where(inside, 0, tq)
                s = jnp.where(k_local > edge, s, NEG)
            else:
                s = s + jnp.where(inside, 0.0, NEG)
            scores.append(s)
            values.append(vw_ref[pl.ds(k0, tq), head_cols(kv)])
        s = jnp.concatenate(scores, axis=0)
        p = jnp.exp2(s - jnp.max(s, axis=0, keepdims=True))
        l_w = jnp.sum(p, axis=0, keepdims=True)
        acc_w = lax.dot_general(jnp.concatenate(values, axis=0), p.astype(BF16), _TN, preferred_element_type=F32)
        o_win = acc_w * (1.0 / l_w)

        gts = gates_t[kv * GATE_STRIDE:(kv + 1) * GATE_STRIDE]
        for g in range(NSA_GROUP):
            sl = slice(g * tq, (g + 1) * tq)
            o_t = (gts[g:g + 1, :] * o_cmp[kv][:, sl]
                   + gts[NSA_GROUP + g:NSA_GROUP + g + 1, :] * o_slc[kv][:, sl]
                   + gts[2 * NSA_GROUP + g:2 * NSA_GROUP + g + 1, :] * o_win[:, sl])
            head = kv * NSA_GROUP + g
            o_ref[:, head * HEAD_DIM:(head + 1) * HEAD_DIM] = o_t.T


def _nsa_attention(proj, cmp_kv, ovt, batch, *, tq=256):
    t = proj.shape[0]
    assert WINDOW % tq == 0 and tq % SLC_LEN == 0 and SEQ % tq == 0
    nq = SEQ // tq
    est = (4 * 2 * SEQ * NSA_KV_WIDTH * 2 + 2 * tq * NSA_WIDTH * (2 + 4)
           + N_NSA_KV * 10 * (WINDOW + tq) * NSA_GROUP * tq * 4)

    def kv_spec(col):
        return pl.BlockSpec((SEQ, NSA_KV_WIDTH), lambda b, i: (b, col // NSA_KV_WIDTH))

    def cmp_spec(which, kv):
        return pl.BlockSpec((1, N_CMP_PAD, HEAD_DIM), lambda b, i: (which, kv * batch + b, 0))

    return pl.pallas_call(
        functools.partial(_nsa_kernel, tq=tq),
        grid=(batch, nq),
        in_specs=[
            pl.BlockSpec((tq, NSA_WIDTH), lambda b, i: (b * nq + i, 0)),
            cmp_spec(0, 0), cmp_spec(1, 0), cmp_spec(0, 1), cmp_spec(1, 1),
            kv_spec(COL_KS), kv_spec(COL_VS), kv_spec(COL_KW), kv_spec(COL_VW),
            pl.BlockSpec((tq, HEAD_DIM), lambda b, i: (b * nq + i, COL_GATE // HEAD_DIM)),
            pl.BlockSpec((N_SEL, N_CMP_PAD), lambda b, i: (0, 0)),
        ],
        out_specs=pl.BlockSpec((tq, NSA_WIDTH), lambda b, i: (b * nq + i, 0)),
        out_shape=jax.ShapeDtypeStruct((t, NSA_WIDTH), F32),
        scratch_shapes=[pltpu.VMEM((N_NSA_KV, N_SEL, NSA_GROUP * tq), F32)],
        compiler_params=_params(("parallel", "arbitrary"), est),
        name="nsa_attn",
    )(proj, cmp_kv, cmp_kv, cmp_kv, cmp_kv, proj, proj, proj, proj, proj, ovt)


def _diff_kernel(q_ref, k_ref, v_ref, lq1_ref, lk1_ref, lq2_ref, lk2_ref, sub_ref, o_ref,
                 *, tq, lambda_init):
    qi = pl.program_id(2)
    wide = 2 * HEAD_DIM
    lam = (jnp.exp(jnp.sum(lq1_ref[...] * lk1_ref[...], axis=-1, keepdims=True))
           - jnp.exp(jnp.sum(lq2_ref[...] * lk2_ref[...], axis=-1, keepdims=True)) + lambda_init)
    q = q_ref[...]

    def chain(h):
        q1 = q[:, h * wide:h * wide + HEAD_DIM]
        q2 = q[:, h * wide + HEAD_DIM:(h + 1) * wide]

        def scores(kt):
            kk = k_ref[pl.ds(pl.multiple_of(kt * tq, tq), tq), h * wide:(h + 1) * wide]
            s1 = lax.dot_general(kk[:, :HEAD_DIM], q1, _NT, preferred_element_type=F32)
            s2 = lax.dot_general(kk[:, HEAD_DIM:], q2, _NT, preferred_element_type=F32)
            return jnp.concatenate([s1, s2], axis=1)

        def values(kt):
            return v_ref[pl.ds(pl.multiple_of(kt * tq, tq), tq), h * wide:(h + 1) * wide]

        return scores, values

    k_local = lax.broadcasted_iota(jnp.int32, (tq, 2 * tq), 0)
    q_local = lax.broadcasted_iota(jnp.int32, (1, 2 * tq), 1) & (tq - 1)
    outs = _causal_flash([chain(h) for h in range(DIFF_HEADS_PER_STEP)], qi, k_local <= q_local, wide, 2 * tq)
    for h, o_n in enumerate(outs):
        o_t = o_n[:, :tq] - lam * o_n[:, tq:]
        o = _rmsnorm(o_t.T, sub_ref[...]) * (1.0 - lambda_init)
        o_ref[:, h * wide:(h + 1) * wide] = o.astype(o_ref.dtype)


DIFF_HEADS_PER_STEP = 2


def _diff_attention(proj, lq1, lk1, lq2, lk2, subln, batch, lambda_init, *, tq=512):
    t = proj.shape[0]
    nq = SEQ // tq
    wide = 2 * HEAD_DIM
    step_cols = DIFF_HEADS_PER_STEP * wide
    est = 2 * 2 * SEQ * step_cols * 2 + 4 * tq * step_cols * 4 + DIFF_HEADS_PER_STEP * 16 * tq * 2 * tq * 4
    vec = pl.BlockSpec((1, HEAD_DIM), lambda b, h, i: (0, 0))
    return pl.pallas_call(
        functools.partial(_diff_kernel, tq=tq, lambda_init=lambda_init),
        grid=(batch, N_DIFF_HEADS // DIFF_HEADS_PER_STEP, nq),
        in_specs=[
            pl.BlockSpec((tq, step_cols), lambda b, h, i: (b * nq + i, COL_DQ // step_cols + h)),
            pl.BlockSpec((SEQ, step_cols), lambda b, h, i: (b, COL_DK // step_cols + h)),
            pl.BlockSpec((SEQ, step_cols), lambda b, h, i: (b, COL_DV // step_cols + h)),
            vec, vec, vec, vec,
            pl.BlockSpec((1, wide), lambda b, h, i: (0, 0)),
        ],
        out_specs=pl.BlockSpec((tq, step_cols), lambda b, h, i: (b * nq + i, h)),
        out_shape=jax.ShapeDtypeStruct((t, DIFF_WIDTH), BF16),
        compiler_params=_params(("parallel", "parallel", "arbitrary"), est),
        name="diff_attn",
    )(proj, proj, proj, lq1, lk1, lq2, lk2, subln)


def _oproj_kernel(yn_ref, g_ref, yd_ref, wo_ref, x_ref, o_ref):
    yn = _rmsnorm(yn_ref[...], g_ref[...]).astype(BF16)
    acc = jnp.dot(yn, wo_ref[:NSA_WIDTH, :], preferred_element_type=F32)
    acc = acc + jnp.dot(yd_ref[...], wo_ref[NSA_WIDTH:, :], preferred_element_type=F32)
    o_ref[...] = x_ref[...] + acc


def _out_proj(y_nsa, g, y_diff, w_o, x2, *, tm=512):
    t = x2.shape[0]
    est = (2 * (NSA_WIDTH + DIFF_WIDTH) * D_MODEL * 2 + 2 * tm * NSA_WIDTH * 4 + 2 * tm * DIFF_WIDTH * 2
           + 5 * tm * D_MODEL * 4)
    return pl.pallas_call(
        _oproj_kernel,
        grid=(t // tm,),
        in_specs=[
            pl.BlockSpec((tm, NSA_WIDTH), lambda i: (i, 0)),
            pl.BlockSpec((1, NSA_WIDTH), lambda i: (0, 0)),
            pl.BlockSpec((tm, DIFF_WIDTH), lambda i: (i, 0)),
            pl.BlockSpec((NSA_WIDTH + DIFF_WIDTH, D_MODEL), lambda i: (0, 0)),
            pl.BlockSpec((tm, D_MODEL), lambda i: (i, 0)),
        ],
        out_specs=pl.BlockSpec((tm, D_MODEL), lambda i: (i, 0)),
        out_shape=jax.ShapeDtypeStruct((t, D_MODEL), F32),
        compiler_params=_params(("parallel",), est),
        name="out_proj",
    )(y_nsa, g, y_diff, w_o, x2)


HALO = 8
FFN_CHUNK = 256


def _ffn_kernel(h_ref, halo_ref, g_ref, wu_ref, wg_ref, cwu_ref, cwg_ref, cbu_ref, cbg_ref, wd_ref,
                o_ref, xn_ref, raw_ref, act_ref, acc_ref, *, tm):
    i = pl.program_id(0)
    j = pl.program_id(1)

    @pl.when(j == 0)
    def _():
        keep = jnp.where((i * tm) % SEQ == 0, 0.0, 1.0)
        xn_ref[0:HALO, :] = (_rmsnorm(halo_ref[...], g_ref[...]) * keep).astype(BF16)
        xn_ref[HALO:, :] = _rmsnorm(h_ref[...], g_ref[...]).astype(BF16)
        acc_ref[...] = jnp.zeros_like(acc_ref)

    xn = xn_ref[...]

    def conv(h, cw_ref, cb_ref, cols):
        cw = cw_ref[:, cols]
        h3 = h.reshape((HALO + tm) // SUBLANES, SUBLANES, h.shape[1])
        sub = lax.broadcasted_iota(jnp.int32, h3.shape[1:], 0)
        out = cw[CONV_W - 1:CONV_W, :] * h3[1:] + cb_ref[:, cols]
        for back in range(1, CONV_W):
            rolled = pltpu.roll(h3, back, 1)
            shifted = jnp.where(sub < back, rolled[:-1], rolled[1:])
            out = out + cw[CONV_W - 1 - back:CONV_W - back, :] * shifted
        return out.reshape(tm, h.shape[1])

    slot = j % 2
    for lo in range(0, wu_ref.shape[1], FFN_CHUNK):
        cols = slice(lo, lo + FFN_CHUNK)
        raw_ref[slot, 0, :, cols] = jnp.dot(xn, wu_ref[:, cols], preferred_element_type=F32)
        raw_ref[slot, 1, :, cols] = jnp.dot(xn, wg_ref[:, cols], preferred_element_type=F32)
        u = conv(raw_ref[slot, 0, :, cols], cwu_ref, cbu_ref, cols)
        gate = conv(raw_ref[slot, 1, :, cols], cwg_ref, cbg_ref, cols)
        act_ref[:, cols] = (jax.nn.silu(gate) * u).astype(BF16)
    acc_ref[...] += jnp.dot(act_ref[...], wd_ref[...], preferred_element_type=F32)

    @pl.when(j == pl.num_programs(1) - 1)
    def _():
        o_ref[...] = h_ref[...] + acc_ref[...]


def _conv_ffn(h1, g, w_up, conv_w, conv_b, w_down, *, tm=512, tf=512):
    t = h1.shape[0]
    assert t % tm == 0 and SEQ % tm == 0 and D_FF % tf == 0
    nf = D_FF // tf
    est = (4 * tm * D_MODEL * 4 + 2 * 3 * D_MODEL * tf * 2 + (tm + HALO) * D_MODEL * 2 + tm * D_MODEL * 4
           + 8 * (tm + HALO) * tf * 4)
    return pl.pallas_call(
        functools.partial(_ffn_kernel, tm=tm),
        grid=(t // tm, nf),
        in_specs=[
            pl.BlockSpec((tm, D_MODEL), lambda i, j: (i, 0)),
            pl.BlockSpec((HALO, D_MODEL), lambda i, j: (jnp.maximum(i * (tm // HALO) - 1, 0), 0)),
            pl.BlockSpec((1, D_MODEL), lambda i, j: (0, 0)),
            pl.BlockSpec((D_MODEL, tf), lambda i, j: (0, j)),
            pl.BlockSpec((D_MODEL, tf), lambda i, j: (0, nf + j)),
            pl.BlockSpec((CONV_W, tf), lambda i, j: (0, j)),
            pl.BlockSpec((CONV_W, tf), lambda i, j: (0, nf + j)),
            pl.BlockSpec((1, tf), lambda i, j: (0, j)),
            pl.BlockSpec((1, tf), lambda i, j: (0, nf + j)),
            pl.BlockSpec((tf, D_MODEL), lambda i, j: (j, 0)),
        ],
        out_specs=pl.BlockSpec((tm, D_MODEL), lambda i, j: (i, 0)),
        out_shape=jax.ShapeDtypeStruct((t, D_MODEL), F32),
        scratch_shapes=[pltpu.VMEM((tm + HALO, D_MODEL), BF16), pltpu.VMEM((2, 2, tm + HALO, tf), F32),
                        pltpu.VMEM((tm, tf), BF16),
                        pltpu.VMEM((tm, D_MODEL), F32)],
        compiler_params=_params(("parallel", "arbitrary"), est),
        name="conv_ffn",
    )(h1, h1, g, w_up, w_up, conv_w, conv_w, conv_b, conv_b, w_down)


PLE_CHUNK = 512


def _ple_kernel(h_ref, gp_ref, wg_ref, p_ref, wp_ref, gf_ref, o_ref):
    hn = _rmsnorm(h_ref[...], gp_ref[...]).astype(BF16)
    pb = p_ref[...].astype(BF16)
    ssq = None
    for lo in range(0, D_MODEL, PLE_CHUNK):
        cols = slice(lo, lo + PLE_CHUNK)
        gate = jax.nn.sigmoid(jnp.dot(hn, wg_ref[:, cols], preferred_element_type=F32))
        emb = jnp.dot(pb, wp_ref[:, cols], preferred_element_type=F32)
        h3 = h_ref[:, cols] + gate * emb
        o_ref[:, cols] = h3
        part = jnp.sum(h3 * h3, axis=-1, keepdims=True)
        ssq = part if ssq is None else ssq + part
    o_ref[...] = o_ref[...] * lax.rsqrt(ssq * (1.0 / D_MODEL) + EPS) * gf_ref[...]


def _ple_out(h2, g_ple, w_gate, p2, w_proj, g_final, *, tm=256):
    t = h2.shape[0]
    est = 2 * D_MODEL * D_MODEL * 2 + 2 * PLE_DIM * D_MODEL * 2 + 4 * tm * D_MODEL * 4 + 6 * tm * D_MODEL * 4
    return pl.pallas_call(
        _ple_kernel,
        grid=(t // tm,),
        in_specs=[
            pl.BlockSpec((tm, D_MODEL), lambda i: (i, 0)),
            pl.BlockSpec((1, D_MODEL), lambda i: (0, 0)),
            pl.BlockSpec((D_MODEL, D_MODEL), lambda i: (0, 0)),
            pl.BlockSpec((tm, PLE_DIM), lambda i: (i, 0)),
            pl.BlockSpec((PLE_DIM, D_MODEL), lambda i: (0, 0)),
            pl.BlockSpec((1, D_MODEL), lambda i: (0, 0)),
        ],
        out_specs=pl.BlockSpec((tm, D_MODEL), lambda i: (i, 0)),
        out_shape=jax.ShapeDtypeStruct((t, D_MODEL), F32),
        compiler_params=_params(("parallel",), est),
        name="ple_out",
    )(h2, g_ple, w_gate, p2, w_proj, g_final)


def _split_w_in(w):
    gate_lo = NSA_WIDTH + 6 * NSA_KV_WIDTH
    gate = w[:, gate_lo:gate_lo + N_GATES]
    gate = gate.reshape(D_MODEL, 3, N_NSA_KV, NSA_GROUP).transpose(0, 2, 1, 3).reshape(D_MODEL, N_NSA_KV, 3 * NSA_GROUP)
    gate = jnp.pad(gate, ((0, 0), (0, 0), (0, GATE_STRIDE - 3 * NSA_GROUP))).reshape(D_MODEL, N_NSA_KV * GATE_STRIDE)
    gate = jnp.pad(gate, ((0, 0), (0, HEAD_DIM - N_NSA_KV * GATE_STRIDE)))
    return w[:, :gate_lo].astype(BF16), gate.astype(BF16), w[:, gate_lo + N_GATES:].astype(BF16)


def _rope_tables():
    inv = 1.0 / (ROPE_THETA ** (jnp.arange(0, ROPE_DIM, 2, dtype=F32) / ROPE_DIM))
    ang = jnp.arange(SEQ, dtype=F32)[:, None] * inv[None, :]
    cos, sin = jnp.cos(ang), jnp.sin(ang)
    rest = HEAD_DIM - ROPE_DIM
    cos_t = jnp.concatenate([cos, cos, jnp.ones((SEQ, rest), F32)], axis=1)
    sin_t = jnp.concatenate([-sin, sin, jnp.zeros((SEQ, rest), F32)], axis=1)
    return cos_t, sin_t


def _overlap_t():
    cmp_starts = np.arange(N_CMP) * CMP_STRIDE
    sel_starts = np.arange(N_SEL) * SLC_LEN
    ov = np.clip(np.minimum(cmp_starts[:, None] + CMP_LEN, sel_starts[None, :] + SLC_LEN)
                 - np.maximum(cmp_starts[:, None], sel_starts[None, :]), 0, None).astype(np.float32) / CMP_LEN
    ovt = np.zeros((N_SEL, N_CMP_PAD), np.float32)
    ovt[:, :N_CMP] = ov.T
    return jnp.asarray(ovt, BF16)


def kernel(x, p, attn_norm, w_in, cmp_k_pos, cmp_k_w1, cmp_k_w2, cmp_v_pos, cmp_v_w1, cmp_v_w2, nsa_out_norm, diff_lq1, diff_lk1, diff_lq2, diff_lk2, diff_subln, w_o, ffn_norm, w_up, conv_w, conv_b, w_down, ple_norm, w_ple_gate, w_ple_proj, final_norm):
    batch, seq, _ = x.shape
    assert seq == SEQ and p.shape[0] == 1
    t = batch * seq
    layer = 0
    lambda_init = 0.8 - 0.6 * math.exp(-0.3 * layer)
    x2 = x.reshape(t, D_MODEL)
    cos_t, sin_t = _rope_tables()

    proj, hkv = _in_proj(x2, attn_norm[layer][None], *_split_w_in(w_in[layer]), cos_t, sin_t)

    hkv = hkv.reshape(2, N_NSA_KV * t // HALF_BLOCK, HALF_BLOCK * HEAD_DIM)
    w1 = jnp.stack([cmp_k_w1[layer], cmp_v_w1[layer]]).astype(BF16)
    w2 = jnp.stack([cmp_k_w2[layer], cmp_v_w2[layer]]).astype(BF16)
    pos = jnp.stack([cmp_k_pos[layer], cmp_v_pos[layer]]).reshape(2, 1, CMP_LEN * HEAD_DIM)
    pos = jnp.broadcast_to(pos, (2, 8, CMP_LEN * HEAD_DIM)).astype(BF16)
    cmp_kv = _compress(hkv, w1, pos, w2)

    y_nsa = _nsa_attention(proj, cmp_kv, _overlap_t(), batch)
    y_diff = _diff_attention(proj, diff_lq1[layer][None], diff_lk1[layer][None], diff_lq2[layer][None],
                             diff_lk2[layer][None], diff_subln[layer][None], batch, lambda_init)
    h1 = _out_proj(y_nsa, nsa_out_norm[layer][None], y_diff, w_o[layer].astype(BF16), x2)
    h2 = _conv_ffn(h1, ffn_norm[layer][None], w_up[layer].astype(BF16), conv_w[layer], conv_b[layer][None],
                   w_down[layer].astype(BF16))
    out = _ple_out(h2, ple_norm[layer][None], w_ple_gate[layer].astype(BF16), p[layer].reshape(t, PLE_DIM),
                   w_ple_proj[layer].astype(BF16), final_norm[None])
    return out.reshape(batch, seq, D_MODEL)
```

```python
import functools
import math

import numpy as np
import jax
import jax.numpy as jnp
from jax import lax
from jax.experimental import pallas as pl
from jax.experimental.pallas import tpu as pltpu

D_MODEL = 2048
SEQ = 2048
HEAD_DIM = 128
ROPE_DIM = HEAD_DIM // 4
ROPE_THETA = 500000.0
N_NSA_HEADS = 8
N_NSA_KV = 2
NSA_GROUP = N_NSA_HEADS // N_NSA_KV
CMP_LEN = 32
CMP_STRIDE = 16
CMP_HIDDEN = 256
SLC_LEN = 64
SLC_TOP = 16
WINDOW = 512
N_DIFF_HEADS = 4
D_FF = 5632
CONV_W = 3
PLE_DIM = 256
EPS = 1e-6

NSA_WIDTH = N_NSA_HEADS * HEAD_DIM
NSA_KV_WIDTH = N_NSA_KV * HEAD_DIM
DIFF_WIDTH = N_DIFF_HEADS * 2 * HEAD_DIM
N_GATES = 3 * N_NSA_HEADS
N_CMP = (SEQ - CMP_LEN) // CMP_STRIDE + 1
N_CMP_PAD = SEQ // CMP_STRIDE
N_SEL = SEQ // SLC_LEN
GATE_STRIDE = 16

COL_NQ = 0
COL_KS = COL_NQ + NSA_WIDTH
COL_KW = COL_KS + NSA_KV_WIDTH
COL_DQ = COL_KW + NSA_KV_WIDTH
COL_DK = COL_DQ + DIFF_WIDTH
COL_VS = COL_DK + DIFF_WIDTH
COL_VW = COL_VS + NSA_KV_WIDTH
COL_DV = COL_VW + NSA_KV_WIDTH
COL_GATE = COL_DV + DIFF_WIDTH
PROJ_COLS = COL_GATE + HEAD_DIM

V7X_LANES = 128
SUBLANES = 8
V7X_VMEM_REQUEST_CAP = 56 * 1024 * 1024
NEG = -1e30
QSCALE = HEAD_DIM ** -0.5 * math.log2(math.e)

_NT = (((1,), (1,)), ((), ()))
_TN = (((0,), (0,)), ((), ()))
BF16 = jnp.bfloat16
F32 = jnp.float32


def _params(semantics, vmem_estimate_bytes, flags=None):
    limit = min(max(int(vmem_estimate_bytes), 32 * 1024 * 1024), V7X_VMEM_REQUEST_CAP)
    return pltpu.CompilerParams(dimension_semantics=semantics, vmem_limit_bytes=limit, flags=flags)


def _rmsnorm(x, g):
    return x * lax.rsqrt(jnp.mean(x * x, axis=-1, keepdims=True) + EPS) * g


INPROJ_CHUNK = 1024
HALF_BLOCK = CMP_STRIDE

_HEADS_A = ([(COL_NQ + h * HEAD_DIM, True, True) for h in range(N_NSA_HEADS)]
            + [(("cmp", 0, h), True, False) for h in range(N_NSA_KV)]
            + [(("cmp", 1, h), False, False) for h in range(N_NSA_KV)]
            + [(COL_KS + h * HEAD_DIM, True, False) for h in range(N_NSA_KV)]
            + [(COL_VS + h * HEAD_DIM, False, False) for h in range(N_NSA_KV)]
            + [(COL_KW + h * HEAD_DIM, True, False) for h in range(N_NSA_KV)]
            + [(COL_VW + h * HEAD_DIM, False, False) for h in range(N_NSA_KV)])
_HEADS_B = ([(COL_DQ + h * HEAD_DIM, True, True) for h in range(2 * N_DIFF_HEADS)]
            + [(COL_DK + h * HEAD_DIM, True, False) for h in range(2 * N_DIFF_HEADS)]
            + [(COL_DV + h * HEAD_DIM, False, False) for h in range(2 * N_DIFF_HEADS)])
_HEADS_GATE = [(COL_GATE, False, False)]


def _inproj_kernel(x_ref, g_ref, wa_ref, wg_ref, wb_ref, cos_ref, sin_ref, o_ref, hkv_ref, stage_ref):
    tm = x_ref.shape[0]
    xn = _rmsnorm(x_ref[...], g_ref[...]).astype(BF16)
    c = cos_ref[...]
    s = sin_ref[...]
    cq = c * QSCALE
    sq = s * QSCALE
    first_half = lax.broadcasted_iota(jnp.int32, c.shape, 1) < ROPE_DIM // 2
    n_staged = 0
    for w_ref, heads in ((wa_ref, _HEADS_A), (wg_ref, _HEADS_GATE), (wb_ref, _HEADS_B)):
        for lo in range(0, w_ref.shape[1], INPROJ_CHUNK):
            hi = min(lo + INPROJ_CHUNK, w_ref.shape[1])
            acc = jnp.dot(xn, w_ref[:, lo:hi], preferred_element_type=F32)
            for src in range(lo, hi, HEAD_DIM):
                dest, rope, is_query = heads[src // HEAD_DIM]
                a = acc[:, src - lo:src - lo + HEAD_DIM]
                if rope:
                    partner = jnp.where(first_half,
                                        pltpu.roll(a, HEAD_DIM - ROPE_DIM // 2, 1),
                                        pltpu.roll(a, ROPE_DIM // 2, 1))
                    a = a * cq + partner * sq if is_query else a * c + partner * s
                if isinstance(dest, tuple):
                    _, which, head = dest
                    stage = stage_ref.at[n_staged]
                    n_staged += 1
                    stage[...] = a
                    for r in range(HALF_BLOCK):
                        rows = stage[pl.ds(r, tm // HALF_BLOCK, stride=HALF_BLOCK), :]
                        hkv_ref[which, head, :, r * HEAD_DIM:(r + 1) * HEAD_DIM] = rows.astype(hkv_ref.dtype)
                else:
                    o_ref[:, dest:dest + HEAD_DIM] = a.astype(o_ref.dtype)


def _in_proj(x2, g, w_a, w_g, w_b, cos_t, sin_t, *, tm=256):
    t = x2.shape[0]
    assert t % tm == 0 and SEQ % tm == 0 and tm % (HALF_BLOCK * 16) == 0
    seq_tiles = SEQ // tm
    n_w = w_a.shape[1] + w_g.shape[1] + w_b.shape[1]
    est = (D_MODEL * n_w * 2 + 2 * tm * D_MODEL * 4 + 2 * tm * n_w * 2 + tm * D_MODEL * 2
           + 3 * tm * INPROJ_CHUNK * 4)
    resident = dict(pipeline_mode=pl.Buffered(1))
    half_cols = HALF_BLOCK * HEAD_DIM
    return pl.pallas_call(
        _inproj_kernel,
        grid=(t // tm,),
        in_specs=[
            pl.BlockSpec((tm, D_MODEL), lambda i: (i, 0)),
            pl.BlockSpec((1, D_MODEL), lambda i: (0, 0)),
            pl.BlockSpec(w_a.shape, lambda i: (0, 0), **resident),
            pl.BlockSpec(w_g.shape, lambda i: (0, 0), **resident),
            pl.BlockSpec(w_b.shape, lambda i: (0, 0), **resident),
            pl.BlockSpec((tm, HEAD_DIM), lambda i: (i % seq_tiles, 0)),
            pl.BlockSpec((tm, HEAD_DIM), lambda i: (i % seq_tiles, 0)),
        ],
        out_specs=[
            pl.BlockSpec((tm, PROJ_COLS), lambda i: (i, 0)),
            pl.BlockSpec((2, N_NSA_KV, tm // HALF_BLOCK, half_cols), lambda i: (0, 0, i, 0)),
        ],
        out_shape=[
            jax.ShapeDtypeStruct((t, PROJ_COLS), BF16),
            jax.ShapeDtypeStruct((2, N_NSA_KV, t // HALF_BLOCK, half_cols), BF16),
        ],
        scratch_shapes=[pltpu.VMEM((2 * N_NSA_KV, tm, HEAD_DIM), F32)],
        compiler_params=_params(("parallel",), est),
        name="in_proj",
    )(x2, g, w_a, w_g, w_b, cos_t, sin_t)


def _compress_kernel(h_ref, w1_ref, pos_ref, w2_ref, o_ref):
    half = CMP_LEN * HEAD_DIM // 2
    h = h_ref[0]
    top = jnp.dot(h, w1_ref[0, :half, :], preferred_element_type=F32)
    bot = jnp.dot(h, w1_ref[0, half:, :], preferred_element_type=F32)
    pos_bias = jnp.dot(pos_ref[0], w1_ref[0], preferred_element_type=F32)[0:1]
    pre = top + pltpu.roll(bot, bot.shape[0] - 1, 0) + pos_bias
    act = jax.nn.gelu(pre)
    o_ref[0] = jnp.dot(act.astype(BF16), w2_ref[0], preferred_element_type=F32).astype(o_ref.dtype)


def _compress(hkv, w1, pos, w2):
    rows = hkv.shape[1]
    kdim = CMP_LEN * HEAD_DIM
    est = 2 * (rows * kdim // 2 * 2 + kdim * CMP_HIDDEN * 2) + 6 * rows * CMP_HIDDEN * 4
    return pl.pallas_call(
        _compress_kernel,
        grid=(2,),
        in_specs=[
            pl.BlockSpec((1, rows, kdim // 2), lambda i: (i, 0, 0)),
            pl.BlockSpec((1, kdim, CMP_HIDDEN), lambda i: (i, 0, 0)),
            pl.BlockSpec((1, 8, kdim), lambda i: (i, 0, 0)),
            pl.BlockSpec((1, CMP_HIDDEN, HEAD_DIM), lambda i: (i, 0, 0)),
        ],
        out_specs=pl.BlockSpec((1, rows, HEAD_DIM), lambda i: (i, 0, 0)),
        out_shape=jax.ShapeDtypeStruct((2, rows, HEAD_DIM), BF16),
        compiler_params=_params(("parallel",), est),
        name="compress",
    )(hkv, w1, pos, w2)


def _causal_flash(chains, n_before, diag_mask, d, lanes):
    def tile_step(kt, carries, mask=None):
        stats = []
        probs = []
        for (score_fn, _), (m, l, _) in zip(chains, carries):
            s = score_fn(kt)
            if mask is not None:
                s = jnp.where(mask, s, NEG)
            m_new = jnp.maximum(m, jnp.max(s, axis=0, keepdims=True))
            alpha = jnp.exp2(m - m_new)
            p = jnp.exp2(s - m_new)
            stats.append((m_new, alpha * l + jnp.sum(p, axis=0, keepdims=True), alpha))
            probs.append(p.astype(BF16))
        out = []
        for (_, value_fn), (_, _, acc), (m_new, l_new, alpha), p in zip(chains, carries, stats, probs):
            pv = lax.dot_general(value_fn(kt), p, _TN, preferred_element_type=F32)
            out.append((m_new, l_new, alpha * acc + pv))
        return tuple(out)

    init = (jnp.full((1, lanes), NEG, F32), jnp.zeros((1, lanes), F32), jnp.zeros((d, lanes), F32))
    carries = lax.fori_loop(0, n_before, tile_step, (init,) * len(chains))
    return [acc * (1.0 / l) for _, l, acc in tile_step(n_before, carries, diag_mask)]


def _nsa_kernel(q_ref, kc0_ref, vc0_ref, kc1_ref, vc1_ref, ks_ref, vs_ref, kw_ref, vw_ref, gate_ref, ovt_ref,
                o_ref, bias_ref, *, tq):
    qi = pl.program_id(1)
    start = qi * tq
    nl = NSA_GROUP * tq
    kvs = range(N_NSA_KV)
    cmp_refs = ((kc0_ref, vc0_ref), (kc1_ref, vc1_ref))

    def head_cols(kv):
        return slice(kv * HEAD_DIM, (kv + 1) * HEAD_DIM)

    q_all = q_ref[...]
    qs = [jnp.concatenate([q_all[:, (kv * NSA_GROUP + g) * HEAD_DIM:(kv * NSA_GROUP + g + 1) * HEAD_DIM]
                           for g in range(NSA_GROUP)], axis=0) for kv in kvs]
    q_local = lax.broadcasted_iota(jnp.int32, (1, nl), 1) & (tq - 1)
    t_lane = start + q_local
    k_local = lax.broadcasted_iota(jnp.int32, (tq, nl), 0)
    causal = k_local <= q_local
    c_end = lax.broadcasted_iota(jnp.int32, (N_CMP_PAD, nl), 0) * CMP_STRIDE + (CMP_LEN - 1)
    cmask = c_end <= t_lane
    j_idx = lax.broadcasted_iota(jnp.int32, (N_SEL, tq), 0)
    t_q = start + lax.broadcasted_iota(jnp.int32, (N_SEL, tq), 1)
    cur = t_q // SLC_LEN
    forced = (j_idx == 0) | (j_idx == cur) | (j_idx == cur - 1)
    ovt = ovt_ref[...]

    o_cmp = []
    for kv in kvs:
        kc_ref, vc_ref = cmp_refs[kv]
        s = lax.dot_general(kc_ref[0], qs[kv], _NT, preferred_element_type=F32)
        s = jnp.where(cmask, s, NEG)
        m = jnp.max(s, axis=0, keepdims=True)
        e = jnp.where(cmask, jnp.exp2(s - m), 0.0)
        l = jnp.sum(e, axis=0, keepdims=True)
        p_cmp = e * jnp.where(l > 0.0, 1.0 / l, 0.0)
        o_cmp.append(lax.dot_general(vc_ref[0], p_cmp.astype(BF16), _TN, preferred_element_type=F32))

        p_sum = p_cmp[:, 0:tq]
        for g in range(1, NSA_GROUP):
            p_sum = p_sum + p_cmp[:, g * tq:(g + 1) * tq]
        p_hi = p_sum.astype(BF16)
        p_lo = (p_sum - p_hi.astype(F32)).astype(BF16)
        p_slc = (jnp.dot(ovt, p_hi, preferred_element_type=F32)
                 + jnp.dot(ovt, p_lo, preferred_element_type=F32))
        score = jnp.where(forced, 1e4, jnp.where(j_idx > cur, -1e4, p_slc))
        rank = jnp.zeros((N_SEL, tq), jnp.int32)
        for i in range(N_SEL):
            row = score[i:i + 1, :]
            tie = jnp.where(j_idx > i, 1, 0)
            rank = rank + jnp.where(row > score, 1, jnp.where(row == score, tie, 0))
        bias = jnp.where((rank < SLC_TOP) & (j_idx <= cur), 0.0, NEG)
        bias_ref[kv] = jnp.concatenate([bias] * NSA_GROUP, axis=1)

    blocks_per_tile = tq // SLC_LEN

    def slc_chain(kv):
        def scores(kt):
            k0 = pl.multiple_of(kt * tq, tq)
            s = lax.dot_general(ks_ref[pl.ds(k0, tq), head_cols(kv)], qs[kv], _NT, preferred_element_type=F32)
            parts = [s[c * SLC_LEN:(c + 1) * SLC_LEN, :] + bias_ref[kv, pl.ds(kt * blocks_per_tile + c, 1), :]
                     for c in range(blocks_per_tile)]
            return jnp.concatenate(parts, axis=0)

        def values(kt):
            return vs_ref[pl.ds(pl.multiple_of(kt * tq, tq), tq), head_cols(kv)]

        return scores, values

    o_slc = _causal_flash([slc_chain(kv) for kv in kvs], qi, causal, HEAD_DIM, nl)

    win_p, win_l, win_v = [], [], []
    for kv in kvs:
        scores, values = [], []
        for back in range(WINDOW // tq, -1, -1):
            k0 = start - back * tq
            inside = k0 >= 0
            k0 = pl.multiple_of(jnp.maximum(k0, 0), tq)
            s = lax.dot_general(kw_ref[pl.ds(k0, tq), head_cols(kv)], qs[kv], _NT, preferred_element_type=F32)
            if back == 0:
                s = jnp.where(causal, s, NEG)
            elif back == WINDOW // tq:
                edge = q_local + jnp.where(inside, 0, tq)
                s = jnp.where(k_local > edge, s, NEG)
            else:
                s = s + jnp.where(inside, 0.0, NEG)
            scores.append(s)
            values.append(vw_ref[pl.ds(k0, tq), head_cols(kv)])
        s = jnp.concatenate(scores, axis=0)
        p = jnp.exp2(s - jnp.max(s, axis=0, keepdims=True))
        win_l.append(jnp.sum(p, axis=0, keepdims=True))
        win_p.append(p.astype(BF16))
        win_v.append(jnp.concatenate(values, axis=0))

    gates_t = jax.nn.sigmoid(gate_ref[...].astype(F32)).T
    for kv in kvs:
        acc_w = lax.dot_general(win_v[kv], win_p[kv], _TN, preferred_element_type=F32)
        o_win = acc_w * (1.0 / win_l[kv])

        gts = gates_t[kv * GATE_STRIDE:(kv + 1) * GATE_STRIDE]
        for g in range(NSA_GROUP):
            sl = slice(g * tq, (g + 1) * tq)
            o_t = (gts[g:g + 1, :] * o_cmp[kv][:, sl]
                   + gts[NSA_GROUP + g:NSA_GROUP + g + 1, :] * o_slc[kv][:, sl]
                   + gts[2 * NSA_GROUP + g:2 * NSA_GROUP + g + 1, :] * o_win[:, sl])
            head = kv * NSA_GROUP + g
            o_ref[:, head * HEAD_DIM:(head + 1) * HEAD_DIM] = o_t.T


def _nsa_attention(proj, cmp_kv, ovt, batch, *, tq=256):
    t = proj.shape[0]
    assert WINDOW % tq == 0 and tq % SLC_LEN == 0 and SEQ % tq == 0
    nq = SEQ // tq
    est = (4 * 2 * SEQ * NSA_KV_WIDTH * 2 + 2 * tq * NSA_WIDTH * (2 + 4)
           + N_NSA_KV * 10 * (WINDOW + tq) * NSA_GROUP * tq * 4)

    def kv_spec(col):
        return pl.BlockSpec((SEQ, NSA_KV_WIDTH), lambda b, i: (b, col // NSA_KV_WIDTH))

    def cmp_spec(which, kv):
        return pl.BlockSpec((1, N_CMP_PAD, HEAD_DIM), lambda b, i: (which, kv * batch + b, 0))

    return pl.pallas_call(
        functools.partial(_nsa_kernel, tq=tq),
        grid=(batch, nq),
        in_specs=[
            pl.BlockSpec((tq, NSA_WIDTH), lambda b, i: (b * nq + i, 0)),
            cmp_spec(0, 0), cmp_spec(1, 0), cmp_spec(0, 1), cmp_spec(1, 1),
            kv_spec(COL_KS), kv_spec(COL_VS), kv_spec(COL_KW), kv_spec(COL_VW),
            pl.BlockSpec((tq, HEAD_DIM), lambda b, i: (b * nq + i, COL_GATE // HEAD_DIM)),
            pl.BlockSpec((N_SEL, N_CMP_PAD), lambda b, i: (0, 0)),
        ],
        out_specs=pl.BlockSpec((tq, NSA_WIDTH), lambda b, i: (b * nq + i, 0)),
        out_shape=jax.ShapeDtypeStruct((t, NSA_WIDTH), F32),
        scratch_shapes=[pltpu.VMEM((N_NSA_KV, N_SEL, NSA_GROUP * tq), F32)],
        compiler_params=_params(("parallel", "arbitrary"), est),
        name="nsa_attn",
    )(proj, cmp_kv, cmp_kv, cmp_kv, cmp_kv, proj, proj, proj, proj, proj, ovt)


def _diff_kernel(q_ref, k_ref, v_ref, lq1_ref, lk1_ref, lq2_ref, lk2_ref, sub_ref, o_ref,
                 *, tq, lambda_init):
    qi = pl.program_id(2)
    wide = 2 * HEAD_DIM
    lam = (jnp.exp(jnp.sum(lq1_ref[...] * lk1_ref[...], axis=-1, keepdims=True))
           - jnp.exp(jnp.sum(lq2_ref[...] * lk2_ref[...], axis=-1, keepdims=True)) + lambda_init)
    q = q_ref[...]

    def chain(h):
        q1 = q[:, h * wide:h * wide + HEAD_DIM]
        q2 = q[:, h * wide + HEAD_DIM:(h + 1) * wide]

        def scores(kt):
            kk = k_ref[pl.ds(pl.multiple_of(kt * tq, tq), tq), h * wide:(h + 1) * wide]
            s1 = lax.dot_general(kk[:, :HEAD_DIM], q1, _NT, preferred_element_type=F32)
            s2 = lax.dot_general(kk[:, HEAD_DIM:], q2, _NT, preferred_element_type=F32)
            return jnp.concatenate([s1, s2], axis=1)

        def values(kt):
            return v_ref[pl.ds(pl.multiple_of(kt * tq, tq), tq), h * wide:(h + 1) * wide]

        return scores, values

    k_local = lax.broadcasted_iota(jnp.int32, (tq, 2 * tq), 0)
    q_local = lax.broadcasted_iota(jnp.int32, (1, 2 * tq), 1) & (tq - 1)
    outs = _causal_flash([chain(h) for h in range(DIFF_HEADS_PER_STEP)], qi, k_local <= q_local, wide, 2 * tq)
    for h, o_n in enumerate(outs):
        o_t = o_n[:, :tq] - lam * o_n[:, tq:]
        o = _rmsnorm(o_t.T, sub_ref[...]) * (1.0 - lambda_init)
        o_ref[:, h * wide:(h + 1) * wide] = o.astype(o_ref.dtype)


DIFF_HEADS_PER_STEP = 2


def _diff_attention(proj, lq1, lk1, lq2, lk2, subln, batch, lambda_init, *, tq=512):
    t = proj.shape[0]
    nq = SEQ // tq
    wide = 2 * HEAD_DIM
    step_cols = DIFF_HEADS_PER_STEP * wide
    est = 2 * 2 * SEQ * step_cols * 2 + 4 * tq * step_cols * 4 + DIFF_HEADS_PER_STEP * 16 * tq * 2 * tq * 4
    vec = pl.BlockSpec((1, HEAD_DIM), lambda b, h, i: (0, 0))
    return pl.pallas_call(
        functools.partial(_diff_kernel, tq=tq, lambda_init=lambda_init),
        grid=(batch, N_DIFF_HEADS // DIFF_HEADS_PER_STEP, nq),
        in_specs=[
            pl.BlockSpec((tq, step_cols), lambda b, h, i: (b * nq + i, COL_DQ // step_cols + h)),
            pl.BlockSpec((SEQ, step_cols), lambda b, h, i: (b, COL_DK // step_cols + h)),
            pl.BlockSpec((SEQ, step_cols), lambda b, h, i: (b, COL_DV // step_cols + h)),
            vec, vec, vec, vec,
            pl.BlockSpec((1, wide), lambda b, h, i: (0, 0)),
        ],
        out_specs=pl.BlockSpec((tq, step_cols), lambda b, h, i: (b * nq + i, h)),
        out_shape=jax.ShapeDtypeStruct((t, DIFF_WIDTH), BF16),
        compiler_params=_params(("parallel", "parallel", "arbitrary"), est),
        name="diff_attn",
    )(proj, proj, proj, lq1, lk1, lq2, lk2, subln)


def _oproj_kernel(yn_ref, g_ref, yd_ref, wo_ref, x_ref, o_ref):
    yn = _rmsnorm(yn_ref[...], g_ref[...]).astype(BF16)
    acc = jnp.dot(yn, wo_ref[:NSA_WIDTH, :], preferred_element_type=F32)
    acc = acc + jnp.dot(yd_ref[...], wo_ref[NSA_WIDTH:, :], preferred_element_type=F32)
    o_ref[...] = x_ref[...] + acc


def _out_proj(y_nsa, g, y_diff, w_o, x2, *, tm=512):
    t = x2.shape[0]
    est = (2 * (NSA_WIDTH + DIFF_WIDTH) * D_MODEL * 2 + 2 * tm * NSA_WIDTH * 4 + 2 * tm * DIFF_WIDTH * 2
           + 5 * tm * D_MODEL * 4)
    return pl.pallas_call(
        _oproj_kernel,
        grid=(t // tm,),
        in_specs=[
            pl.BlockSpec((tm, NSA_WIDTH), lambda i: (i, 0)),
            pl.BlockSpec((1, NSA_WIDTH), lambda i: (0, 0)),
            pl.BlockSpec((tm, DIFF_WIDTH), lambda i: (i, 0)),
            pl.BlockSpec((NSA_WIDTH + DIFF_WIDTH, D_MODEL), lambda i: (0, 0)),
            pl.BlockSpec((tm, D_MODEL), lambda i: (i, 0)),
        ],
        out_specs=pl.BlockSpec((tm, D_MODEL), lambda i: (i, 0)),
        out_shape=jax.ShapeDtypeStruct((t, D_MODEL), F32),
        compiler_params=_params(("parallel",), est),
        name="out_proj",
    )(y_nsa, g, y_diff, w_o, x2)


HALO = 8
FFN_CHUNK = 256
FFN_ROW_BLOCKS = 4


def _ffn_kernel(h_ref, halo_ref, g_ref, wu_ref, wg_ref, cwu_ref, cwg_ref, cbu_ref, cbg_ref, wd_ref,
                o_ref, xn_ref, raw_ref, act_ref, acc_ref, *, tm):
    i = pl.program_id(0)
    j = pl.program_id(1)

    @pl.when(j == 0)
    def _():
        keep = jnp.where((i * tm) % SEQ == 0, 0.0, 1.0)
        xn_ref[0:HALO, :] = (_rmsnorm(halo_ref[...], g_ref[...]) * keep).astype(BF16)
        xn_ref[HALO:, :] = _rmsnorm(h_ref[...], g_ref[...]).astype(BF16)
        acc_ref[...] = jnp.zeros_like(acc_ref)

    xn = xn_ref[...]

    def conv(h, cw_ref, cb_ref, cols):
        cw = cw_ref[:, cols]
        h3 = h.reshape(h.shape[0] // SUBLANES, SUBLANES, h.shape[1])
        sub = lax.broadcasted_iota(jnp.int32, h3.shape[1:], 0)
        out = cw[CONV_W - 1:CONV_W, :] * h3[1:] + cb_ref[:, cols]
        for back in range(1, CONV_W):
            rolled = pltpu.roll(h3, back, 1)
            shifted = jnp.where(sub < back, rolled[:-1], rolled[1:])
            out = out + cw[CONV_W - 1 - back:CONV_W - back, :] * shifted
        return out.reshape(h.shape[0] - HALO, h.shape[1])

    def gated(rows, cols):
        src = slice(rows.start, rows.stop + HALO)
        u = conv(raw_ref[slot, 0, src, cols], cwu_ref, cbu_ref, cols)
        gate = conv(raw_ref[slot, 1, src, cols], cwg_ref, cbg_ref, cols)
        act_ref[rows, cols] = (jax.nn.silu(gate) * u).astype(BF16)

    slot = j % 2
    chunks = [slice(lo, lo + FFN_CHUNK) for lo in range(0, wu_ref.shape[1], FFN_CHUNK)]
    row_blocks = [slice(lo, lo + tm // FFN_ROW_BLOCKS) for lo in range(0, tm, tm // FFN_ROW_BLOCKS)]
    for cols in chunks:
        raw_ref[slot, 0, :, cols] = jnp.dot(xn, wu_ref[:, cols], preferred_element_type=F32)
        raw_ref[slot, 1, :, cols] = jnp.dot(xn, wg_ref[:, cols], preferred_element_type=F32)
        if cols is not chunks[-1]:
            gated(slice(0, tm), cols)
    gated(row_blocks[0], chunks[-1])
    for n, rows in enumerate(row_blocks):
        if n + 1 < len(row_blocks):
            gated(row_blocks[n + 1], chunks[-1])
        acc_ref[rows, :] += jnp.dot(act_ref[rows, :], wd_ref[...], preferred_element_type=F32)

    @pl.when(j == pl.num_programs(1) - 1)
    def _():
        o_ref[...] = h_ref[...] + acc_ref[...]


def _conv_ffn(h1, g, w_up, conv_w, conv_b, w_down, *, tm=512, tf=512):
    t = h1.shape[0]
    assert t % tm == 0 and SEQ % tm == 0 and D_FF % tf == 0
    nf = D_FF // tf
    est = (4 * tm * D_MODEL * 4 + 2 * 3 * D_MODEL * tf * 2 + (tm + HALO) * D_MODEL * 2 + tm * D_MODEL * 4
           + 8 * (tm + HALO) * tf * 4)
    return pl.pallas_call(
        functools.partial(_ffn_kernel, tm=tm),
        grid=(t // tm, nf),
        in_specs=[
            pl.BlockSpec((tm, D_MODEL), lambda i, j: (i, 0)),
            pl.BlockSpec((HALO, D_MODEL), lambda i, j: (jnp.maximum(i * (tm // HALO) - 1, 0), 0)),
            pl.BlockSpec((1, D_MODEL), lambda i, j: (0, 0)),
            pl.BlockSpec((D_MODEL, tf), lambda i, j: (0, j)),
            pl.BlockSpec((D_MODEL, tf), lambda i, j: (0, nf + j)),
            pl.BlockSpec((CONV_W, tf), lambda i, j: (0, j)),
            pl.BlockSpec((CONV_W, tf), lambda i, j: (0, nf + j)),
            pl.BlockSpec((1, tf), lambda i, j: (0, j)),
            pl.BlockSpec((1, tf), lambda i, j: (0, nf + j)),
            pl.BlockSpec((tf, D_MODEL), lambda i, j: (j, 0)),
        ],
        out_specs=pl.BlockSpec((tm, D_MODEL), lambda i, j: (i, 0)),
        out_shape=jax.ShapeDtypeStruct((t, D_MODEL), F32),
        scratch_shapes=[pltpu.VMEM((tm + HALO, D_MODEL), BF16), pltpu.VMEM((2, 2, tm + HALO, tf), F32),
                        pltpu.VMEM((tm, tf), BF16),
                        pltpu.VMEM((tm, D_MODEL), F32)],
        compiler_params=_params(("parallel", "arbitrary"), est),
        name="conv_ffn",
    )(h1, h1, g, w_up, w_up, conv_w, conv_w, conv_b, conv_b, w_down)


PLE_CHUNK = 512


def _ple_kernel(h_ref, gp_ref, wg_ref, p_ref, wp_ref, gf_ref, o_ref):
    hn = _rmsnorm(h_ref[...], gp_ref[...]).astype(BF16)
    pb = p_ref[...].astype(BF16)
    ssq = None
    for lo in range(0, D_MODEL, PLE_CHUNK):
        cols = slice(lo, lo + PLE_CHUNK)
        gate = jax.nn.sigmoid(jnp.dot(hn, wg_ref[:, cols], preferred_element_type=F32))
        emb = jnp.dot(pb, wp_ref[:, cols], preferred_element_type=F32)
        h3 = h_ref[:, cols] + gate * emb
        o_ref[:, cols] = h3
        part = jnp.sum(h3 * h3, axis=-1, keepdims=True)
        ssq = part if ssq is None else ssq + part
    o_ref[...] = o_ref[...] * lax.rsqrt(ssq * (1.0 / D_MODEL) + EPS) * gf_ref[...]


def _ple_out(h2, g_ple, w_gate, p2, w_proj, g_final, *, tm=256):
    t = h2.shape[0]
    est = 2 * D_MODEL * D_MODEL * 2 + 2 * PLE_DIM * D_MODEL * 2 + 4 * tm * D_MODEL * 4 + 6 * tm * D_MODEL * 4
    return pl.pallas_call(
        _ple_kernel,
        grid=(t // tm,),
        in_specs=[
            pl.BlockSpec((tm, D_MODEL), lambda i: (i, 0)),
            pl.BlockSpec((1, D_MODEL), lambda i: (0, 0)),
            pl.BlockSpec((D_MODEL, D_MODEL), lambda i: (0, 0)),
            pl.BlockSpec((tm, PLE_DIM), lambda i: (i, 0)),
            pl.BlockSpec((PLE_DIM, D_MODEL), lambda i: (0, 0)),
            pl.BlockSpec((1, D_MODEL), lambda i: (0, 0)),
        ],
        out_specs=pl.BlockSpec((tm, D_MODEL), lambda i: (i, 0)),
        out_shape=jax.ShapeDtypeStruct((t, D_MODEL), F32),
        compiler_params=_params(("parallel",), est),
        name="ple_out",
    )(h2, g_ple, w_gate, p2, w_proj, g_final)


def _split_w_in(w):
    gate_lo = NSA_WIDTH + 6 * NSA_KV_WIDTH
    gate = w[:, gate_lo:gate_lo + N_GATES]
    gate = gate.reshape(D_MODEL, 3, N_NSA_KV, NSA_GROUP).transpose(0, 2, 1, 3).reshape(D_MODEL, N_NSA_KV, 3 * NSA_GROUP)
    gate = jnp.pad(gate, ((0, 0), (0, 0), (0, GATE_STRIDE - 3 * NSA_GROUP))).reshape(D_MODEL, N_NSA_KV * GATE_STRIDE)
    gate = jnp.pad(gate, ((0, 0), (0, HEAD_DIM - N_NSA_KV * GATE_STRIDE)))
    return w[:, :gate_lo].astype(BF16), gate.astype(BF16), w[:, gate_lo + N_GATES:].astype(BF16)


def _rope_tables():
    inv = 1.0 / (ROPE_THETA ** (jnp.arange(0, ROPE_DIM, 2, dtype=F32) / ROPE_DIM))
    ang = jnp.arange(SEQ, dtype=F32)[:, None] * inv[None, :]
    cos, sin = jnp.cos(ang), jnp.sin(ang)
    rest = HEAD_DIM - ROPE_DIM
    cos_t = jnp.concatenate([cos, cos, jnp.ones((SEQ, rest), F32)], axis=1)
    sin_t = jnp.concatenate([-sin, sin, jnp.zeros((SEQ, rest), F32)], axis=1)
    return cos_t, sin_t


def _overlap_t():
    cmp_starts = np.arange(N_CMP) * CMP_STRIDE
    sel_starts = np.arange(N_SEL) * SLC_LEN
    ov = np.clip(np.minimum(cmp_starts[:, None] + CMP_LEN, sel_starts[None, :] + SLC_LEN)
                 - np.maximum(cmp_starts[:, None], sel_starts[None, :]), 0, None).astype(np.float32) / CMP_LEN
    ovt = np.zeros((N_SEL, N_CMP_PAD), np.float32)
    ovt[:, :N_CMP] = ov.T
    return jnp.asarray(ovt, BF16)


def kernel(x, p, attn_norm, w_in, cmp_k_pos, cmp_k_w1, cmp_k_w2, cmp_v_pos, cmp_v_w1, cmp_v_w2, nsa_out_norm, diff_lq1, diff_lk1, diff_lq2, diff_lk2, diff_subln, w_o, ffn_norm, w_up, conv_w, conv_b, w_down, ple_norm, w_ple_gate, w_ple_proj, final_norm):
    batch, seq, _ = x.shape
    assert seq == SEQ and p.shape[0] == 1
    t = batch * seq
    layer = 0
    lambda_init = 0.8 - 0.6 * math.exp(-0.3 * layer)
    x2 = x.reshape(t, D_MODEL)
    cos_t, sin_t = _rope_tables()

    proj, hkv = _in_proj(x2, attn_norm[layer][None], *_split_w_in(w_in[layer]), cos_t, sin_t)

    hkv = hkv.reshape(2, N_NSA_KV * t // HALF_BLOCK, HALF_BLOCK * HEAD_DIM)
    w1 = jnp.stack([cmp_k_w1[layer], cmp_v_w1[layer]]).astype(BF16)
    w2 = jnp.stack([cmp_k_w2[layer], cmp_v_w2[layer]]).astype(BF16)
    pos = jnp.stack([cmp_k_pos[layer], cmp_v_pos[layer]]).reshape(2, 1, CMP_LEN * HEAD_DIM)
    pos = jnp.broadcast_to(pos, (2, 8, CMP_LEN * HEAD_DIM)).astype(BF16)
    cmp_kv = _compress(hkv, w1, pos, w2)

    y_nsa = _nsa_attention(proj, cmp_kv, _overlap_t(), batch)
    y_diff = _diff_attention(proj, diff_lq1[layer][None], diff_lk1[layer][None], diff_lq2[layer][None],
                             diff_lk2[layer][None], diff_subln[layer][None], batch, lambda_init)
    h1 = _out_proj(y_nsa, nsa_out_norm[layer][None], y_diff, w_o[layer].astype(BF16), x2)
    h2 = _conv_ffn(h1, ffn_norm[layer][None], w_up[layer].astype(BF16), conv_w[layer], conv_b[layer][None],
                   w_down[layer].astype(BF16))
    out = _ple_out(h2, ple_norm[layer][None], w_ple_gate[layer].astype(BF16), p[layer].reshape(t, PLE_DIM),
                   w_ple_proj[layer].astype(BF16), final_norm[None])
    return out.reshape(batch, seq, D_MODEL)
```

```python
import functools
import math

import numpy as np
import jax
import jax.numpy as jnp
from jax import lax
from jax.experimental import pallas as pl
from jax.experimental.pallas import tpu as pltpu

D_MODEL = 2048
SEQ = 2048
HEAD_DIM = 128
ROPE_DIM = HEAD_DIM // 4
ROPE_THETA = 500000.0
N_NSA_HEADS = 8
N_NSA_KV = 2
NSA_GROUP = N_NSA_HEADS // N_NSA_KV
CMP_LEN = 32
CMP_STRIDE = 16
CMP_HIDDEN = 256
SLC_LEN = 64
SLC_TOP = 16
WINDOW = 512
N_DIFF_HEADS = 4
D_FF = 5632
CONV_W = 3
PLE_DIM = 256
EPS = 1e-6

NSA_WIDTH = N_NSA_HEADS * HEAD_DIM
NSA_KV_WIDTH = N_NSA_KV * HEAD_DIM
DIFF_WIDTH = N_DIFF_HEADS * 2 * HEAD_DIM
N_GATES = 3 * N_NSA_HEADS
N_CMP = (SEQ - CMP_LEN) // CMP_STRIDE + 1
N_CMP_PAD = SEQ // CMP_STRIDE
N_SEL = SEQ // SLC_LEN
GATE_STRIDE = 16

COL_NQ = 0
COL_KS = COL_NQ + NSA_WIDTH
COL_KW = COL_KS + NSA_KV_WIDTH
COL_DQ = COL_KW + NSA_KV_WIDTH
COL_DK = COL_DQ + DIFF_WIDTH
COL_VS = COL_DK + DIFF_WIDTH
COL_VW = COL_VS + NSA_KV_WIDTH
COL_DV = COL_VW + NSA_KV_WIDTH
COL_GATE = COL_DV + DIFF_WIDTH
PROJ_COLS = COL_GATE + HEAD_DIM

V7X_LANES = 128
SUBLANES = 8
V7X_VMEM_REQUEST_CAP = 56 * 1024 * 1024
NEG = -1e30
QSCALE = HEAD_DIM ** -0.5 * math.log2(math.e)

_NT = (((1,), (1,)), ((), ()))
_TN = (((0,), (0,)), ((), ()))
BF16 = jnp.bfloat16
F32 = jnp.float32


def _params(semantics, vmem_estimate_bytes, flags=None):
    limit = min(max(int(vmem_estimate_bytes), 32 * 1024 * 1024), V7X_VMEM_REQUEST_CAP)
    return pltpu.CompilerParams(dimension_semantics=semantics, vmem_limit_bytes=limit, flags=flags)


def _rmsnorm(x, g):
    return x * lax.rsqrt(jnp.mean(x * x, axis=-1, keepdims=True) + EPS) * g


INPROJ_CHUNK = 1024
HALF_BLOCK = CMP_STRIDE

_HEADS_A = ([(COL_NQ + h * HEAD_DIM, True, True) for h in range(N_NSA_HEADS)]
            + [(("cmp", 0, h), True, False) for h in range(N_NSA_KV)]
            + [(("cmp", 1, h), False, False) for h in range(N_NSA_KV)]
            + [(COL_KS + h * HEAD_DIM, True, False) for h in range(N_NSA_KV)]
            + [(COL_VS + h * HEAD_DIM, False, False) for h in range(N_NSA_KV)]
            + [(COL_KW + h * HEAD_DIM, True, False) for h in range(N_NSA_KV)]
            + [(COL_VW + h * HEAD_DIM, False, False) for h in range(N_NSA_KV)])
_HEADS_B = ([(COL_DQ + h * HEAD_DIM, True, True) for h in range(2 * N_DIFF_HEADS)]
            + [(COL_DK + h * HEAD_DIM, True, False) for h in range(2 * N_DIFF_HEADS)]
            + [(COL_DV + h * HEAD_DIM, False, False) for h in range(2 * N_DIFF_HEADS)])
_HEADS_GATE = [(COL_GATE, False, False)]


def _inproj_kernel(x_ref, g_ref, wa_ref, wg_ref, wb_ref, cos_ref, sin_ref, o_ref, hkv_ref, stage_ref):
    tm = x_ref.shape[0]
    xn = _rmsnorm(x_ref[...], g_ref[...]).astype(BF16)
    c = cos_ref[...]
    s = sin_ref[...]
    cq = c * QSCALE
    sq = s * QSCALE
    first_half = lax.broadcasted_iota(jnp.int32, c.shape, 1) < ROPE_DIM // 2
    n_staged = 0
    for w_ref, heads in ((wa_ref, _HEADS_A), (wg_ref, _HEADS_GATE), (wb_ref, _HEADS_B)):
        for lo in range(0, w_ref.shape[1], INPROJ_CHUNK):
            hi = min(lo + INPROJ_CHUNK, w_ref.shape[1])
            acc = jnp.dot(xn, w_ref[:, lo:hi], preferred_element_type=F32)
            for src in range(lo, hi, HEAD_DIM):
                dest, rope, is_query = heads[src // HEAD_DIM]
                a = acc[:, src - lo:src - lo + HEAD_DIM]
                if rope:
                    partner = jnp.where(first_half,
                                        pltpu.roll(a, HEAD_DIM - ROPE_DIM // 2, 1),
                                        pltpu.roll(a, ROPE_DIM // 2, 1))
                    a = a * cq + partner * sq if is_query else a * c + partner * s
                if isinstance(dest, tuple):
                    _, which, head = dest
                    stage = stage_ref.at[n_staged]
                    n_staged += 1
                    stage[...] = a
                    for r in range(HALF_BLOCK):
                        rows = stage[pl.ds(r, tm // HALF_BLOCK, stride=HALF_BLOCK), :]
                        hkv_ref[which, head, :, r * HEAD_DIM:(r + 1) * HEAD_DIM] = rows.astype(hkv_ref.dtype)
                else:
                    o_ref[:, dest:dest + HEAD_DIM] = a.astype(o_ref.dtype)


def _in_proj(x2, g, w_a, w_g, w_b, cos_t, sin_t, *, tm=256):
    t = x2.shape[0]
    assert t % tm == 0 and SEQ % tm == 0 and tm % (HALF_BLOCK * 16) == 0
    seq_tiles = SEQ // tm
    n_w = w_a.shape[1] + w_g.shape[1] + w_b.shape[1]
    est = (D_MODEL * n_w * 2 + 2 * tm * D_MODEL * 4 + 2 * tm * n_w * 2 + tm * D_MODEL * 2
           + 3 * tm * INPROJ_CHUNK * 4)
    resident = dict(pipeline_mode=pl.Buffered(1))
    half_cols = HALF_BLOCK * HEAD_DIM
    return pl.pallas_call(
        _inproj_kernel,
        grid=(t // tm,),
        in_specs=[
            pl.BlockSpec((tm, D_MODEL), lambda i: (i, 0)),
            pl.BlockSpec((1, D_MODEL), lambda i: (0, 0)),
            pl.BlockSpec(w_a.shape, lambda i: (0, 0), **resident),
            pl.BlockSpec(w_g.shape, lambda i: (0, 0), **resident),
            pl.BlockSpec(w_b.shape, lambda i: (0, 0), **resident),
            pl.BlockSpec((tm, HEAD_DIM), lambda i: (i % seq_tiles, 0)),
            pl.BlockSpec((tm, HEAD_DIM), lambda i: (i % seq_tiles, 0)),
        ],
        out_specs=[
            pl.BlockSpec((tm, PROJ_COLS), lambda i: (i, 0)),
            pl.BlockSpec((2, N_NSA_KV, tm // HALF_BLOCK, half_cols), lambda i: (0, 0, i, 0)),
        ],
        out_shape=[
            jax.ShapeDtypeStruct((t, PROJ_COLS), BF16),
            jax.ShapeDtypeStruct((2, N_NSA_KV, t // HALF_BLOCK, half_cols), BF16),
        ],
        scratch_shapes=[pltpu.VMEM((2 * N_NSA_KV, tm, HEAD_DIM), F32)],
        compiler_params=_params(("parallel",), est),
        name="in_proj",
    )(x2, g, w_a, w_g, w_b, cos_t, sin_t)


def _compress_kernel(h_ref, w1_ref, pos_ref, w2_ref, o_ref):
    half = CMP_LEN * HEAD_DIM // 2
    h = h_ref[0]
    top = jnp.dot(h, w1_ref[0, :half, :], preferred_element_type=F32)
    bot = jnp.dot(h, w1_ref[0, half:, :], preferred_element_type=F32)
    pos_bias = jnp.dot(pos_ref[0], w1_ref[0], preferred_element_type=F32)[0:1]
    pre = top + pltpu.roll(bot, bot.shape[0] - 1, 0) + pos_bias
    act = jax.nn.gelu(pre)
    o_ref[0] = jnp.dot(act.astype(BF16), w2_ref[0], preferred_element_type=F32).astype(o_ref.dtype)


def _compress(hkv, w1, pos, w2):
    rows = hkv.shape[1]
    kdim = CMP_LEN * HEAD_DIM
    est = 2 * (rows * kdim // 2 * 2 + kdim * CMP_HIDDEN * 2) + 6 * rows * CMP_HIDDEN * 4
    return pl.pallas_call(
        _compress_kernel,
        grid=(2,),
        in_specs=[
            pl.BlockSpec((1, rows, kdim // 2), lambda i: (i, 0, 0)),
            pl.BlockSpec((1, kdim, CMP_HIDDEN), lambda i: (i, 0, 0)),
            pl.BlockSpec((1, 8, kdim), lambda i: (i, 0, 0)),
            pl.BlockSpec((1, CMP_HIDDEN, HEAD_DIM), lambda i: (i, 0, 0)),
        ],
        out_specs=pl.BlockSpec((1, rows, HEAD_DIM), lambda i: (i, 0, 0)),
        out_shape=jax.ShapeDtypeStruct((2, rows, HEAD_DIM), BF16),
        compiler_params=_params(("parallel",), est),
        name="compress",
    )(hkv, w1, pos, w2)


def _causal_flash(chains, n_before, diag_mask, d, lanes):
    def tile_step(kt, carries, mask=None):
        stats = []
        probs = []
        for (score_fn, _), (m, l, _) in zip(chains, carries):
            s = score_fn(kt)
            if mask is not None:
                s = jnp.where(mask, s, NEG)
            m_new = jnp.maximum(m, jnp.max(s, axis=0, keepdims=True))
            alpha = jnp.exp2(m - m_new)
            p = jnp.exp2(s - m_new)
            stats.append((m_new, alpha * l + jnp.sum(p, axis=0, keepdims=True), alpha))
            probs.append(p.astype(BF16))
        out = []
        for (_, value_fn), (_, _, acc), (m_new, l_new, alpha), p in zip(chains, carries, stats, probs):
            pv = lax.dot_general(value_fn(kt), p, _TN, preferred_element_type=F32)
            out.append((m_new, l_new, alpha * acc + pv))
        return tuple(out)

    init = (jnp.full((1, lanes), NEG, F32), jnp.zeros((1, lanes), F32), jnp.zeros((d, lanes), F32))
    carries = lax.fori_loop(0, n_before, tile_step, (init,) * len(chains))
    return [acc * (1.0 / l) for _, l, acc in tile_step(n_before, carries, diag_mask)]


def _nsa_kernel(q_ref, kc0_ref, vc0_ref, kc1_ref, vc1_ref, ks_ref, vs_ref, kw_ref, vw_ref, gate_ref, ovt_ref,
                o_ref, bias_ref, *, tq):
    qi = pl.program_id(1)
    start = qi * tq
    nl = NSA_GROUP * tq
    kvs = range(N_NSA_KV)
    cmp_refs = ((kc0_ref, vc0_ref), (kc1_ref, vc1_ref))

    def head_cols(kv):
        return slice(kv * HEAD_DIM, (kv + 1) * HEAD_DIM)

    q_all = q_ref[...]
    qs = [jnp.concatenate([q_all[:, (kv * NSA_GROUP + g) * HEAD_DIM:(kv * NSA_GROUP + g + 1) * HEAD_DIM]
                           for g in range(NSA_GROUP)], axis=0) for kv in kvs]
    q_local = lax.broadcasted_iota(jnp.int32, (1, nl), 1) & (tq - 1)
    t_lane = start + q_local
    k_local = lax.broadcasted_iota(jnp.int32, (tq, nl), 0)
    causal = k_local <= q_local
    c_end = lax.broadcasted_iota(jnp.int32, (N_CMP_PAD, nl), 0) * CMP_STRIDE + (CMP_LEN - 1)
    cmask = c_end <= t_lane
    j_idx = lax.broadcasted_iota(jnp.int32, (N_SEL, tq), 0)
    t_q = start + lax.broadcasted_iota(jnp.int32, (N_SEL, tq), 1)
    cur = t_q // SLC_LEN
    forced = (j_idx == 0) | (j_idx == cur) | (j_idx == cur - 1)
    ovt = ovt_ref[...]

    o_cmp = []
    for kv in kvs:
        kc_ref, vc_ref = cmp_refs[kv]
        s = lax.dot_general(kc_ref[0], qs[kv], _NT, preferred_element_type=F32)
        s = jnp.where(cmask, s, NEG)
        m = jnp.max(s, axis=0, keepdims=True)
        e = jnp.where(cmask, jnp.exp2(s - m), 0.0)
        l = jnp.sum(e, axis=0, keepdims=True)
        p_cmp = e * jnp.where(l > 0.0, 1.0 / l, 0.0)
        o_cmp.append(lax.dot_general(vc_ref[0], p_cmp.astype(BF16), _TN, preferred_element_type=F32))

        p_sum = p_cmp[:, 0:tq]
        for g in range(1, NSA_GROUP):
            p_sum = p_sum + p_cmp[:, g * tq:(g + 1) * tq]
        p_hi = p_sum.astype(BF16)
        p_lo = (p_sum - p_hi.astype(F32)).astype(BF16)
        p_slc = (jnp.dot(ovt, p_hi, preferred_element_type=F32)
                 + jnp.dot(ovt, p_lo, preferred_element_type=F32))
        score = jnp.where(forced, 1e4, jnp.where(j_idx > cur, -1e4, p_slc))
        rank = jnp.zeros((N_SEL, tq), jnp.int32)
        for i in range(N_SEL):
            row = score[i:i + 1, :]
            tie = jnp.where(j_idx > i, 1, 0)
            rank = rank + jnp.where(row > score, 1, jnp.where(row == score, tie, 0))
        bias = jnp.where((rank < SLC_TOP) & (j_idx <= cur), 0.0, NEG)
        bias_ref[kv] = jnp.concatenate([bias] * NSA_GROUP, axis=1)

    blocks_per_tile = tq // SLC_LEN

    def slc_chain(kv):
        def scores(kt):
            k0 = pl.multiple_of(kt * tq, tq)
            s = lax.dot_general(ks_ref[pl.ds(k0, tq), head_cols(kv)], qs[kv], _NT, preferred_element_type=F32)
            parts = [s[c * SLC_LEN:(c + 1) * SLC_LEN, :] + bias_ref[kv, pl.ds(kt * blocks_per_tile + c, 1), :]
                     for c in range(blocks_per_tile)]
            return jnp.concatenate(parts, axis=0)

        def values(kt):
            return vs_ref[pl.ds(pl.multiple_of(kt * tq, tq), tq), head_cols(kv)]

        return scores, values

    o_slc = _causal_flash([slc_chain(kv) for kv in kvs], qi, causal, HEAD_DIM, nl)

    win_p, win_l, win_v = [], [], []
    for kv in kvs:
        scores, values = [], []
        for back in range(WINDOW // tq, -1, -1):
            k0 = start - back * tq
            inside = k0 >= 0
            k0 = pl.multiple_of(jnp.maximum(k0, 0), tq)
            s = lax.dot_general(kw_ref[pl.ds(k0, tq), head_cols(kv)], qs[kv], _NT, preferred_element_type=F32)
            if back == 0:
                s = jnp.where(causal, s, NEG)
            elif back == WINDOW // tq:
                edge = q_local + jnp.where(inside, 0, tq)
                s = jnp.where(k_local > edge, s, NEG)
            else:
                s = s + jnp.where(inside, 0.0, NEG)
            scores.append(s)
            values.append(vw_ref[pl.ds(k0, tq), head_cols(kv)])
        s = jnp.concatenate(scores, axis=0)
        p = jnp.exp2(s - jnp.max(s, axis=0, keepdims=True))
        win_l.append(jnp.sum(p, axis=0, keepdims=True))
        win_p.append(p.astype(BF16))
        win_v.append(jnp.concatenate(values, axis=0))

    gates_t = jax.nn.sigmoid(gate_ref[...].astype(F32)).T
    for kv in kvs:
        acc_w = lax.dot_general(win_v[kv], win_p[kv], _TN, preferred_element_type=F32)
        o_win = acc_w * (1.0 / win_l[kv])

        gts = gates_t[kv * GATE_STRIDE:(kv + 1) * GATE_STRIDE]
        for g in range(NSA_GROUP):
            sl = slice(g * tq, (g + 1) * tq)
            o_t = (gts[g:g + 1, :] * o_cmp[kv][:, sl]
                   + gts[NSA_GROUP + g:NSA_GROUP + g + 1, :] * o_slc[kv][:, sl]
                   + gts[2 * NSA_GROUP + g:2 * NSA_GROUP + g + 1, :] * o_win[:, sl])
            head = kv * NSA_GROUP + g
            o_ref[:, head * HEAD_DIM:(head + 1) * HEAD_DIM] = o_t.T


def _nsa_attention(proj, cmp_kv, ovt, batch, *, tq=256):
    t = proj.shape[0]
    assert WINDOW % tq == 0 and tq % SLC_LEN == 0 and SEQ % tq == 0
    nq = SEQ // tq
    est = (4 * 2 * SEQ * NSA_KV_WIDTH * 2 + 2 * tq * NSA_WIDTH * (2 + 4)
           + N_NSA_KV * 10 * (WINDOW + tq) * NSA_GROUP * tq * 4)

    def kv_spec(col):
        return pl.BlockSpec((SEQ, NSA_KV_WIDTH), lambda b, i: (b, col // NSA_KV_WIDTH))

    def cmp_spec(which, kv):
        return pl.BlockSpec((1, N_CMP_PAD, HEAD_DIM), lambda b, i: (which, kv * batch + b, 0))

    return pl.pallas_call(
        functools.partial(_nsa_kernel, tq=tq),
        grid=(batch, nq),
        in_specs=[
            pl.BlockSpec((tq, NSA_WIDTH), lambda b, i: (b * nq + i, 0)),
            cmp_spec(0, 0), cmp_spec(1, 0), cmp_spec(0, 1), cmp_spec(1, 1),
            kv_spec(COL_KS), kv_spec(COL_VS), kv_spec(COL_KW), kv_spec(COL_VW),
            pl.BlockSpec((tq, HEAD_DIM), lambda b, i: (b * nq + i, COL_GATE // HEAD_DIM)),
            pl.BlockSpec((N_SEL, N_CMP_PAD), lambda b, i: (0, 0)),
        ],
        out_specs=pl.BlockSpec((tq, NSA_WIDTH), lambda b, i: (b * nq + i, 0)),
        out_shape=jax.ShapeDtypeStruct((t, NSA_WIDTH), F32),
        scratch_shapes=[pltpu.VMEM((N_NSA_KV, N_SEL, NSA_GROUP * tq), F32)],
        compiler_params=_params(("parallel", "arbitrary"), est),
        name="nsa_attn",
    )(proj, cmp_kv, cmp_kv, cmp_kv, cmp_kv, proj, proj, proj, proj, proj, ovt)


def _diff_kernel(q_ref, k_ref, v_ref, lq1_ref, lk1_ref, lq2_ref, lk2_ref, sub_ref, o_ref,
                 *, tq, lambda_init):
    qi = pl.program_id(2)
    wide = 2 * HEAD_DIM
    lam = (jnp.exp(jnp.sum(lq1_ref[...] * lk1_ref[...], axis=-1, keepdims=True))
           - jnp.exp(jnp.sum(lq2_ref[...] * lk2_ref[...], axis=-1, keepdims=True)) + lambda_init)
    q = q_ref[...]

    def chain(h):
        q1 = q[:, h * wide:h * wide + HEAD_DIM]
        q2 = q[:, h * wide + HEAD_DIM:(h + 1) * wide]

        def scores(kt):
            kk = k_ref[pl.ds(pl.multiple_of(kt * tq, tq), tq), h * wide:(h + 1) * wide]
            s1 = lax.dot_general(kk[:, :HEAD_DIM], q1, _NT, preferred_element_type=F32)
            s2 = lax.dot_general(kk[:, HEAD_DIM:], q2, _NT, preferred_element_type=F32)
            return jnp.concatenate([s1, s2], axis=1)

        def values(kt):
            return v_ref[pl.ds(pl.multiple_of(kt * tq, tq), tq), h * wide:(h + 1) * wide]

        return scores, values

    k_local = lax.broadcasted_iota(jnp.int32, (tq, 2 * tq), 0)
    q_local = lax.broadcasted_iota(jnp.int32, (1, 2 * tq), 1) & (tq - 1)
    outs = _causal_flash([chain(h) for h in range(DIFF_HEADS_PER_STEP)], qi, k_local <= q_local, wide, 2 * tq)
    for h, o_n in enumerate(outs):
        o_t = o_n[:, :tq] - lam * o_n[:, tq:]
        o = _rmsnorm(o_t.T, sub_ref[...]) * (1.0 - lambda_init)
        o_ref[:, h * wide:(h + 1) * wide] = o.astype(o_ref.dtype)


DIFF_HEADS_PER_STEP = 2


def _diff_attention(proj, lq1, lk1, lq2, lk2, subln, batch, lambda_init, *, tq=512):
    t = proj.shape[0]
    nq = SEQ // tq
    wide = 2 * HEAD_DIM
    step_cols = DIFF_HEADS_PER_STEP * wide
    est = 2 * 2 * SEQ * step_cols * 2 + 4 * tq * step_cols * 4 + DIFF_HEADS_PER_STEP * 16 * tq * 2 * tq * 4
    vec = pl.BlockSpec((1, HEAD_DIM), lambda b, h, i: (0, 0))
    return pl.pallas_call(
        functools.partial(_diff_kernel, tq=tq, lambda_init=lambda_init),
        grid=(batch, N_DIFF_HEADS // DIFF_HEADS_PER_STEP, nq),
        in_specs=[
            pl.BlockSpec((tq, step_cols), lambda b, h, i: (b * nq + i, COL_DQ // step_cols + h)),
            pl.BlockSpec((SEQ, step_cols), lambda b, h, i: (b, COL_DK // step_cols + h)),
            pl.BlockSpec((SEQ, step_cols), lambda b, h, i: (b, COL_DV // step_cols + h)),
            vec, vec, vec, vec,
            pl.BlockSpec((1, wide), lambda b, h, i: (0, 0)),
        ],
        out_specs=pl.BlockSpec((tq, step_cols), lambda b, h, i: (b * nq + i, h)),
        out_shape=jax.ShapeDtypeStruct((t, DIFF_WIDTH), BF16),
        compiler_params=_params(("parallel", "parallel", "arbitrary"), est),
        name="diff_attn",
    )(proj, proj, proj, lq1, lk1, lq2, lk2, subln)


def _oproj_kernel(yn_ref, g_ref, yd_ref, wo_ref, x_ref, o_ref):
    yn = _rmsnorm(yn_ref[...], g_ref[...]).astype(BF16)
    acc = jnp.dot(yn, wo_ref[:NSA_WIDTH, :], preferred_element_type=F32)
    acc = acc + jnp.dot(yd_ref[...], wo_ref[NSA_WIDTH:, :], preferred_element_type=F32)
    o_ref[...] = x_ref[...] + acc


def _out_proj(y_nsa, g, y_diff, w_o, x2, *, tm=512):
    t = x2.shape[0]
    est = (2 * (NSA_WIDTH + DIFF_WIDTH) * D_MODEL * 2 + 2 * tm * NSA_WIDTH * 4 + 2 * tm * DIFF_WIDTH * 2
           + 5 * tm * D_MODEL * 4)
    return pl.pallas_call(
        _oproj_kernel,
        grid=(t // tm,),
        in_specs=[
            pl.BlockSpec((tm, NSA_WIDTH), lambda i: (i, 0)),
            pl.BlockSpec((1, NSA_WIDTH), lambda i: (0, 0)),
            pl.BlockSpec((tm, DIFF_WIDTH), lambda i: (i, 0)),
            pl.BlockSpec((NSA_WIDTH + DIFF_WIDTH, D_MODEL), lambda i: (0, 0)),
            pl.BlockSpec((tm, D_MODEL), lambda i: (i, 0)),
        ],
        out_specs=pl.BlockSpec((tm, D_MODEL), lambda i: (i, 0)),
        out_shape=jax.ShapeDtypeStruct((t, D_MODEL), F32),
        compiler_params=_params(("parallel",), est),
        name="out_proj",
    )(y_nsa, g, y_diff, w_o, x2)


HALO = 8
FFN_CHUNK = 256
FFN_ROW_BLOCKS = 4


def _ffn_kernel(h_ref, halo_ref, g_ref, wu_ref, wg_ref, cwu_ref, cwg_ref, cbu_ref, cbg_ref, wd_ref,
                o_ref, xn_ref, raw_ref, act_ref, *, tm):
    i = pl.program_id(0)
    j = pl.program_id(1)

    @pl.when(j == 0)
    def _():
        keep = jnp.where((i * tm) % SEQ == 0, 0.0, 1.0)
        h = h_ref[...]
        xn_ref[0:HALO, :] = (_rmsnorm(halo_ref[...], g_ref[...]) * keep).astype(BF16)
        xn_ref[HALO:, :] = _rmsnorm(h, g_ref[...]).astype(BF16)
        o_ref[...] = h

    xn = xn_ref[...]

    def conv(h, cw_ref, cb_ref, cols):
        cw = cw_ref[:, cols]
        h3 = h.reshape(h.shape[0] // SUBLANES, SUBLANES, h.shape[1])
        sub = lax.broadcasted_iota(jnp.int32, h3.shape[1:], 0)
        out = cw[CONV_W - 1:CONV_W, :] * h3[1:] + cb_ref[:, cols]
        for back in range(1, CONV_W):
            rolled = pltpu.roll(h3, back, 1)
            shifted = jnp.where(sub < back, rolled[:-1], rolled[1:])
            out = out + cw[CONV_W - 1 - back:CONV_W - back, :] * shifted
        return out.reshape(h.shape[0] - HALO, h.shape[1])

    def gated(rows, cols):
        src = slice(rows.start, rows.stop + HALO)
        u = conv(raw_ref[slot, 0, src, cols], cwu_ref, cbu_ref, cols)
        gate = conv(raw_ref[slot, 1, src, cols], cwg_ref, cbg_ref, cols)
        act_ref[rows, cols] = (jax.nn.silu(gate) * u).astype(BF16)

    slot = j % 2
    chunks = [slice(lo, lo + FFN_CHUNK) for lo in range(0, wu_ref.shape[1], FFN_CHUNK)]
    row_blocks = [slice(lo, lo + tm // FFN_ROW_BLOCKS) for lo in range(0, tm, tm // FFN_ROW_BLOCKS)]
    for cols in chunks:
        raw_ref[slot, 0, :, cols] = jnp.dot(xn, wu_ref[:, cols], preferred_element_type=F32)
        raw_ref[slot, 1, :, cols] = jnp.dot(xn, wg_ref[:, cols], preferred_element_type=F32)
        if cols is not chunks[-1]:
            gated(slice(0, tm), cols)
    gated(row_blocks[0], chunks[-1])
    for n, rows in enumerate(row_blocks):
        if n + 1 < len(row_blocks):
            gated(row_blocks[n + 1], chunks[-1])
        o_ref[rows, :] += jnp.dot(act_ref[rows, :], wd_ref[...], preferred_element_type=F32)


def _conv_ffn(h1, g, w_up, conv_w, conv_b, w_down, *, tm=1024, tf=512):
    t = h1.shape[0]
    assert t % tm == 0 and SEQ % tm == 0 and D_FF % tf == 0
    nf = D_FF // tf
    est = (4 * tm * D_MODEL * 4 + 2 * 3 * D_MODEL * tf * 2 + (tm + HALO) * D_MODEL * 2
           + 8 * (tm + HALO) * tf * 4)
    return pl.pallas_call(
        functools.partial(_ffn_kernel, tm=tm),
        grid=(t // tm, nf),
        in_specs=[
            pl.BlockSpec((tm, D_MODEL), lambda i, j: (i, 0)),
            pl.BlockSpec((HALO, D_MODEL), lambda i, j: (jnp.maximum(i * (tm // HALO) - 1, 0), 0)),
            pl.BlockSpec((1, D_MODEL), lambda i, j: (0, 0)),
            pl.BlockSpec((D_MODEL, tf), lambda i, j: (0, j)),
            pl.BlockSpec((D_MODEL, tf), lambda i, j: (0, nf + j)),
            pl.BlockSpec((CONV_W, tf), lambda i, j: (0, j)),
            pl.BlockSpec((CONV_W, tf), lambda i, j: (0, nf + j)),
            pl.BlockSpec((1, tf), lambda i, j: (0, j)),
            pl.BlockSpec((1, tf), lambda i, j: (0, nf + j)),
            pl.BlockSpec((tf, D_MODEL), lambda i, j: (j, 0)),
        ],
        out_specs=pl.BlockSpec((tm, D_MODEL), lambda i, j: (i, 0), pipeline_mode=pl.Buffered(1)),
        out_shape=jax.ShapeDtypeStruct((t, D_MODEL), F32),
        scratch_shapes=[pltpu.VMEM((tm + HALO, D_MODEL), BF16), pltpu.VMEM((2, 2, tm + HALO, tf), F32),
                        pltpu.VMEM((tm, tf), BF16)],
        compiler_params=_params(("parallel", "arbitrary"), est),
        name="conv_ffn",
    )(h1, h1, g, w_up, w_up, conv_w, conv_w, conv_b, conv_b, w_down)


PLE_CHUNK = 512


def _ple_kernel(h_ref, gp_ref, wg_ref, p_ref, wp_ref, gf_ref, o_ref):
    hn = _rmsnorm(h_ref[...], gp_ref[...]).astype(BF16)
    pb = p_ref[...].astype(BF16)
    ssq = None
    for lo in range(0, D_MODEL, PLE_CHUNK):
        cols = slice(lo, lo + PLE_CHUNK)
        gate = jax.nn.sigmoid(jnp.dot(hn, wg_ref[:, cols], preferred_element_type=F32))
        emb = jnp.dot(pb, wp_ref[:, cols], preferred_element_type=F32)
        h3 = h_ref[:, cols] + gate * emb
        o_ref[:, cols] = h3
        part = jnp.sum(h3 * h3, axis=-1, keepdims=True)
        ssq = part if ssq is None else ssq + part
    o_ref[...] = o_ref[...] * lax.rsqrt(ssq * (1.0 / D_MODEL) + EPS) * gf_ref[...]


def _ple_out(h2, g_ple, w_gate, p2, w_proj, g_final, *, tm=256):
    t = h2.shape[0]
    est = 2 * D_MODEL * D_MODEL * 2 + 2 * PLE_DIM * D_MODEL * 2 + 4 * tm * D_MODEL * 4 + 6 * tm * D_MODEL * 4
    return pl.pallas_call(
        _ple_kernel,
        grid=(t // tm,),
        in_specs=[
            pl.BlockSpec((tm, D_MODEL), lambda i: (i, 0)),
            pl.BlockSpec((1, D_MODEL), lambda i: (0, 0)),
            pl.BlockSpec((D_MODEL, D_MODEL), lambda i: (0, 0)),
            pl.BlockSpec((tm, PLE_DIM), lambda i: (i, 0)),
            pl.BlockSpec((PLE_DIM, D_MODEL), lambda i: (0, 0)),
            pl.BlockSpec((1, D_MODEL), lambda i: (0, 0)),
        ],
        out_specs=pl.BlockSpec((tm, D_MODEL), lambda i: (i, 0)),
        out_shape=jax.ShapeDtypeStruct((t, D_MODEL), F32),
        compiler_params=_params(("parallel",), est),
        name="ple_out",
    )(h2, g_ple, w_gate, p2, w_proj, g_final)


def _split_w_in(w):
    gate_lo = NSA_WIDTH + 6 * NSA_KV_WIDTH
    gate = w[:, gate_lo:gate_lo + N_GATES]
    gate = gate.reshape(D_MODEL, 3, N_NSA_KV, NSA_GROUP).transpose(0, 2, 1, 3).reshape(D_MODEL, N_NSA_KV, 3 * NSA_GROUP)
    gate = jnp.pad(gate, ((0, 0), (0, 0), (0, GATE_STRIDE - 3 * NSA_GROUP))).reshape(D_MODEL, N_NSA_KV * GATE_STRIDE)
    gate = jnp.pad(gate, ((0, 0), (0, HEAD_DIM - N_NSA_KV * GATE_STRIDE)))
    return w[:, :gate_lo].astype(BF16), gate.astype(BF16), w[:, gate_lo + N_GATES:].astype(BF16)


def _rope_tables():
    inv = 1.0 / (ROPE_THETA ** (jnp.arange(0, ROPE_DIM, 2, dtype=F32) / ROPE_DIM))
    ang = jnp.arange(SEQ, dtype=F32)[:, None] * inv[None, :]
    cos, sin = jnp.cos(ang), jnp.sin(ang)
    rest = HEAD_DIM - ROPE_DIM
    cos_t = jnp.concatenate([cos, cos, jnp.ones((SEQ, rest), F32)], axis=1)
    sin_t = jnp.concatenate([-sin, sin, jnp.zeros((SEQ, rest), F32)], axis=1)
    return cos_t, sin_t


def _overlap_t():
    cmp_starts = np.arange(N_CMP) * CMP_STRIDE
    sel_starts = np.arange(N_SEL) * SLC_LEN
    ov = np.clip(np.minimum(cmp_starts[:, None] + CMP_LEN, sel_starts[None, :] + SLC_LEN)
                 - np.maximum(cmp_starts[:, None], sel_starts[None, :]), 0, None).astype(np.float32) / CMP_LEN
    ovt = np.zeros((N_SEL, N_CMP_PAD), np.float32)
    ovt[:, :N_CMP] = ov.T
    return jnp.asarray(ovt, BF16)


def kernel(x, p, attn_norm, w_in, cmp_k_pos, cmp_k_w1, cmp_k_w2, cmp_v_pos, cmp_v_w1, cmp_v_w2, nsa_out_norm, diff_lq1, diff_lk1, diff_lq2, diff_lk2, diff_subln, w_o, ffn_norm, w_up, conv_w, conv_b, w_down, ple_norm, w_ple_gate, w_ple_proj, final_norm):
    batch, seq, _ = x.shape
    assert seq == SEQ and p.shape[0] == 1
    t = batch * seq
    layer = 0
    lambda_init = 0.8 - 0.6 * math.exp(-0.3 * layer)
    x2 = x.reshape(t, D_MODEL)
    cos_t, sin_t = _rope_tables()

    proj, hkv = _in_proj(x2, attn_norm[layer][None], *_split_w_in(w_in[layer]), cos_t, sin_t)

    hkv = hkv.reshape(2, N_NSA_KV * t // HALF_BLOCK, HALF_BLOCK * HEAD_DIM)
    w1 = jnp.stack([cmp_k_w1[layer], cmp_v_w1[layer]]).astype(BF16)
    w2 = jnp.stack([cmp_k_w2[layer], cmp_v_w2[layer]]).astype(BF16)
    pos = jnp.stack([cmp_k_pos[layer], cmp_v_pos[layer]]).reshape(2, 1, CMP_LEN * HEAD_DIM)
    pos = jnp.broadcast_to(pos, (2, 8, CMP_LEN * HEAD_DIM)).astype(BF16)
    cmp_kv = _compress(hkv, w1, pos, w2)

    y_nsa = _nsa_attention(proj, cmp_kv, _overlap_t(), batch)
    y_diff = _diff_attention(proj, diff_lq1[layer][None], diff_lk1[layer][None], diff_lq2[layer][None],
                             diff_lk2[layer][None], diff_subln[layer][None], batch, lambda_init)
    h1 = _out_proj(y_nsa, nsa_out_norm[layer][None], y_diff, w_o[layer].astype(BF16), x2)
    h2 = _conv_ffn(h1, ffn_norm[layer][None], w_up[layer].astype(BF16), conv_w[layer], conv_b[layer][None],
                   w_down[layer].astype(BF16))
    out = _ple_out(h2, ple_norm[layer][None], w_ple_gate[layer].astype(BF16), p[layer].reshape(t, PLE_DIM),
                   w_ple_proj[layer].astype(BF16), final_norm[None])
    return out.reshape(batch, seq, D_MODEL)
```

```python
import functools
import math

import numpy as np
import jax
import jax.numpy as jnp
from jax import lax
from jax.experimental import pallas as pl
from jax.experimental.pallas import tpu as pltpu

D_MODEL = 2048
SEQ = 2048
HEAD_DIM = 128
ROPE_DIM = HEAD_DIM // 4
ROPE_THETA = 500000.0
N_NSA_HEADS = 8
N_NSA_KV = 2
NSA_GROUP = N_NSA_HEADS // N_NSA_KV
CMP_LEN = 32
CMP_STRIDE = 16
CMP_HIDDEN = 256
SLC_LEN = 64
SLC_TOP = 16
WINDOW = 512
N_DIFF_HEADS = 4
D_FF = 5632
CONV_W = 3
PLE_DIM = 256
EPS = 1e-6

NSA_WIDTH = N_NSA_HEADS * HEAD_DIM
NSA_KV_WIDTH = N_NSA_KV * HEAD_DIM
DIFF_WIDTH = N_DIFF_HEADS * 2 * HEAD_DIM
N_GATES = 3 * N_NSA_HEADS
N_CMP = (SEQ - CMP_LEN) // CMP_STRIDE + 1
N_CMP_PAD = SEQ // CMP_STRIDE
N_SEL = SEQ // SLC_LEN

COL_NQ = 0
COL_DQ = COL_NQ + NSA_WIDTH
COL_DK = COL_DQ + DIFF_WIDTH
COL_DV = COL_DK + DIFF_WIDTH
COL_KS = COL_DV + DIFF_WIDTH
COL_KW = COL_KS + NSA_KV_WIDTH
COL_VS = COL_KW + NSA_KV_WIDTH
COL_VW = COL_VS + NSA_KV_WIDTH
COL_GATE = COL_VW + NSA_KV_WIDTH
PROJ_COLS = COL_GATE + HEAD_DIM

V7X_LANES = 128
SUBLANES = 8
V7X_VMEM_REQUEST_CAP = 56 * 1024 * 1024
NEG = -1e30
QSCALE = HEAD_DIM ** -0.5 * math.log2(math.e)

_NT = (((1,), (1,)), ((), ()))
_TN = (((0,), (0,)), ((), ()))
BF16 = jnp.bfloat16
F32 = jnp.float32


def _params(semantics, vmem_estimate_bytes, flags=None):
    limit = min(max(int(vmem_estimate_bytes), 32 * 1024 * 1024), V7X_VMEM_REQUEST_CAP)
    return pltpu.CompilerParams(dimension_semantics=semantics, vmem_limit_bytes=limit, flags=flags)


def _rmsnorm(x, g):
    return x * lax.rsqrt(jnp.mean(x * x, axis=-1, keepdims=True) + EPS) * g


INPROJ_CHUNK = 1024
HALF_BLOCK = CMP_STRIDE

_HEADS_A = ([(COL_NQ + h * HEAD_DIM, True, True) for h in range(N_NSA_HEADS)]
            + [(("cmp", 0, h), True, False) for h in range(N_NSA_KV)]
            + [(("cmp", 1, h), False, False) for h in range(N_NSA_KV)]
            + [(COL_KS + h * HEAD_DIM, True, False) for h in range(N_NSA_KV)]
            + [(COL_VS + h * HEAD_DIM, False, False) for h in range(N_NSA_KV)]
            + [(COL_KW + h * HEAD_DIM, True, False) for h in range(N_NSA_KV)]
            + [(COL_VW + h * HEAD_DIM, False, False) for h in range(N_NSA_KV)])
_HEADS_B = ([(COL_DQ + h * HEAD_DIM, True, True) for h in range(2 * N_DIFF_HEADS)]
            + [(COL_DK + h * HEAD_DIM, True, False) for h in range(2 * N_DIFF_HEADS)]
            + [(COL_DV + h * HEAD_DIM, False, False) for h in range(2 * N_DIFF_HEADS)])
_HEADS_GATE = [(COL_GATE, False, False)]


def _inproj_kernel(x_ref, g_ref, wa_ref, wg_ref, wb_ref, cos_ref, sin_ref, o_ref, hkv_ref, stage_ref):
    tm = x_ref.shape[0]
    xn = _rmsnorm(x_ref[...], g_ref[...]).astype(BF16)
    c = cos_ref[...]
    s = sin_ref[...]
    cq = c * QSCALE
    sq = s * QSCALE
    first_half = lax.broadcasted_iota(jnp.int32, c.shape, 1) < ROPE_DIM // 2
    n_staged = 0
    for w_ref, heads in ((wa_ref, _HEADS_A), (wg_ref, _HEADS_GATE), (wb_ref, _HEADS_B)):
        for lo in range(0, w_ref.shape[1], INPROJ_CHUNK):
            hi = min(lo + INPROJ_CHUNK, w_ref.shape[1])
            acc = jnp.dot(xn, w_ref[:, lo:hi], preferred_element_type=F32)
            for src in range(lo, hi, HEAD_DIM):
                dest, rope, is_query = heads[src // HEAD_DIM]
                a = acc[:, src - lo:src - lo + HEAD_DIM]
                if rope:
                    partner = jnp.where(first_half,
                                        pltpu.roll(a, HEAD_DIM - ROPE_DIM // 2, 1),
                                        pltpu.roll(a, ROPE_DIM // 2, 1))
                    a = a * cq + partner * sq if is_query else a * c + partner * s
                if isinstance(dest, tuple):
                    _, which, head = dest
                    stage = stage_ref.at[n_staged]
                    n_staged += 1
                    stage[...] = a
                    for r in range(HALF_BLOCK):
                        rows = stage[pl.ds(r, tm // HALF_BLOCK, stride=HALF_BLOCK), :]
                        hkv_ref[which, head, :, r * HEAD_DIM:(r + 1) * HEAD_DIM] = rows.astype(hkv_ref.dtype)
                else:
                    o_ref[:, dest:dest + HEAD_DIM] = a.astype(o_ref.dtype)


def _in_proj(x2, g, w, w_b, cos_t, sin_t, *, tm=256):
    t = x2.shape[0]
    assert t % tm == 0 and SEQ % tm == 0 and tm % (HALF_BLOCK * 16) == 0
    seq_tiles = SEQ // tm
    nsa_cols = NSA_WIDTH + 6 * NSA_KV_WIDTH
    n_w = nsa_cols + HEAD_DIM + w_b.shape[1]
    est = (D_MODEL * n_w * 2 + 2 * tm * D_MODEL * 4 + 2 * tm * n_w * 2 + tm * D_MODEL * 2
           + 3 * tm * INPROJ_CHUNK * 4)
    resident = dict(pipeline_mode=pl.Buffered(1))
    half_cols = HALF_BLOCK * HEAD_DIM
    return pl.pallas_call(
        _inproj_kernel,
        grid=(t // tm,),
        in_specs=[
            pl.BlockSpec((tm, D_MODEL), lambda i: (i, 0)),
            pl.BlockSpec((1, D_MODEL), lambda i: (0, 0)),
            pl.BlockSpec((D_MODEL, nsa_cols), lambda i: (0, 0), **resident),
            pl.BlockSpec((D_MODEL, HEAD_DIM), lambda i: (0, nsa_cols // HEAD_DIM), **resident),
            pl.BlockSpec(w_b.shape, lambda i: (0, 0), **resident),
            pl.BlockSpec((tm, HEAD_DIM), lambda i: (i % seq_tiles, 0)),
            pl.BlockSpec((tm, HEAD_DIM), lambda i: (i % seq_tiles, 0)),
        ],
        out_specs=[
            pl.BlockSpec((tm, PROJ_COLS), lambda i: (i, 0)),
            pl.BlockSpec((2, N_NSA_KV, tm // HALF_BLOCK, half_cols), lambda i: (0, 0, i, 0)),
        ],
        out_shape=[
            jax.ShapeDtypeStruct((t, PROJ_COLS), BF16),
            jax.ShapeDtypeStruct((2, N_NSA_KV, t // HALF_BLOCK, half_cols), BF16),
        ],
        scratch_shapes=[pltpu.VMEM((2 * N_NSA_KV, tm, HEAD_DIM), F32)],
        compiler_params=_params(("parallel",), est),
        name="in_proj",
    )(x2, g, w, w, w_b, cos_t, sin_t)


def _compress_kernel(h_ref, w1_ref, pos_ref, w2_ref, o_ref):
    half = CMP_LEN * HEAD_DIM // 2
    h = h_ref[0]
    top = jnp.dot(h, w1_ref[0, :half, :], preferred_element_type=F32)
    bot = jnp.dot(h, w1_ref[0, half:, :], preferred_element_type=F32)
    pos_bias = jnp.dot(pos_ref[0], w1_ref[0], preferred_element_type=F32)[0:1]
    pre = top + pltpu.roll(bot, bot.shape[0] - 1, 0) + pos_bias
    act = jax.nn.gelu(pre)
    o_ref[0] = jnp.dot(act.astype(BF16), w2_ref[0], preferred_element_type=F32).astype(o_ref.dtype)


def _compress(hkv, w1, pos, w2):
    rows = hkv.shape[1]
    kdim = CMP_LEN * HEAD_DIM
    est = 2 * (rows * kdim // 2 * 2 + kdim * CMP_HIDDEN * 2) + 6 * rows * CMP_HIDDEN * 4
    return pl.pallas_call(
        _compress_kernel,
        grid=(2,),
        in_specs=[
            pl.BlockSpec((1, rows, kdim // 2), lambda i: (i, 0, 0)),
            pl.BlockSpec((1, kdim, CMP_HIDDEN), lambda i: (i, 0, 0)),
            pl.BlockSpec((1, 8, kdim), lambda i: (i, 0, 0)),
            pl.BlockSpec((1, CMP_HIDDEN, HEAD_DIM), lambda i: (i, 0, 0)),
        ],
        out_specs=pl.BlockSpec((1, rows, HEAD_DIM), lambda i: (i, 0, 0)),
        out_shape=jax.ShapeDtypeStruct((2, rows, HEAD_DIM), BF16),
        compiler_params=_params(("parallel",), est),
        name="compress",
    )(hkv, w1, pos, w2)


def _causal_flash(chains, n_before, diag_mask, d, lanes):
    def tile_step(kt, carries, mask=None):
        stats = []
        probs = []
        for (score_fn, _), (m, l, _) in zip(chains, carries):
            s = score_fn(kt)
            if mask is not None:
                s = jnp.where(mask, s, NEG)
            m_new = jnp.maximum(m, jnp.max(s, axis=0, keepdims=True))
            alpha = jnp.exp2(m - m_new)
            p = jnp.exp2(s - m_new)
            stats.append((m_new, alpha * l + jnp.sum(p, axis=0, keepdims=True), alpha))
            probs.append(p.astype(BF16))
        out = []
        for (_, value_fn), (_, _, acc), (m_new, l_new, alpha), p in zip(chains, carries, stats, probs):
            pv = lax.dot_general(value_fn(kt), p, _TN, preferred_element_type=F32)
            out.append((m_new, l_new, alpha * acc + pv))
        return tuple(out)

    init = (jnp.full((1, lanes), NEG, F32), jnp.zeros((1, lanes), F32), jnp.zeros((d, lanes), F32))
    carries = lax.fori_loop(0, n_before, tile_step, (init,) * len(chains))
    return [acc * (1.0 / l) for _, l, acc in tile_step(n_before, carries, diag_mask)]


def _nsa_kernel(q_ref, kc0_ref, vc0_ref, kc1_ref, vc1_ref, ks_ref, vs_ref, kw_ref, vw_ref, gate_ref, ovt_ref,
                o_ref, bias_ref, *, tq):
    qi = pl.program_id(1)
    start = qi * tq
    nl = NSA_GROUP * tq
    kvs = range(N_NSA_KV)
    cmp_refs = ((kc0_ref, vc0_ref), (kc1_ref, vc1_ref))

    def head_cols(kv):
        return slice(kv * HEAD_DIM, (kv + 1) * HEAD_DIM)

    q_all = q_ref[...]
    qs = [jnp.concatenate([q_all[:, (kv * NSA_GROUP + g) * HEAD_DIM:(kv * NSA_GROUP + g + 1) * HEAD_DIM]
                           for g in range(NSA_GROUP)], axis=0) for kv in kvs]
    q_local = lax.broadcasted_iota(jnp.int32, (1, nl), 1) & (tq - 1)
    t_lane = start + q_local
    k_local = lax.broadcasted_iota(jnp.int32, (tq, nl), 0)
    causal = k_local <= q_local
    c_end = lax.broadcasted_iota(jnp.int32, (N_CMP_PAD, nl), 0) * CMP_STRIDE + (CMP_LEN - 1)
    cmask = c_end <= t_lane
    j_idx = lax.broadcasted_iota(jnp.int32, (N_SEL, tq), 0)
    t_q = start + lax.broadcasted_iota(jnp.int32, (N_SEL, tq), 1)
    cur = t_q // SLC_LEN
    forced = (j_idx == 0) | (j_idx == cur) | (j_idx == cur - 1)
    ovt = ovt_ref[...]

    o_cmp = []
    for kv in kvs:
        kc_ref, vc_ref = cmp_refs[kv]
        s = lax.dot_general(kc_ref[0], qs[kv], _NT, preferred_element_type=F32)
        s = jnp.where(cmask, s, NEG)
        m = jnp.max(s, axis=0, keepdims=True)
        e = jnp.where(cmask, jnp.exp2(s - m), 0.0)
        l = jnp.sum(e, axis=0, keepdims=True)
        p_cmp = e * jnp.where(l > 0.0, 1.0 / l, 0.0)
        o_cmp.append(lax.dot_general(vc_ref[0], p_cmp.astype(BF16), _TN, preferred_element_type=F32))

        p_sum = p_cmp[:, 0:tq]
        for g in range(1, NSA_GROUP):
            p_sum = p_sum + p_cmp[:, g * tq:(g + 1) * tq]
        p_hi = p_sum.astype(BF16)
        p_lo = (p_sum - p_hi.astype(F32)).astype(BF16)
        p_slc = (jnp.dot(ovt, p_hi, preferred_element_type=F32)
                 + jnp.dot(ovt, p_lo, preferred_element_type=F32))
        score = jnp.where(forced, 1e4, jnp.where(j_idx > cur, -1e4, p_slc))
        rank = jnp.zeros((N_SEL, tq), jnp.int32)
        for i in range(N_SEL):
            row = score[i:i + 1, :]
            tie = jnp.where(j_idx > i, 1, 0)
            rank = rank + jnp.where(row > score, 1, jnp.where(row == score, tie, 0))
        bias = jnp.where((rank < SLC_TOP) & (j_idx <= cur), 0.0, NEG)
        bias_ref[kv] = jnp.concatenate([bias] * NSA_GROUP, axis=1)

    blocks_per_tile = tq // SLC_LEN

    def slc_chain(kv):
        def scores(kt):
            k0 = pl.multiple_of(kt * tq, tq)
            s = lax.dot_general(ks_ref[pl.ds(k0, tq), head_cols(kv)], qs[kv], _NT, preferred_element_type=F32)
            parts = [s[c * SLC_LEN:(c + 1) * SLC_LEN, :] + bias_ref[kv, pl.ds(kt * blocks_per_tile + c, 1), :]
                     for c in range(blocks_per_tile)]
            return jnp.concatenate(parts, axis=0)

        def values(kt):
            return vs_ref[pl.ds(pl.multiple_of(kt * tq, tq), tq), head_cols(kv)]

        return scores, values

    o_slc = _causal_flash([slc_chain(kv) for kv in kvs], qi, causal, HEAD_DIM, nl)

    win_p, win_l, win_v = [], [], []
    for kv in kvs:
        scores, values = [], []
        for back in range(WINDOW // tq, -1, -1):
            k0 = start - back * tq
            inside = k0 >= 0
            k0 = pl.multiple_of(jnp.maximum(k0, 0), tq)
            s = lax.dot_general(kw_ref[pl.ds(k0, tq), head_cols(kv)], qs[kv], _NT, preferred_element_type=F32)
            if back == 0:
                s = jnp.where(causal, s, NEG)
            elif back == WINDOW // tq:
                edge = q_local + jnp.where(inside, 0, tq)
                s = jnp.where(k_local > edge, s, NEG)
            else:
                s = s + jnp.where(inside, 0.0, NEG)
            scores.append(s)
            values.append(vw_ref[pl.ds(k0, tq), head_cols(kv)])
        s = jnp.concatenate(scores, axis=0)
        p = jnp.exp2(s - jnp.max(s, axis=0, keepdims=True))
        win_l.append(jnp.sum(p, axis=0, keepdims=True))
        win_p.append(p.astype(BF16))
        win_v.append(jnp.concatenate(values, axis=0))

    gates_t = jax.nn.sigmoid(gate_ref[...].astype(F32)).T
    for kv in kvs:
        acc_w = lax.dot_general(win_v[kv], win_p[kv], _TN, preferred_element_type=F32)
        o_win = acc_w * (1.0 / win_l[kv])

        for g in range(NSA_GROUP):
            sl = slice(g * tq, (g + 1) * tq)
            head = kv * NSA_GROUP + g
            gate_c, gate_s, gate_w = (gates_t[br * N_NSA_HEADS + head:br * N_NSA_HEADS + head + 1, :]
                                      for br in range(3))
            o_t = gate_c * o_cmp[kv][:, sl] + gate_s * o_slc[kv][:, sl] + gate_w * o_win[:, sl]
            o_ref[:, head * HEAD_DIM:(head + 1) * HEAD_DIM] = o_t.T


def _nsa_attention(proj, cmp_kv, ovt, batch, *, tq=256):
    t = proj.shape[0]
    assert WINDOW % tq == 0 and tq % SLC_LEN == 0 and SEQ % tq == 0
    nq = SEQ // tq
    est = (4 * 2 * SEQ * NSA_KV_WIDTH * 2 + 2 * tq * NSA_WIDTH * (2 + 4)
           + N_NSA_KV * 10 * (WINDOW + tq) * NSA_GROUP * tq * 4)

    def kv_spec(col):
        return pl.BlockSpec((SEQ, NSA_KV_WIDTH), lambda b, i: (b, col // NSA_KV_WIDTH))

    def cmp_spec(which, kv):
        return pl.BlockSpec((1, N_CMP_PAD, HEAD_DIM), lambda b, i: (which, kv * batch + b, 0))

    return pl.pallas_call(
        functools.partial(_nsa_kernel, tq=tq),
        grid=(batch, nq),
        in_specs=[
            pl.BlockSpec((tq, NSA_WIDTH), lambda b, i: (b * nq + i, 0)),
            cmp_spec(0, 0), cmp_spec(1, 0), cmp_spec(0, 1), cmp_spec(1, 1),
            kv_spec(COL_KS), kv_spec(COL_VS), kv_spec(COL_KW), kv_spec(COL_VW),
            pl.BlockSpec((tq, HEAD_DIM), lambda b, i: (b * nq + i, COL_GATE // HEAD_DIM)),
            pl.BlockSpec((N_SEL, N_CMP_PAD), lambda b, i: (0, 0)),
        ],
        out_specs=pl.BlockSpec((tq, NSA_WIDTH), lambda b, i: (b * nq + i, 0)),
        out_shape=jax.ShapeDtypeStruct((t, NSA_WIDTH), F32),
        scratch_shapes=[pltpu.VMEM((N_NSA_KV, N_SEL, NSA_GROUP * tq), F32)],
        compiler_params=_params(("parallel", "arbitrary"), est),
        name="nsa_attn",
    )(proj, cmp_kv, cmp_kv, cmp_kv, cmp_kv, proj, proj, proj, proj, proj, ovt)


def _diff_kernel(q_ref, k_ref, v_ref, lq1_ref, lk1_ref, lq2_ref, lk2_ref, sub_ref, o_ref,
                 *, tq, lambda_init):
    qi = pl.program_id(2)
    wide = 2 * HEAD_DIM
    lam = (jnp.exp(jnp.sum(lq1_ref[...] * lk1_ref[...], axis=-1, keepdims=True))
           - jnp.exp(jnp.sum(lq2_ref[...] * lk2_ref[...], axis=-1, keepdims=True)) + lambda_init)
    q = q_ref[...]

    def chain(h):
        q1 = q[:, h * wide:h * wide + HEAD_DIM]
        q2 = q[:, h * wide + HEAD_DIM:(h + 1) * wide]

        def scores(kt):
            kk = k_ref[pl.ds(pl.multiple_of(kt * tq, tq), tq), h * wide:(h + 1) * wide]
            s1 = lax.dot_general(kk[:, :HEAD_DIM], q1, _NT, preferred_element_type=F32)
            s2 = lax.dot_general(kk[:, HEAD_DIM:], q2, _NT, preferred_element_type=F32)
            return jnp.concatenate([s1, s2], axis=1)

        def values(kt):
            return v_ref[pl.ds(pl.multiple_of(kt * tq, tq), tq), h * wide:(h + 1) * wide]

        return scores, values

    k_local = lax.broadcasted_iota(jnp.int32, (tq, 2 * tq), 0)
    q_local = lax.broadcasted_iota(jnp.int32, (1, 2 * tq), 1) & (tq - 1)
    outs = _causal_flash([chain(h) for h in range(DIFF_HEADS_PER_STEP)], qi, k_local <= q_local, wide, 2 * tq)
    for h, o_n in enumerate(outs):
        o_t = o_n[:, :tq] - lam * o_n[:, tq:]
        o = _rmsnorm(o_t.T, sub_ref[...]) * (1.0 - lambda_init)
        o_ref[:, h * wide:(h + 1) * wide] = o.astype(o_ref.dtype)


DIFF_HEADS_PER_STEP = 4


def _diff_attention(proj, lq1, lk1, lq2, lk2, subln, batch, lambda_init, *, tq=512):
    t = proj.shape[0]
    nq = SEQ // tq
    wide = 2 * HEAD_DIM
    step_cols = DIFF_HEADS_PER_STEP * wide
    assert COL_DQ % step_cols == 0 and COL_DK % step_cols == 0 and COL_DV % step_cols == 0
    est = 2 * 2 * SEQ * step_cols * 2 + 4 * tq * step_cols * 4 + DIFF_HEADS_PER_STEP * 16 * tq * 2 * tq * 4
    vec = pl.BlockSpec((1, HEAD_DIM), lambda b, h, i: (0, 0))
    return pl.pallas_call(
        functools.partial(_diff_kernel, tq=tq, lambda_init=lambda_init),
        grid=(batch, N_DIFF_HEADS // DIFF_HEADS_PER_STEP, nq),
        in_specs=[
            pl.BlockSpec((tq, step_cols), lambda b, h, i: (b * nq + i, COL_DQ // step_cols + h)),
            pl.BlockSpec((SEQ, step_cols), lambda b, h, i: (b, COL_DK // step_cols + h)),
            pl.BlockSpec((SEQ, step_cols), lambda b, h, i: (b, COL_DV // step_cols + h)),
            vec, vec, vec, vec,
            pl.BlockSpec((1, wide), lambda b, h, i: (0, 0)),
        ],
        out_specs=pl.BlockSpec((tq, step_cols), lambda b, h, i: (b * nq + i, h)),
        out_shape=jax.ShapeDtypeStruct((t, DIFF_WIDTH), BF16),
        compiler_params=_params(("parallel", "parallel", "arbitrary"), est),
        name="diff_attn",
    )(proj, proj, proj, lq1, lk1, lq2, lk2, subln)


def _oproj_kernel(yn_ref, g_ref, yd_ref, wo_ref, x_ref, o_ref):
    yn = _rmsnorm(yn_ref[...], g_ref[...]).astype(BF16)
    acc = jnp.dot(yn, wo_ref[:NSA_WIDTH, :], preferred_element_type=F32)
    acc = acc + jnp.dot(yd_ref[...], wo_ref[NSA_WIDTH:, :], preferred_element_type=F32)
    o_ref[...] = x_ref[...] + acc


def _out_proj(y_nsa, g, y_diff, w_o, x2, *, tm=512):
    t = x2.shape[0]
    est = (2 * (NSA_WIDTH + DIFF_WIDTH) * D_MODEL * 2 + 2 * tm * NSA_WIDTH * 4 + 2 * tm * DIFF_WIDTH * 2
           + 5 * tm * D_MODEL * 4)
    return pl.pallas_call(
        _oproj_kernel,
        grid=(t // tm,),
        in_specs=[
            pl.BlockSpec((tm, NSA_WIDTH), lambda i: (i, 0)),
            pl.BlockSpec((1, NSA_WIDTH), lambda i: (0, 0)),
            pl.BlockSpec((tm, DIFF_WIDTH), lambda i: (i, 0)),
            pl.BlockSpec((NSA_WIDTH + DIFF_WIDTH, D_MODEL), lambda i: (0, 0)),
            pl.BlockSpec((tm, D_MODEL), lambda i: (i, 0)),
        ],
        out_specs=pl.BlockSpec((tm, D_MODEL), lambda i: (i, 0)),
        out_shape=jax.ShapeDtypeStruct((t, D_MODEL), F32),
        compiler_params=_params(("parallel",), est),
        name="out_proj",
    )(y_nsa, g, y_diff, w_o, x2)


HALO = 8
FFN_CHUNK = 256
FFN_ROW_BLOCKS = 4


def _ffn_kernel(h_ref, halo_ref, g_ref, wu_ref, wg_ref, cwu_ref, cwg_ref, cbu_ref, cbg_ref, wd_ref,
                o_ref, xn_ref, raw_ref, act_ref, *, tm):
    i = pl.program_id(0)
    j = pl.program_id(1)

    @pl.when(j == 0)
    def _():
        keep = jnp.where((i * tm) % SEQ == 0, 0.0, 1.0)
        h = h_ref[...]
        xn_ref[0:HALO, :] = (_rmsnorm(halo_ref[...], g_ref[...]) * keep).astype(BF16)
        xn_ref[HALO:, :] = _rmsnorm(h, g_ref[...]).astype(BF16)
        o_ref[...] = h

    xn = xn_ref[...]

    def conv(h, cw_ref, cb_ref, cols):
        cw = cw_ref[:, cols]
        h3 = h.reshape(h.shape[0] // SUBLANES, SUBLANES, h.shape[1])
        sub = lax.broadcasted_iota(jnp.int32, h3.shape[1:], 0)
        out = cw[CONV_W - 1:CONV_W, :] * h3[1:] + cb_ref[:, cols]
        for back in range(1, CONV_W):
            rolled = pltpu.roll(h3, back, 1)
            shifted = jnp.where(sub < back, rolled[:-1], rolled[1:])
            out = out + cw[CONV_W - 1 - back:CONV_W - back, :] * shifted
        return out.reshape(h.shape[0] - HALO, h.shape[1])

    def gated(rows, cols):
        src = slice(rows.start, rows.stop + HALO)
        u = conv(raw_ref[slot, 0, src, cols], cwu_ref, cbu_ref, cols)
        gate = conv(raw_ref[slot, 1, src, cols], cwg_ref, cbg_ref, cols)
        act_ref[rows, cols] = (jax.nn.silu(gate) * u).astype(BF16)

    slot = j % 2
    chunks = [slice(lo, lo + FFN_CHUNK) for lo in range(0, wu_ref.shape[1], FFN_CHUNK)]
    row_blocks = [slice(lo, lo + tm // FFN_ROW_BLOCKS) for lo in range(0, tm, tm // FFN_ROW_BLOCKS)]
    for cols in chunks:
        raw_ref[slot, 0, :, cols] = jnp.dot(xn, wu_ref[:, cols], preferred_element_type=F32)
        raw_ref[slot, 1, :, cols] = jnp.dot(xn, wg_ref[:, cols], preferred_element_type=F32)
        if cols is not chunks[-1]:
            gated(slice(0, tm), cols)
    gated(row_blocks[0], chunks[-1])
    for n, rows in enumerate(row_blocks):
        if n + 1 < len(row_blocks):
            gated(row_blocks[n + 1], chunks[-1])
        o_ref[rows, :] += jnp.dot(act_ref[rows, :], wd_ref[...], preferred_element_type=F32)


def _conv_ffn(h1, g, w_up, conv_w, conv_b, w_down, *, tm=1024, tf=512):
    t = h1.shape[0]
    assert t % tm == 0 and SEQ % tm == 0 and D_FF % tf == 0
    nf = D_FF // tf
    est = (4 * tm * D_MODEL * 4 + 2 * 3 * D_MODEL * tf * 2 + (tm + HALO) * D_MODEL * 2
           + 8 * (tm + HALO) * tf * 4)
    return pl.pallas_call(
        functools.partial(_ffn_kernel, tm=tm),
        grid=(t // tm, nf),
        in_specs=[
            pl.BlockSpec((tm, D_MODEL), lambda i, j: (i, 0)),
            pl.BlockSpec((HALO, D_MODEL), lambda i, j: (jnp.maximum(i * (tm // HALO) - 1, 0), 0)),
            pl.BlockSpec((1, D_MODEL), lambda i, j: (0, 0)),
            pl.BlockSpec((D_MODEL, tf), lambda i, j: (0, j)),
            pl.BlockSpec((D_MODEL, tf), lambda i, j: (0, nf + j)),
            pl.BlockSpec((CONV_W, tf), lambda i, j: (0, j)),
            pl.BlockSpec((CONV_W, tf), lambda i, j: (0, nf + j)),
            pl.BlockSpec((1, tf), lambda i, j: (0, j)),
            pl.BlockSpec((1, tf), lambda i, j: (0, nf + j)),
            pl.BlockSpec((tf, D_MODEL), lambda i, j: (j, 0)),
        ],
        out_specs=pl.BlockSpec((tm, D_MODEL), lambda i, j: (i, 0), pipeline_mode=pl.Buffered(1)),
        out_shape=jax.ShapeDtypeStruct((t, D_MODEL), F32),
        scratch_shapes=[pltpu.VMEM((tm + HALO, D_MODEL), BF16), pltpu.VMEM((2, 2, tm + HALO, tf), F32),
                        pltpu.VMEM((tm, tf), BF16)],
        compiler_params=_params(("parallel", "arbitrary"), est),
        name="conv_ffn",
    )(h1, h1, g, w_up, w_up, conv_w, conv_w, conv_b, conv_b, w_down)


PLE_CHUNK = 512


def _ple_kernel(h_ref, gp_ref, wg_ref, p_ref, wp_ref, gf_ref, o_ref):
    hn = _rmsnorm(h_ref[...], gp_ref[...]).astype(BF16)
    pb = p_ref[...].astype(BF16)
    ssq = None
    for lo in range(0, D_MODEL, PLE_CHUNK):
        cols = slice(lo, lo + PLE_CHUNK)
        gate = jax.nn.sigmoid(jnp.dot(hn, wg_ref[:, cols], preferred_element_type=F32))
        emb = jnp.dot(pb, wp_ref[:, cols], preferred_element_type=F32)
        h3 = h_ref[:, cols] + gate * emb
        o_ref[:, cols] = h3
        part = jnp.sum(h3 * h3, axis=-1, keepdims=True)
        ssq = part if ssq is None else ssq + part
    o_ref[...] = o_ref[...] * lax.rsqrt(ssq * (1.0 / D_MODEL) + EPS) * gf_ref[...]


def _ple_out(h2, g_ple, w_gate, p2, w_proj, g_final, *, tm=256):
    t = h2.shape[0]
    est = 2 * D_MODEL * D_MODEL * 2 + 2 * PLE_DIM * D_MODEL * 2 + 4 * tm * D_MODEL * 4 + 6 * tm * D_MODEL * 4
    return pl.pallas_call(
        _ple_kernel,
        grid=(t // tm,),
        in_specs=[
            pl.BlockSpec((tm, D_MODEL), lambda i: (i, 0)),
            pl.BlockSpec((1, D_MODEL), lambda i: (0, 0)),
            pl.BlockSpec((D_MODEL, D_MODEL), lambda i: (0, 0)),
            pl.BlockSpec((tm, PLE_DIM), lambda i: (i, 0)),
            pl.BlockSpec((PLE_DIM, D_MODEL), lambda i: (0, 0)),
            pl.BlockSpec((1, D_MODEL), lambda i: (0, 0)),
        ],
        out_specs=pl.BlockSpec((tm, D_MODEL), lambda i: (i, 0)),
        out_shape=jax.ShapeDtypeStruct((t, D_MODEL), F32),
        compiler_params=_params(("parallel",), est),
        name="ple_out",
    )(h2, g_ple, w_gate, p2, w_proj, g_final)


def _split_w_in(w):
    w = w.astype(BF16)
    return w, w[:, NSA_WIDTH + 6 * NSA_KV_WIDTH + N_GATES:]


def _rope_tables():
    inv = 1.0 / (ROPE_THETA ** (jnp.arange(0, ROPE_DIM, 2, dtype=F32) / ROPE_DIM))
    ang = jnp.arange(SEQ, dtype=F32)[:, None] * inv[None, :]
    cos, sin = jnp.cos(ang), jnp.sin(ang)
    rest = HEAD_DIM - ROPE_DIM
    cos_t = jnp.concatenate([cos, cos, jnp.ones((SEQ, rest), F32)], axis=1)
    sin_t = jnp.concatenate([-sin, sin, jnp.zeros((SEQ, rest), F32)], axis=1)
    return cos_t, sin_t


def _overlap_t():
    cmp_starts = np.arange(N_CMP) * CMP_STRIDE
    sel_starts = np.arange(N_SEL) * SLC_LEN
    ov = np.clip(np.minimum(cmp_starts[:, None] + CMP_LEN, sel_starts[None, :] + SLC_LEN)
                 - np.maximum(cmp_starts[:, None], sel_starts[None, :]), 0, None).astype(np.float32) / CMP_LEN
    ovt = np.zeros((N_SEL, N_CMP_PAD), np.float32)
    ovt[:, :N_CMP] = ov.T
    return jnp.asarray(ovt, BF16)


def kernel(x, p, attn_norm, w_in, cmp_k_pos, cmp_k_w1, cmp_k_w2, cmp_v_pos, cmp_v_w1, cmp_v_w2, nsa_out_norm, diff_lq1, diff_lk1, diff_lq2, diff_lk2, diff_subln, w_o, ffn_norm, w_up, conv_w, conv_b, w_down, ple_norm, w_ple_gate, w_ple_proj, final_norm):
    batch, seq, _ = x.shape
    assert seq == SEQ and p.shape[0] == 1
    t = batch * seq
    layer = 0
    lambda_init = 0.8 - 0.6 * math.exp(-0.3 * layer)
    x2 = x.reshape(t, D_MODEL)
    cos_t, sin_t = _rope_tables()

    proj, hkv = _in_proj(x2, attn_norm[layer][None], *_split_w_in(w_in[layer]), cos_t, sin_t)

    hkv = hkv.reshape(2, N_NSA_KV * t // HALF_BLOCK, HALF_BLOCK * HEAD_DIM)
    w1 = jnp.stack([cmp_k_w1[layer], cmp_v_w1[layer]]).astype(BF16)
    w2 = jnp.stack([cmp_k_w2[layer], cmp_v_w2[layer]]).astype(BF16)
    pos = jnp.stack([cmp_k_pos[layer], cmp_v_pos[layer]]).reshape(2, 1, CMP_LEN * HEAD_DIM)
    pos = jnp.broadcast_to(pos, (2, 8, CMP_LEN * HEAD_DIM)).astype(BF16)
    cmp_kv = _compress(hkv, w1, pos, w2)

    y_nsa = _nsa_attention(proj, cmp_kv, _overlap_t(), batch)
    y_diff = _diff_attention(proj, diff_lq1[layer][None], diff_lk1[layer][None], diff_lq2[layer][None],
                             diff_lk2[layer][None], diff_subln[layer][None], batch, lambda_init)
    h1 = _out_proj(y_nsa, nsa_out_norm[layer][None], y_diff, w_o[layer].astype(BF16), x2)
    h2 = _conv_ffn(h1, ffn_norm[layer][None], w_up[layer].astype(BF16), conv_w[layer], conv_b[layer][None],
                   w_down[layer].astype(BF16))
    out = _ple_out(h2, ple_norm[layer][None], w_ple_gate[layer].astype(BF16), p[layer].reshape(t, PLE_DIM),
                   w_ple_proj[layer].astype(BF16), final_norm[None])
    return out.reshape(batch, seq, D_MODEL)
```

```python
import functools
import math

import numpy as np
import jax
import jax.numpy as jnp
from jax import lax
from jax.experimental import pallas as pl
from jax.experimental.pallas import tpu as pltpu

D_MODEL = 2048
SEQ = 2048
HEAD_DIM = 128
ROPE_DIM = HEAD_DIM // 4
ROPE_THETA = 500000.0
N_NSA_HEADS = 8
N_NSA_KV = 2
NSA_GROUP = N_NSA_HEADS // N_NSA_KV
CMP_LEN = 32
CMP_STRIDE = 16
CMP_HIDDEN = 256
SLC_LEN = 64
SLC_TOP = 16
WINDOW = 512
N_DIFF_HEADS = 4
D_FF = 5632
CONV_W = 3
PLE_DIM = 256
EPS = 1e-6

NSA_WIDTH = N_NSA_HEADS * HEAD_DIM
NSA_KV_WIDTH = N_NSA_KV * HEAD_DIM
DIFF_WIDTH = N_DIFF_HEADS * 2 * HEAD_DIM
N_GATES = 3 * N_NSA_HEADS
N_CMP = (SEQ - CMP_LEN) // CMP_STRIDE + 1
N_CMP_PAD = SEQ // CMP_STRIDE
N_SEL = SEQ // SLC_LEN
GATE_STRIDE = 16

COL_NQ = 0
COL_DQ = COL_NQ + NSA_WIDTH
COL_DK = COL_DQ + DIFF_WIDTH
COL_DV = COL_DK + DIFF_WIDTH
COL_KS = COL_DV + DIFF_WIDTH
COL_KW = COL_KS + NSA_KV_WIDTH
COL_VS = COL_KW + NSA_KV_WIDTH
COL_VW = COL_VS + NSA_KV_WIDTH
COL_GATE = COL_VW + NSA_KV_WIDTH
PROJ_COLS = COL_GATE + HEAD_DIM

V7X_LANES = 128
SUBLANES = 8
BF16_ROWS = 16
V7X_VMEM_REQUEST_CAP = 56 * 1024 * 1024
NEG = -1e30
QSCALE = HEAD_DIM ** -0.5 * math.log2(math.e)

_NT = (((1,), (1,)), ((), ()))
_TN = (((0,), (0,)), ((), ()))
BF16 = jnp.bfloat16
F32 = jnp.float32


def _params(semantics, vmem_estimate_bytes, flags=None):
    limit = min(max(int(vmem_estimate_bytes), 32 * 1024 * 1024), V7X_VMEM_REQUEST_CAP)
    return pltpu.CompilerParams(dimension_semantics=semantics, vmem_limit_bytes=limit, flags=flags)


def _rmsnorm(x, g):
    return x * lax.rsqrt(jnp.mean(x * x, axis=-1, keepdims=True) + EPS) * g


INPROJ_CHUNK = 1024
HALF_BLOCK = CMP_STRIDE

_HEADS_A = ([(COL_NQ + h * HEAD_DIM, True, True) for h in range(N_NSA_HEADS)]
            + [(("cmp", 0, h), True, False) for h in range(N_NSA_KV)]
            + [(("cmp", 1, h), False, False) for h in range(N_NSA_KV)]
            + [(COL_KS + h * HEAD_DIM, True, False) for h in range(N_NSA_KV)]
            + [(COL_VS + h * HEAD_DIM, False, False) for h in range(N_NSA_KV)]
            + [(COL_KW + h * HEAD_DIM, True, False) for h in range(N_NSA_KV)]
            + [(COL_VW + h * HEAD_DIM, False, False) for h in range(N_NSA_KV)])
_HEADS_B = ([(COL_DQ + h * HEAD_DIM, True, True) for h in range(2 * N_DIFF_HEADS)]
            + [(COL_DK + h * HEAD_DIM, True, False) for h in range(2 * N_DIFF_HEADS)]
            + [(COL_DV + h * HEAD_DIM, False, False) for h in range(2 * N_DIFF_HEADS)])
_HEADS_GATE = [(COL_GATE, False, False)]


def _inproj_kernel(*refs, n_cast):
    x_ref, g_ref, wa_ref, wg_ref, wb_ref, cos_ref, sin_ref = refs[:7]
    cast_in = refs[7:7 + n_cast]
    o_ref, hkv_ref = refs[7 + n_cast:9 + n_cast]
    cast_out = refs[9 + n_cast:9 + 2 * n_cast]
    stage_ref = refs[9 + 2 * n_cast]
    for src_ref, dst_ref in zip(cast_in, cast_out):
        dst_ref[...] = src_ref[...].astype(dst_ref.dtype)
    tm = x_ref.shape[0]
    xn = _rmsnorm(x_ref[...], g_ref[...]).astype(BF16)
    c = cos_ref[...]
    s = sin_ref[...]
    cq = c * QSCALE
    sq = s * QSCALE
    first_half = lax.broadcasted_iota(jnp.int32, c.shape, 1) < ROPE_DIM // 2
    n_staged = 0
    for w_ref, heads in ((wa_ref, _HEADS_A), (wg_ref, _HEADS_GATE), (wb_ref, _HEADS_B)):
        for lo in range(0, w_ref.shape[1], INPROJ_CHUNK):
            hi = min(lo + INPROJ_CHUNK, w_ref.shape[1])
            acc = jnp.dot(xn, w_ref[:, lo:hi], preferred_element_type=F32)
            for src in range(lo, hi, HEAD_DIM):
                dest, rope, is_query = heads[src // HEAD_DIM]
                a = acc[:, src - lo:src - lo + HEAD_DIM]
                if rope:
                    partner = jnp.where(first_half,
                                        pltpu.roll(a, HEAD_DIM - ROPE_DIM // 2, 1),
                                        pltpu.roll(a, ROPE_DIM // 2, 1))
                    a = a * cq + partner * sq if is_query else a * c + partner * s
                if isinstance(dest, tuple):
                    _, which, head = dest
                    stage = stage_ref.at[n_staged]
                    n_staged += 1
                    stage[...] = a
                    for r in range(HALF_BLOCK):
                        rows = stage[pl.ds(r, tm // HALF_BLOCK, stride=HALF_BLOCK), :]
                        hkv_ref[which, head, :, r * HEAD_DIM:(r + 1) * HEAD_DIM] = rows.astype(hkv_ref.dtype)
                else:
                    o_ref[:, dest:dest + HEAD_DIM] = a.astype(o_ref.dtype)


def _in_proj(x2, g, w, w_g, w_b, cos_t, sin_t, later_weights=(), *, tm=256):
    t = x2.shape[0]
    assert t % tm == 0 and SEQ % tm == 0 and tm % (HALF_BLOCK * BF16_ROWS) == 0
    n_steps = t // tm
    seq_tiles = SEQ // tm
    nsa_cols = NSA_WIDTH + 6 * NSA_KV_WIDTH
    n_w = nsa_cols + HEAD_DIM + w_b.shape[1]
    est = (D_MODEL * n_w * 2 + 2 * tm * D_MODEL * 4 + 2 * tm * n_w * 2 + tm * D_MODEL * 2
           + 3 * tm * INPROJ_CHUNK * 4)
    cast_specs = []
    for lw in later_weights:
        assert lw.shape[0] % (n_steps * BF16_ROWS) == 0
        cast_specs.append(pl.BlockSpec((lw.shape[0] // n_steps, lw.shape[1]), lambda i: (i, 0)))
        est += 2 * (lw.size // n_steps) * (4 + 2)
    resident = dict(pipeline_mode=pl.Buffered(1))
    half_cols = HALF_BLOCK * HEAD_DIM
    return pl.pallas_call(
        functools.partial(_inproj_kernel, n_cast=len(later_weights)),
        grid=(n_steps,),
        in_specs=[
            pl.BlockSpec((tm, D_MODEL), lambda i: (i, 0)),
            pl.BlockSpec((1, D_MODEL), lambda i: (0, 0)),
            pl.BlockSpec((D_MODEL, nsa_cols), lambda i: (0, 0), **resident),
            pl.BlockSpec(w_g.shape, lambda i: (0, 0), **resident),
            pl.BlockSpec(w_b.shape, lambda i: (0, 0), **resident),
            pl.BlockSpec((tm, HEAD_DIM), lambda i: (i % seq_tiles, 0)),
            pl.BlockSpec((tm, HEAD_DIM), lambda i: (i % seq_tiles, 0)),
        ] + cast_specs,
        out_specs=[
            pl.BlockSpec((tm, PROJ_COLS), lambda i: (i, 0)),
            pl.BlockSpec((2, N_NSA_KV, tm // HALF_BLOCK, half_cols), lambda i: (0, 0, i, 0)),
        ] + cast_specs,
        out_shape=[
            jax.ShapeDtypeStruct((t, PROJ_COLS), BF16),
            jax.ShapeDtypeStruct((2, N_NSA_KV, t // HALF_BLOCK, half_cols), BF16),
        ] + [jax.ShapeDtypeStruct(lw.shape, BF16) for lw in later_weights],
        scratch_shapes=[pltpu.VMEM((2 * N_NSA_KV, tm, HEAD_DIM), F32)],
        compiler_params=_params(("parallel",), est),
        name="in_proj",
    )(x2, g, w, w_g, w_b, cos_t, sin_t, *later_weights)


def _compress_kernel(h_ref, w1_ref, pos_ref, w2_ref, o_ref):
    half = CMP_LEN * HEAD_DIM // 2
    h = h_ref[0]
    top = jnp.dot(h, w1_ref[0, :half, :], preferred_element_type=F32)
    bot = jnp.dot(h, w1_ref[0, half:, :], preferred_element_type=F32)
    pos_bias = jnp.dot(pos_ref[0], w1_ref[0], preferred_element_type=F32)[0:1]
    pre = top + pltpu.roll(bot, bot.shape[0] - 1, 0) + pos_bias
    act = jax.nn.gelu(pre)
    o_ref[0] = jnp.dot(act.astype(BF16), w2_ref[0], preferred_element_type=F32).astype(o_ref.dtype)


def _compress(hkv, w1, pos, w2):
    rows = hkv.shape[1]
    kdim = CMP_LEN * HEAD_DIM
    est = 2 * (rows * kdim // 2 * 2 + kdim * CMP_HIDDEN * 2) + 6 * rows * CMP_HIDDEN * 4
    return pl.pallas_call(
        _compress_kernel,
        grid=(2,),
        in_specs=[
            pl.BlockSpec((1, rows, kdim // 2), lambda i: (i, 0, 0)),
            pl.BlockSpec((1, kdim, CMP_HIDDEN), lambda i: (i, 0, 0)),
            pl.BlockSpec((1, 8, kdim), lambda i: (i, 0, 0)),
            pl.BlockSpec((1, CMP_HIDDEN, HEAD_DIM), lambda i: (i, 0, 0)),
        ],
        out_specs=pl.BlockSpec((1, rows, HEAD_DIM), lambda i: (i, 0, 0)),
        out_shape=jax.ShapeDtypeStruct((2, rows, HEAD_DIM), BF16),
        compiler_params=_params(("parallel",), est),
        name="compress",
    )(hkv, w1, pos, w2)


def _causal_flash(chains, n_before, diag_mask, d, lanes):
    def tile_step(kt, carries, mask=None):
        stats = []
        probs = []
        for (score_fn, _), (m, l, _) in zip(chains, carries):
            s = score_fn(kt)
            if mask is not None:
                s = jnp.where(mask, s, NEG)
            m_new = jnp.maximum(m, jnp.max(s, axis=0, keepdims=True))
            alpha = jnp.exp2(m - m_new)
            p = jnp.exp2(s - m_new)
            stats.append((m_new, alpha * l + jnp.sum(p, axis=0, keepdims=True), alpha))
            probs.append(p.astype(BF16))
        out = []
        for (_, value_fn), (_, _, acc), (m_new, l_new, alpha), p in zip(chains, carries, stats, probs):
            pv = lax.dot_general(value_fn(kt), p, _TN, preferred_element_type=F32)
            out.append((m_new, l_new, alpha * acc + pv))
        return tuple(out)

    init = (jnp.full((1, lanes), NEG, F32), jnp.zeros((1, lanes), F32), jnp.zeros((d, lanes), F32))
    carries = lax.fori_loop(0, n_before, tile_step, (init,) * len(chains))
    return [acc * (1.0 / l) for _, l, acc in tile_step(n_before, carries, diag_mask)]


def _nsa_kernel(q_ref, kc0_ref, vc0_ref, kc1_ref, vc1_ref, ks_ref, vs_ref, kw_ref, vw_ref, gate_ref, ovt_ref,
                o_ref, bias_ref, *, tq):
    qi = pl.program_id(1)
    start = qi * tq
    nl = NSA_GROUP * tq
    kvs = range(N_NSA_KV)
    cmp_refs = ((kc0_ref, vc0_ref), (kc1_ref, vc1_ref))

    def head_cols(kv):
        return slice(kv * HEAD_DIM, (kv + 1) * HEAD_DIM)

    q_all = q_ref[...]
    qs = [jnp.concatenate([q_all[:, (kv * NSA_GROUP + g) * HEAD_DIM:(kv * NSA_GROUP + g + 1) * HEAD_DIM]
                           for g in range(NSA_GROUP)], axis=0) for kv in kvs]
    q_local = lax.broadcasted_iota(jnp.int32, (1, nl), 1) & (tq - 1)
    t_lane = start + q_local
    k_local = lax.broadcasted_iota(jnp.int32, (tq, nl), 0)
    causal = k_local <= q_local
    c_end = lax.broadcasted_iota(jnp.int32, (N_CMP_PAD, nl), 0) * CMP_STRIDE + (CMP_LEN - 1)
    cmask = c_end <= t_lane
    j_idx = lax.broadcasted_iota(jnp.int32, (N_SEL, tq), 0)
    t_q = start + lax.broadcasted_iota(jnp.int32, (N_SEL, tq), 1)
    cur = t_q // SLC_LEN
    forced = (j_idx == 0) | (j_idx == cur) | (j_idx == cur - 1)
    ovt = ovt_ref[...]

    o_cmp = []
    for kv in kvs:
        kc_ref, vc_ref = cmp_refs[kv]
        s = lax.dot_general(kc_ref[0], qs[kv], _NT, preferred_element_type=F32)
        s = jnp.where(cmask, s, NEG)
        m = jnp.max(s, axis=0, keepdims=True)
        e = jnp.where(cmask, jnp.exp2(s - m), 0.0)
        l = jnp.sum(e, axis=0, keepdims=True)
        p_cmp = e * jnp.where(l > 0.0, 1.0 / l, 0.0)
        o_cmp.append(lax.dot_general(vc_ref[0], p_cmp.astype(BF16), _TN, preferred_element_type=F32))

        p_sum = p_cmp[:, 0:tq]
        for g in range(1, NSA_GROUP):
            p_sum = p_sum + p_cmp[:, g * tq:(g + 1) * tq]
        p_hi = p_sum.astype(BF16)
        p_lo = (p_sum - p_hi.astype(F32)).astype(BF16)
        p_slc = (jnp.dot(ovt, p_hi, preferred_element_type=F32)
                 + jnp.dot(ovt, p_lo, preferred_element_type=F32))
        score = jnp.where(forced, 1e4, jnp.where(j_idx > cur, -1e4, p_slc))
        rank = jnp.zeros((N_SEL, tq), jnp.int32)
        for i in range(N_SEL):
            row = score[i:i + 1, :]
            tie = jnp.where(j_idx > i, 1, 0)
            rank = rank + jnp.where(row > score, 1, jnp.where(row == score, tie, 0))
        bias = jnp.where((rank < SLC_TOP) & (j_idx <= cur), 0.0, NEG)
        bias_ref[kv] = jnp.concatenate([bias] * NSA_GROUP, axis=1)

    blocks_per_tile = tq // SLC_LEN

    def slc_chain(kv):
        def scores(kt):
            k0 = pl.multiple_of(kt * tq, tq)
            s = lax.dot_general(ks_ref[pl.ds(k0, tq), head_cols(kv)], qs[kv], _NT, preferred_element_type=F32)
            parts = [s[c * SLC_LEN:(c + 1) * SLC_LEN, :] + bias_ref[kv, pl.ds(kt * blocks_per_tile + c, 1), :]
                     for c in range(blocks_per_tile)]
            return jnp.concatenate(parts, axis=0)

        def values(kt):
            return vs_ref[pl.ds(pl.multiple_of(kt * tq, tq), tq), head_cols(kv)]

        return scores, values

    o_slc = _causal_flash([slc_chain(kv) for kv in kvs], qi, causal, HEAD_DIM, nl)

    win_p, win_l, win_v = [], [], []
    for kv in kvs:
        scores, values = [], []
        for back in range(WINDOW // tq, -1, -1):
            k0 = start - back * tq
            inside = k0 >= 0
            k0 = pl.multiple_of(jnp.maximum(k0, 0), tq)
            s = lax.dot_general(kw_ref[pl.ds(k0, tq), head_cols(kv)], qs[kv], _NT, preferred_element_type=F32)
            if back == 0:
                s = jnp.where(causal, s, NEG)
            elif back == WINDOW // tq:
                edge = q_local + jnp.where(inside, 0, tq)
                s = jnp.where(k_local > edge, s, NEG)
            else:
                s = s + jnp.where(inside, 0.0, NEG)
            scores.append(s)
            values.append(vw_ref[pl.ds(k0, tq), head_cols(kv)])
        s = jnp.concatenate(scores, axis=0)
        p = jnp.exp2(s - jnp.max(s, axis=0, keepdims=True))
        win_l.append(jnp.sum(p, axis=0, keepdims=True))
        win_p.append(p.astype(BF16))
        win_v.append(jnp.concatenate(values, axis=0))

    gates_t = jax.nn.sigmoid(gate_ref[...].astype(F32)).T
    for kv in kvs:
        acc_w = lax.dot_general(win_v[kv], win_p[kv], _TN, preferred_element_type=F32)
        o_win = acc_w * (1.0 / win_l[kv])

        gts = gates_t[kv * GATE_STRIDE:(kv + 1) * GATE_STRIDE]
        for g in range(NSA_GROUP):
            sl = slice(g * tq, (g + 1) * tq)
            o_t = (gts[g:g + 1, :] * o_cmp[kv][:, sl]
                   + gts[NSA_GROUP + g:NSA_GROUP + g + 1, :] * o_slc[kv][:, sl]
                   + gts[2 * NSA_GROUP + g:2 * NSA_GROUP + g + 1, :] * o_win[:, sl])
            head = kv * NSA_GROUP + g
            o_ref[:, head * HEAD_DIM:(head + 1) * HEAD_DIM] = o_t.T


def _nsa_attention(proj, cmp_kv, ovt, batch, *, tq=256):
    t = proj.shape[0]
    assert WINDOW % tq == 0 and tq % SLC_LEN == 0 and SEQ % tq == 0
    nq = SEQ // tq
    est = (4 * 2 * SEQ * NSA_KV_WIDTH * 2 + 2 * tq * NSA_WIDTH * (2 + 4)
           + N_NSA_KV * 10 * (WINDOW + tq) * NSA_GROUP * tq * 4)

    def kv_spec(col):
        return pl.BlockSpec((SEQ, NSA_KV_WIDTH), lambda b, i: (b, col // NSA_KV_WIDTH))

    def cmp_spec(which, kv):
        return pl.BlockSpec((1, N_CMP_PAD, HEAD_DIM), lambda b, i: (which, kv * batch + b, 0))

    return pl.pallas_call(
        functools.partial(_nsa_kernel, tq=tq),
        grid=(batch, nq),
        in_specs=[
            pl.BlockSpec((tq, NSA_WIDTH), lambda b, i: (b * nq + i, 0)),
            cmp_spec(0, 0), cmp_spec(1, 0), cmp_spec(0, 1), cmp_spec(1, 1),
            kv_spec(COL_KS), kv_spec(COL_VS), kv_spec(COL_KW), kv_spec(COL_VW),
            pl.BlockSpec((tq, HEAD_DIM), lambda b, i: (b * nq + i, COL_GATE // HEAD_DIM)),
            pl.BlockSpec((N_SEL, N_CMP_PAD), lambda b, i: (0, 0)),
        ],
        out_specs=pl.BlockSpec((tq, NSA_WIDTH), lambda b, i: (b * nq + i, 0)),
        out_shape=jax.ShapeDtypeStruct((t, NSA_WIDTH), F32),
        scratch_shapes=[pltpu.VMEM((N_NSA_KV, N_SEL, NSA_GROUP * tq), F32)],
        compiler_params=_params(("parallel", "arbitrary"), est),
        name="nsa_attn",
    )(proj, cmp_kv, cmp_kv, cmp_kv, cmp_kv, proj, proj, proj, proj, proj, ovt)


def _diff_kernel(q_ref, k_ref, v_ref, lq1_ref, lk1_ref, lq2_ref, lk2_ref, sub_ref, o_ref,
                 *, tq, lambda_init):
    qi = pl.program_id(2)
    wide = 2 * HEAD_DIM
    lam = (jnp.exp(jnp.sum(lq1_ref[...] * lk1_ref[...], axis=-1, keepdims=True))
           - jnp.exp(jnp.sum(lq2_ref[...] * lk2_ref[...], axis=-1, keepdims=True)) + lambda_init)
    q = q_ref[...]

    def chain(h):
        q1 = q[:, h * wide:h * wide + HEAD_DIM]
        q2 = q[:, h * wide + HEAD_DIM:(h + 1) * wide]

        def scores(kt):
            kk = k_ref[pl.ds(pl.multiple_of(kt * tq, tq), tq), h * wide:(h + 1) * wide]
            s1 = lax.dot_general(kk[:, :HEAD_DIM], q1, _NT, preferred_element_type=F32)
            s2 = lax.dot_general(kk[:, HEAD_DIM:], q2, _NT, preferred_element_type=F32)
            return jnp.concatenate([s1, s2], axis=1)

        def values(kt):
            return v_ref[pl.ds(pl.multiple_of(kt * tq, tq), tq), h * wide:(h + 1) * wide]

        return scores, values

    k_local = lax.broadcasted_iota(jnp.int32, (tq, 2 * tq), 0)
    q_local = lax.broadcasted_iota(jnp.int32, (1, 2 * tq), 1) & (tq - 1)
    outs = _causal_flash([chain(h) for h in range(DIFF_HEADS_PER_STEP)], qi, k_local <= q_local, wide, 2 * tq)
    for h, o_n in enumerate(outs):
        o_t = o_n[:, :tq] - lam * o_n[:, tq:]
        o = _rmsnorm(o_t.T, sub_ref[...]) * (1.0 - lambda_init)
        o_ref[:, h * wide:(h + 1) * wide] = o.astype(o_ref.dtype)


DIFF_HEADS_PER_STEP = 4


def _diff_attention(proj, lq1, lk1, lq2, lk2, subln, batch, lambda_init, *, tq=512):
    t = proj.shape[0]
    nq = SEQ // tq
    wide = 2 * HEAD_DIM
    step_cols = DIFF_HEADS_PER_STEP * wide
    assert COL_DQ % step_cols == 0 and COL_DK % step_cols == 0 and COL_DV % step_cols == 0
    est = 2 * 2 * SEQ * step_cols * 2 + 4 * tq * step_cols * 4 + DIFF_HEADS_PER_STEP * 16 * tq * 2 * tq * 4
    vec = pl.BlockSpec((1, HEAD_DIM), lambda b, h, i: (0, 0))
    return pl.pallas_call(
        functools.partial(_diff_kernel, tq=tq, lambda_init=lambda_init),
        grid=(batch, N_DIFF_HEADS // DIFF_HEADS_PER_STEP, nq),
        in_specs=[
            pl.BlockSpec((tq, step_cols), lambda b, h, i: (b * nq + i, COL_DQ // step_cols + h)),
            pl.BlockSpec((SEQ, step_cols), lambda b, h, i: (b, COL_DK // step_cols + h)),
            pl.BlockSpec((SEQ, step_cols), lambda b, h, i: (b, COL_DV // step_cols + h)),
            vec, vec, vec, vec,
            pl.BlockSpec((1, wide), lambda b, h, i: (0, 0)),
        ],
        out_specs=pl.BlockSpec((tq, step_cols), lambda b, h, i: (b * nq + i, h)),
        out_shape=jax.ShapeDtypeStruct((t, DIFF_WIDTH), BF16),
        compiler_params=_params(("parallel", "parallel", "arbitrary"), est),
        name="diff_attn",
    )(proj, proj, proj, lq1, lk1, lq2, lk2, subln)


def _oproj_kernel(yn_ref, g_ref, yd_ref, wo_ref, x_ref, o_ref):
    yn = _rmsnorm(yn_ref[...], g_ref[...]).astype(BF16)
    acc = jnp.dot(yn, wo_ref[:NSA_WIDTH, :], preferred_element_type=F32)
    acc = acc + jnp.dot(yd_ref[...], wo_ref[NSA_WIDTH:, :], preferred_element_type=F32)
    o_ref[...] = x_ref[...] + acc


def _out_proj(y_nsa, g, y_diff, w_o, x2, *, tm=512):
    t = x2.shape[0]
    est = (2 * (NSA_WIDTH + DIFF_WIDTH) * D_MODEL * 2 + 2 * tm * NSA_WIDTH * 4 + 2 * tm * DIFF_WIDTH * 2
           + 5 * tm * D_MODEL * 4)
    return pl.pallas_call(
        _oproj_kernel,
        grid=(t // tm,),
        in_specs=[
            pl.BlockSpec((tm, NSA_WIDTH), lambda i: (i, 0)),
            pl.BlockSpec((1, NSA_WIDTH), lambda i: (0, 0)),
            pl.BlockSpec((tm, DIFF_WIDTH), lambda i: (i, 0)),
            pl.BlockSpec((NSA_WIDTH + DIFF_WIDTH, D_MODEL), lambda i: (0, 0)),
            pl.BlockSpec((tm, D_MODEL), lambda i: (i, 0)),
        ],
        out_specs=pl.BlockSpec((tm, D_MODEL), lambda i: (i, 0)),
        out_shape=jax.ShapeDtypeStruct((t, D_MODEL), F32),
        compiler_params=_params(("parallel",), est),
        name="out_proj",
    )(y_nsa, g, y_diff, w_o, x2)


HALO = 8
FFN_CHUNK = 256
FFN_ROW_BLOCKS = 4


def _ffn_kernel(h_ref, halo_ref, g_ref, wu_ref, wg_ref, cwu_ref, cwg_ref, cbu_ref, cbg_ref, wd_ref,
                o_ref, xn_ref, raw_ref, act_ref, *, tm):
    i = pl.program_id(0)
    j = pl.program_id(1)

    @pl.when(j == 0)
    def _():
        keep = jnp.where((i * tm) % SEQ == 0, 0.0, 1.0)
        h = h_ref[...]
        xn_ref[0:HALO, :] = (_rmsnorm(halo_ref[...], g_ref[...]) * keep).astype(BF16)
        xn_ref[HALO:, :] = _rmsnorm(h, g_ref[...]).astype(BF16)
        o_ref[...] = h

    xn = xn_ref[...]

    def conv(h, cw_ref, cb_ref, cols):
        cw = cw_ref[:, cols]
        h3 = h.reshape(h.shape[0] // SUBLANES, SUBLANES, h.shape[1])
        sub = lax.broadcasted_iota(jnp.int32, h3.shape[1:], 0)
        out = cw[CONV_W - 1:CONV_W, :] * h3[1:] + cb_ref[:, cols]
        for back in range(1, CONV_W):
            rolled = pltpu.roll(h3, back, 1)
            shifted = jnp.where(sub < back, rolled[:-1], rolled[1:])
            out = out + cw[CONV_W - 1 - back:CONV_W - back, :] * shifted
        return out.reshape(h.shape[0] - HALO, h.shape[1])

    def gated(rows, cols):
        src = slice(rows.start, rows.stop + HALO)
        u = conv(raw_ref[slot, 0, src, cols], cwu_ref, cbu_ref, cols)
        gate = conv(raw_ref[slot, 1, src, cols], cwg_ref, cbg_ref, cols)
        act_ref[rows, cols] = (jax.nn.silu(gate) * u).astype(BF16)

    slot = j % 2
    chunks = [slice(lo, lo + FFN_CHUNK) for lo in range(0, wu_ref.shape[1], FFN_CHUNK)]
    row_blocks = [slice(lo, lo + tm // FFN_ROW_BLOCKS) for lo in range(0, tm, tm // FFN_ROW_BLOCKS)]
    for cols in chunks:
        raw_ref[slot, 0, :, cols] = jnp.dot(xn, wu_ref[:, cols], preferred_element_type=F32)
        raw_ref[slot, 1, :, cols] = jnp.dot(xn, wg_ref[:, cols], preferred_element_type=F32)
        if cols is not chunks[-1]:
            gated(slice(0, tm), cols)
    gated(row_blocks[0], chunks[-1])
    for n, rows in enumerate(row_blocks):
        if n + 1 < len(row_blocks):
            gated(row_blocks[n + 1], chunks[-1])
        o_ref[rows, :] += jnp.dot(act_ref[rows, :], wd_ref[...], preferred_element_type=F32)


def _conv_ffn(h1, g, w_up, conv_w, conv_b, w_down, *, tm=1024, tf=512):
    t = h1.shape[0]
    assert t % tm == 0 and SEQ % tm == 0 and D_FF % tf == 0
    nf = D_FF // tf
    est = (4 * tm * D_MODEL * 4 + 2 * 3 * D_MODEL * tf * 2 + (tm + HALO) * D_MODEL * 2
           + 8 * (tm + HALO) * tf * 4)
    return pl.pallas_call(
        functools.partial(_ffn_kernel, tm=tm),
        grid=(t // tm, nf),
        in_specs=[
            pl.BlockSpec((tm, D_MODEL), lambda i, j: (i, 0)),
            pl.BlockSpec((HALO, D_MODEL), lambda i, j: (jnp.maximum(i * (tm // HALO) - 1, 0), 0)),
            pl.BlockSpec((1, D_MODEL), lambda i, j: (0, 0)),
            pl.BlockSpec((D_MODEL, tf), lambda i, j: (0, j)),
            pl.BlockSpec((D_MODEL, tf), lambda i, j: (0, nf + j)),
            pl.BlockSpec((CONV_W, tf), lambda i, j: (0, j)),
            pl.BlockSpec((CONV_W, tf), lambda i, j: (0, nf + j)),
            pl.BlockSpec((1, tf), lambda i, j: (0, j)),
            pl.BlockSpec((1, tf), lambda i, j: (0, nf + j)),
            pl.BlockSpec((tf, D_MODEL), lambda i, j: (j, 0)),
        ],
        out_specs=pl.BlockSpec((tm, D_MODEL), lambda i, j: (i, 0), pipeline_mode=pl.Buffered(1)),
        out_shape=jax.ShapeDtypeStruct((t, D_MODEL), F32),
        scratch_shapes=[pltpu.VMEM((tm + HALO, D_MODEL), BF16), pltpu.VMEM((2, 2, tm + HALO, tf), F32),
                        pltpu.VMEM((tm, tf), BF16)],
        compiler_params=_params(("parallel", "arbitrary"), est),
        name="conv_ffn",
    )(h1, h1, g, w_up, w_up, conv_w, conv_w, conv_b, conv_b, w_down)


PLE_CHUNK = 512


def _ple_kernel(h_ref, gp_ref, wg_ref, p_ref, wp_ref, gf_ref, o_ref):
    hn = _rmsnorm(h_ref[...], gp_ref[...]).astype(BF16)
    pb = p_ref[...].astype(BF16)
    ssq = None
    for lo in range(0, D_MODEL, PLE_CHUNK):
        cols = slice(lo, lo + PLE_CHUNK)
        gate = jax.nn.sigmoid(jnp.dot(hn, wg_ref[:, cols], preferred_element_type=F32))
        emb = jnp.dot(pb, wp_ref[:, cols], preferred_element_type=F32)
        h3 = h_ref[:, cols] + gate * emb
        o_ref[:, cols] = h3
        part = jnp.sum(h3 * h3, axis=-1, keepdims=True)
        ssq = part if ssq is None else ssq + part
    o_ref[...] = o_ref[...] * lax.rsqrt(ssq * (1.0 / D_MODEL) + EPS) * gf_ref[...]


def _ple_out(h2, g_ple, w_gate, p2, w_proj, g_final, *, tm=256):
    t = h2.shape[0]
    est = 2 * D_MODEL * D_MODEL * 2 + 2 * PLE_DIM * D_MODEL * 2 + 4 * tm * D_MODEL * 4 + 6 * tm * D_MODEL * 4
    return pl.pallas_call(
        _ple_kernel,
        grid=(t // tm,),
        in_specs=[
            pl.BlockSpec((tm, D_MODEL), lambda i: (i, 0)),
            pl.BlockSpec((1, D_MODEL), lambda i: (0, 0)),
            pl.BlockSpec((D_MODEL, D_MODEL), lambda i: (0, 0)),
            pl.BlockSpec((tm, PLE_DIM), lambda i: (i, 0)),
            pl.BlockSpec((PLE_DIM, D_MODEL), lambda i: (0, 0)),
            pl.BlockSpec((1, D_MODEL), lambda i: (0, 0)),
        ],
        out_specs=pl.BlockSpec((tm, D_MODEL), lambda i: (i, 0)),
        out_shape=jax.ShapeDtypeStruct((t, D_MODEL), F32),
        compiler_params=_params(("parallel",), est),
        name="ple_out",
    )(h2, g_ple, w_gate, p2, w_proj, g_final)


def _split_w_in(w):
    w = w.astype(BF16)
    gate_lo = NSA_WIDTH + 6 * NSA_KV_WIDTH
    gate = w[:, gate_lo:gate_lo + N_GATES].reshape(D_MODEL, 3, N_NSA_KV, NSA_GROUP).transpose(0, 2, 1, 3)
    gate = jnp.pad(gate.reshape(D_MODEL, N_NSA_KV, 3 * NSA_GROUP), ((0, 0), (0, 0), (0, GATE_STRIDE - 3 * NSA_GROUP)))
    gate = jnp.pad(gate.reshape(D_MODEL, N_NSA_KV * GATE_STRIDE), ((0, 0), (0, HEAD_DIM - N_NSA_KV * GATE_STRIDE)))
    return w, gate, w[:, gate_lo + N_GATES:]


def _rope_tables():
    inv = 1.0 / (ROPE_THETA ** (jnp.arange(0, ROPE_DIM, 2, dtype=F32) / ROPE_DIM))
    ang = jnp.arange(SEQ, dtype=F32)[:, None] * inv[None, :]
    cos, sin = jnp.cos(ang), jnp.sin(ang)
    rest = HEAD_DIM - ROPE_DIM
    cos_t = jnp.concatenate([cos, cos, jnp.ones((SEQ, rest), F32)], axis=1)
    sin_t = jnp.concatenate([-sin, sin, jnp.zeros((SEQ, rest), F32)], axis=1)
    return cos_t, sin_t


def _overlap_t():
    cmp_starts = np.arange(N_CMP) * CMP_STRIDE
    sel_starts = np.arange(N_SEL) * SLC_LEN
    ov = np.clip(np.minimum(cmp_starts[:, None] + CMP_LEN, sel_starts[None, :] + SLC_LEN)
                 - np.maximum(cmp_starts[:, None], sel_starts[None, :]), 0, None).astype(np.float32) / CMP_LEN
    ovt = np.zeros((N_SEL, N_CMP_PAD), np.float32)
    ovt[:, :N_CMP] = ov.T
    return jnp.asarray(ovt, BF16)


def kernel(x, p, attn_norm, w_in, cmp_k_pos, cmp_k_w1, cmp_k_w2, cmp_v_pos, cmp_v_w1, cmp_v_w2, nsa_out_norm, diff_lq1, diff_lk1, diff_lq2, diff_lk2, diff_subln, w_o, ffn_norm, w_up, conv_w, conv_b, w_down, ple_norm, w_ple_gate, w_ple_proj, final_norm):
    batch, seq, _ = x.shape
    assert seq == SEQ and p.shape[0] == 1
    t = batch * seq
    layer = 0
    lambda_init = 0.8 - 0.6 * math.exp(-0.3 * layer)
    x2 = x.reshape(t, D_MODEL)
    cos_t, sin_t = _rope_tables()

    proj, hkv, w_o_b, w_up_b, w_down_b, w_gate_b = _in_proj(
        x2, attn_norm[layer][None], *_split_w_in(w_in[layer]), cos_t, sin_t,
        later_weights=(w_o[layer], w_up[layer], w_down[layer], w_ple_gate[layer]))

    hkv = hkv.reshape(2, N_NSA_KV * t // HALF_BLOCK, HALF_BLOCK * HEAD_DIM)
    w1 = jnp.stack([cmp_k_w1[layer], cmp_v_w1[layer]]).astype(BF16)
    w2 = jnp.stack([cmp_k_w2[layer], cmp_v_w2[layer]]).astype(BF16)
    pos = jnp.stack([cmp_k_pos[layer], cmp_v_pos[layer]]).reshape(2, 1, CMP_LEN * HEAD_DIM)
    pos = jnp.broadcast_to(pos, (2, 8, CMP_LEN * HEAD_DIM)).astype(BF16)
    cmp_kv = _compress(hkv, w1, pos, w2)

    y_nsa = _nsa_attention(proj, cmp_kv, _overlap_t(), batch)
    y_diff = _diff_attention(proj, diff_lq1[layer][None], diff_lk1[layer][None], diff_lq2[layer][None],
                             diff_lk2[layer][None], diff_subln[layer][None], batch, lambda_init)
    h1 = _out_proj(y_nsa, nsa_out_norm[layer][None], y_diff, w_o_b, x2)
    h2 = _conv_ffn(h1, ffn_norm[layer][None], w_up_b, conv_w[layer], conv_b[layer][None], w_down_b)
    out = _ple_out(h2, ple_norm[layer][None], w_gate_b, p[layer].reshape(t, PLE_DIM),
                   w_ple_proj[layer].astype(BF16), final_norm[None])
    return out.reshape(batch, seq, D_MODEL)
```

```python
import functools
import math

import numpy as np
import jax
import jax.numpy as jnp
from jax import lax
from jax.experimental import pallas as pl
from jax.experimental.pallas import tpu as pltpu

D_MODEL = 2048
SEQ = 2048
HEAD_DIM = 128
ROPE_DIM = HEAD_DIM // 4
ROPE_THETA = 500000.0
N_NSA_HEADS = 8
N_NSA_KV = 2
NSA_GROUP = N_NSA_HEADS // N_NSA_KV
CMP_LEN = 32
CMP_STRIDE = 16
CMP_HIDDEN = 256
SLC_LEN = 64
SLC_TOP = 16
WINDOW = 512
N_DIFF_HEADS = 4
D_FF = 5632
CONV_W = 3
PLE_DIM = 256
EPS = 1e-6

NSA_WIDTH = N_NSA_HEADS * HEAD_DIM
NSA_KV_WIDTH = N_NSA_KV * HEAD_DIM
DIFF_WIDTH = N_DIFF_HEADS * 2 * HEAD_DIM
N_GATES = 3 * N_NSA_HEADS
N_CMP = (SEQ - CMP_LEN) // CMP_STRIDE + 1
N_CMP_PAD = SEQ // CMP_STRIDE
N_SEL = SEQ // SLC_LEN
GATE_STRIDE = 16

COL_NQ = 0
COL_DQ = COL_NQ + NSA_WIDTH
COL_DK = COL_DQ + DIFF_WIDTH
COL_DV = COL_DK + DIFF_WIDTH
COL_KS = COL_DV + DIFF_WIDTH
COL_KW = COL_KS + NSA_KV_WIDTH
COL_VS = COL_KW + NSA_KV_WIDTH
COL_VW = COL_VS + NSA_KV_WIDTH
COL_GATE = COL_VW + NSA_KV_WIDTH
PROJ_COLS = COL_GATE + HEAD_DIM

V7X_LANES = 128
SUBLANES = 8
BF16_ROWS = 16
V7X_VMEM_REQUEST_CAP = 56 * 1024 * 1024
NEG = -1e30
QSCALE = HEAD_DIM ** -0.5 * math.log2(math.e)

_NT = (((1,), (1,)), ((), ()))
_TN = (((0,), (0,)), ((), ()))
BF16 = jnp.bfloat16
F32 = jnp.float32


def _params(semantics, vmem_estimate_bytes, flags=None):
    limit = min(max(int(vmem_estimate_bytes), 32 * 1024 * 1024), V7X_VMEM_REQUEST_CAP)
    return pltpu.CompilerParams(dimension_semantics=semantics, vmem_limit_bytes=limit, flags=flags)


def _rmsnorm(x, g):
    return x * lax.rsqrt(jnp.mean(x * x, axis=-1, keepdims=True) + EPS) * g


INPROJ_CHUNK = 1024
HALF_BLOCK = CMP_STRIDE

_HEADS_A = ([(COL_NQ + h * HEAD_DIM, True, True) for h in range(N_NSA_HEADS)]
            + [(("cmp", 0, h), True, False) for h in range(N_NSA_KV)]
            + [(("cmp", 1, h), False, False) for h in range(N_NSA_KV)]
            + [(COL_KS + h * HEAD_DIM, True, False) for h in range(N_NSA_KV)]
            + [(COL_VS + h * HEAD_DIM, False, False) for h in range(N_NSA_KV)]
            + [(COL_KW + h * HEAD_DIM, True, False) for h in range(N_NSA_KV)]
            + [(COL_VW + h * HEAD_DIM, False, False) for h in range(N_NSA_KV)])
_HEADS_B = ([(COL_DQ + h * HEAD_DIM, True, True) for h in range(2 * N_DIFF_HEADS)]
            + [(COL_DK + h * HEAD_DIM, True, False) for h in range(2 * N_DIFF_HEADS)]
            + [(COL_DV + h * HEAD_DIM, False, False) for h in range(2 * N_DIFF_HEADS)])
_HEADS_GATE = [(COL_GATE, False, False)]


def _inproj_kernel(*refs, n_cast):
    x_ref, g_ref, wa_ref, wg_ref, wb_ref, cos_ref, sin_ref = refs[:7]
    cast_in = refs[7:7 + n_cast]
    o_ref, hkv_ref = refs[7 + n_cast:9 + n_cast]
    cast_out = refs[9 + n_cast:9 + 2 * n_cast]
    stage_ref = refs[9 + 2 * n_cast]
    for src_ref, dst_ref in zip(cast_in, cast_out):
        dst_ref[...] = src_ref[...].astype(dst_ref.dtype)
    tm = x_ref.shape[0]
    xn = _rmsnorm(x_ref[...], g_ref[...]).astype(BF16)
    c = cos_ref[...]
    s = sin_ref[...]
    cq = c * QSCALE
    sq = s * QSCALE
    first_half = lax.broadcasted_iota(jnp.int32, c.shape, 1) < ROPE_DIM // 2
    n_staged = 0
    for w_ref, heads in ((wa_ref, _HEADS_A), (wg_ref, _HEADS_GATE), (wb_ref, _HEADS_B)):
        for lo in range(0, w_ref.shape[1], INPROJ_CHUNK):
            hi = min(lo + INPROJ_CHUNK, w_ref.shape[1])
            acc = jnp.dot(xn, w_ref[:, lo:hi], preferred_element_type=F32)
            for src in range(lo, hi, HEAD_DIM):
                dest, rope, is_query = heads[src // HEAD_DIM]
                a = acc[:, src - lo:src - lo + HEAD_DIM]
                if rope:
                    partner = jnp.where(first_half,
                                        pltpu.roll(a, HEAD_DIM - ROPE_DIM // 2, 1),
                                        pltpu.roll(a, ROPE_DIM // 2, 1))
                    a = a * cq + partner * sq if is_query else a * c + partner * s
                if isinstance(dest, tuple):
                    _, which, head = dest
                    stage = stage_ref.at[n_staged]
                    n_staged += 1
                    stage[...] = a
                    for r in range(HALF_BLOCK):
                        rows = stage[pl.ds(r, tm // HALF_BLOCK, stride=HALF_BLOCK), :]
                        hkv_ref[which, head, :, r * HEAD_DIM:(r + 1) * HEAD_DIM] = rows.astype(hkv_ref.dtype)
                else:
                    o_ref[:, dest:dest + HEAD_DIM] = a.astype(o_ref.dtype)


def _in_proj(x2, g, w, w_g, w_b, cos_t, sin_t, later_weights=(), *, tm=256):
    t = x2.shape[0]
    assert t % tm == 0 and SEQ % tm == 0 and tm % (HALF_BLOCK * BF16_ROWS) == 0
    n_steps = t // tm
    seq_tiles = SEQ // tm
    nsa_cols = NSA_WIDTH + 6 * NSA_KV_WIDTH
    n_w = nsa_cols + HEAD_DIM + w_b.shape[1]
    est = (D_MODEL * n_w * 2 + 2 * tm * D_MODEL * 4 + 2 * tm * n_w * 2 + tm * D_MODEL * 2
           + 3 * tm * INPROJ_CHUNK * 4)
    cast_specs = []
    for lw in later_weights:
        assert lw.shape[0] % (n_steps * BF16_ROWS) == 0
        cast_specs.append(pl.BlockSpec((lw.shape[0] // n_steps, lw.shape[1]), lambda i: (i, 0)))
        est += 2 * (lw.size // n_steps) * (4 + 2)
    resident = dict(pipeline_mode=pl.Buffered(1))
    half_cols = HALF_BLOCK * HEAD_DIM
    return pl.pallas_call(
        functools.partial(_inproj_kernel, n_cast=len(later_weights)),
        grid=(n_steps,),
        in_specs=[
            pl.BlockSpec((tm, D_MODEL), lambda i: (i, 0)),
            pl.BlockSpec((1, D_MODEL), lambda i: (0, 0)),
            pl.BlockSpec((D_MODEL, nsa_cols), lambda i: (0, 0), **resident),
            pl.BlockSpec(w_g.shape, lambda i: (0, 0), **resident),
            pl.BlockSpec(w_b.shape, lambda i: (0, 0), **resident),
            pl.BlockSpec((tm, HEAD_DIM), lambda i: (i % seq_tiles, 0)),
            pl.BlockSpec((tm, HEAD_DIM), lambda i: (i % seq_tiles, 0)),
        ] + cast_specs,
        out_specs=[
            pl.BlockSpec((tm, PROJ_COLS), lambda i: (i, 0)),
            pl.BlockSpec((2, N_NSA_KV, tm // HALF_BLOCK, half_cols), lambda i: (0, 0, i, 0)),
        ] + cast_specs,
        out_shape=[
            jax.ShapeDtypeStruct((t, PROJ_COLS), BF16),
            jax.ShapeDtypeStruct((2, N_NSA_KV, t // HALF_BLOCK, half_cols), BF16),
        ] + [jax.ShapeDtypeStruct(lw.shape, BF16) for lw in later_weights],
        scratch_shapes=[pltpu.VMEM((2 * N_NSA_KV, tm, HEAD_DIM), F32)],
        compiler_params=_params(("parallel",), est),
        name="in_proj",
    )(x2, g, w, w_g, w_b, cos_t, sin_t, *later_weights)


def _compress_kernel(h_ref, w1_ref, pos_ref, w2_ref, o_ref):
    half = CMP_LEN * HEAD_DIM // 2
    h = h_ref[0]
    top = jnp.dot(h, w1_ref[0, :half, :], preferred_element_type=F32)
    bot = jnp.dot(h, w1_ref[0, half:, :], preferred_element_type=F32)
    pos_bias = jnp.dot(pos_ref[0], w1_ref[0], preferred_element_type=F32)[0:1]
    pre = top + pltpu.roll(bot, bot.shape[0] - 1, 0) + pos_bias
    act = jax.nn.gelu(pre)
    o_ref[0] = jnp.dot(act.astype(BF16), w2_ref[0], preferred_element_type=F32).astype(o_ref.dtype)


def _compress(hkv, w1, pos, w2):
    rows = hkv.shape[1]
    kdim = CMP_LEN * HEAD_DIM
    est = 2 * (rows * kdim // 2 * 2 + kdim * CMP_HIDDEN * 2) + 6 * rows * CMP_HIDDEN * 4
    return pl.pallas_call(
        _compress_kernel,
        grid=(2,),
        in_specs=[
            pl.BlockSpec((1, rows, kdim // 2), lambda i: (i, 0, 0)),
            pl.BlockSpec((1, kdim, CMP_HIDDEN), lambda i: (i, 0, 0)),
            pl.BlockSpec((1, 8, kdim), lambda i: (i, 0, 0)),
            pl.BlockSpec((1, CMP_HIDDEN, HEAD_DIM), lambda i: (i, 0, 0)),
        ],
        out_specs=pl.BlockSpec((1, rows, HEAD_DIM), lambda i: (i, 0, 0)),
        out_shape=jax.ShapeDtypeStruct((2, rows, HEAD_DIM), BF16),
        compiler_params=_params(("parallel",), est),
        name="compress",
    )(hkv, w1, pos, w2)


def _causal_flash(chains, n_before, diag_mask, d, lanes):
    def tile_step(kt, carries, mask=None):
        stats = []
        probs = []
        for (score_fn, _), (m, l, _) in zip(chains, carries):
            s = score_fn(kt)
            if mask is not None:
                s = jnp.where(mask, s, NEG)
            m_new = jnp.maximum(m, jnp.max(s, axis=0, keepdims=True))
            alpha = jnp.exp2(m - m_new)
            p = jnp.exp2(s - m_new)
            stats.append((m_new, alpha * l + jnp.sum(p, axis=0, keepdims=True), alpha))
            probs.append(p.astype(BF16))
        out = []
        for (_, value_fn), (_, _, acc), (m_new, l_new, alpha), p in zip(chains, carries, stats, probs):
            pv = lax.dot_general(value_fn(kt), p, _TN, preferred_element_type=F32)
            out.append((m_new, l_new, alpha * acc + pv))
        return tuple(out)

    init = (jnp.full((1, lanes), NEG, F32), jnp.zeros((1, lanes), F32), jnp.zeros((d, lanes), F32))
    carries = lax.fori_loop(0, n_before, tile_step, (init,) * len(chains))
    return [acc * (1.0 / l) for _, l, acc in tile_step(n_before, carries, diag_mask)]


def _nsa_kernel(q_ref, kc0_ref, vc0_ref, kc1_ref, vc1_ref, ks_ref, vs_ref, kw_ref, vw_ref, gate_ref, ovt_ref,
                o_ref, bias_ref, *, tq):
    qi = pl.program_id(1)
    start = qi * tq
    nl = NSA_GROUP * tq
    kvs = range(N_NSA_KV)
    cmp_refs = ((kc0_ref, vc0_ref), (kc1_ref, vc1_ref))

    def head_cols(kv):
        return slice(kv * HEAD_DIM, (kv + 1) * HEAD_DIM)

    q_all = q_ref[...]
    qs = [jnp.concatenate([q_all[:, (kv * NSA_GROUP + g) * HEAD_DIM:(kv * NSA_GROUP + g + 1) * HEAD_DIM]
                           for g in range(NSA_GROUP)], axis=0) for kv in kvs]
    q_local = lax.broadcasted_iota(jnp.int32, (1, nl), 1) & (tq - 1)
    t_lane = start + q_local
    k_local = lax.broadcasted_iota(jnp.int32, (tq, nl), 0)
    causal = k_local <= q_local
    c_end = lax.broadcasted_iota(jnp.int32, (N_CMP_PAD, nl), 0) * CMP_STRIDE + (CMP_LEN - 1)
    cmask = c_end <= t_lane
    j_idx = lax.broadcasted_iota(jnp.int32, (N_SEL, tq), 0)
    t_q = start + lax.broadcasted_iota(jnp.int32, (N_SEL, tq), 1)
    cur = t_q // SLC_LEN
    forced = (j_idx == 0) | (j_idx == cur) | (j_idx == cur - 1)
    ovt = ovt_ref[...]

    o_cmp = []
    for kv in kvs:
        kc_ref, vc_ref = cmp_refs[kv]
        s = lax.dot_general(kc_ref[0], qs[kv], _NT, preferred_element_type=F32)
        s = jnp.where(cmask, s, NEG)
        m = jnp.max(s, axis=0, keepdims=True)
        e = jnp.where(cmask, jnp.exp2(s - m), 0.0)
        l = jnp.sum(e, axis=0, keepdims=True)
        p_cmp = e * jnp.where(l > 0.0, 1.0 / l, 0.0)
        o_cmp.append(lax.dot_general(vc_ref[0], p_cmp.astype(BF16), _TN, preferred_element_type=F32))

        p_sum = p_cmp[:, 0:tq]
        for g in range(1, NSA_GROUP):
            p_sum = p_sum + p_cmp[:, g * tq:(g + 1) * tq]
        p_hi = p_sum.astype(BF16)
        p_lo = (p_sum - p_hi.astype(F32)).astype(BF16)
        p_slc = (jnp.dot(ovt, p_hi, preferred_element_type=F32)
                 + jnp.dot(ovt, p_lo, preferred_element_type=F32))
        score = jnp.where(forced, 1e4, jnp.where(j_idx > cur, -1e4, p_slc))
        rank = jnp.zeros((N_SEL, tq), jnp.int32)
        for i in range(N_SEL):
            row = score[i:i + 1, :]
            tie = jnp.where(j_idx > i, 1, 0)
            rank = rank + jnp.where(row > score, 1, jnp.where(row == score, tie, 0))
        bias = jnp.where((rank < SLC_TOP) & (j_idx <= cur), 0.0, NEG)
        bias_ref[kv] = jnp.concatenate([bias] * NSA_GROUP, axis=1)

    blocks_per_tile = tq // SLC_LEN

    def slc_chain(kv):
        def scores(kt):
            k0 = pl.multiple_of(kt * tq, tq)
            s = lax.dot_general(ks_ref[pl.ds(k0, tq), head_cols(kv)], qs[kv], _NT, preferred_element_type=F32)
            parts = [s[c * SLC_LEN:(c + 1) * SLC_LEN, :] + bias_ref[kv, pl.ds(kt * blocks_per_tile + c, 1), :]
                     for c in range(blocks_per_tile)]
            return jnp.concatenate(parts, axis=0)

        def values(kt):
            return vs_ref[pl.ds(pl.multiple_of(kt * tq, tq), tq), head_cols(kv)]

        return scores, values

    o_slc = _causal_flash([slc_chain(kv) for kv in kvs], qi, causal, HEAD_DIM, nl)

    win_p, win_l, win_v = [], [], []
    for kv in kvs:
        scores, values = [], []
        for back in range(WINDOW // tq, -1, -1):
            k0 = start - back * tq
            inside = k0 >= 0
            k0 = pl.multiple_of(jnp.maximum(k0, 0), tq)
            s = lax.dot_general(kw_ref[pl.ds(k0, tq), head_cols(kv)], qs[kv], _NT, preferred_element_type=F32)
            if back == 0:
                s = jnp.where(causal, s, NEG)
            elif back == WINDOW // tq:
                edge = q_local + jnp.where(inside, 0, tq)
                s = jnp.where(k_local > edge, s, NEG)
            else:
                s = s + jnp.where(inside, 0.0, NEG)
            scores.append(s)
            values.append(vw_ref[pl.ds(k0, tq), head_cols(kv)])
        s = jnp.concatenate(scores, axis=0)
        p = jnp.exp2(s - jnp.max(s, axis=0, keepdims=True))
        win_l.append(jnp.sum(p, axis=0, keepdims=True))
        win_p.append(p.astype(BF16))
        win_v.append(jnp.concatenate(values, axis=0))

    gates_t = jax.nn.sigmoid(gate_ref[...].astype(F32)).T
    for kv in kvs:
        acc_w = lax.dot_general(win_v[kv], win_p[kv], _TN, preferred_element_type=F32)
        o_win = acc_w * (1.0 / win_l[kv])

        gts = gates_t[kv * GATE_STRIDE:(kv + 1) * GATE_STRIDE]
        for g in range(NSA_GROUP):
            sl = slice(g * tq, (g + 1) * tq)
            o_t = (gts[g:g + 1, :] * o_cmp[kv][:, sl]
                   + gts[NSA_GROUP + g:NSA_GROUP + g + 1, :] * o_slc[kv][:, sl]
                   + gts[2 * NSA_GROUP + g:2 * NSA_GROUP + g + 1, :] * o_win[:, sl])
            head = kv * NSA_GROUP + g
            o_ref[:, head * HEAD_DIM:(head + 1) * HEAD_DIM] = o_t.T


def _nsa_attention(proj, cmp_kv, ovt, batch, *, tq=256):
    t = proj.shape[0]
    assert WINDOW % tq == 0 and tq % SLC_LEN == 0 and SEQ % tq == 0
    nq = SEQ // tq
    est = (4 * 2 * SEQ * NSA_KV_WIDTH * 2 + 2 * tq * NSA_WIDTH * (2 + 4)
           + N_NSA_KV * 10 * (WINDOW + tq) * NSA_GROUP * tq * 4)

    def kv_spec(col):
        return pl.BlockSpec((SEQ, NSA_KV_WIDTH), lambda b, i: (b, col // NSA_KV_WIDTH))

    def cmp_spec(which, kv):
        return pl.BlockSpec((1, N_CMP_PAD, HEAD_DIM), lambda b, i: (which, kv * batch + b, 0))

    return pl.pallas_call(
        functools.partial(_nsa_kernel, tq=tq),
        grid=(batch, nq),
        in_specs=[
            pl.BlockSpec((tq, NSA_WIDTH), lambda b, i: (b * nq + i, 0)),
            cmp_spec(0, 0), cmp_spec(1, 0), cmp_spec(0, 1), cmp_spec(1, 1),
            kv_spec(COL_KS), kv_spec(COL_VS), kv_spec(COL_KW), kv_spec(COL_VW),
            pl.BlockSpec((tq, HEAD_DIM), lambda b, i: (b * nq + i, COL_GATE // HEAD_DIM)),
            pl.BlockSpec((N_SEL, N_CMP_PAD), lambda b, i: (0, 0)),
        ],
        out_specs=pl.BlockSpec((tq, NSA_WIDTH), lambda b, i: (b * nq + i, 0)),
        out_shape=jax.ShapeDtypeStruct((t, NSA_WIDTH), F32),
        scratch_shapes=[pltpu.VMEM((N_NSA_KV, N_SEL, NSA_GROUP * tq), F32)],
        compiler_params=_params(("parallel", "arbitrary"), est),
        name="nsa_attn",
    )(proj, cmp_kv, cmp_kv, cmp_kv, cmp_kv, proj, proj, proj, proj, proj, ovt)


def _diff_kernel(q_ref, k_ref, v_ref, lq1_ref, lk1_ref, lq2_ref, lk2_ref, sub_ref, o_ref,
                 *, tq, lambda_init):
    qi = pl.program_id(2)
    wide = 2 * HEAD_DIM
    lam = (jnp.exp(jnp.sum(lq1_ref[...] * lk1_ref[...], axis=-1, keepdims=True))
           - jnp.exp(jnp.sum(lq2_ref[...] * lk2_ref[...], axis=-1, keepdims=True)) + lambda_init)
    q = q_ref[...]

    def chain(h):
        q1 = q[:, h * wide:h * wide + HEAD_DIM]
        q2 = q[:, h * wide + HEAD_DIM:(h + 1) * wide]

        def scores(kt):
            kk = k_ref[pl.ds(pl.multiple_of(kt * tq, tq), tq), h * wide:(h + 1) * wide]
            s1 = lax.dot_general(kk[:, :HEAD_DIM], q1, _NT, preferred_element_type=F32)
            s2 = lax.dot_general(kk[:, HEAD_DIM:], q2, _NT, preferred_element_type=F32)
            return jnp.concatenate([s1, s2], axis=1)

        def values(kt):
            return v_ref[pl.ds(pl.multiple_of(kt * tq, tq), tq), h * wide:(h + 1) * wide]

        return scores, values

    k_local = lax.broadcasted_iota(jnp.int32, (tq, 2 * tq), 0)
    q_local = lax.broadcasted_iota(jnp.int32, (1, 2 * tq), 1) & (tq - 1)
    outs = _causal_flash([chain(h) for h in range(DIFF_HEADS_PER_STEP)], qi, k_local <= q_local, wide, 2 * tq)
    for h, o_n in enumerate(outs):
        o_t = o_n[:, :tq] - lam * o_n[:, tq:]
        o = _rmsnorm(o_t.T, sub_ref[...]) * (1.0 - lambda_init)
        o_ref[:, h * wide:(h + 1) * wide] = o.astype(o_ref.dtype)


DIFF_HEADS_PER_STEP = 4


def _diff_attention(proj, lq1, lk1, lq2, lk2, subln, batch, lambda_init, *, tq=512):
    t = proj.shape[0]
    nq = SEQ // tq
    wide = 2 * HEAD_DIM
    step_cols = DIFF_HEADS_PER_STEP * wide
    assert COL_DQ % step_cols == 0 and COL_DK % step_cols == 0 and COL_DV % step_cols == 0
    est = 2 * 2 * SEQ * step_cols * 2 + 4 * tq * step_cols * 4 + DIFF_HEADS_PER_STEP * 16 * tq * 2 * tq * 4
    vec = pl.BlockSpec((1, HEAD_DIM), lambda b, h, i: (0, 0))
    return pl.pallas_call(
        functools.partial(_diff_kernel, tq=tq, lambda_init=lambda_init),
        grid=(batch, N_DIFF_HEADS // DIFF_HEADS_PER_STEP, nq),
        in_specs=[
            pl.BlockSpec((tq, step_cols), lambda b, h, i: (b * nq + i, COL_DQ // step_cols + h)),
            pl.BlockSpec((SEQ, step_cols), lambda b, h, i: (b, COL_DK // step_cols + h)),
            pl.BlockSpec((SEQ, step_cols), lambda b, h, i: (b, COL_DV // step_cols + h)),
            vec, vec, vec, vec,
            pl.BlockSpec((1, wide), lambda b, h, i: (0, 0)),
        ],
        out_specs=pl.BlockSpec((tq, step_cols), lambda b, h, i: (b * nq + i, h)),
        out_shape=jax.ShapeDtypeStruct((t, DIFF_WIDTH), BF16),
        compiler_params=_params(("parallel", "parallel", "arbitrary"), est),
        name="diff_attn",
    )(proj, proj, proj, lq1, lk1, lq2, lk2, subln)


def _oproj_kernel(yn_ref, g_ref, yd_ref, wo_ref, x_ref, o_ref):
    yn = _rmsnorm(yn_ref[...], g_ref[...]).astype(BF16)
    acc = jnp.dot(yn, wo_ref[:NSA_WIDTH, :], preferred_element_type=F32)
    acc = acc + jnp.dot(yd_ref[...], wo_ref[NSA_WIDTH:, :], preferred_element_type=F32)
    o_ref[...] = x_ref[...] + acc


def _out_proj(y_nsa, g, y_diff, w_o, x2, *, tm=512):
    t = x2.shape[0]
    est = (2 * (NSA_WIDTH + DIFF_WIDTH) * D_MODEL * 2 + 2 * tm * NSA_WIDTH * 4 + 2 * tm * DIFF_WIDTH * 2
           + 5 * tm * D_MODEL * 4)
    return pl.pallas_call(
        _oproj_kernel,
        grid=(t // tm,),
        in_specs=[
            pl.BlockSpec((tm, NSA_WIDTH), lambda i: (i, 0)),
            pl.BlockSpec((1, NSA_WIDTH), lambda i: (0, 0)),
            pl.BlockSpec((tm, DIFF_WIDTH), lambda i: (i, 0)),
            pl.BlockSpec((NSA_WIDTH + DIFF_WIDTH, D_MODEL), lambda i: (0, 0)),
            pl.BlockSpec((tm, D_MODEL), lambda i: (i, 0)),
        ],
        out_specs=pl.BlockSpec((tm, D_MODEL), lambda i: (i, 0)),
        out_shape=jax.ShapeDtypeStruct((t, D_MODEL), F32),
        compiler_params=_params(("parallel",), est),
        name="out_proj",
    )(y_nsa, g, y_diff, w_o, x2)


HALO = 8
FFN_CHUNK = 256
FFN_ROW_BLOCKS = 4


def _ffn_kernel(h_ref, halo_ref, g_ref, wu_ref, wg_ref, cwu_ref, cwg_ref, cbu_ref, cbg_ref, wd_ref,
                o_ref, xn_ref, raw_ref, act_ref, *, tm):
    i = pl.program_id(0)
    j = pl.program_id(1)

    @pl.when(j == 0)
    def _():
        keep = jnp.where((i * tm) % SEQ == 0, 0.0, 1.0)
        h = h_ref[...]
        xn_ref[0:HALO, :] = (_rmsnorm(halo_ref[...], g_ref[...]) * keep).astype(BF16)
        xn_ref[HALO:, :] = _rmsnorm(h, g_ref[...]).astype(BF16)
        o_ref[...] = h

    xn = xn_ref[...]

    def conv(h, cw_ref, cb_ref, cols):
        cw = cw_ref[:, cols]
        h3 = h.reshape(h.shape[0] // SUBLANES, SUBLANES, h.shape[1])
        sub = lax.broadcasted_iota(jnp.int32, h3.shape[1:], 0)
        out = cw[CONV_W - 1:CONV_W, :] * h3[1:] + cb_ref[:, cols]
        for back in range(1, CONV_W):
            rolled = pltpu.roll(h3, back, 1)
            shifted = jnp.where(sub < back, rolled[:-1], rolled[1:])
            out = out + cw[CONV_W - 1 - back:CONV_W - back, :] * shifted
        return out.reshape(h.shape[0] - HALO, h.shape[1])

    def gated(rows, cols):
        src = slice(rows.start, rows.stop + HALO)
        u = conv(raw_ref[slot, 0, src, cols], cwu_ref, cbu_ref, cols)
        gate = conv(raw_ref[slot, 1, src, cols], cwg_ref, cbg_ref, cols)
        act_ref[rows, cols] = (jax.nn.silu(gate) * u).astype(BF16)

    slot = j % 2
    chunks = [slice(lo, lo + FFN_CHUNK) for lo in range(0, wu_ref.shape[1], FFN_CHUNK)]
    row_blocks = [slice(lo, lo + tm // FFN_ROW_BLOCKS) for lo in range(0, tm, tm // FFN_ROW_BLOCKS)]
    for cols in chunks:
        raw_ref[slot, 0, :, cols] = jnp.dot(xn, wu_ref[:, cols], preferred_element_type=F32)
        raw_ref[slot, 1, :, cols] = jnp.dot(xn, wg_ref[:, cols], preferred_element_type=F32)
        if cols is not chunks[-1]:
            gated(slice(0, tm), cols)
    gated(row_blocks[0], chunks[-1])
    for n, rows in enumerate(row_blocks):
        if n + 1 < len(row_blocks):
            gated(row_blocks[n + 1], chunks[-1])
        o_ref[rows, :] += jnp.dot(act_ref[rows, :], wd_ref[...], preferred_element_type=F32)


def _conv_ffn(h1, g, w_up, conv_w, conv_b, w_down, *, tm=1024, tf=512):
    t = h1.shape[0]
    assert t % tm == 0 and SEQ % tm == 0 and D_FF % tf == 0
    nf = D_FF // tf
    est = (4 * tm * D_MODEL * 4 + 2 * 3 * D_MODEL * tf * 2 + (tm + HALO) * D_MODEL * 2
           + 8 * (tm + HALO) * tf * 4)
    return pl.pallas_call(
        functools.partial(_ffn_kernel, tm=tm),
        grid=(t // tm, nf),
        in_specs=[
            pl.BlockSpec((tm, D_MODEL), lambda i, j: (i, 0)),
            pl.BlockSpec((HALO, D_MODEL), lambda i, j: (jnp.maximum(i * (tm // HALO) - 1, 0), 0)),
            pl.BlockSpec((1, D_MODEL), lambda i, j: (0, 0)),
            pl.BlockSpec((D_MODEL, tf), lambda i, j: (0, j)),
            pl.BlockSpec((D_MODEL, tf), lambda i, j: (0, nf + j)),
            pl.BlockSpec((CONV_W, tf), lambda i, j: (0, j)),
            pl.BlockSpec((CONV_W, tf), lambda i, j: (0, nf + j)),
            pl.BlockSpec((1, tf), lambda i, j: (0, j)),
            pl.BlockSpec((1, tf), lambda i, j: (0, nf + j)),
            pl.BlockSpec((tf, D_MODEL), lambda i, j: (j, 0)),
        ],
        out_specs=pl.BlockSpec((tm, D_MODEL), lambda i, j: (i, 0), pipeline_mode=pl.Buffered(1)),
        out_shape=jax.ShapeDtypeStruct((t, D_MODEL), F32),
        scratch_shapes=[pltpu.VMEM((tm + HALO, D_MODEL), BF16), pltpu.VMEM((2, 2, tm + HALO, tf), F32),
                        pltpu.VMEM((tm, tf), BF16)],
        compiler_params=_params(("parallel", "arbitrary"), est),
        name="conv_ffn",
    )(h1, h1, g, w_up, w_up, conv_w, conv_w, conv_b, conv_b, w_down)


PLE_CHUNK = 512


def _ple_kernel(h_ref, gp_ref, wg_ref, p_ref, wp_ref, gf_ref, o_ref):
    hn = _rmsnorm(h_ref[...], gp_ref[...]).astype(BF16)
    pb = p_ref[...].astype(BF16)
    ssq = None
    for lo in range(0, D_MODEL, PLE_CHUNK):
        cols = slice(lo, lo + PLE_CHUNK)
        gate = jax.nn.sigmoid(jnp.dot(hn, wg_ref[:, cols], preferred_element_type=F32))
        emb = jnp.dot(pb, wp_ref[:, cols], preferred_element_type=F32)
        h3 = h_ref[:, cols] + gate * emb
        o_ref[:, cols] = h3
        part = jnp.sum(h3 * h3, axis=-1, keepdims=True)
        ssq = part if ssq is None else ssq + part
    o_ref[...] = o_ref[...] * lax.rsqrt(ssq * (1.0 / D_MODEL) + EPS) * gf_ref[...]


def _ple_out(h2, g_ple, w_gate, p2, w_proj, g_final, *, tm=256):
    t = h2.shape[0]
    est = 2 * D_MODEL * D_MODEL * 2 + 2 * PLE_DIM * D_MODEL * 2 + 4 * tm * D_MODEL * 4 + 6 * tm * D_MODEL * 4
    return pl.pallas_call(
        _ple_kernel,
        grid=(t // tm,),
        in_specs=[
            pl.BlockSpec((tm, D_MODEL), lambda i: (i, 0)),
            pl.BlockSpec((1, D_MODEL), lambda i: (0, 0)),
            pl.BlockSpec((D_MODEL, D_MODEL), lambda i: (0, 0)),
            pl.BlockSpec((tm, PLE_DIM), lambda i: (i, 0)),
            pl.BlockSpec((PLE_DIM, D_MODEL), lambda i: (0, 0)),
            pl.BlockSpec((1, D_MODEL), lambda i: (0, 0)),
        ],
        out_specs=pl.BlockSpec((tm, D_MODEL), lambda i: (i, 0)),
        out_shape=jax.ShapeDtypeStruct((t, D_MODEL), F32),
        compiler_params=_params(("parallel",), est),
        name="ple_out",
    )(h2, g_ple, w_gate, p2, w_proj, g_final)


def _split_w_in(w):
    w_a, gate, w_b = _cast_w_in(w)
    gate = gate[:, :N_GATES].reshape(D_MODEL, 3, N_NSA_KV, NSA_GROUP).transpose(0, 2, 1, 3)
    gate = jnp.pad(gate.reshape(D_MODEL, N_NSA_KV, 3 * NSA_GROUP), ((0, 0), (0, 0), (0, GATE_STRIDE - 3 * NSA_GROUP)))
    gate = jnp.pad(gate.reshape(D_MODEL, N_NSA_KV * GATE_STRIDE), ((0, 0), (0, HEAD_DIM - N_NSA_KV * GATE_STRIDE)))
    return w_a, gate, w_b


CAST_BLOCK = 512


def _cast_w_in_kernel(cur_ref, nxt_ref, wa_ref, wg_ref, wb_ref, *, n_a):
    s = pl.program_id(0)

    @pl.when(s < n_a)
    def _():
        wa_ref[...] = cur_ref[...].T.astype(BF16)

    @pl.when((s >= n_a) & (s < pl.num_programs(0) - 1))
    def _():
        rows = jnp.concatenate([cur_ref[...], nxt_ref[...]], axis=0)[N_GATES:N_GATES + CAST_BLOCK]
        wb_ref[...] = rows.T.astype(BF16)

    @pl.when(s == pl.num_programs(0) - 1)
    def _():
        wg_ref[...] = cur_ref[0:HEAD_DIM, :].T.astype(BF16)


def _cast_w_in(w):
    d, n = w.shape
    nsa_cols = NSA_WIDTH + 6 * NSA_KV_WIDTH
    diff_cols = n - nsa_cols - N_GATES
    assert nsa_cols % CAST_BLOCK == 0 and diff_cols % CAST_BLOCK == 0 and N_GATES % SUBLANES == 0
    n_a, n_b = nsa_cols // CAST_BLOCK, diff_cols // CAST_BLOCK
    last_in = (n - 1) // CAST_BLOCK
    est = 2 * 2 * CAST_BLOCK * d * 4 + 2 * 2 * CAST_BLOCK * d * 2 + 6 * CAST_BLOCK * d * 4

    def in_block(s):
        return jnp.where(s == n_a + n_b, n_a, s)

    return pl.pallas_call(
        functools.partial(_cast_w_in_kernel, n_a=n_a),
        grid=(n_a + n_b + 1,),
        in_specs=[pl.BlockSpec((CAST_BLOCK, d), lambda s: (in_block(s), 0)),
                  pl.BlockSpec((CAST_BLOCK, d), lambda s: (jnp.minimum(in_block(s) + 1, last_in), 0))],
        out_specs=[pl.BlockSpec((d, CAST_BLOCK), lambda s: (0, jnp.minimum(s, n_a - 1))),
                   pl.BlockSpec((d, HEAD_DIM), lambda s: (0, 0)),
                   pl.BlockSpec((d, CAST_BLOCK), lambda s: (0, jnp.clip(s - n_a, 0, n_b - 1)))],
        out_shape=[jax.ShapeDtypeStruct((d, nsa_cols), BF16), jax.ShapeDtypeStruct((d, HEAD_DIM), BF16),
                   jax.ShapeDtypeStruct((d, diff_cols), BF16)],
        compiler_params=_params(("arbitrary",), est),
        name="cast_w_in",
    )(w.T, w.T)


def _rope_tables():
    inv = 1.0 / (ROPE_THETA ** (jnp.arange(0, ROPE_DIM, 2, dtype=F32) / ROPE_DIM))
    ang = jnp.arange(SEQ, dtype=F32)[:, None] * inv[None, :]
    cos, sin = jnp.cos(ang), jnp.sin(ang)
    rest = HEAD_DIM - ROPE_DIM
    cos_t = jnp.concatenate([cos, cos, jnp.ones((SEQ, rest), F32)], axis=1)
    sin_t = jnp.concatenate([-sin, sin, jnp.zeros((SEQ, rest), F32)], axis=1)
    return cos_t, sin_t


def _overlap_t():
    cmp_starts = np.arange(N_CMP) * CMP_STRIDE
    sel_starts = np.arange(N_SEL) * SLC_LEN
    ov = np.clip(np.minimum(cmp_starts[:, None] + CMP_LEN, sel_starts[None, :] + SLC_LEN)
                 - np.maximum(cmp_starts[:, None], sel_starts[None, :]), 0, None).astype(np.float32) / CMP_LEN
    ovt = np.zeros((N_SEL, N_CMP_PAD), np.float32)
    ovt[:, :N_CMP] = ov.T
    return jnp.asarray(ovt, BF16)


def kernel(x, p, attn_norm, w_in, cmp_k_pos, cmp_k_w1, cmp_k_w2, cmp_v_pos, cmp_v_w1, cmp_v_w2, nsa_out_norm, diff_lq1, diff_lk1, diff_lq2, diff_lk2, diff_subln, w_o, ffn_norm, w_up, conv_w, conv_b, w_down, ple_norm, w_ple_gate, w_ple_proj, final_norm):
    batch, seq, _ = x.shape
    assert seq == SEQ and p.shape[0] == 1
    t = batch * seq
    layer = 0
    lambda_init = 0.8 - 0.6 * math.exp(-0.3 * layer)
    x2 = x.reshape(t, D_MODEL)
    cos_t, sin_t = _rope_tables()

    proj, hkv, w_o_b, w_up_b, w_down_b, w_gate_b = _in_proj(
        x2, attn_norm[layer][None], *_split_w_in(w_in[layer]), cos_t, sin_t,
        later_weights=(w_o[layer], w_up[layer], w_down[layer], w_ple_gate[layer]))

    hkv = hkv.reshape(2, N_NSA_KV * t // HALF_BLOCK, HALF_BLOCK * HEAD_DIM)
    w1 = jnp.stack([cmp_k_w1[layer], cmp_v_w1[layer]]).astype(BF16)
    w2 = jnp.stack([cmp_k_w2[layer], cmp_v_w2[layer]]).astype(BF16)
    pos = jnp.stack([cmp_k_pos[layer], cmp_v_pos[layer]]).reshape(2, 1, CMP_LEN * HEAD_DIM)
    pos = jnp.broadcast_to(pos, (2, 8, CMP_LEN * HEAD_DIM)).astype(BF16)
    cmp_kv = _compress(hkv, w1, pos, w2)

    y_nsa = _nsa_attention(proj, cmp_kv, _overlap_t(), batch)
    y_diff = _diff_attention(proj, diff_lq1[layer][None], diff_lk1[layer][None], diff_lq2[layer][None],
                             diff_lk2[layer][None], diff_subln[layer][None], batch, lambda_init)
    h1 = _out_proj(y_nsa, nsa_out_norm[layer][None], y_diff, w_o_b, x2)
    h2 = _conv_ffn(h1, ffn_norm[layer][None], w_up_b, conv_w[layer], conv_b[layer][None], w_down_b)
    out = _ple_out(h2, ple_norm[layer][None], w_gate_b, p[layer].reshape(t, PLE_DIM),
                   w_ple_proj[layer].astype(BF16), final_norm[None])
    return out.reshape(batch, seq, D_MODEL)
```

```python
import functools
import math

import numpy as np
import jax
import jax.numpy as jnp
from jax import lax
from jax.experimental import pallas as pl
from jax.experimental.pallas import tpu as pltpu

D_MODEL = 2048
SEQ = 2048
HEAD_DIM = 128
ROPE_DIM = HEAD_DIM // 4
ROPE_THETA = 500000.0
N_NSA_HEADS = 8
N_NSA_KV = 2
NSA_GROUP = N_NSA_HEADS // N_NSA_KV
CMP_LEN = 32
CMP_STRIDE = 16
CMP_HIDDEN = 256
SLC_LEN = 64
SLC_TOP = 16
WINDOW = 512
N_DIFF_HEADS = 4
D_FF = 5632
CONV_W = 3
PLE_DIM = 256
EPS = 1e-6

NSA_WIDTH = N_NSA_HEADS * HEAD_DIM
NSA_KV_WIDTH = N_NSA_KV * HEAD_DIM
DIFF_WIDTH = N_DIFF_HEADS * 2 * HEAD_DIM
N_GATES = 3 * N_NSA_HEADS
N_CMP = (SEQ - CMP_LEN) // CMP_STRIDE + 1
N_CMP_PAD = SEQ // CMP_STRIDE
N_SEL = SEQ // SLC_LEN
GATE_STRIDE = 16

COL_NQ = 0
COL_DQ = COL_NQ + NSA_WIDTH
COL_DK = COL_DQ + DIFF_WIDTH
COL_DV = COL_DK + DIFF_WIDTH
COL_KS = COL_DV + DIFF_WIDTH
COL_KW = COL_KS + NSA_KV_WIDTH
COL_VS = COL_KW + NSA_KV_WIDTH
COL_VW = COL_VS + NSA_KV_WIDTH
COL_GATE = COL_VW + NSA_KV_WIDTH
PROJ_COLS = COL_GATE + HEAD_DIM

V7X_LANES = 128
SUBLANES = 8
BF16_ROWS = 16
V7X_VMEM_REQUEST_CAP = 56 * 1024 * 1024
NEG = -1e30
QSCALE = HEAD_DIM ** -0.5 * math.log2(math.e)

_NT = (((1,), (1,)), ((), ()))
_TN = (((0,), (0,)), ((), ()))
BF16 = jnp.bfloat16
F32 = jnp.float32


def _params(semantics, vmem_estimate_bytes, flags=None):
    limit = min(max(int(vmem_estimate_bytes), 32 * 1024 * 1024), V7X_VMEM_REQUEST_CAP)
    return pltpu.CompilerParams(dimension_semantics=semantics, vmem_limit_bytes=limit, flags=flags)


def _rmsnorm(x, g):
    return x * lax.rsqrt(jnp.mean(x * x, axis=-1, keepdims=True) + EPS) * g


INPROJ_CHUNK = 1024
HALF_BLOCK = CMP_STRIDE

_HEADS_A = ([(COL_NQ + h * HEAD_DIM, True, True) for h in range(N_NSA_HEADS)]
            + [(("cmp", 0, h), True, False) for h in range(N_NSA_KV)]
            + [(("cmp", 1, h), False, False) for h in range(N_NSA_KV)]
            + [(COL_KS + h * HEAD_DIM, True, False) for h in range(N_NSA_KV)]
            + [(COL_VS + h * HEAD_DIM, False, False) for h in range(N_NSA_KV)]
            + [(COL_KW + h * HEAD_DIM, True, False) for h in range(N_NSA_KV)]
            + [(COL_VW + h * HEAD_DIM, False, False) for h in range(N_NSA_KV)])
_HEADS_B = ([(COL_DQ + h * HEAD_DIM, True, True) for h in range(2 * N_DIFF_HEADS)]
            + [(COL_DK + h * HEAD_DIM, True, False) for h in range(2 * N_DIFF_HEADS)]
            + [(COL_DV + h * HEAD_DIM, False, False) for h in range(2 * N_DIFF_HEADS)])
_HEADS_GATE = [(COL_GATE, False, False)]


def _inproj_kernel(*refs, n_cast):
    x_ref, g_ref, wa_ref, wg_ref, wb_ref, cos_ref, sin_ref = refs[:7]
    cast_in = refs[7:7 + n_cast]
    o_ref, hkv_ref = refs[7 + n_cast:9 + n_cast]
    cast_out = refs[9 + n_cast:9 + 2 * n_cast]
    stage_ref = refs[9 + 2 * n_cast]
    for src_ref, dst_ref in zip(cast_in, cast_out):
        dst_ref[...] = src_ref[...].astype(dst_ref.dtype)
    tm = x_ref.shape[0]
    xn = _rmsnorm(x_ref[...], g_ref[...]).astype(BF16)
    c = cos_ref[...]
    s = sin_ref[...]
    cq = c * QSCALE
    sq = s * QSCALE
    first_half = lax.broadcasted_iota(jnp.int32, c.shape, 1) < ROPE_DIM // 2
    n_staged = 0
    for w_ref, heads in ((wa_ref, _HEADS_A), (wg_ref, _HEADS_GATE), (wb_ref, _HEADS_B)):
        for lo in range(0, w_ref.shape[1], INPROJ_CHUNK):
            hi = min(lo + INPROJ_CHUNK, w_ref.shape[1])
            acc = jnp.dot(xn, w_ref[:, lo:hi], preferred_element_type=F32)
            for src in range(lo, hi, HEAD_DIM):
                dest, rope, is_query = heads[src // HEAD_DIM]
                a = acc[:, src - lo:src - lo + HEAD_DIM]
                if rope:
                    partner = jnp.where(first_half,
                                        pltpu.roll(a, HEAD_DIM - ROPE_DIM // 2, 1),
                                        pltpu.roll(a, ROPE_DIM // 2, 1))
                    a = a * cq + partner * sq if is_query else a * c + partner * s
                if isinstance(dest, tuple):
                    _, which, head = dest
                    stage = stage_ref.at[n_staged]
                    n_staged += 1
                    stage[...] = a
                    for r in range(HALF_BLOCK):
                        rows = stage[pl.ds(r, tm // HALF_BLOCK, stride=HALF_BLOCK), :]
                        hkv_ref[which, head, :, r * HEAD_DIM:(r + 1) * HEAD_DIM] = rows.astype(hkv_ref.dtype)
                else:
                    o_ref[:, dest:dest + HEAD_DIM] = a.astype(o_ref.dtype)


def _in_proj(x2, g, w, w_g, w_b, cos_t, sin_t, later_weights=(), *, tm=256):
    t = x2.shape[0]
    assert t % tm == 0 and SEQ % tm == 0 and tm % (HALF_BLOCK * BF16_ROWS) == 0
    n_steps = t // tm
    seq_tiles = SEQ // tm
    nsa_cols = NSA_WIDTH + 6 * NSA_KV_WIDTH
    n_w = nsa_cols + HEAD_DIM + w_b.shape[1]
    est = (D_MODEL * n_w * 2 + 2 * tm * D_MODEL * 4 + 2 * tm * n_w * 2 + tm * D_MODEL * 2
           + 3 * tm * INPROJ_CHUNK * 4)
    cast_specs = []
    for lw in later_weights:
        assert lw.shape[0] % (n_steps * BF16_ROWS) == 0
        cast_specs.append(pl.BlockSpec((lw.shape[0] // n_steps, lw.shape[1]), lambda i: (i, 0)))
        est += 2 * (lw.size // n_steps) * (4 + 2)
    resident = dict(pipeline_mode=pl.Buffered(1))
    half_cols = HALF_BLOCK * HEAD_DIM
    return pl.pallas_call(
        functools.partial(_inproj_kernel, n_cast=len(later_weights)),
        grid=(n_steps,),
        in_specs=[
            pl.BlockSpec((tm, D_MODEL), lambda i: (i, 0)),
            pl.BlockSpec((1, D_MODEL), lambda i: (0, 0)),
            pl.BlockSpec((D_MODEL, nsa_cols), lambda i: (0, 0), **resident),
            pl.BlockSpec(w_g.shape, lambda i: (0, 0), **resident),
            pl.BlockSpec(w_b.shape, lambda i: (0, 0), **resident),
            pl.BlockSpec((tm, HEAD_DIM), lambda i: (i % seq_tiles, 0)),
            pl.BlockSpec((tm, HEAD_DIM), lambda i: (i % seq_tiles, 0)),
        ] + cast_specs,
        out_specs=[
            pl.BlockSpec((tm, PROJ_COLS), lambda i: (i, 0)),
            pl.BlockSpec((2, N_NSA_KV, tm // HALF_BLOCK, half_cols), lambda i: (0, 0, i, 0)),
        ] + cast_specs,
        out_shape=[
            jax.ShapeDtypeStruct((t, PROJ_COLS), BF16),
            jax.ShapeDtypeStruct((2, N_NSA_KV, t // HALF_BLOCK, half_cols), BF16),
        ] + [jax.ShapeDtypeStruct(lw.shape, BF16) for lw in later_weights],
        scratch_shapes=[pltpu.VMEM((2 * N_NSA_KV, tm, HEAD_DIM), F32)],
        compiler_params=_params(("parallel",), est),
        name="in_proj",
    )(x2, g, w, w_g, w_b, cos_t, sin_t, *later_weights)


def _compress_kernel(h_ref, w1_ref, pos_ref, w2_ref, o_ref):
    half = CMP_LEN * HEAD_DIM // 2
    h = h_ref[0]
    top = jnp.dot(h, w1_ref[0, :half, :], preferred_element_type=F32)
    bot = jnp.dot(h, w1_ref[0, half:, :], preferred_element_type=F32)
    pos_bias = jnp.dot(pos_ref[0], w1_ref[0], preferred_element_type=F32)[0:1]
    pre = top + pltpu.roll(bot, bot.shape[0] - 1, 0) + pos_bias
    act = jax.nn.gelu(pre)
    o_ref[0] = jnp.dot(act.astype(BF16), w2_ref[0], preferred_element_type=F32).astype(o_ref.dtype)


def _compress(hkv, w1, pos, w2):
    rows = hkv.shape[1]
    kdim = CMP_LEN * HEAD_DIM
    est = 2 * (rows * kdim // 2 * 2 + kdim * CMP_HIDDEN * 2) + 6 * rows * CMP_HIDDEN * 4
    return pl.pallas_call(
        _compress_kernel,
        grid=(2,),
        in_specs=[
            pl.BlockSpec((1, rows, kdim // 2), lambda i: (i, 0, 0)),
            pl.BlockSpec((1, kdim, CMP_HIDDEN), lambda i: (i, 0, 0)),
            pl.BlockSpec((1, 8, kdim), lambda i: (i, 0, 0)),
            pl.BlockSpec((1, CMP_HIDDEN, HEAD_DIM), lambda i: (i, 0, 0)),
        ],
        out_specs=pl.BlockSpec((1, rows, HEAD_DIM), lambda i: (i, 0, 0)),
        out_shape=jax.ShapeDtypeStruct((2, rows, HEAD_DIM), BF16),
        compiler_params=_params(("parallel",), est),
        name="compress",
    )(hkv, w1, pos, w2)


def _causal_flash(chains, n_before, diag_mask, d, lanes):
    def tile_step(kt, carries, mask=None):
        stats = []
        probs = []
        for (score_fn, _), (m, l, _) in zip(chains, carries):
            s = score_fn(kt)
            if mask is not None:
                s = jnp.where(mask, s, NEG)
            m_new = jnp.maximum(m, jnp.max(s, axis=0, keepdims=True))
            alpha = jnp.exp2(m - m_new)
            p = jnp.exp2(s - m_new)
            stats.append((m_new, alpha * l + jnp.sum(p, axis=0, keepdims=True), alpha))
            probs.append(p.astype(BF16))
        out = []
        for (_, value_fn), (_, _, acc), (m_new, l_new, alpha), p in zip(chains, carries, stats, probs):
            pv = lax.dot_general(value_fn(kt), p, _TN, preferred_element_type=F32)
            out.append((m_new, l_new, alpha * acc + pv))
        return tuple(out)

    init = (jnp.full((1, lanes), NEG, F32), jnp.zeros((1, lanes), F32), jnp.zeros((d, lanes), F32))
    carries = lax.fori_loop(0, n_before, tile_step, (init,) * len(chains))
    return [acc * (1.0 / l) for _, l, acc in tile_step(n_before, carries, diag_mask)]


def _nsa_kernel(q_ref, kc0_ref, vc0_ref, kc1_ref, vc1_ref, ks_ref, vs_ref, kw_ref, vw_ref, gate_ref, ovt_ref,
                o_ref, bias_ref, *, tq):
    qi = pl.program_id(1)
    start = qi * tq
    nl = NSA_GROUP * tq
    kvs = range(N_NSA_KV)
    cmp_refs = ((kc0_ref, vc0_ref), (kc1_ref, vc1_ref))

    def head_cols(kv):
        return slice(kv * HEAD_DIM, (kv + 1) * HEAD_DIM)

    q_all = q_ref[...]
    qs = [jnp.concatenate([q_all[:, (kv * NSA_GROUP + g) * HEAD_DIM:(kv * NSA_GROUP + g + 1) * HEAD_DIM]
                           for g in range(NSA_GROUP)], axis=0) for kv in kvs]
    q_local = lax.broadcasted_iota(jnp.int32, (1, nl), 1) & (tq - 1)
    t_lane = start + q_local
    k_local = lax.broadcasted_iota(jnp.int32, (tq, nl), 0)
    causal = k_local <= q_local
    c_end = lax.broadcasted_iota(jnp.int32, (N_CMP_PAD, nl), 0) * CMP_STRIDE + (CMP_LEN - 1)
    cmask = c_end <= t_lane
    j_idx = lax.broadcasted_iota(jnp.int32, (N_SEL, tq), 0)
    t_q = start + lax.broadcasted_iota(jnp.int32, (N_SEL, tq), 1)
    cur = t_q // SLC_LEN
    forced = (j_idx == 0) | (j_idx == cur) | (j_idx == cur - 1)
    ovt = ovt_ref[...]

    o_cmp = []
    for kv in kvs:
        kc_ref, vc_ref = cmp_refs[kv]
        s = lax.dot_general(kc_ref[0], qs[kv], _NT, preferred_element_type=F32)
        s = jnp.where(cmask, s, NEG)
        m = jnp.max(s, axis=0, keepdims=True)
        e = jnp.where(cmask, jnp.exp2(s - m), 0.0)
        l = jnp.sum(e, axis=0, keepdims=True)
        p_cmp = e * jnp.where(l > 0.0, 1.0 / l, 0.0)
        o_cmp.append(lax.dot_general(vc_ref[0], p_cmp.astype(BF16), _TN, preferred_element_type=F32))

        p_sum = p_cmp[:, 0:tq]
        for g in range(1, NSA_GROUP):
            p_sum = p_sum + p_cmp[:, g * tq:(g + 1) * tq]
        p_hi = p_sum.astype(BF16)
        p_lo = (p_sum - p_hi.astype(F32)).astype(BF16)
        p_slc = (jnp.dot(ovt, p_hi, preferred_element_type=F32)
                 + jnp.dot(ovt, p_lo, preferred_element_type=F32))
        score = jnp.where(forced, 1e4, jnp.where(j_idx > cur, -1e4, p_slc))
        rank = jnp.zeros((N_SEL, tq), jnp.int32)
        for i in range(N_SEL):
            row = score[i:i + 1, :]
            tie = jnp.where(j_idx > i, 1, 0)
            rank = rank + jnp.where(row > score, 1, jnp.where(row == score, tie, 0))
        bias = jnp.where((rank < SLC_TOP) & (j_idx <= cur), 0.0, NEG)
        bias_ref[kv] = jnp.concatenate([bias] * NSA_GROUP, axis=1)

    blocks_per_tile = tq // SLC_LEN

    def slc_chain(kv):
        def scores(kt):
            k0 = pl.multiple_of(kt * tq, tq)
            s = lax.dot_general(ks_ref[pl.ds(k0, tq), head_cols(kv)], qs[kv], _NT, preferred_element_type=F32)
            parts = [s[c * SLC_LEN:(c + 1) * SLC_LEN, :] + bias_ref[kv, pl.ds(kt * blocks_per_tile + c, 1), :]
                     for c in range(blocks_per_tile)]
            return jnp.concatenate(parts, axis=0)

        def values(kt):
            return vs_ref[pl.ds(pl.multiple_of(kt * tq, tq), tq), head_cols(kv)]

        return scores, values

    o_slc = _causal_flash([slc_chain(kv) for kv in kvs], qi, causal, HEAD_DIM, nl)

    win_p, win_l, win_v = [], [], []
    for kv in kvs:
        scores, values = [], []
        for back in range(WINDOW // tq, -1, -1):
            k0 = start - back * tq
            inside = k0 >= 0
            k0 = pl.multiple_of(jnp.maximum(k0, 0), tq)
            s = lax.dot_general(kw_ref[pl.ds(k0, tq), head_cols(kv)], qs[kv], _NT, preferred_element_type=F32)
            if back == 0:
                s = jnp.where(causal, s, NEG)
            elif back == WINDOW // tq:
                edge = q_local + jnp.where(inside, 0, tq)
                s = jnp.where(k_local > edge, s, NEG)
            else:
                s = s + jnp.where(inside, 0.0, NEG)
            scores.append(s)
            values.append(vw_ref[pl.ds(k0, tq), head_cols(kv)])
        s = jnp.concatenate(scores, axis=0)
        p = jnp.exp2(s - jnp.max(s, axis=0, keepdims=True))
        win_l.append(jnp.sum(p, axis=0, keepdims=True))
        win_p.append(p.astype(BF16))
        win_v.append(jnp.concatenate(values, axis=0))

    gates_t = jax.nn.sigmoid(gate_ref[...].astype(F32)).T
    for kv in kvs:
        acc_w = lax.dot_general(win_v[kv], win_p[kv], _TN, preferred_element_type=F32)
        o_win = acc_w * (1.0 / win_l[kv])

        gts = gates_t[kv * GATE_STRIDE:(kv + 1) * GATE_STRIDE]
        for g in range(NSA_GROUP):
            sl = slice(g * tq, (g + 1) * tq)
            o_t = (gts[g:g + 1, :] * o_cmp[kv][:, sl]
                   + gts[NSA_GROUP + g:NSA_GROUP + g + 1, :] * o_slc[kv][:, sl]
                   + gts[2 * NSA_GROUP + g:2 * NSA_GROUP + g + 1, :] * o_win[:, sl])
            head = kv * NSA_GROUP + g
            o_ref[:, head * HEAD_DIM:(head + 1) * HEAD_DIM] = o_t.T


def _nsa_attention(proj, cmp_kv, ovt, batch, *, tq=256):
    t = proj.shape[0]
    assert WINDOW % tq == 0 and tq % SLC_LEN == 0 and SEQ % tq == 0
    nq = SEQ // tq
    est = (4 * 2 * SEQ * NSA_KV_WIDTH * 2 + 2 * tq * NSA_WIDTH * (2 + 4)
           + N_NSA_KV * 10 * (WINDOW + tq) * NSA_GROUP * tq * 4)

    def kv_spec(col):
        return pl.BlockSpec((SEQ, NSA_KV_WIDTH), lambda b, i: (b, col // NSA_KV_WIDTH))

    def cmp_spec(which, kv):
        return pl.BlockSpec((1, N_CMP_PAD, HEAD_DIM), lambda b, i: (which, kv * batch + b, 0))

    return pl.pallas_call(
        functools.partial(_nsa_kernel, tq=tq),
        grid=(batch, nq),
        in_specs=[
            pl.BlockSpec((tq, NSA_WIDTH), lambda b, i: (b * nq + i, 0)),
            cmp_spec(0, 0), cmp_spec(1, 0), cmp_spec(0, 1), cmp_spec(1, 1),
            kv_spec(COL_KS), kv_spec(COL_VS), kv_spec(COL_KW), kv_spec(COL_VW),
            pl.BlockSpec((tq, HEAD_DIM), lambda b, i: (b * nq + i, COL_GATE // HEAD_DIM)),
            pl.BlockSpec((N_SEL, N_CMP_PAD), lambda b, i: (0, 0)),
        ],
        out_specs=pl.BlockSpec((tq, NSA_WIDTH), lambda b, i: (b * nq + i, 0)),
        out_shape=jax.ShapeDtypeStruct((t, NSA_WIDTH), F32),
        scratch_shapes=[pltpu.VMEM((N_NSA_KV, N_SEL, NSA_GROUP * tq), F32)],
        compiler_params=_params(("parallel", "arbitrary"), est),
        name="nsa_attn",
    )(proj, cmp_kv, cmp_kv, cmp_kv, cmp_kv, proj, proj, proj, proj, proj, ovt)


def _diff_kernel(q_ref, k_ref, v_ref, lq1_ref, lk1_ref, lq2_ref, lk2_ref, sub_ref, o_ref,
                 *, tq, lambda_init):
    qi = pl.program_id(2)
    wide = 2 * HEAD_DIM
    lam = (jnp.exp(jnp.sum(lq1_ref[...] * lk1_ref[...], axis=-1, keepdims=True))
           - jnp.exp(jnp.sum(lq2_ref[...] * lk2_ref[...], axis=-1, keepdims=True)) + lambda_init)
    q = q_ref[...]

    def chain(h):
        q1 = q[:, h * wide:h * wide + HEAD_DIM]
        q2 = q[:, h * wide + HEAD_DIM:(h + 1) * wide]

        def scores(kt):
            kk = k_ref[pl.ds(pl.multiple_of(kt * tq, tq), tq), h * wide:(h + 1) * wide]
            s1 = lax.dot_general(kk[:, :HEAD_DIM], q1, _NT, preferred_element_type=F32)
            s2 = lax.dot_general(kk[:, HEAD_DIM:], q2, _NT, preferred_element_type=F32)
            return jnp.concatenate([s1, s2], axis=1)

        def values(kt):
            return v_ref[pl.ds(pl.multiple_of(kt * tq, tq), tq), h * wide:(h + 1) * wide]

        return scores, values

    k_local = lax.broadcasted_iota(jnp.int32, (tq, 2 * tq), 0)
    q_local = lax.broadcasted_iota(jnp.int32, (1, 2 * tq), 1) & (tq - 1)
    outs = _causal_flash([chain(h) for h in range(DIFF_HEADS_PER_STEP)], qi, k_local <= q_local, wide, 2 * tq)
    for h, o_n in enumerate(outs):
        o_t = o_n[:, :tq] - lam * o_n[:, tq:]
        o = _rmsnorm(o_t.T, sub_ref[...]) * (1.0 - lambda_init)
        o_ref[:, h * wide:(h + 1) * wide] = o.astype(o_ref.dtype)


DIFF_HEADS_PER_STEP = 4


def _diff_attention(proj, lq1, lk1, lq2, lk2, subln, batch, lambda_init, *, tq=512):
    t = proj.shape[0]
    nq = SEQ // tq
    wide = 2 * HEAD_DIM
    step_cols = DIFF_HEADS_PER_STEP * wide
    assert COL_DQ % step_cols == 0 and COL_DK % step_cols == 0 and COL_DV % step_cols == 0
    est = 2 * 2 * SEQ * step_cols * 2 + 4 * tq * step_cols * 4 + DIFF_HEADS_PER_STEP * 16 * tq * 2 * tq * 4
    vec = pl.BlockSpec((1, HEAD_DIM), lambda b, h, i: (0, 0))
    return pl.pallas_call(
        functools.partial(_diff_kernel, tq=tq, lambda_init=lambda_init),
        grid=(batch, N_DIFF_HEADS // DIFF_HEADS_PER_STEP, nq),
        in_specs=[
            pl.BlockSpec((tq, step_cols), lambda b, h, i: (b * nq + i, COL_DQ // step_cols + h)),
            pl.BlockSpec((SEQ, step_cols), lambda b, h, i: (b, COL_DK // step_cols + h)),
            pl.BlockSpec((SEQ, step_cols), lambda b, h, i: (b, COL_DV // step_cols + h)),
            vec, vec, vec, vec,
            pl.BlockSpec((1, wide), lambda b, h, i: (0, 0)),
        ],
        out_specs=pl.BlockSpec((tq, step_cols), lambda b, h, i: (b * nq + i, h)),
        out_shape=jax.ShapeDtypeStruct((t, DIFF_WIDTH), BF16),
        compiler_params=_params(("parallel", "parallel", "arbitrary"), est),
        name="diff_attn",
    )(proj, proj, proj, lq1, lk1, lq2, lk2, subln)


def _oproj_kernel(yn_ref, g_ref, yd_ref, wo_ref, x_ref, o_ref):
    yn = _rmsnorm(yn_ref[...], g_ref[...]).astype(BF16)
    acc = jnp.dot(yn, wo_ref[:NSA_WIDTH, :], preferred_element_type=F32)
    acc = acc + jnp.dot(yd_ref[...], wo_ref[NSA_WIDTH:, :], preferred_element_type=F32)
    o_ref[...] = x_ref[...] + acc


def _out_proj(y_nsa, g, y_diff, w_o, x2, *, tm=512):
    t = x2.shape[0]
    est = (2 * (NSA_WIDTH + DIFF_WIDTH) * D_MODEL * 2 + 2 * tm * NSA_WIDTH * 4 + 2 * tm * DIFF_WIDTH * 2
           + 5 * tm * D_MODEL * 4)
    return pl.pallas_call(
        _oproj_kernel,
        grid=(t // tm,),
        in_specs=[
            pl.BlockSpec((tm, NSA_WIDTH), lambda i: (i, 0)),
            pl.BlockSpec((1, NSA_WIDTH), lambda i: (0, 0)),
            pl.BlockSpec((tm, DIFF_WIDTH), lambda i: (i, 0)),
            pl.BlockSpec((NSA_WIDTH + DIFF_WIDTH, D_MODEL), lambda i: (0, 0)),
            pl.BlockSpec((tm, D_MODEL), lambda i: (i, 0)),
        ],
        out_specs=pl.BlockSpec((tm, D_MODEL), lambda i: (i, 0)),
        out_shape=jax.ShapeDtypeStruct((t, D_MODEL), F32),
        compiler_params=_params(("parallel",), est),
        name="out_proj",
    )(y_nsa, g, y_diff, w_o, x2)


HALO = 8
FFN_CHUNK = 256
FFN_ROW_BLOCKS = 4


def _ffn_kernel(h_ref, halo_ref, g_ref, wu_ref, wg_ref, cwu_ref, cwg_ref, cbu_ref, cbg_ref, wd_ref,
                o_ref, xn_ref, raw_ref, act_ref, *, tm):
    i = pl.program_id(0)
    j = pl.program_id(1)

    @pl.when(j == 0)
    def _():
        keep = jnp.where((i * tm) % SEQ == 0, 0.0, 1.0)
        h = h_ref[...]
        xn_ref[0:HALO, :] = (_rmsnorm(halo_ref[...], g_ref[...]) * keep).astype(BF16)
        xn_ref[HALO:, :] = _rmsnorm(h, g_ref[...]).astype(BF16)
        o_ref[...] = h

    xn = xn_ref[...]

    def conv(h, cw_ref, cb_ref, cols):
        cw = cw_ref[:, cols]
        h3 = h.reshape(h.shape[0] // SUBLANES, SUBLANES, h.shape[1])
        sub = lax.broadcasted_iota(jnp.int32, h3.shape[1:], 0)
        out = cw[CONV_W - 1:CONV_W, :] * h3[1:] + cb_ref[:, cols]
        for back in range(1, CONV_W):
            rolled = pltpu.roll(h3, back, 1)
            shifted = jnp.where(sub < back, rolled[:-1], rolled[1:])
            out = out + cw[CONV_W - 1 - back:CONV_W - back, :] * shifted
        return out.reshape(h.shape[0] - HALO, h.shape[1])

    def gated(rows, cols):
        src = slice(rows.start, rows.stop + HALO)
        u = conv(raw_ref[slot, 0, src, cols], cwu_ref, cbu_ref, cols)
        gate = conv(raw_ref[slot, 1, src, cols], cwg_ref, cbg_ref, cols)
        act_ref[rows, cols] = (jax.nn.silu(gate) * u).astype(BF16)

    slot = j % 2
    chunks = [slice(lo, lo + FFN_CHUNK) for lo in range(0, wu_ref.shape[1], FFN_CHUNK)]
    row_blocks = [slice(lo, lo + tm // FFN_ROW_BLOCKS) for lo in range(0, tm, tm // FFN_ROW_BLOCKS)]
    for cols in chunks:
        raw_ref[slot, 0, :, cols] = jnp.dot(xn, wu_ref[:, cols], preferred_element_type=F32)
        raw_ref[slot, 1, :, cols] = jnp.dot(xn, wg_ref[:, cols], preferred_element_type=F32)
        if cols is not chunks[-1]:
            gated(slice(0, tm), cols)
    gated(row_blocks[0], chunks[-1])
    for n, rows in enumerate(row_blocks):
        if n + 1 < len(row_blocks):
            gated(row_blocks[n + 1], chunks[-1])
        o_ref[rows, :] += jnp.dot(act_ref[rows, :], wd_ref[...], preferred_element_type=F32)


def _conv_ffn(h1, g, w_up, conv_w, conv_b, w_down, *, tm=1024, tf=512):
    t = h1.shape[0]
    assert t % tm == 0 and SEQ % tm == 0 and D_FF % tf == 0
    nf = D_FF // tf
    est = (4 * tm * D_MODEL * 4 + 2 * 3 * D_MODEL * tf * 2 + (tm + HALO) * D_MODEL * 2
           + 8 * (tm + HALO) * tf * 4)
    return pl.pallas_call(
        functools.partial(_ffn_kernel, tm=tm),
        grid=(t // tm, nf),
        in_specs=[
            pl.BlockSpec((tm, D_MODEL), lambda i, j: (i, 0)),
            pl.BlockSpec((HALO, D_MODEL), lambda i, j: (jnp.maximum(i * (tm // HALO) - 1, 0), 0)),
            pl.BlockSpec((1, D_MODEL), lambda i, j: (0, 0)),
            pl.BlockSpec((D_MODEL, tf), lambda i, j: (0, j)),
            pl.BlockSpec((D_MODEL, tf), lambda i, j: (0, nf + j)),
            pl.BlockSpec((CONV_W, tf), lambda i, j: (0, j)),
            pl.BlockSpec((CONV_W, tf), lambda i, j: (0, nf + j)),
            pl.BlockSpec((1, tf), lambda i, j: (0, j)),
            pl.BlockSpec((1, tf), lambda i, j: (0, nf + j)),
            pl.BlockSpec((tf, D_MODEL), lambda i, j: (j, 0)),
        ],
        out_specs=pl.BlockSpec((tm, D_MODEL), lambda i, j: (i, 0), pipeline_mode=pl.Buffered(1)),
        out_shape=jax.ShapeDtypeStruct((t, D_MODEL), F32),
        scratch_shapes=[pltpu.VMEM((tm + HALO, D_MODEL), BF16), pltpu.VMEM((2, 2, tm + HALO, tf), F32),
                        pltpu.VMEM((tm, tf), BF16)],
        compiler_params=_params(("parallel", "arbitrary"), est),
        name="conv_ffn",
    )(h1, h1, g, w_up, w_up, conv_w, conv_w, conv_b, conv_b, w_down)


PLE_CHUNK = 512


def _ple_kernel(h_ref, gp_ref, wg_ref, p_ref, wp_ref, gf_ref, o_ref):
    hn = _rmsnorm(h_ref[...], gp_ref[...]).astype(BF16)
    pb = p_ref[...].astype(BF16)
    ssq = None
    for lo in range(0, D_MODEL, PLE_CHUNK):
        cols = slice(lo, lo + PLE_CHUNK)
        gate = jax.nn.sigmoid(jnp.dot(hn, wg_ref[:, cols], preferred_element_type=F32))
        emb = jnp.dot(pb, wp_ref[:, cols], preferred_element_type=F32)
        h3 = h_ref[:, cols] + gate * emb
        o_ref[:, cols] = h3
        part = jnp.sum(h3 * h3, axis=-1, keepdims=True)
        ssq = part if ssq is None else ssq + part
    o_ref[...] = o_ref[...] * lax.rsqrt(ssq * (1.0 / D_MODEL) + EPS) * gf_ref[...]


def _ple_out(h2, g_ple, w_gate, p2, w_proj, g_final, *, tm=512):
    t = h2.shape[0]
    est = D_MODEL * D_MODEL * 2 + 2 * PLE_DIM * D_MODEL * 2 + 4 * tm * D_MODEL * 4 + 6 * tm * D_MODEL * 4
    return pl.pallas_call(
        _ple_kernel,
        grid=(t // tm,),
        in_specs=[
            pl.BlockSpec((tm, D_MODEL), lambda i: (i, 0)),
            pl.BlockSpec((1, D_MODEL), lambda i: (0, 0)),
            pl.BlockSpec((D_MODEL, D_MODEL), lambda i: (0, 0), pipeline_mode=pl.Buffered(1)),
            pl.BlockSpec((tm, PLE_DIM), lambda i: (i, 0)),
            pl.BlockSpec((PLE_DIM, D_MODEL), lambda i: (0, 0)),
            pl.BlockSpec((1, D_MODEL), lambda i: (0, 0)),
        ],
        out_specs=pl.BlockSpec((tm, D_MODEL), lambda i: (i, 0)),
        out_shape=jax.ShapeDtypeStruct((t, D_MODEL), F32),
        compiler_params=_params(("parallel",), est),
        name="ple_out",
    )(h2, g_ple, w_gate, p2, w_proj, g_final)


def _split_w_in(w):
    w_a, gate, w_b = _cast_w_in(w)
    gate = gate[:, :N_GATES].reshape(D_MODEL, 3, N_NSA_KV, NSA_GROUP).transpose(0, 2, 1, 3)
    gate = jnp.pad(gate.reshape(D_MODEL, N_NSA_KV, 3 * NSA_GROUP), ((0, 0), (0, 0), (0, GATE_STRIDE - 3 * NSA_GROUP)))
    gate = jnp.pad(gate.reshape(D_MODEL, N_NSA_KV * GATE_STRIDE), ((0, 0), (0, HEAD_DIM - N_NSA_KV * GATE_STRIDE)))
    return w_a, gate, w_b


CAST_BLOCK = 512


def _cast_w_in_kernel(cur_ref, nxt_ref, wa_ref, wg_ref, wb_ref, *, n_a):
    s = pl.program_id(0)

    @pl.when(s < n_a)
    def _():
        wa_ref[...] = cur_ref[...].T.astype(BF16)

    @pl.when((s >= n_a) & (s < pl.num_programs(0) - 1))
    def _():
        rows = jnp.concatenate([cur_ref[N_GATES:, :], nxt_ref[0:N_GATES, :]], axis=0)
        wb_ref[...] = rows.T.astype(BF16)

    @pl.when(s == pl.num_programs(0) - 1)
    def _():
        wg_ref[...] = cur_ref[0:HEAD_DIM, :].T.astype(BF16)


def _cast_w_in(w):
    d, n = w.shape
    nsa_cols = NSA_WIDTH + 6 * NSA_KV_WIDTH
    diff_cols = n - nsa_cols - N_GATES
    assert nsa_cols % CAST_BLOCK == 0 and diff_cols % CAST_BLOCK == 0 and N_GATES % SUBLANES == 0
    n_a, n_b = nsa_cols // CAST_BLOCK, diff_cols // CAST_BLOCK
    spill_rows = 32
    assert N_GATES <= spill_rows and CAST_BLOCK % spill_rows == 0
    est = 2 * CAST_BLOCK * d * 4 + 2 * 2 * CAST_BLOCK * d * 2 + 6 * CAST_BLOCK * d * 4

    def in_block(s):
        return jnp.where(s == n_a + n_b, n_a, s)

    return pl.pallas_call(
        functools.partial(_cast_w_in_kernel, n_a=n_a),
        grid=(n_a + n_b + 1,),
        in_specs=[pl.BlockSpec((CAST_BLOCK, d), lambda s: (in_block(s), 0)),
                  pl.BlockSpec((spill_rows, d),
                               lambda s: (jnp.clip(s + 1, n_a + 1, n_a + n_b) * (CAST_BLOCK // spill_rows), 0))],
        out_specs=[pl.BlockSpec((d, CAST_BLOCK), lambda s: (0, jnp.minimum(s, n_a - 1))),
                   pl.BlockSpec((d, HEAD_DIM), lambda s: (0, 0)),
                   pl.BlockSpec((d, CAST_BLOCK), lambda s: (0, jnp.clip(s - n_a, 0, n_b - 1)))],
        out_shape=[jax.ShapeDtypeStruct((d, nsa_cols), BF16), jax.ShapeDtypeStruct((d, HEAD_DIM), BF16),
                   jax.ShapeDtypeStruct((d, diff_cols), BF16)],
        compiler_params=_params(("arbitrary",), est),
        name="cast_w_in",
    )(w.T, w.T)


def _rope_tables():
    inv = 1.0 / (ROPE_THETA ** (jnp.arange(0, ROPE_DIM, 2, dtype=F32) / ROPE_DIM))
    ang = jnp.arange(SEQ, dtype=F32)[:, None] * inv[None, :]
    cos, sin = jnp.cos(ang), jnp.sin(ang)
    rest = HEAD_DIM - ROPE_DIM
    cos_t = jnp.concatenate([cos, cos, jnp.ones((SEQ, rest), F32)], axis=1)
    sin_t = jnp.concatenate([-sin, sin, jnp.zeros((SEQ, rest), F32)], axis=1)
    return cos_t, sin_t


def _overlap_t():
    cmp_starts = np.arange(N_CMP) * CMP_STRIDE
    sel_starts = np.arange(N_SEL) * SLC_LEN
    ov = np.clip(np.minimum(cmp_starts[:, None] + CMP_LEN, sel_starts[None, :] + SLC_LEN)
                 - np.maximum(cmp_starts[:, None], sel_starts[None, :]), 0, None).astype(np.float32) / CMP_LEN
    ovt = np.zeros((N_SEL, N_CMP_PAD), np.float32)
    ovt[:, :N_CMP] = ov.T
    return jnp.asarray(ovt, BF16)


def kernel(x, p, attn_norm, w_in, cmp_k_pos, cmp_k_w1, cmp_k_w2, cmp_v_pos, cmp_v_w1, cmp_v_w2, nsa_out_norm, diff_lq1, diff_lk1, diff_lq2, diff_lk2, diff_subln, w_o, ffn_norm, w_up, conv_w, conv_b, w_down, ple_norm, w_ple_gate, w_ple_proj, final_norm):
    batch, seq, _ = x.shape
    assert seq == SEQ and p.shape[0] == 1
    t = batch * seq
    layer = 0
    lambda_init = 0.8 - 0.6 * math.exp(-0.3 * layer)
    x2 = x.reshape(t, D_MODEL)
    cos_t, sin_t = _rope_tables()

    proj, hkv, w_o_b, w_up_b, w_down_b, w_gate_b = _in_proj(
        x2, attn_norm[layer][None], *_split_w_in(w_in[layer]), cos_t, sin_t,
        later_weights=(w_o[layer], w_up[layer], w_down[layer], w_ple_gate[layer]))

    hkv = hkv.reshape(2, N_NSA_KV * t // HALF_BLOCK, HALF_BLOCK * HEAD_DIM)
    w1 = jnp.stack([cmp_k_w1[layer], cmp_v_w1[layer]]).astype(BF16)
    w2 = jnp.stack([cmp_k_w2[layer], cmp_v_w2[layer]]).astype(BF16)
    pos = jnp.stack([cmp_k_pos[layer], cmp_v_pos[layer]]).reshape(2, 1, CMP_LEN * HEAD_DIM)
    pos = jnp.broadcast_to(pos, (2, 8, CMP_LEN * HEAD_DIM)).astype(BF16)
    cmp_kv = _compress(hkv, w1, pos, w2)

    y_nsa = _nsa_attention(proj, cmp_kv, _overlap_t(), batch)
    y_diff = _diff_attention(proj, diff_lq1[layer][None], diff_lk1[layer][None], diff_lq2[layer][None],
                             diff_lk2[layer][None], diff_subln[layer][None], batch, lambda_init)
    h1 = _out_proj(y_nsa, nsa_out_norm[layer][None], y_diff, w_o_b, x2)
    h2 = _conv_ffn(h1, ffn_norm[layer][None], w_up_b, conv_w[layer], conv_b[layer][None], w_down_b)
    out = _ple_out(h2, ple_norm[layer][None], w_gate_b, p[layer].reshape(t, PLE_DIM),
                   w_ple_proj[layer].astype(BF16), final_norm[None])
    return out.reshape(batch, seq, D_MODEL)
```

```python
import functools
import math

import numpy as np
import jax
import jax.numpy as jnp
from jax import lax
from jax.experimental import pallas as pl
from jax.experimental.pallas import tpu as pltpu

D_MODEL = 2048
SEQ = 2048
HEAD_DIM = 128
ROPE_DIM = HEAD_DIM // 4
ROPE_THETA = 500000.0
N_NSA_HEADS = 8
N_NSA_KV = 2
NSA_GROUP = N_NSA_HEADS // N_NSA_KV
CMP_LEN = 32
CMP_STRIDE = 16
CMP_HIDDEN = 256
SLC_LEN = 64
SLC_TOP = 16
WINDOW = 512
N_DIFF_HEADS = 4
D_FF = 5632
CONV_W = 3
PLE_DIM = 256
EPS = 1e-6

NSA_WIDTH = N_NSA_HEADS * HEAD_DIM
NSA_KV_WIDTH = N_NSA_KV * HEAD_DIM
DIFF_WIDTH = N_DIFF_HEADS * 2 * HEAD_DIM
N_GATES = 3 * N_NSA_HEADS
N_CMP = (SEQ - CMP_LEN) // CMP_STRIDE + 1
N_CMP_PAD = SEQ // CMP_STRIDE
N_SEL = SEQ // SLC_LEN
GATE_STRIDE = 16

COL_NQ = 0
COL_DQ = COL_NQ + NSA_WIDTH
COL_DK = COL_DQ + DIFF_WIDTH
COL_DV = COL_DK + DIFF_WIDTH
COL_KS = COL_DV + DIFF_WIDTH
COL_KW = COL_KS + NSA_KV_WIDTH
COL_VS = COL_KW + NSA_KV_WIDTH
COL_VW = COL_VS + NSA_KV_WIDTH
COL_GATE = COL_VW + NSA_KV_WIDTH
PROJ_COLS = COL_GATE + HEAD_DIM

V7X_LANES = 128
SUBLANES = 8
BF16_ROWS = 16
V7X_VMEM_REQUEST_CAP = 56 * 1024 * 1024
NEG = -1e30
QSCALE = HEAD_DIM ** -0.5 * math.log2(math.e)

_NT = (((1,), (1,)), ((), ()))
_TN = (((0,), (0,)), ((), ()))
BF16 = jnp.bfloat16
F32 = jnp.float32


def _params(semantics, vmem_estimate_bytes, flags=None):
    limit = min(max(int(vmem_estimate_bytes), 32 * 1024 * 1024), V7X_VMEM_REQUEST_CAP)
    return pltpu.CompilerParams(dimension_semantics=semantics, vmem_limit_bytes=limit, flags=flags)


def _rmsnorm(x, g):
    return x * lax.rsqrt(jnp.mean(x * x, axis=-1, keepdims=True) + EPS) * g


INPROJ_CHUNK = 1024
HALF_BLOCK = CMP_STRIDE

_HEADS_A = ([(COL_NQ + h * HEAD_DIM, True, True) for h in range(N_NSA_HEADS)]
            + [(("cmp", 0, h), True, False) for h in range(N_NSA_KV)]
            + [(("cmp", 1, h), False, False) for h in range(N_NSA_KV)]
            + [(COL_KS + h * HEAD_DIM, True, False) for h in range(N_NSA_KV)]
            + [(COL_VS + h * HEAD_DIM, False, False) for h in range(N_NSA_KV)]
            + [(COL_KW + h * HEAD_DIM, True, False) for h in range(N_NSA_KV)]
            + [(COL_VW + h * HEAD_DIM, False, False) for h in range(N_NSA_KV)])
_HEADS_B = ([(COL_DQ + h * HEAD_DIM, True, True) for h in range(2 * N_DIFF_HEADS)]
            + [(COL_DK + h * HEAD_DIM, True, False) for h in range(2 * N_DIFF_HEADS)]
            + [(COL_DV + h * HEAD_DIM, False, False) for h in range(2 * N_DIFF_HEADS)])
_HEADS_GATE = [(COL_GATE, False, False)]


def _inproj_kernel(*refs, n_cast):
    x_ref, g_ref, wa_ref, wg_ref, wb_ref, cos_ref, sin_ref = refs[:7]
    cast_in = refs[7:7 + n_cast]
    o_ref, hkv_ref = refs[7 + n_cast:9 + n_cast]
    cast_out = refs[9 + n_cast:9 + 2 * n_cast]
    stage_ref = refs[9 + 2 * n_cast]
    for src_ref, dst_ref in zip(cast_in, cast_out):
        dst_ref[...] = src_ref[...].astype(dst_ref.dtype)
    tm = x_ref.shape[0]
    xn = _rmsnorm(x_ref[...], g_ref[...]).astype(BF16)
    c = cos_ref[...]
    s = sin_ref[...]
    cq = c * QSCALE
    sq = s * QSCALE
    first_half = lax.broadcasted_iota(jnp.int32, c.shape, 1) < ROPE_DIM // 2
    n_staged = 0
    for w_ref, heads in ((wa_ref, _HEADS_A), (wg_ref, _HEADS_GATE), (wb_ref, _HEADS_B)):
        for lo in range(0, w_ref.shape[1], INPROJ_CHUNK):
            hi = min(lo + INPROJ_CHUNK, w_ref.shape[1])
            acc = jnp.dot(xn, w_ref[:, lo:hi], preferred_element_type=F32)
            for src in range(lo, hi, HEAD_DIM):
                dest, rope, is_query = heads[src // HEAD_DIM]
                a = acc[:, src - lo:src - lo + HEAD_DIM]
                if rope:
                    partner = jnp.where(first_half,
                                        pltpu.roll(a, HEAD_DIM - ROPE_DIM // 2, 1),
                                        pltpu.roll(a, ROPE_DIM // 2, 1))
                    a = a * cq + partner * sq if is_query else a * c + partner * s
                if isinstance(dest, tuple):
                    _, which, head = dest
                    stage = stage_ref.at[n_staged]
                    n_staged += 1
                    stage[...] = a
                    for r in range(HALF_BLOCK):
                        rows = stage[pl.ds(r, tm // HALF_BLOCK, stride=HALF_BLOCK), :]
                        hkv_ref[which, head, :, r * HEAD_DIM:(r + 1) * HEAD_DIM] = rows.astype(hkv_ref.dtype)
                else:
                    o_ref[:, dest:dest + HEAD_DIM] = a.astype(o_ref.dtype)


def _in_proj(x2, g, w, w_g, w_b, cos_t, sin_t, later_weights=(), *, tm=256):
    t = x2.shape[0]
    assert t % tm == 0 and SEQ % tm == 0 and tm % (HALF_BLOCK * BF16_ROWS) == 0
    n_steps = t // tm
    seq_tiles = SEQ // tm
    nsa_cols = NSA_WIDTH + 6 * NSA_KV_WIDTH
    n_w = nsa_cols + HEAD_DIM + w_b.shape[1]
    est = (D_MODEL * n_w * 2 + 2 * tm * D_MODEL * 4 + 2 * tm * n_w * 2 + tm * D_MODEL * 2
           + 3 * tm * INPROJ_CHUNK * 4)
    cast_specs = []
    for lw in later_weights:
        assert lw.shape[0] % (n_steps * BF16_ROWS) == 0
        cast_specs.append(pl.BlockSpec((lw.shape[0] // n_steps, lw.shape[1]), lambda i: (i, 0)))
        est += 2 * (lw.size // n_steps) * (4 + 2)
    resident = dict(pipeline_mode=pl.Buffered(1))
    half_cols = HALF_BLOCK * HEAD_DIM
    return pl.pallas_call(
        functools.partial(_inproj_kernel, n_cast=len(later_weights)),
        grid=(n_steps,),
        in_specs=[
            pl.BlockSpec((tm, D_MODEL), lambda i: (i, 0)),
            pl.BlockSpec((1, D_MODEL), lambda i: (0, 0)),
            pl.BlockSpec((D_MODEL, nsa_cols), lambda i: (0, 0), **resident),
            pl.BlockSpec(w_g.shape, lambda i: (0, 0), **resident),
            pl.BlockSpec(w_b.shape, lambda i: (0, 0), **resident),
            pl.BlockSpec((tm, HEAD_DIM), lambda i: (i % seq_tiles, 0)),
            pl.BlockSpec((tm, HEAD_DIM), lambda i: (i % seq_tiles, 0)),
        ] + cast_specs,
        out_specs=[
            pl.BlockSpec((tm, PROJ_COLS), lambda i: (i, 0)),
            pl.BlockSpec((2, N_NSA_KV, tm // HALF_BLOCK, half_cols), lambda i: (0, 0, i, 0)),
        ] + cast_specs,
        out_shape=[
            jax.ShapeDtypeStruct((t, PROJ_COLS), BF16),
            jax.ShapeDtypeStruct((2, N_NSA_KV, t // HALF_BLOCK, half_cols), BF16),
        ] + [jax.ShapeDtypeStruct(lw.shape, BF16) for lw in later_weights],
        scratch_shapes=[pltpu.VMEM((2 * N_NSA_KV, tm, HEAD_DIM), F32)],
        compiler_params=_params(("parallel",), est),
        name="in_proj",
    )(x2, g, w, w_g, w_b, cos_t, sin_t, *later_weights)


def _compress_kernel(h_ref, w1_ref, pos_ref, w2_ref, o_ref):
    half = CMP_LEN * HEAD_DIM // 2
    h = h_ref[0]
    top = jnp.dot(h, w1_ref[0, :half, :], preferred_element_type=F32)
    bot = jnp.dot(h, w1_ref[0, half:, :], preferred_element_type=F32)
    pos_bias = jnp.dot(pos_ref[0], w1_ref[0], preferred_element_type=F32)[0:1]
    pre = top + pltpu.roll(bot, bot.shape[0] - 1, 0) + pos_bias
    act = jax.nn.gelu(pre)
    o_ref[0] = jnp.dot(act.astype(BF16), w2_ref[0], preferred_element_type=F32).astype(o_ref.dtype)


def _compress(hkv, w1, pos, w2):
    rows = hkv.shape[1]
    kdim = CMP_LEN * HEAD_DIM
    est = 2 * (rows * kdim // 2 * 2 + kdim * CMP_HIDDEN * 2) + 6 * rows * CMP_HIDDEN * 4
    return pl.pallas_call(
        _compress_kernel,
        grid=(2,),
        in_specs=[
            pl.BlockSpec((1, rows, kdim // 2), lambda i: (i, 0, 0)),
            pl.BlockSpec((1, kdim, CMP_HIDDEN), lambda i: (i, 0, 0)),
            pl.BlockSpec((1, 8, kdim), lambda i: (i, 0, 0)),
            pl.BlockSpec((1, CMP_HIDDEN, HEAD_DIM), lambda i: (i, 0, 0)),
        ],
        out_specs=pl.BlockSpec((1, rows, HEAD_DIM), lambda i: (i, 0, 0)),
        out_shape=jax.ShapeDtypeStruct((2, rows, HEAD_DIM), BF16),
        compiler_params=_params(("parallel",), est),
        name="compress",
    )(hkv, w1, pos, w2)


def _causal_flash(chains, n_before, diag_mask, d, lanes):
    def tile_step(kt, carries, mask=None):
        stats = []
        probs = []
        for (score_fn, _), (m, l, _) in zip(chains, carries):
            s = score_fn(kt)
            if mask is not None:
                s = jnp.where(mask, s, NEG)
            m_new = jnp.maximum(m, jnp.max(s, axis=0, keepdims=True))
            alpha = jnp.exp2(m - m_new)
            p = jnp.exp2(s - m_new)
            stats.append((m_new, alpha * l + jnp.sum(p, axis=0, keepdims=True), alpha))
            probs.append(p.astype(BF16))
        out = []
        for (_, value_fn), (_, _, acc), (m_new, l_new, alpha), p in zip(chains, carries, stats, probs):
            pv = lax.dot_general(value_fn(kt), p, _TN, preferred_element_type=F32)
            out.append((m_new, l_new, alpha * acc + pv))
        return tuple(out)

    init = (jnp.full((1, lanes), NEG, F32), jnp.zeros((1, lanes), F32), jnp.zeros((d, lanes), F32))
    carries = lax.fori_loop(0, n_before, tile_step, (init,) * len(chains))
    return [acc * (1.0 / l) for _, l, acc in tile_step(n_before, carries, diag_mask)]


def _nsa_kernel(q_ref, kc0_ref, vc0_ref, kc1_ref, vc1_ref, ks_ref, vs_ref, kw_ref, vw_ref, gate_ref, ovt_ref,
                o_ref, *, tq):
    qi = pl.program_id(1)
    start = qi * tq
    nl = NSA_GROUP * tq
    kvs = range(N_NSA_KV)
    cmp_refs = ((kc0_ref, vc0_ref), (kc1_ref, vc1_ref))

    def head_cols(kv):
        return slice(kv * HEAD_DIM, (kv + 1) * HEAD_DIM)

    q_all = q_ref[...]
    qs = [jnp.concatenate([q_all[:, (kv * NSA_GROUP + g) * HEAD_DIM:(kv * NSA_GROUP + g + 1) * HEAD_DIM]
                           for g in range(NSA_GROUP)], axis=0) for kv in kvs]
    q_local = lax.broadcasted_iota(jnp.int32, (1, nl), 1) & (tq - 1)
    t_lane = start + q_local
    k_local = lax.broadcasted_iota(jnp.int32, (tq, nl), 0)
    causal = k_local <= q_local
    c_end = lax.broadcasted_iota(jnp.int32, (N_CMP_PAD, nl), 0) * CMP_STRIDE + (CMP_LEN - 1)
    cmask = c_end <= t_lane
    j_idx = lax.broadcasted_iota(jnp.int32, (N_SEL, tq), 0)
    t_q = start + lax.broadcasted_iota(jnp.int32, (N_SEL, tq), 1)
    cur = t_q // SLC_LEN
    forced = (j_idx == 0) | (j_idx == cur) | (j_idx == cur - 1)
    ovt = ovt_ref[...]

    o_cmp = []
    qs_masked = []
    for kv in kvs:
        kc_ref, vc_ref = cmp_refs[kv]
        s = lax.dot_general(kc_ref[0], qs[kv], _NT, preferred_element_type=F32)
        s = jnp.where(cmask, s, NEG)
        m = jnp.max(s, axis=0, keepdims=True)
        e = jnp.where(cmask, jnp.exp2(s - m), 0.0)
        l = jnp.sum(e, axis=0, keepdims=True)
        p_cmp = e * jnp.where(l > 0.0, 1.0 / l, 0.0)
        o_cmp.append(lax.dot_general(vc_ref[0], p_cmp.astype(BF16), _TN, preferred_element_type=F32))

        p_sum = p_cmp[:, 0:tq]
        for g in range(1, NSA_GROUP):
            p_sum = p_sum + p_cmp[:, g * tq:(g + 1) * tq]
        p_hi = p_sum.astype(BF16)
        p_lo = (p_sum - p_hi.astype(F32)).astype(BF16)
        p_slc = (jnp.dot(ovt, p_hi, preferred_element_type=F32)
                 + jnp.dot(ovt, p_lo, preferred_element_type=F32))
        score = jnp.where(forced, 1e4, jnp.where(j_idx > cur, -1e4, p_slc))
        rank = jnp.zeros((N_SEL, tq), jnp.int32)
        for i in range(N_SEL):
            row = score[i:i + 1, :]
            tie = jnp.where(j_idx > i, 1, 0)
            rank = rank + jnp.where(row > score, 1, jnp.where(row == score, tie, 0))
        bias = jnp.where((rank < SLC_TOP) & (j_idx <= cur), 0.0, NEG)
        bias_t = jnp.concatenate([bias, jnp.zeros((HEAD_DIM - N_SEL, tq), F32)], axis=0).T.astype(BF16)
        qs_masked.append(jnp.concatenate([qs[kv], jnp.concatenate([bias_t] * NSA_GROUP, axis=0)], axis=1))

    key_block = lax.broadcasted_iota(jnp.int32, (tq, HEAD_DIM), 0) // SLC_LEN
    block_lane = lax.broadcasted_iota(jnp.int32, (tq, HEAD_DIM), 1)

    def slc_chain(kv):
        def scores(kt):
            k0 = pl.multiple_of(kt * tq, tq)
            block_onehot = jnp.where(kt * (tq // SLC_LEN) + key_block == block_lane, 1.0, 0.0).astype(BF16)
            keys = jnp.concatenate([ks_ref[pl.ds(k0, tq), head_cols(kv)], block_onehot], axis=1)
            return lax.dot_general(keys, qs_masked[kv], _NT, preferred_element_type=F32)

        def values(kt):
            return vs_ref[pl.ds(pl.multiple_of(kt * tq, tq), tq), head_cols(kv)]

        return scores, values

    o_slc = _causal_flash([slc_chain(kv) for kv in kvs], qi, causal, HEAD_DIM, nl)

    win_p, win_l, win_v = [], [], []
    for kv in kvs:
        scores, values = [], []
        for back in range(WINDOW // tq, -1, -1):
            k0 = start - back * tq
            inside = k0 >= 0
            k0 = pl.multiple_of(jnp.maximum(k0, 0), tq)
            s = lax.dot_general(kw_ref[pl.ds(k0, tq), head_cols(kv)], qs[kv], _NT, preferred_element_type=F32)
            if back == 0:
                s = jnp.where(causal, s, NEG)
            elif back == WINDOW // tq:
                edge = q_local + jnp.where(inside, 0, tq)
                s = jnp.where(k_local > edge, s, NEG)
            else:
                s = s + jnp.where(inside, 0.0, NEG)
            scores.append(s)
            values.append(vw_ref[pl.ds(k0, tq), head_cols(kv)])
        s = jnp.concatenate(scores, axis=0)
        p = jnp.exp2(s - jnp.max(s, axis=0, keepdims=True))
        win_l.append(jnp.sum(p, axis=0, keepdims=True))
        win_p.append(p.astype(BF16))
        win_v.append(jnp.concatenate(values, axis=0))

    gates_t = jax.nn.sigmoid(gate_ref[...].astype(F32)).T
    for kv in kvs:
        acc_w = lax.dot_general(win_v[kv], win_p[kv], _TN, preferred_element_type=F32)
        o_win = acc_w * (1.0 / win_l[kv])

        gts = gates_t[kv * GATE_STRIDE:(kv + 1) * GATE_STRIDE]
        for g in range(NSA_GROUP):
            sl = slice(g * tq, (g + 1) * tq)
            o_t = (gts[g:g + 1, :] * o_cmp[kv][:, sl]
                   + gts[NSA_GROUP + g:NSA_GROUP + g + 1, :] * o_slc[kv][:, sl]
                   + gts[2 * NSA_GROUP + g:2 * NSA_GROUP + g + 1, :] * o_win[:, sl])
            head = kv * NSA_GROUP + g
            o_ref[:, head * HEAD_DIM:(head + 1) * HEAD_DIM] = o_t.T


def _nsa_attention(proj, cmp_kv, ovt, batch, *, tq=256):
    t = proj.shape[0]
    assert WINDOW % tq == 0 and tq % SLC_LEN == 0 and SEQ % tq == 0
    nq = SEQ // tq
    est = (4 * 2 * SEQ * NSA_KV_WIDTH * 2 + 2 * tq * NSA_WIDTH * (2 + 4)
           + N_NSA_KV * 10 * (WINDOW + tq) * NSA_GROUP * tq * 4)

    def kv_spec(col):
        return pl.BlockSpec((SEQ, NSA_KV_WIDTH), lambda b, i: (b, col // NSA_KV_WIDTH))

    def cmp_spec(which, kv):
        return pl.BlockSpec((1, N_CMP_PAD, HEAD_DIM), lambda b, i: (which, kv * batch + b, 0))

    return pl.pallas_call(
        functools.partial(_nsa_kernel, tq=tq),
        grid=(batch, nq),
        in_specs=[
            pl.BlockSpec((tq, NSA_WIDTH), lambda b, i: (b * nq + i, 0)),
            cmp_spec(0, 0), cmp_spec(1, 0), cmp_spec(0, 1), cmp_spec(1, 1),
            kv_spec(COL_KS), kv_spec(COL_VS), kv_spec(COL_KW), kv_spec(COL_VW),
            pl.BlockSpec((tq, HEAD_DIM), lambda b, i: (b * nq + i, COL_GATE // HEAD_DIM)),
            pl.BlockSpec((N_SEL, N_CMP_PAD), lambda b, i: (0, 0)),
        ],
        out_specs=pl.BlockSpec((tq, NSA_WIDTH), lambda b, i: (b * nq + i, 0)),
        out_shape=jax.ShapeDtypeStruct((t, NSA_WIDTH), F32),
        compiler_params=_params(("parallel", "arbitrary"), est),
        name="nsa_attn",
    )(proj, cmp_kv, cmp_kv, cmp_kv, cmp_kv, proj, proj, proj, proj, proj, ovt)


def _diff_kernel(q_ref, k_ref, v_ref, lq1_ref, lk1_ref, lq2_ref, lk2_ref, sub_ref, o_ref,
                 *, tq, lambda_init):
    qi = pl.program_id(2)
    wide = 2 * HEAD_DIM
    lam = (jnp.exp(jnp.sum(lq1_ref[...] * lk1_ref[...], axis=-1, keepdims=True))
           - jnp.exp(jnp.sum(lq2_ref[...] * lk2_ref[...], axis=-1, keepdims=True)) + lambda_init)
    q = q_ref[...]

    def chain(h):
        q1 = q[:, h * wide:h * wide + HEAD_DIM]
        q2 = q[:, h * wide + HEAD_DIM:(h + 1) * wide]

        def scores(kt):
            kk = k_ref[pl.ds(pl.multiple_of(kt * tq, tq), tq), h * wide:(h + 1) * wide]
            s1 = lax.dot_general(kk[:, :HEAD_DIM], q1, _NT, preferred_element_type=F32)
            s2 = lax.dot_general(kk[:, HEAD_DIM:], q2, _NT, preferred_element_type=F32)
            return jnp.concatenate([s1, s2], axis=1)

        def values(kt):
            return v_ref[pl.ds(pl.multiple_of(kt * tq, tq), tq), h * wide:(h + 1) * wide]

        return scores, values

    k_local = lax.broadcasted_iota(jnp.int32, (tq, 2 * tq), 0)
    q_local = lax.broadcasted_iota(jnp.int32, (1, 2 * tq), 1) & (tq - 1)
    outs = _causal_flash([chain(h) for h in range(DIFF_HEADS_PER_STEP)], qi, k_local <= q_local, wide, 2 * tq)
    for h, o_n in enumerate(outs):
        o_t = o_n[:, :tq] - lam * o_n[:, tq:]
        o = _rmsnorm(o_t.T, sub_ref[...]) * (1.0 - lambda_init)
        o_ref[:, h * wide:(h + 1) * wide] = o.astype(o_ref.dtype)


DIFF_HEADS_PER_STEP = 4


def _diff_attention(proj, lq1, lk1, lq2, lk2, subln, batch, lambda_init, *, tq=512):
    t = proj.shape[0]
    nq = SEQ // tq
    wide = 2 * HEAD_DIM
    step_cols = DIFF_HEADS_PER_STEP * wide
    assert COL_DQ % step_cols == 0 and COL_DK % step_cols == 0 and COL_DV % step_cols == 0
    est = 2 * 2 * SEQ * step_cols * 2 + 4 * tq * step_cols * 4 + DIFF_HEADS_PER_STEP * 16 * tq * 2 * tq * 4
    vec = pl.BlockSpec((1, HEAD_DIM), lambda b, h, i: (0, 0))
    return pl.pallas_call(
        functools.partial(_diff_kernel, tq=tq, lambda_init=lambda_init),
        grid=(batch, N_DIFF_HEADS // DIFF_HEADS_PER_STEP, nq),
        in_specs=[
            pl.BlockSpec((tq, step_cols), lambda b, h, i: (b * nq + i, COL_DQ // step_cols + h)),
            pl.BlockSpec((SEQ, step_cols), lambda b, h, i: (b, COL_DK // step_cols + h)),
            pl.BlockSpec((SEQ, step_cols), lambda b, h, i: (b, COL_DV // step_cols + h)),
            vec, vec, vec, vec,
            pl.BlockSpec((1, wide), lambda b, h, i: (0, 0)),
        ],
        out_specs=pl.BlockSpec((tq, step_cols), lambda b, h, i: (b * nq + i, h)),
        out_shape=jax.ShapeDtypeStruct((t, DIFF_WIDTH), BF16),
        compiler_params=_params(("parallel", "parallel", "arbitrary"), est),
        name="diff_attn",
    )(proj, proj, proj, lq1, lk1, lq2, lk2, subln)


def _oproj_kernel(yn_ref, g_ref, yd_ref, wo_ref, x_ref, o_ref):
    yn = _rmsnorm(yn_ref[...], g_ref[...]).astype(BF16)
    acc = jnp.dot(yn, wo_ref[:NSA_WIDTH, :], preferred_element_type=F32)
    acc = acc + jnp.dot(yd_ref[...], wo_ref[NSA_WIDTH:, :], preferred_element_type=F32)
    o_ref[...] = x_ref[...] + acc


def _out_proj(y_nsa, g, y_diff, w_o, x2, *, tm=512):
    t = x2.shape[0]
    est = (2 * (NSA_WIDTH + DIFF_WIDTH) * D_MODEL * 2 + 2 * tm * NSA_WIDTH * 4 + 2 * tm * DIFF_WIDTH * 2
           + 5 * tm * D_MODEL * 4)
    return pl.pallas_call(
        _oproj_kernel,
        grid=(t // tm,),
        in_specs=[
            pl.BlockSpec((tm, NSA_WIDTH), lambda i: (i, 0)),
            pl.BlockSpec((1, NSA_WIDTH), lambda i: (0, 0)),
            pl.BlockSpec((tm, DIFF_WIDTH), lambda i: (i, 0)),
            pl.BlockSpec((NSA_WIDTH + DIFF_WIDTH, D_MODEL), lambda i: (0, 0)),
            pl.BlockSpec((tm, D_MODEL), lambda i: (i, 0)),
        ],
        out_specs=pl.BlockSpec((tm, D_MODEL), lambda i: (i, 0)),
        out_shape=jax.ShapeDtypeStruct((t, D_MODEL), F32),
        compiler_params=_params(("parallel",), est),
        name="out_proj",
    )(y_nsa, g, y_diff, w_o, x2)


HALO = 8
FFN_CHUNK = 256
FFN_ROW_BLOCKS = 4


def _ffn_kernel(h_ref, halo_ref, g_ref, wu_ref, wg_ref, cwu_ref, cwg_ref, cbu_ref, cbg_ref, wd_ref,
                o_ref, xn_ref, raw_ref, act_ref, *, tm):
    i = pl.program_id(0)
    j = pl.program_id(1)

    @pl.when(j == 0)
    def _():
        keep = jnp.where((i * tm) % SEQ == 0, 0.0, 1.0)
        h = h_ref[...]
        xn_ref[0:HALO, :] = (_rmsnorm(halo_ref[...], g_ref[...]) * keep).astype(BF16)
        xn_ref[HALO:, :] = _rmsnorm(h, g_ref[...]).astype(BF16)
        o_ref[...] = h

    xn = xn_ref[...]

    def conv(h, cw_ref, cb_ref, cols):
        cw = cw_ref[:, cols]
        h3 = h.reshape(h.shape[0] // SUBLANES, SUBLANES, h.shape[1])
        sub = lax.broadcasted_iota(jnp.int32, h3.shape[1:], 0)
        out = cw[CONV_W - 1:CONV_W, :] * h3[1:] + cb_ref[:, cols]
        for back in range(1, CONV_W):
            rolled = pltpu.roll(h3, back, 1)
            shifted = jnp.where(sub < back, rolled[:-1], rolled[1:])
            out = out + cw[CONV_W - 1 - back:CONV_W - back, :] * shifted
        return out.reshape(h.shape[0] - HALO, h.shape[1])

    def gated(rows, cols):
        src = slice(rows.start, rows.stop + HALO)
        u = conv(raw_ref[slot, 0, src, cols], cwu_ref, cbu_ref, cols)
        gate = conv(raw_ref[slot, 1, src, cols], cwg_ref, cbg_ref, cols)
        act_ref[rows, cols] = (jax.nn.silu(gate) * u).astype(BF16)

    slot = j % 2
    chunks = [slice(lo, lo + FFN_CHUNK) for lo in range(0, wu_ref.shape[1], FFN_CHUNK)]
    row_blocks = [slice(lo, lo + tm // FFN_ROW_BLOCKS) for lo in range(0, tm, tm // FFN_ROW_BLOCKS)]
    for cols in chunks:
        raw_ref[slot, 0, :, cols] = jnp.dot(xn, wu_ref[:, cols], preferred_element_type=F32)
        raw_ref[slot, 1, :, cols] = jnp.dot(xn, wg_ref[:, cols], preferred_element_type=F32)
        if cols is not chunks[-1]:
            gated(slice(0, tm), cols)
    gated(row_blocks[0], chunks[-1])
    for n, rows in enumerate(row_blocks):
        if n + 1 < len(row_blocks):
            gated(row_blocks[n + 1], chunks[-1])
        o_ref[rows, :] += jnp.dot(act_ref[rows, :], wd_ref[...], preferred_element_type=F32)


def _conv_ffn(h1, g, w_up, conv_w, conv_b, w_down, *, tm=1024, tf=512):
    t = h1.shape[0]
    assert t % tm == 0 and SEQ % tm == 0 and D_FF % tf == 0
    nf = D_FF // tf
    est = (4 * tm * D_MODEL * 4 + 2 * 3 * D_MODEL * tf * 2 + (tm + HALO) * D_MODEL * 2
           + 8 * (tm + HALO) * tf * 4)
    return pl.pallas_call(
        functools.partial(_ffn_kernel, tm=tm),
        grid=(t // tm, nf),
        in_specs=[
            pl.BlockSpec((tm, D_MODEL), lambda i, j: (i, 0)),
            pl.BlockSpec((HALO, D_MODEL), lambda i, j: (jnp.maximum(i * (tm // HALO) - 1, 0), 0)),
            pl.BlockSpec((1, D_MODEL), lambda i, j: (0, 0)),
            pl.BlockSpec((D_MODEL, tf), lambda i, j: (0, j)),
            pl.BlockSpec((D_MODEL, tf), lambda i, j: (0, nf + j)),
            pl.BlockSpec((CONV_W, tf), lambda i, j: (0, j)),
            pl.BlockSpec((CONV_W, tf), lambda i, j: (0, nf + j)),
            pl.BlockSpec((1, tf), lambda i, j: (0, j)),
            pl.BlockSpec((1, tf), lambda i, j: (0, nf + j)),
            pl.BlockSpec((tf, D_MODEL), lambda i, j: (j, 0)),
        ],
        out_specs=pl.BlockSpec((tm, D_MODEL), lambda i, j: (i, 0), pipeline_mode=pl.Buffered(1)),
        out_shape=jax.ShapeDtypeStruct((t, D_MODEL), F32),
        scratch_shapes=[pltpu.VMEM((tm + HALO, D_MODEL), BF16), pltpu.VMEM((2, 2, tm + HALO, tf), F32),
                        pltpu.VMEM((tm, tf), BF16)],
        compiler_params=_params(("parallel", "arbitrary"), est),
        name="conv_ffn",
    )(h1, h1, g, w_up, w_up, conv_w, conv_w, conv_b, conv_b, w_down)


PLE_CHUNK = 512


def _ple_kernel(h_ref, gp_ref, wg_ref, p_ref, wp_ref, gf_ref, o_ref):
    hn = _rmsnorm(h_ref[...], gp_ref[...]).astype(BF16)
    pb = p_ref[...].astype(BF16)
    ssq = None
    for lo in range(0, D_MODEL, PLE_CHUNK):
        cols = slice(lo, lo + PLE_CHUNK)
        gate = jax.nn.sigmoid(jnp.dot(hn, wg_ref[:, cols], preferred_element_type=F32))
        emb = jnp.dot(pb, wp_ref[:, cols], preferred_element_type=F32)
        h3 = h_ref[:, cols] + gate * emb
        o_ref[:, cols] = h3
        part = jnp.sum(h3 * h3, axis=-1, keepdims=True)
        ssq = part if ssq is None else ssq + part
    o_ref[...] = o_ref[...] * lax.rsqrt(ssq * (1.0 / D_MODEL) + EPS) * gf_ref[...]


def _ple_out(h2, g_ple, w_gate, p2, w_proj, g_final, *, tm=256):
    t = h2.shape[0]
    est = D_MODEL * D_MODEL * 2 + 2 * PLE_DIM * D_MODEL * 2 + 4 * tm * D_MODEL * 4 + 6 * tm * D_MODEL * 4
    return pl.pallas_call(
        _ple_kernel,
        grid=(t // tm,),
        in_specs=[
            pl.BlockSpec((tm, D_MODEL), lambda i: (i, 0)),
            pl.BlockSpec((1, D_MODEL), lambda i: (0, 0)),
            pl.BlockSpec((D_MODEL, D_MODEL), lambda i: (0, 0), pipeline_mode=pl.Buffered(1)),
            pl.BlockSpec((tm, PLE_DIM), lambda i: (i, 0)),
            pl.BlockSpec((PLE_DIM, D_MODEL), lambda i: (0, 0)),
            pl.BlockSpec((1, D_MODEL), lambda i: (0, 0)),
        ],
        out_specs=pl.BlockSpec((tm, D_MODEL), lambda i: (i, 0)),
        out_shape=jax.ShapeDtypeStruct((t, D_MODEL), F32),
        compiler_params=_params(("parallel",), est),
        name="ple_out",
    )(h2, g_ple, w_gate, p2, w_proj, g_final)


def _split_w_in(w):
    w_a, gate, w_b = _cast_w_in(w)
    gate = gate[:, :N_GATES].reshape(D_MODEL, 3, N_NSA_KV, NSA_GROUP).transpose(0, 2, 1, 3)
    gate = jnp.pad(gate.reshape(D_MODEL, N_NSA_KV, 3 * NSA_GROUP), ((0, 0), (0, 0), (0, GATE_STRIDE - 3 * NSA_GROUP)))
    gate = jnp.pad(gate.reshape(D_MODEL, N_NSA_KV * GATE_STRIDE), ((0, 0), (0, HEAD_DIM - N_NSA_KV * GATE_STRIDE)))
    return w_a, gate, w_b


CAST_BLOCK = 512


def _cast_w_in_kernel(cur_ref, nxt_ref, wa_ref, wg_ref, wb_ref, *, n_a):
    s = pl.program_id(0)

    @pl.when(s < n_a)
    def _():
        wa_ref[...] = cur_ref[...].T.astype(BF16)

    @pl.when((s >= n_a) & (s < pl.num_programs(0) - 1))
    def _():
        rows = jnp.concatenate([cur_ref[N_GATES:, :], nxt_ref[0:N_GATES, :]], axis=0)
        wb_ref[...] = rows.T.astype(BF16)

    @pl.when(s == pl.num_programs(0) - 1)
    def _():
        wg_ref[...] = cur_ref[0:HEAD_DIM, :].T.astype(BF16)


def _cast_w_in(w):
    d, n = w.shape
    nsa_cols = NSA_WIDTH + 6 * NSA_KV_WIDTH
    diff_cols = n - nsa_cols - N_GATES
    assert nsa_cols % CAST_BLOCK == 0 and diff_cols % CAST_BLOCK == 0 and N_GATES % SUBLANES == 0
    n_a, n_b = nsa_cols // CAST_BLOCK, diff_cols // CAST_BLOCK
    spill_rows = 32
    assert N_GATES <= spill_rows and CAST_BLOCK % spill_rows == 0
    est = 2 * CAST_BLOCK * d * 4 + 2 * 2 * CAST_BLOCK * d * 2 + 6 * CAST_BLOCK * d * 4

    def in_block(s):
        return jnp.where(s == n_a + n_b, n_a, s)

    return pl.pallas_call(
        functools.partial(_cast_w_in_kernel, n_a=n_a),
        grid=(n_a + n_b + 1,),
        in_specs=[pl.BlockSpec((CAST_BLOCK, d), lambda s: (in_block(s), 0)),
                  pl.BlockSpec((spill_rows, d),
                               lambda s: (jnp.clip(s + 1, n_a + 1, n_a + n_b) * (CAST_BLOCK // spill_rows), 0))],
        out_specs=[pl.BlockSpec((d, CAST_BLOCK), lambda s: (0, jnp.minimum(s, n_a - 1))),
                   pl.BlockSpec((d, HEAD_DIM), lambda s: (0, 0)),
                   pl.BlockSpec((d, CAST_BLOCK), lambda s: (0, jnp.clip(s - n_a, 0, n_b - 1)))],
        out_shape=[jax.ShapeDtypeStruct((d, nsa_cols), BF16), jax.ShapeDtypeStruct((d, HEAD_DIM), BF16),
                   jax.ShapeDtypeStruct((d, diff_cols), BF16)],
        compiler_params=_params(("arbitrary",), est),
        name="cast_w_in",
    )(w.T, w.T)


def _rope_tables():
    inv = 1.0 / (ROPE_THETA ** (jnp.arange(0, ROPE_DIM, 2, dtype=F32) / ROPE_DIM))
    ang = jnp.arange(SEQ, dtype=F32)[:, None] * inv[None, :]
    cos, sin = jnp.cos(ang), jnp.sin(ang)
    rest = HEAD_DIM - ROPE_DIM
    cos_t = jnp.concatenate([cos, cos, jnp.ones((SEQ, rest), F32)], axis=1)
    sin_t = jnp.concatenate([-sin, sin, jnp.zeros((SEQ, rest), F32)], axis=1)
    return cos_t, sin_t


def _overlap_t():
    cmp_starts = np.arange(N_CMP) * CMP_STRIDE
    sel_starts = np.arange(N_SEL) * SLC_LEN
    ov = np.clip(np.minimum(cmp_starts[:, None] + CMP_LEN, sel_starts[None, :] + SLC_LEN)
                 - np.maximum(cmp_starts[:, None], sel_starts[None, :]), 0, None).astype(np.float32) / CMP_LEN
    ovt = np.zeros((N_SEL, N_CMP_PAD), np.float32)
    ovt[:, :N_CMP] = ov.T
    return jnp.asarray(ovt, BF16)


def kernel(x, p, attn_norm, w_in, cmp_k_pos, cmp_k_w1, cmp_k_w2, cmp_v_pos, cmp_v_w1, cmp_v_w2, nsa_out_norm, diff_lq1, diff_lk1, diff_lq2, diff_lk2, diff_subln, w_o, ffn_norm, w_up, conv_w, conv_b, w_down, ple_norm, w_ple_gate, w_ple_proj, final_norm):
    batch, seq, _ = x.shape
    assert seq == SEQ and p.shape[0] == 1
    t = batch * seq
    layer = 0
    lambda_init = 0.8 - 0.6 * math.exp(-0.3 * layer)
    x2 = x.reshape(t, D_MODEL)
    cos_t, sin_t = _rope_tables()

    proj, hkv, w_o_b, w_up_b, w_down_b, w_gate_b = _in_proj(
        x2, attn_norm[layer][None], *_split_w_in(w_in[layer]), cos_t, sin_t,
        later_weights=(w_o[layer], w_up[layer], w_down[layer], w_ple_gate[layer]))

    hkv = hkv.reshape(2, N_NSA_KV * t // HALF_BLOCK, HALF_BLOCK * HEAD_DIM)
    w1 = jnp.stack([cmp_k_w1[layer], cmp_v_w1[layer]]).astype(BF16)
    w2 = jnp.stack([cmp_k_w2[layer], cmp_v_w2[layer]]).astype(BF16)
    pos = jnp.stack([cmp_k_pos[layer], cmp_v_pos[layer]]).reshape(2, 1, CMP_LEN * HEAD_DIM)
    pos = jnp.broadcast_to(pos, (2, 8, CMP_LEN * HEAD_DIM)).astype(BF16)
    cmp_kv = _compress(hkv, w1, pos, w2)

    y_nsa = _nsa_attention(proj, cmp_kv, _overlap_t(), batch)
    y_diff = _diff_attention(proj, diff_lq1[layer][None], diff_lk1[layer][None], diff_lq2[layer][None],
                             diff_lk2[layer][None], diff_subln[layer][None], batch, lambda_init)
    h1 = _out_proj(y_nsa, nsa_out_norm[layer][None], y_diff, w_o_b, x2)
    h2 = _conv_ffn(h1, ffn_norm[layer][None], w_up_b, conv_w[layer], conv_b[layer][None], w_down_b)
    out = _ple_out(h2, ple_norm[layer][None], w_gate_b, p[layer].reshape(t, PLE_DIM),
                   w_ple_proj[layer].astype(BF16), final_norm[None])
    return out.reshape(batch, seq, D_MODEL)
```

```python
import functools
import math

import numpy as np
import jax
import jax.numpy as jnp
from jax import lax
from jax.experimental import pallas as pl
from jax.experimental.pallas import tpu as pltpu

D_MODEL = 2048
SEQ = 2048
HEAD_DIM = 128
ROPE_DIM = HEAD_DIM // 4
ROPE_THETA = 500000.0
N_NSA_HEADS = 8
N_NSA_KV = 2
NSA_GROUP = N_NSA_HEADS // N_NSA_KV
CMP_LEN = 32
CMP_STRIDE = 16
CMP_HIDDEN = 256
SLC_LEN = 64
SLC_TOP = 16
WINDOW = 512
N_DIFF_HEADS = 4
D_FF = 5632
CONV_W = 3
PLE_DIM = 256
EPS = 1e-6

NSA_WIDTH = N_NSA_HEADS * HEAD_DIM
NSA_KV_WIDTH = N_NSA_KV * HEAD_DIM
DIFF_WIDTH = N_DIFF_HEADS * 2 * HEAD_DIM
N_GATES = 3 * N_NSA_HEADS
N_CMP = (SEQ - CMP_LEN) // CMP_STRIDE + 1
N_CMP_PAD = SEQ // CMP_STRIDE
N_SEL = SEQ // SLC_LEN
GATE_STRIDE = 16

COL_NQ = 0
COL_DQ = COL_NQ + NSA_WIDTH
COL_DK = COL_DQ + DIFF_WIDTH
COL_DV = COL_DK + DIFF_WIDTH
COL_KS = COL_DV + DIFF_WIDTH
COL_KW = COL_KS + NSA_KV_WIDTH
COL_VS = COL_KW + NSA_KV_WIDTH
COL_VW = COL_VS + NSA_KV_WIDTH
COL_GATE = COL_VW + NSA_KV_WIDTH
PROJ_COLS = COL_GATE + HEAD_DIM

V7X_LANES = 128
SUBLANES = 8
BF16_ROWS = 16
V7X_VMEM_REQUEST_CAP = 56 * 1024 * 1024
NEG = -1e30
QSCALE = HEAD_DIM ** -0.5 * math.log2(math.e)

_NT = (((1,), (1,)), ((), ()))
_TN = (((0,), (0,)), ((), ()))
BF16 = jnp.bfloat16
F32 = jnp.float32


def _params(semantics, vmem_estimate_bytes, flags=None):
    limit = min(max(int(vmem_estimate_bytes), 32 * 1024 * 1024), V7X_VMEM_REQUEST_CAP)
    return pltpu.CompilerParams(dimension_semantics=semantics, vmem_limit_bytes=limit, flags=flags)


def _rmsnorm(x, g):
    return x * lax.rsqrt(jnp.mean(x * x, axis=-1, keepdims=True) + EPS) * g


INPROJ_CHUNK = 1024
HALF_BLOCK = CMP_STRIDE

_HEADS_A = ([(COL_NQ + h * HEAD_DIM, True, True) for h in range(N_NSA_HEADS)]
            + [(("cmp", 0, h), True, False) for h in range(N_NSA_KV)]
            + [(("cmp", 1, h), False, False) for h in range(N_NSA_KV)]
            + [(COL_KS + h * HEAD_DIM, True, False) for h in range(N_NSA_KV)]
            + [(COL_VS + h * HEAD_DIM, False, False) for h in range(N_NSA_KV)]
            + [(COL_KW + h * HEAD_DIM, True, False) for h in range(N_NSA_KV)]
            + [(COL_VW + h * HEAD_DIM, False, False) for h in range(N_NSA_KV)])
_HEADS_B = ([(COL_DQ + h * HEAD_DIM, True, True) for h in range(2 * N_DIFF_HEADS)]
            + [(COL_DK + h * HEAD_DIM, True, False) for h in range(2 * N_DIFF_HEADS)]
            + [(COL_DV + h * HEAD_DIM, False, False) for h in range(2 * N_DIFF_HEADS)])
_HEADS_GATE = [(COL_GATE, False, False)]


def _inproj_kernel(*refs, n_cast):
    x_ref, g_ref, wa_ref, wg_ref, wb_ref, cos_ref, sin_ref = refs[:7]
    cast_in = refs[7:7 + n_cast]
    o_ref, hkv_ref = refs[7 + n_cast:9 + n_cast]
    cast_out = refs[9 + n_cast:9 + 2 * n_cast]
    stage_ref = refs[9 + 2 * n_cast]
    for src_ref, dst_ref in zip(cast_in, cast_out):
        dst_ref[...] = src_ref[...].astype(dst_ref.dtype)
    tm = x_ref.shape[0]
    xn = _rmsnorm(x_ref[...], g_ref[...]).astype(BF16)
    c = cos_ref[...]
    s = sin_ref[...]
    cq = c * QSCALE
    sq = s * QSCALE
    first_half = lax.broadcasted_iota(jnp.int32, c.shape, 1) < ROPE_DIM // 2
    n_staged = 0
    for w_ref, heads in ((wa_ref, _HEADS_A), (wg_ref, _HEADS_GATE), (wb_ref, _HEADS_B)):
        for lo in range(0, w_ref.shape[1], INPROJ_CHUNK):
            hi = min(lo + INPROJ_CHUNK, w_ref.shape[1])
            acc = jnp.dot(xn, w_ref[:, lo:hi], preferred_element_type=F32)
            for src in range(lo, hi, HEAD_DIM):
                dest, rope, is_query = heads[src // HEAD_DIM]
                a = acc[:, src - lo:src - lo + HEAD_DIM]
                if rope:
                    partner = jnp.where(first_half,
                                        pltpu.roll(a, HEAD_DIM - ROPE_DIM // 2, 1),
                                        pltpu.roll(a, ROPE_DIM // 2, 1))
                    a = a * cq + partner * sq if is_query else a * c + partner * s
                if isinstance(dest, tuple):
                    _, which, head = dest
                    stage = stage_ref.at[n_staged]
                    n_staged += 1
                    stage[...] = a
                    for r in range(HALF_BLOCK):
                        rows = stage[pl.ds(r, tm // HALF_BLOCK, stride=HALF_BLOCK), :]
                        hkv_ref[which, head, :, r * HEAD_DIM:(r + 1) * HEAD_DIM] = rows.astype(hkv_ref.dtype)
                else:
                    o_ref[:, dest:dest + HEAD_DIM] = a.astype(o_ref.dtype)


def _in_proj(x2, g, w, w_g, w_b, cos_t, sin_t, later_weights=(), *, tm=256):
    t = x2.shape[0]
    assert t % tm == 0 and SEQ % tm == 0 and tm % (HALF_BLOCK * BF16_ROWS) == 0
    n_steps = t // tm
    seq_tiles = SEQ // tm
    nsa_cols = NSA_WIDTH + 6 * NSA_KV_WIDTH
    n_w = nsa_cols + HEAD_DIM + w_b.shape[1]
    est = (D_MODEL * n_w * 2 + 2 * tm * D_MODEL * 4 + 2 * tm * n_w * 2 + tm * D_MODEL * 2
           + 3 * tm * INPROJ_CHUNK * 4)
    cast_specs = []
    for lw in later_weights:
        assert lw.shape[0] % (n_steps * BF16_ROWS) == 0
        cast_specs.append(pl.BlockSpec((lw.shape[0] // n_steps, lw.shape[1]), lambda i: (i, 0)))
        est += 2 * (lw.size // n_steps) * (4 + 2)
    resident = dict(pipeline_mode=pl.Buffered(1))
    half_cols = HALF_BLOCK * HEAD_DIM
    return pl.pallas_call(
        functools.partial(_inproj_kernel, n_cast=len(later_weights)),
        grid=(n_steps,),
        in_specs=[
            pl.BlockSpec((tm, D_MODEL), lambda i: (i, 0)),
            pl.BlockSpec((1, D_MODEL), lambda i: (0, 0)),
            pl.BlockSpec((D_MODEL, nsa_cols), lambda i: (0, 0), **resident),
            pl.BlockSpec(w_g.shape, lambda i: (0, 0), **resident),
            pl.BlockSpec(w_b.shape, lambda i: (0, 0), **resident),
            pl.BlockSpec((tm, HEAD_DIM), lambda i: (i % seq_tiles, 0)),
            pl.BlockSpec((tm, HEAD_DIM), lambda i: (i % seq_tiles, 0)),
        ] + cast_specs,
        out_specs=[
            pl.BlockSpec((tm, PROJ_COLS), lambda i: (i, 0)),
            pl.BlockSpec((2, N_NSA_KV, tm // HALF_BLOCK, half_cols), lambda i: (0, 0, i, 0)),
        ] + cast_specs,
        out_shape=[
            jax.ShapeDtypeStruct((t, PROJ_COLS), BF16),
            jax.ShapeDtypeStruct((2, N_NSA_KV, t // HALF_BLOCK, half_cols), BF16),
        ] + [jax.ShapeDtypeStruct(lw.shape, BF16) for lw in later_weights],
        scratch_shapes=[pltpu.VMEM((2 * N_NSA_KV, tm, HEAD_DIM), F32)],
        compiler_params=_params(("parallel",), est),
        name="in_proj",
    )(x2, g, w, w_g, w_b, cos_t, sin_t, *later_weights)


def _compress_kernel(h_ref, w1_ref, pos_ref, w2_ref, o_ref):
    half = CMP_LEN * HEAD_DIM // 2
    h = h_ref[0]
    top = jnp.dot(h, w1_ref[0, :half, :], preferred_element_type=F32)
    bot = jnp.dot(h, w1_ref[0, half:, :], preferred_element_type=F32)
    pos_bias = jnp.dot(pos_ref[0], w1_ref[0], preferred_element_type=F32)[0:1]
    pre = top + pltpu.roll(bot, bot.shape[0] - 1, 0) + pos_bias
    act = jax.nn.gelu(pre)
    o_ref[0] = jnp.dot(act.astype(BF16), w2_ref[0], preferred_element_type=F32).astype(o_ref.dtype)


def _compress(hkv, w1, pos, w2):
    rows = hkv.shape[1]
    kdim = CMP_LEN * HEAD_DIM
    est = 2 * (rows * kdim // 2 * 2 + kdim * CMP_HIDDEN * 2) + 6 * rows * CMP_HIDDEN * 4
    return pl.pallas_call(
        _compress_kernel,
        grid=(2,),
        in_specs=[
            pl.BlockSpec((1, rows, kdim // 2), lambda i: (i, 0, 0)),
            pl.BlockSpec((1, kdim, CMP_HIDDEN), lambda i: (i, 0, 0)),
            pl.BlockSpec((1, 8, kdim), lambda i: (i, 0, 0)),
            pl.BlockSpec((1, CMP_HIDDEN, HEAD_DIM), lambda i: (i, 0, 0)),
        ],
        out_specs=pl.BlockSpec((1, rows, HEAD_DIM), lambda i: (i, 0, 0)),
        out_shape=jax.ShapeDtypeStruct((2, rows, HEAD_DIM), BF16),
        compiler_params=_params(("parallel",), est),
        name="compress",
    )(hkv, w1, pos, w2)


def _causal_flash(chains, n_before, diag_mask, d, lanes):
    def tile_step(kt, carries, mask=None):
        stats = []
        probs = []
        for (score_fn, _), (m, l, _) in zip(chains, carries):
            s = score_fn(kt)
            if mask is not None:
                s = jnp.where(mask, s, NEG)
            m_new = jnp.maximum(m, jnp.max(s, axis=0, keepdims=True))
            alpha = jnp.exp2(m - m_new)
            p = jnp.exp2(s - m_new)
            stats.append((m_new, alpha * l + jnp.sum(p, axis=0, keepdims=True), alpha))
            probs.append(p.astype(BF16))
        out = []
        for (_, value_fn), (_, _, acc), (m_new, l_new, alpha), p in zip(chains, carries, stats, probs):
            pv = lax.dot_general(value_fn(kt), p, _TN, preferred_element_type=F32)
            out.append((m_new, l_new, alpha * acc + pv))
        return tuple(out)

    init = (jnp.full((1, lanes), NEG, F32), jnp.zeros((1, lanes), F32), jnp.zeros((d, lanes), F32))
    carries = lax.fori_loop(0, n_before, tile_step, (init,) * len(chains))
    return [acc * (1.0 / l) for _, l, acc in tile_step(n_before, carries, diag_mask)]


def _nsa_kernel(q_ref, kc0_ref, vc0_ref, kc1_ref, vc1_ref, ks_ref, vs_ref, kw_ref, vw_ref, gate_ref, ovt_ref,
                o_ref, *, tq):
    qi = pl.program_id(1)
    start = qi * tq
    nl = NSA_GROUP * tq
    kvs = range(N_NSA_KV)
    cmp_refs = ((kc0_ref, vc0_ref), (kc1_ref, vc1_ref))

    def head_cols(kv):
        return slice(kv * HEAD_DIM, (kv + 1) * HEAD_DIM)

    q_all = q_ref[...]
    qs = [jnp.concatenate([q_all[:, (kv * NSA_GROUP + g) * HEAD_DIM:(kv * NSA_GROUP + g + 1) * HEAD_DIM]
                           for g in range(NSA_GROUP)], axis=0) for kv in kvs]
    q_local = lax.broadcasted_iota(jnp.int32, (1, nl), 1) & (tq - 1)
    t_lane = start + q_local
    k_local = lax.broadcasted_iota(jnp.int32, (tq, nl), 0)
    causal = k_local <= q_local
    c_end = lax.broadcasted_iota(jnp.int32, (N_CMP_PAD, nl), 0) * CMP_STRIDE + (CMP_LEN - 1)
    cmask = c_end <= t_lane
    j_idx = lax.broadcasted_iota(jnp.int32, (N_SEL, tq), 0)
    t_q = start + lax.broadcasted_iota(jnp.int32, (N_SEL, tq), 1)
    cur = t_q // SLC_LEN
    forced = (j_idx == 0) | (j_idx == cur) | (j_idx == cur - 1)
    ovt = ovt_ref[...]

    o_cmp = []
    qs_masked = []
    for kv in kvs:
        kc_ref, vc_ref = cmp_refs[kv]
        s = lax.dot_general(kc_ref[0], qs[kv], _NT, preferred_element_type=F32)
        s = jnp.where(cmask, s, NEG)
        m = jnp.max(s, axis=0, keepdims=True)
        e = jnp.where(cmask, jnp.exp2(s - m), 0.0)
        l = jnp.sum(e, axis=0, keepdims=True)
        p_cmp = e * jnp.where(l > 0.0, 1.0 / l, 0.0)
        o_cmp.append(lax.dot_general(vc_ref[0], p_cmp.astype(BF16), _TN, preferred_element_type=F32))

        p_sum = p_cmp[:, 0:tq]
        for g in range(1, NSA_GROUP):
            p_sum = p_sum + p_cmp[:, g * tq:(g + 1) * tq]
        p_hi = p_sum.astype(BF16)
        p_lo = (p_sum - p_hi.astype(F32)).astype(BF16)
        p_slc = (jnp.dot(ovt, p_hi, preferred_element_type=F32)
                 + jnp.dot(ovt, p_lo, preferred_element_type=F32))
        score = jnp.where(forced, 1e4, jnp.where(j_idx > cur, -1e4, p_slc))
        rank = jnp.zeros((N_SEL, tq), jnp.int32)
        for i in range(N_SEL):
            row = score[i:i + 1, :]
            tie = jnp.where(j_idx > i, 1, 0)
            rank = rank + jnp.where(row > score, 1, jnp.where(row == score, tie, 0))
        bias = jnp.where((rank < SLC_TOP) & (j_idx <= cur), 0.0, NEG)
        bias_t = jnp.concatenate([bias, jnp.zeros((HEAD_DIM - N_SEL, tq), F32)], axis=0).T.astype(BF16)
        qs_masked.append(jnp.concatenate([qs[kv], jnp.concatenate([bias_t] * NSA_GROUP, axis=0)], axis=1))

    key_block = lax.broadcasted_iota(jnp.int32, (tq, HEAD_DIM), 0) // SLC_LEN
    block_lane = lax.broadcasted_iota(jnp.int32, (tq, HEAD_DIM), 1)

    def slc_chain(kv):
        def scores(kt):
            k0 = pl.multiple_of(kt * tq, tq)
            block_onehot = jnp.where(kt * (tq // SLC_LEN) + key_block == block_lane, 1.0, 0.0).astype(BF16)
            keys = jnp.concatenate([ks_ref[pl.ds(k0, tq), head_cols(kv)], block_onehot], axis=1)
            return lax.dot_general(keys, qs_masked[kv], _NT, preferred_element_type=F32)

        def values(kt):
            return vs_ref[pl.ds(pl.multiple_of(kt * tq, tq), tq), head_cols(kv)]

        return scores, values

    o_slc = _causal_flash([slc_chain(kv) for kv in kvs], qi, causal, HEAD_DIM, nl)

    win_p, win_l, win_v = [], [], []
    for kv in kvs:
        scores, values = [], []
        for back in range(WINDOW // tq, -1, -1):
            k0 = start - back * tq
            inside = k0 >= 0
            k0 = pl.multiple_of(jnp.maximum(k0, 0), tq)
            s = lax.dot_general(kw_ref[pl.ds(k0, tq), head_cols(kv)], qs[kv], _NT, preferred_element_type=F32)
            if back == 0:
                s = jnp.where(causal, s, NEG)
            elif back == WINDOW // tq:
                edge = q_local + jnp.where(inside, 0, tq)
                s = jnp.where(k_local > edge, s, NEG)
            else:
                s = s + jnp.where(inside, 0.0, NEG)
            scores.append(s)
            values.append(vw_ref[pl.ds(k0, tq), head_cols(kv)])
        s = jnp.concatenate(scores, axis=0)
        p = jnp.exp2(s - jnp.max(s, axis=0, keepdims=True))
        win_l.append(jnp.sum(p, axis=0, keepdims=True))
        win_p.append(p.astype(BF16))
        win_v.append(jnp.concatenate(values, axis=0))

    gates_t = jax.nn.sigmoid(gate_ref[...].astype(F32)).T
    for kv in kvs:
        acc_w = lax.dot_general(win_v[kv], win_p[kv], _TN, preferred_element_type=F32)
        o_win = acc_w * (1.0 / win_l[kv])

        gts = gates_t[kv * GATE_STRIDE:(kv + 1) * GATE_STRIDE]
        for g in range(NSA_GROUP):
            sl = slice(g * tq, (g + 1) * tq)
            o_t = (gts[g:g + 1, :] * o_cmp[kv][:, sl]
                   + gts[NSA_GROUP + g:NSA_GROUP + g + 1, :] * o_slc[kv][:, sl]
                   + gts[2 * NSA_GROUP + g:2 * NSA_GROUP + g + 1, :] * o_win[:, sl])
            head = kv * NSA_GROUP + g
            o_ref[:, head * HEAD_DIM:(head + 1) * HEAD_DIM] = o_t.T


def _nsa_attention(proj, cmp_kv, ovt, batch, *, tq=256):
    t = proj.shape[0]
    assert WINDOW % tq == 0 and tq % SLC_LEN == 0 and SEQ % tq == 0
    nq = SEQ // tq
    est = (4 * 2 * SEQ * NSA_KV_WIDTH * 2 + 2 * tq * NSA_WIDTH * (2 + 4)
           + N_NSA_KV * 10 * (WINDOW + tq) * NSA_GROUP * tq * 4)

    def kv_spec(col):
        return pl.BlockSpec((SEQ, NSA_KV_WIDTH), lambda b, i: (b, col // NSA_KV_WIDTH))

    def cmp_spec(which, kv):
        return pl.BlockSpec((1, N_CMP_PAD, HEAD_DIM), lambda b, i: (which, kv * batch + b, 0))

    return pl.pallas_call(
        functools.partial(_nsa_kernel, tq=tq),
        grid=(batch, nq),
        in_specs=[
            pl.BlockSpec((tq, NSA_WIDTH), lambda b, i: (b * nq + i, 0)),
            cmp_spec(0, 0), cmp_spec(1, 0), cmp_spec(0, 1), cmp_spec(1, 1),
            kv_spec(COL_KS), kv_spec(COL_VS), kv_spec(COL_KW), kv_spec(COL_VW),
            pl.BlockSpec((tq, HEAD_DIM), lambda b, i: (b * nq + i, COL_GATE // HEAD_DIM)),
            pl.BlockSpec((N_SEL, N_CMP_PAD), lambda b, i: (0, 0)),
        ],
        out_specs=pl.BlockSpec((tq, NSA_WIDTH), lambda b, i: (b * nq + i, 0)),
        out_shape=jax.ShapeDtypeStruct((t, NSA_WIDTH), F32),
        compiler_params=_params(("parallel", "arbitrary"), est),
        name="nsa_attn",
    )(proj, cmp_kv, cmp_kv, cmp_kv, cmp_kv, proj, proj, proj, proj, proj, ovt)


def _diff_kernel(q_ref, k_ref, v_ref, lq1_ref, lk1_ref, lq2_ref, lk2_ref, sub_ref, o_ref,
                 *, tq, lambda_init):
    qi = pl.program_id(2)
    wide = 2 * HEAD_DIM
    lam = (jnp.exp(jnp.sum(lq1_ref[...] * lk1_ref[...], axis=-1, keepdims=True))
           - jnp.exp(jnp.sum(lq2_ref[...] * lk2_ref[...], axis=-1, keepdims=True)) + lambda_init)
    q = q_ref[...]

    def chain(h):
        q1 = q[:, h * wide:h * wide + HEAD_DIM]
        q2 = q[:, h * wide + HEAD_DIM:(h + 1) * wide]

        def scores(kt):
            kk = k_ref[pl.ds(pl.multiple_of(kt * tq, tq), tq), h * wide:(h + 1) * wide]
            s1 = lax.dot_general(kk[:, :HEAD_DIM], q1, _NT, preferred_element_type=F32)
            s2 = lax.dot_general(kk[:, HEAD_DIM:], q2, _NT, preferred_element_type=F32)
            return jnp.concatenate([s1, s2], axis=1)

        def values(kt):
            return v_ref[pl.ds(pl.multiple_of(kt * tq, tq), tq), h * wide:(h + 1) * wide]

        return scores, values

    k_local = lax.broadcasted_iota(jnp.int32, (tq, 2 * tq), 0)
    q_local = lax.broadcasted_iota(jnp.int32, (1, 2 * tq), 1) & (tq - 1)
    outs = _causal_flash([chain(h) for h in range(DIFF_HEADS_PER_STEP)], qi, k_local <= q_local, wide, 2 * tq)
    for h, o_n in enumerate(outs):
        o_t = o_n[:, :tq] - lam * o_n[:, tq:]
        o = _rmsnorm(o_t.T, sub_ref[...]) * (1.0 - lambda_init)
        o_ref[:, h * wide:(h + 1) * wide] = o.astype(o_ref.dtype)


DIFF_HEADS_PER_STEP = 4


def _diff_attention(proj, lq1, lk1, lq2, lk2, subln, batch, lambda_init, *, tq=512):
    t = proj.shape[0]
    nq = SEQ // tq
    wide = 2 * HEAD_DIM
    step_cols = DIFF_HEADS_PER_STEP * wide
    assert COL_DQ % step_cols == 0 and COL_DK % step_cols == 0 and COL_DV % step_cols == 0
    est = 2 * 2 * SEQ * step_cols * 2 + 4 * tq * step_cols * 4 + DIFF_HEADS_PER_STEP * 16 * tq * 2 * tq * 4
    vec = pl.BlockSpec((1, HEAD_DIM), lambda b, h, i: (0, 0))
    return pl.pallas_call(
        functools.partial(_diff_kernel, tq=tq, lambda_init=lambda_init),
        grid=(batch, N_DIFF_HEADS // DIFF_HEADS_PER_STEP, nq),
        in_specs=[
            pl.BlockSpec((tq, step_cols), lambda b, h, i: (b * nq + i, COL_DQ // step_cols + h)),
            pl.BlockSpec((SEQ, step_cols), lambda b, h, i: (b, COL_DK // step_cols + h)),
            pl.BlockSpec((SEQ, step_cols), lambda b, h, i: (b, COL_DV // step_cols + h)),
            vec, vec, vec, vec,
            pl.BlockSpec((1, wide), lambda b, h, i: (0, 0)),
        ],
        out_specs=pl.BlockSpec((tq, step_cols), lambda b, h, i: (b * nq + i, h)),
        out_shape=jax.ShapeDtypeStruct((t, DIFF_WIDTH), BF16),
        compiler_params=_params(("parallel", "parallel", "arbitrary"), est),
        name="diff_attn",
    )(proj, proj, proj, lq1, lk1, lq2, lk2, subln)


def _oproj_kernel(yn_ref, g_ref, yd_ref, wo_ref, x_ref, o_ref):
    yn = _rmsnorm(yn_ref[...], g_ref[...]).astype(BF16)
    acc = jnp.dot(yn, wo_ref[:NSA_WIDTH, :], preferred_element_type=F32)
    acc = acc + jnp.dot(yd_ref[...], wo_ref[NSA_WIDTH:, :], preferred_element_type=F32)
    o_ref[...] = x_ref[...] + acc


def _out_proj(y_nsa, g, y_diff, w_o, x2, *, tm=512):
    t = x2.shape[0]
    est = (2 * (NSA_WIDTH + DIFF_WIDTH) * D_MODEL * 2 + 2 * tm * NSA_WIDTH * 4 + 2 * tm * DIFF_WIDTH * 2
           + 5 * tm * D_MODEL * 4)
    return pl.pallas_call(
        _oproj_kernel,
        grid=(t // tm,),
        in_specs=[
            pl.BlockSpec((tm, NSA_WIDTH), lambda i: (i, 0)),
            pl.BlockSpec((1, NSA_WIDTH), lambda i: (0, 0)),
            pl.BlockSpec((tm, DIFF_WIDTH), lambda i: (i, 0)),
            pl.BlockSpec((NSA_WIDTH + DIFF_WIDTH, D_MODEL), lambda i: (0, 0)),
            pl.BlockSpec((tm, D_MODEL), lambda i: (i, 0)),
        ],
        out_specs=pl.BlockSpec((tm, D_MODEL), lambda i: (i, 0)),
        out_shape=jax.ShapeDtypeStruct((t, D_MODEL), F32),
        compiler_params=_params(("parallel",), est),
        name="out_proj",
    )(y_nsa, g, y_diff, w_o, x2)


HALO = 8
FFN_CHUNK = 256
FFN_ROW_BLOCKS = 4


def _ffn_kernel(h_ref, halo_ref, g_ref, wu_ref, wg_ref, cwu_ref, cwg_ref, cbu_ref, cbg_ref, wd_ref,
                o_ref, xn_ref, raw_ref, act_ref, *, tm):
    i = pl.program_id(0)
    j = pl.program_id(1)

    @pl.when(j == 0)
    def _():
        keep = jnp.where((i * tm) % SEQ == 0, 0.0, 1.0)
        h = h_ref[...]
        xn_ref[0:HALO, :] = (_rmsnorm(halo_ref[...], g_ref[...]) * keep).astype(BF16)
        xn_ref[HALO:, :] = _rmsnorm(h, g_ref[...]).astype(BF16)
        o_ref[...] = h

    xn = xn_ref[...]

    def conv(h, cw_ref, cb_ref, cols):
        cw = cw_ref[:, cols]
        h3 = h.reshape(h.shape[0] // SUBLANES, SUBLANES, h.shape[1])
        sub = lax.broadcasted_iota(jnp.int32, h3.shape[1:], 0)
        out = cw[CONV_W - 1:CONV_W, :] * h3[1:] + cb_ref[:, cols]
        for back in range(1, CONV_W):
            rolled = pltpu.roll(h3, back, 1)
            shifted = jnp.where(sub < back, rolled[:-1], rolled[1:])
            out = out + cw[CONV_W - 1 - back:CONV_W - back, :] * shifted
        return out.reshape(h.shape[0] - HALO, h.shape[1])

    def gated(rows, cols):
        src = slice(rows.start, rows.stop + HALO)
        u = conv(raw_ref[slot, 0, src, cols], cwu_ref, cbu_ref, cols)
        gate = conv(raw_ref[slot, 1, src, cols], cwg_ref, cbg_ref, cols)
        half_gate = 0.5 * gate
        act_ref[rows, cols] = (half_gate * (1.0 + jnp.tanh(half_gate)) * u).astype(BF16)

    slot = j % 2
    chunks = [slice(lo, lo + FFN_CHUNK) for lo in range(0, wu_ref.shape[1], FFN_CHUNK)]
    row_blocks = [slice(lo, lo + tm // FFN_ROW_BLOCKS) for lo in range(0, tm, tm // FFN_ROW_BLOCKS)]
    for cols in chunks:
        raw_ref[slot, 0, :, cols] = jnp.dot(xn, wu_ref[:, cols], preferred_element_type=F32)
        raw_ref[slot, 1, :, cols] = jnp.dot(xn, wg_ref[:, cols], preferred_element_type=F32)
        if cols is not chunks[-1]:
            gated(slice(0, tm), cols)
    gated(row_blocks[0], chunks[-1])
    for n, rows in enumerate(row_blocks):
        if n + 1 < len(row_blocks):
            gated(row_blocks[n + 1], chunks[-1])
        o_ref[rows, :] += jnp.dot(act_ref[rows, :], wd_ref[...], preferred_element_type=F32)


def _conv_ffn(h1, g, w_up, conv_w, conv_b, w_down, *, tm=1024, tf=512):
    t = h1.shape[0]
    assert t % tm == 0 and SEQ % tm == 0 and D_FF % tf == 0
    nf = D_FF // tf
    est = (4 * tm * D_MODEL * 4 + 2 * 3 * D_MODEL * tf * 2 + (tm + HALO) * D_MODEL * 2
           + 8 * (tm + HALO) * tf * 4)
    return pl.pallas_call(
        functools.partial(_ffn_kernel, tm=tm),
        grid=(t // tm, nf),
        in_specs=[
            pl.BlockSpec((tm, D_MODEL), lambda i, j: (i, 0)),
            pl.BlockSpec((HALO, D_MODEL), lambda i, j: (jnp.maximum(i * (tm // HALO) - 1, 0), 0)),
            pl.BlockSpec((1, D_MODEL), lambda i, j: (0, 0)),
            pl.BlockSpec((D_MODEL, tf), lambda i, j: (0, j)),
            pl.BlockSpec((D_MODEL, tf), lambda i, j: (0, nf + j)),
            pl.BlockSpec((CONV_W, tf), lambda i, j: (0, j)),
            pl.BlockSpec((CONV_W, tf), lambda i, j: (0, nf + j)),
            pl.BlockSpec((1, tf), lambda i, j: (0, j)),
            pl.BlockSpec((1, tf), lambda i, j: (0, nf + j)),
            pl.BlockSpec((tf, D_MODEL), lambda i, j: (j, 0)),
        ],
        out_specs=pl.BlockSpec((tm, D_MODEL), lambda i, j: (i, 0), pipeline_mode=pl.Buffered(1)),
        out_shape=jax.ShapeDtypeStruct((t, D_MODEL), F32),
        scratch_shapes=[pltpu.VMEM((tm + HALO, D_MODEL), BF16), pltpu.VMEM((2, 2, tm + HALO, tf), F32),
                        pltpu.VMEM((tm, tf), BF16)],
        compiler_params=_params(("parallel", "arbitrary"), est),
        name="conv_ffn",
    )(h1, h1, g, w_up, w_up, conv_w, conv_w, conv_b, conv_b, w_down)


PLE_CHUNK = 512


def _ple_kernel(h_ref, gp_ref, wg_ref, p_ref, wp_ref, gf_ref, o_ref):
    hn = _rmsnorm(h_ref[...], gp_ref[...]).astype(BF16)
    pb = p_ref[...].astype(BF16)
    ssq = None
    for lo in range(0, D_MODEL, PLE_CHUNK):
        cols = slice(lo, lo + PLE_CHUNK)
        gate = jax.nn.sigmoid(jnp.dot(hn, wg_ref[:, cols], preferred_element_type=F32))
        emb = jnp.dot(pb, wp_ref[:, cols], preferred_element_type=F32)
        h3 = h_ref[:, cols] + gate * emb
        o_ref[:, cols] = h3
        part = jnp.sum(h3 * h3, axis=-1, keepdims=True)
        ssq = part if ssq is None else ssq + part
    o_ref[...] = o_ref[...] * lax.rsqrt(ssq * (1.0 / D_MODEL) + EPS) * gf_ref[...]


def _ple_out(h2, g_ple, w_gate, p2, w_proj, g_final, *, tm=256):
    t = h2.shape[0]
    est = D_MODEL * D_MODEL * 2 + 2 * PLE_DIM * D_MODEL * 2 + 4 * tm * D_MODEL * 4 + 6 * tm * D_MODEL * 4
    return pl.pallas_call(
        _ple_kernel,
        grid=(t // tm,),
        in_specs=[
            pl.BlockSpec((tm, D_MODEL), lambda i: (i, 0)),
            pl.BlockSpec((1, D_MODEL), lambda i: (0, 0)),
            pl.BlockSpec((D_MODEL, D_MODEL), lambda i: (0, 0), pipeline_mode=pl.Buffered(1)),
            pl.BlockSpec((tm, PLE_DIM), lambda i: (i, 0)),
            pl.BlockSpec((PLE_DIM, D_MODEL), lambda i: (0, 0)),
            pl.BlockSpec((1, D_MODEL), lambda i: (0, 0)),
        ],
        out_specs=pl.BlockSpec((tm, D_MODEL), lambda i: (i, 0)),
        out_shape=jax.ShapeDtypeStruct((t, D_MODEL), F32),
        compiler_params=_params(("parallel",), est),
        name="ple_out",
    )(h2, g_ple, w_gate, p2, w_proj, g_final)


def _split_w_in(w):
    w_a, gate, w_b = _cast_w_in(w)
    gate = gate[:, :N_GATES].reshape(D_MODEL, 3, N_NSA_KV, NSA_GROUP).transpose(0, 2, 1, 3)
    gate = jnp.pad(gate.reshape(D_MODEL, N_NSA_KV, 3 * NSA_GROUP), ((0, 0), (0, 0), (0, GATE_STRIDE - 3 * NSA_GROUP)))
    gate = jnp.pad(gate.reshape(D_MODEL, N_NSA_KV * GATE_STRIDE), ((0, 0), (0, HEAD_DIM - N_NSA_KV * GATE_STRIDE)))
    return w_a, gate, w_b


CAST_BLOCK = 512


def _cast_w_in_kernel(cur_ref, nxt_ref, wa_ref, wg_ref, wb_ref, *, n_a):
    s = pl.program_id(0)

    @pl.when(s < n_a)
    def _():
        wa_ref[...] = cur_ref[...].T.astype(BF16)

    @pl.when((s >= n_a) & (s < pl.num_programs(0) - 1))
    def _():
        rows = jnp.concatenate([cur_ref[N_GATES:, :], nxt_ref[0:N_GATES, :]], axis=0)
        wb_ref[...] = rows.T.astype(BF16)

    @pl.when(s == pl.num_programs(0) - 1)
    def _():
        wg_ref[...] = cur_ref[0:HEAD_DIM, :].T.astype(BF16)


def _cast_w_in(w):
    d, n = w.shape
    nsa_cols = NSA_WIDTH + 6 * NSA_KV_WIDTH
    diff_cols = n - nsa_cols - N_GATES
    assert nsa_cols % CAST_BLOCK == 0 and diff_cols % CAST_BLOCK == 0 and N_GATES % SUBLANES == 0
    n_a, n_b = nsa_cols // CAST_BLOCK, diff_cols // CAST_BLOCK
    spill_rows = 32
    assert N_GATES <= spill_rows and CAST_BLOCK % spill_rows == 0
    est = 2 * CAST_BLOCK * d * 4 + 2 * 2 * CAST_BLOCK * d * 2 + 6 * CAST_BLOCK * d * 4

    def in_block(s):
        return jnp.where(s == n_a + n_b, n_a, s)

    return pl.pallas_call(
        functools.partial(_cast_w_in_kernel, n_a=n_a),
        grid=(n_a + n_b + 1,),
        in_specs=[pl.BlockSpec((CAST_BLOCK, d), lambda s: (in_block(s), 0)),
                  pl.BlockSpec((spill_rows, d),
                               lambda s: (jnp.clip(s + 1, n_a + 1, n_a + n_b) * (CAST_BLOCK // spill_rows), 0))],
        out_specs=[pl.BlockSpec((d, CAST_BLOCK), lambda s: (0, jnp.minimum(s, n_a - 1))),
                   pl.BlockSpec((d, HEAD_DIM), lambda s: (0, 0)),
                   pl.BlockSpec((d, CAST_BLOCK), lambda s: (0, jnp.clip(s - n_a, 0, n_b - 1)))],
        out_shape=[jax.ShapeDtypeStruct((d, nsa_cols), BF16), jax.ShapeDtypeStruct((d, HEAD_DIM), BF16),
                   jax.ShapeDtypeStruct((d, diff_cols), BF16)],
        compiler_params=_params(("arbitrary",), est),
        name="cast_w_in",
    )(w.T, w.T)


def _rope_tables():
    inv = 1.0 / (ROPE_THETA ** (jnp.arange(0, ROPE_DIM, 2, dtype=F32) / ROPE_DIM))
    ang = jnp.arange(SEQ, dtype=F32)[:, None] * inv[None, :]
    cos, sin = jnp.cos(ang), jnp.sin(ang)
    rest = HEAD_DIM - ROPE_DIM
    cos_t = jnp.concatenate([cos, cos, jnp.ones((SEQ, rest), F32)], axis=1)
    sin_t = jnp.concatenate([-sin, sin, jnp.zeros((SEQ, rest), F32)], axis=1)
    return cos_t, sin_t


def _overlap_t():
    cmp_starts = np.arange(N_CMP) * CMP_STRIDE
    sel_starts = np.arange(N_SEL) * SLC_LEN
    ov = np.clip(np.minimum(cmp_starts[:, None] + CMP_LEN, sel_starts[None, :] + SLC_LEN)
                 - np.maximum(cmp_starts[:, None], sel_starts[None, :]), 0, None).astype(np.float32) / CMP_LEN
    ovt = np.zeros((N_SEL, N_CMP_PAD), np.float32)
    ovt[:, :N_CMP] = ov.T
    return jnp.asarray(ovt, BF16)


def kernel(x, p, attn_norm, w_in, cmp_k_pos, cmp_k_w1, cmp_k_w2, cmp_v_pos, cmp_v_w1, cmp_v_w2, nsa_out_norm, diff_lq1, diff_lk1, diff_lq2, diff_lk2, diff_subln, w_o, ffn_norm, w_up, conv_w, conv_b, w_down, ple_norm, w_ple_gate, w_ple_proj, final_norm):
    batch, seq, _ = x.shape
    assert seq == SEQ and p.shape[0] == 1
    t = batch * seq
    layer = 0
    lambda_init = 0.8 - 0.6 * math.exp(-0.3 * layer)
    x2 = x.reshape(t, D_MODEL)
    cos_t, sin_t = _rope_tables()

    proj, hkv, w_o_b, w_up_b, w_down_b, w_gate_b = _in_proj(
        x2, attn_norm[layer][None], *_split_w_in(w_in[layer]), cos_t, sin_t,
        later_weights=(w_o[layer], w_up[layer], w_down[layer], w_ple_gate[layer]))

    hkv = hkv.reshape(2, N_NSA_KV * t // HALF_BLOCK, HALF_BLOCK * HEAD_DIM)
    w1 = jnp.stack([cmp_k_w1[layer], cmp_v_w1[layer]]).astype(BF16)
    w2 = jnp.stack([cmp_k_w2[layer], cmp_v_w2[layer]]).astype(BF16)
    pos = jnp.stack([cmp_k_pos[layer], cmp_v_pos[layer]]).reshape(2, 1, CMP_LEN * HEAD_DIM)
    pos = jnp.broadcast_to(pos, (2, 8, CMP_LEN * HEAD_DIM)).astype(BF16)
    cmp_kv = _compress(hkv, w1, pos, w2)

    y_nsa = _nsa_attention(proj, cmp_kv, _overlap_t(), batch)
    y_diff = _diff_attention(proj, diff_lq1[layer][None], diff_lk1[layer][None], diff_lq2[layer][None],
                             diff_lk2[layer][None], diff_subln[layer][None], batch, lambda_init)
    h1 = _out_proj(y_nsa, nsa_out_norm[layer][None], y_diff, w_o_b, x2)
    h2 = _conv_ffn(h1, ffn_norm[layer][None], w_up_b, conv_w[layer], conv_b[layer][None], w_down_b)
    out = _ple_out(h2, ple_norm[layer][None], w_gate_b, p[layer].reshape(t, PLE_DIM),
                   w_ple_proj[layer].astype(BF16), final_norm[None])
    return out.reshape(batch, seq, D_MODEL)
```

```python
import functools
import math

import numpy as np
import jax
import jax.numpy as jnp
from jax import lax
from jax.experimental import pallas as pl
from jax.experimental.pallas import tpu as pltpu

D_MODEL = 2048
SEQ = 2048
HEAD_DIM = 128
ROPE_DIM = HEAD_DIM // 4
ROPE_THETA = 500000.0
N_NSA_HEADS = 8
N_NSA_KV = 2
NSA_GROUP = N_NSA_HEADS // N_NSA_KV
CMP_LEN = 32
CMP_STRIDE = 16
CMP_HIDDEN = 256
SLC_LEN = 64
SLC_TOP = 16
WINDOW = 512
N_DIFF_HEADS = 4
D_FF = 5632
CONV_W = 3
PLE_DIM = 256
EPS = 1e-6

NSA_WIDTH = N_NSA_HEADS * HEAD_DIM
NSA_KV_WIDTH = N_NSA_KV * HEAD_DIM
DIFF_WIDTH = N_DIFF_HEADS * 2 * HEAD_DIM
N_GATES = 3 * N_NSA_HEADS
N_CMP = (SEQ - CMP_LEN) // CMP_STRIDE + 1
N_CMP_PAD = SEQ // CMP_STRIDE
N_SEL = SEQ // SLC_LEN
GATE_STRIDE = 16

COL_NQ = 0
COL_DQ = COL_NQ + NSA_WIDTH
COL_DK = COL_DQ + DIFF_WIDTH
COL_DV = COL_DK + DIFF_WIDTH
COL_KS = COL_DV + DIFF_WIDTH
COL_KW = COL_KS + NSA_KV_WIDTH
COL_VS = COL_KW + NSA_KV_WIDTH
COL_VW = COL_VS + NSA_KV_WIDTH
COL_GATE = COL_VW + NSA_KV_WIDTH
PROJ_COLS = COL_GATE + HEAD_DIM

V7X_LANES = 128
SUBLANES = 8
BF16_ROWS = 16
V7X_VMEM_REQUEST_CAP = 56 * 1024 * 1024
V7X_VMEM_REQUEST_FLOOR = 56 * 1024 * 1024
NEG = -1e30
QSCALE = HEAD_DIM ** -0.5 * math.log2(math.e)

_NT = (((1,), (1,)), ((), ()))
_TN = (((0,), (0,)), ((), ()))
BF16 = jnp.bfloat16
F32 = jnp.float32


def _params(semantics, vmem_estimate_bytes, flags=None):
    limit = min(max(int(vmem_estimate_bytes), V7X_VMEM_REQUEST_FLOOR), V7X_VMEM_REQUEST_CAP)
    return pltpu.CompilerParams(dimension_semantics=semantics, vmem_limit_bytes=limit, flags=flags)


def _rmsnorm(x, g):
    return x * lax.rsqrt(jnp.mean(x * x, axis=-1, keepdims=True) + EPS) * g


INPROJ_CHUNK = 1024
HALF_BLOCK = CMP_STRIDE

_HEADS_A = ([(COL_NQ + h * HEAD_DIM, True, True) for h in range(N_NSA_HEADS)]
            + [(("cmp", 0, h), True, False) for h in range(N_NSA_KV)]
            + [(("cmp", 1, h), False, False) for h in range(N_NSA_KV)]
            + [(COL_KS + h * HEAD_DIM, True, False) for h in range(N_NSA_KV)]
            + [(COL_VS + h * HEAD_DIM, False, False) for h in range(N_NSA_KV)]
            + [(COL_KW + h * HEAD_DIM, True, False) for h in range(N_NSA_KV)]
            + [(COL_VW + h * HEAD_DIM, False, False) for h in range(N_NSA_KV)])
_HEADS_B = ([(COL_DQ + h * HEAD_DIM, True, True) for h in range(2 * N_DIFF_HEADS)]
            + [(COL_DK + h * HEAD_DIM, True, False) for h in range(2 * N_DIFF_HEADS)]
            + [(COL_DV + h * HEAD_DIM, False, False) for h in range(2 * N_DIFF_HEADS)])
_HEADS_GATE = [(COL_GATE, False, False)]


def _inproj_kernel(*refs, n_cast):
    x_ref, g_ref, wa_ref, wg_ref, wb_ref, cos_ref, sin_ref = refs[:7]
    cast_in = refs[7:7 + n_cast]
    o_ref, hkv_ref = refs[7 + n_cast:9 + n_cast]
    cast_out = refs[9 + n_cast:9 + 2 * n_cast]
    stage_ref = refs[9 + 2 * n_cast]
    for src_ref, dst_ref in zip(cast_in, cast_out):
        dst_ref[...] = src_ref[...].astype(dst_ref.dtype)
    tm = x_ref.shape[0]
    xn = _rmsnorm(x_ref[...], g_ref[...]).astype(BF16)
    c = cos_ref[...]
    s = sin_ref[...]
    cq = c * QSCALE
    sq = s * QSCALE
    first_half = lax.broadcasted_iota(jnp.int32, c.shape, 1) < ROPE_DIM // 2
    n_staged = 0
    for w_ref, heads in ((wa_ref, _HEADS_A), (wg_ref, _HEADS_GATE), (wb_ref, _HEADS_B)):
        for lo in range(0, w_ref.shape[1], INPROJ_CHUNK):
            hi = min(lo + INPROJ_CHUNK, w_ref.shape[1])
            acc = jnp.dot(xn, w_ref[:, lo:hi], preferred_element_type=F32)
            for src in range(lo, hi, HEAD_DIM):
                dest, rope, is_query = heads[src // HEAD_DIM]
                a = acc[:, src - lo:src - lo + HEAD_DIM]
                if rope:
                    partner = jnp.where(first_half,
                                        pltpu.roll(a, HEAD_DIM - ROPE_DIM // 2, 1),
                                        pltpu.roll(a, ROPE_DIM // 2, 1))
                    a = a * cq + partner * sq if is_query else a * c + partner * s
                if isinstance(dest, tuple):
                    _, which, head = dest
                    stage = stage_ref.at[n_staged]
                    n_staged += 1
                    stage[...] = a
                    for r in range(HALF_BLOCK):
                        rows = stage[pl.ds(r, tm // HALF_BLOCK, stride=HALF_BLOCK), :]
                        hkv_ref[which, head, :, r * HEAD_DIM:(r + 1) * HEAD_DIM] = rows.astype(hkv_ref.dtype)
                else:
                    o_ref[:, dest:dest + HEAD_DIM] = a.astype(o_ref.dtype)


def _in_proj(x2, g, w, w_g, w_b, cos_t, sin_t, later_weights=(), *, tm=256):
    t = x2.shape[0]
    assert t % tm == 0 and SEQ % tm == 0 and tm % (HALF_BLOCK * BF16_ROWS) == 0
    n_steps = t // tm
    seq_tiles = SEQ // tm
    nsa_cols = NSA_WIDTH + 6 * NSA_KV_WIDTH
    n_w = nsa_cols + HEAD_DIM + w_b.shape[1]
    est = (D_MODEL * n_w * 2 + 2 * tm * D_MODEL * 4 + 2 * tm * n_w * 2 + tm * D_MODEL * 2
           + 3 * tm * INPROJ_CHUNK * 4)
    cast_specs = []
    for lw in later_weights:
        assert lw.shape[0] % (n_steps * BF16_ROWS) == 0
        cast_specs.append(pl.BlockSpec((lw.shape[0] // n_steps, lw.shape[1]), lambda i: (i, 0)))
        est += 2 * (lw.size // n_steps) * (4 + 2)
    resident = dict(pipeline_mode=pl.Buffered(1))
    half_cols = HALF_BLOCK * HEAD_DIM
    return pl.pallas_call(
        functools.partial(_inproj_kernel, n_cast=len(later_weights)),
        grid=(n_steps,),
        in_specs=[
            pl.BlockSpec((tm, D_MODEL), lambda i: (i, 0)),
            pl.BlockSpec((1, D_MODEL), lambda i: (0, 0)),
            pl.BlockSpec((D_MODEL, nsa_cols), lambda i: (0, 0), **resident),
            pl.BlockSpec(w_g.shape, lambda i: (0, 0), **resident),
            pl.BlockSpec(w_b.shape, lambda i: (0, 0), **resident),
            pl.BlockSpec((tm, HEAD_DIM), lambda i: (i % seq_tiles, 0)),
            pl.BlockSpec((tm, HEAD_DIM), lambda i: (i % seq_tiles, 0)),
        ] + cast_specs,
        out_specs=[
            pl.BlockSpec((tm, PROJ_COLS), lambda i: (i, 0)),
            pl.BlockSpec((2, N_NSA_KV, tm // HALF_BLOCK, half_cols), lambda i: (0, 0, i, 0)),
        ] + cast_specs,
        out_shape=[
            jax.ShapeDtypeStruct((t, PROJ_COLS), BF16),
            jax.ShapeDtypeStruct((2, N_NSA_KV, t // HALF_BLOCK, half_cols), BF16),
        ] + [jax.ShapeDtypeStruct(lw.shape, BF16) for lw in later_weights],
        scratch_shapes=[pltpu.VMEM((2 * N_NSA_KV, tm, HEAD_DIM), F32)],
        compiler_params=_params(("parallel",), est),
        name="in_proj",
    )(x2, g, w, w_g, w_b, cos_t, sin_t, *later_weights)


def _compress_kernel(h_ref, w1_ref, pos_ref, w2_ref, o_ref):
    half = CMP_LEN * HEAD_DIM // 2
    h = h_ref[0]
    top = jnp.dot(h, w1_ref[0, :half, :], preferred_element_type=F32)
    bot = jnp.dot(h, w1_ref[0, half:, :], preferred_element_type=F32)
    pos_bias = jnp.dot(pos_ref[0], w1_ref[0], preferred_element_type=F32)[0:1]
    pre = top + pltpu.roll(bot, bot.shape[0] - 1, 0) + pos_bias
    act = jax.nn.gelu(pre)
    o_ref[0] = jnp.dot(act.astype(BF16), w2_ref[0], preferred_element_type=F32).astype(o_ref.dtype)


def _compress(hkv, w1, pos, w2):
    rows = hkv.shape[1]
    kdim = CMP_LEN * HEAD_DIM
    est = 2 * (rows * kdim // 2 * 2 + kdim * CMP_HIDDEN * 2) + 6 * rows * CMP_HIDDEN * 4
    return pl.pallas_call(
        _compress_kernel,
        grid=(2,),
        in_specs=[
            pl.BlockSpec((1, rows, kdim // 2), lambda i: (i, 0, 0)),
            pl.BlockSpec((1, kdim, CMP_HIDDEN), lambda i: (i, 0, 0)),
            pl.BlockSpec((1, 8, kdim), lambda i: (i, 0, 0)),
            pl.BlockSpec((1, CMP_HIDDEN, HEAD_DIM), lambda i: (i, 0, 0)),
        ],
        out_specs=pl.BlockSpec((1, rows, HEAD_DIM), lambda i: (i, 0, 0)),
        out_shape=jax.ShapeDtypeStruct((2, rows, HEAD_DIM), BF16),
        compiler_params=_params(("parallel",), est),
        name="compress",
    )(hkv, w1, pos, w2)


def _causal_flash(chains, n_before, diag_mask, d, lanes):
    def tile_step(kt, carries, mask=None):
        stats = []
        probs = []
        for (score_fn, _), (m, l, _) in zip(chains, carries):
            s = score_fn(kt)
            if mask is not None:
                s = jnp.where(mask, s, NEG)
            m_new = jnp.maximum(m, jnp.max(s, axis=0, keepdims=True))
            alpha = jnp.exp2(m - m_new)
            p = jnp.exp2(s - m_new)
            stats.append((m_new, alpha * l + jnp.sum(p, axis=0, keepdims=True), alpha))
            probs.append(p.astype(BF16))
        out = []
        for (_, value_fn), (_, _, acc), (m_new, l_new, alpha), p in zip(chains, carries, stats, probs):
            pv = lax.dot_general(value_fn(kt), p, _TN, preferred_element_type=F32)
            out.append((m_new, l_new, alpha * acc + pv))
        return tuple(out)

    init = (jnp.full((1, lanes), NEG, F32), jnp.zeros((1, lanes), F32), jnp.zeros((d, lanes), F32))
    carries = lax.fori_loop(0, n_before, tile_step, (init,) * len(chains))
    return [acc * (1.0 / l) for _, l, acc in tile_step(n_before, carries, diag_mask)]


def _nsa_kernel(q_ref, kc0_ref, vc0_ref, kc1_ref, vc1_ref, ks_ref, vs_ref, kw_ref, vw_ref, gate_ref, ovt_ref,
                o_ref, *, tq):
    qi = pl.program_id(1)
    start = qi * tq
    nl = NSA_GROUP * tq
    kvs = range(N_NSA_KV)
    cmp_refs = ((kc0_ref, vc0_ref), (kc1_ref, vc1_ref))

    def head_cols(kv):
        return slice(kv * HEAD_DIM, (kv + 1) * HEAD_DIM)

    q_all = q_ref[...]
    qs = [jnp.concatenate([q_all[:, (kv * NSA_GROUP + g) * HEAD_DIM:(kv * NSA_GROUP + g + 1) * HEAD_DIM]
                           for g in range(NSA_GROUP)], axis=0) for kv in kvs]
    q_local = lax.broadcasted_iota(jnp.int32, (1, nl), 1) & (tq - 1)
    t_lane = start + q_local
    k_local = lax.broadcasted_iota(jnp.int32, (tq, nl), 0)
    causal = k_local <= q_local
    c_end = lax.broadcasted_iota(jnp.int32, (N_CMP_PAD, nl), 0) * CMP_STRIDE + (CMP_LEN - 1)
    cmask = c_end <= t_lane
    j_idx = lax.broadcasted_iota(jnp.int32, (N_SEL, tq), 0)
    t_q = start + lax.broadcasted_iota(jnp.int32, (N_SEL, tq), 1)
    cur = t_q // SLC_LEN
    forced = (j_idx == 0) | (j_idx == cur) | (j_idx == cur - 1)
    ovt = ovt_ref[...]

    o_cmp = []
    qs_masked = []
    for kv in kvs:
        kc_ref, vc_ref = cmp_refs[kv]
        s = lax.dot_general(kc_ref[0], qs[kv], _NT, preferred_element_type=F32)
        s = jnp.where(cmask, s, NEG)
        m = jnp.max(s, axis=0, keepdims=True)
        e = jnp.where(cmask, jnp.exp2(s - m), 0.0)
        l = jnp.sum(e, axis=0, keepdims=True)
        p_cmp = e * jnp.where(l > 0.0, 1.0 / l, 0.0)
        o_cmp.append(lax.dot_general(vc_ref[0], p_cmp.astype(BF16), _TN, preferred_element_type=F32))

        p_sum = p_cmp[:, 0:tq]
        for g in range(1, NSA_GROUP):
            p_sum = p_sum + p_cmp[:, g * tq:(g + 1) * tq]
        p_hi = p_sum.astype(BF16)
        p_lo = (p_sum - p_hi.astype(F32)).astype(BF16)
        p_slc = (jnp.dot(ovt, p_hi, preferred_element_type=F32)
                 + jnp.dot(ovt, p_lo, preferred_element_type=F32))
        score = jnp.where(forced, 1e4, jnp.where(j_idx > cur, -1e4, p_slc))
        rank = jnp.zeros((N_SEL, tq), jnp.int32)
        for i in range(N_SEL):
            row = score[i:i + 1, :]
            tie = jnp.where(j_idx > i, 1, 0)
            rank = rank + jnp.where(row > score, 1, jnp.where(row == score, tie, 0))
        bias = jnp.where((rank < SLC_TOP) & (j_idx <= cur), 0.0, NEG)
        bias_t = jnp.concatenate([bias, jnp.zeros((HEAD_DIM - N_SEL, tq), F32)], axis=0).T.astype(BF16)
        qs_masked.append(jnp.concatenate([qs[kv], jnp.concatenate([bias_t] * NSA_GROUP, axis=0)], axis=1))

    key_block = lax.broadcasted_iota(jnp.int32, (tq, HEAD_DIM), 0) // SLC_LEN
    block_lane = lax.broadcasted_iota(jnp.int32, (tq, HEAD_DIM), 1)

    def slc_chain(kv):
        def scores(kt):
            k0 = pl.multiple_of(kt * tq, tq)
            block_onehot = jnp.where(kt * (tq // SLC_LEN) + key_block == block_lane, 1.0, 0.0).astype(BF16)
            keys = jnp.concatenate([ks_ref[pl.ds(k0, tq), head_cols(kv)], block_onehot], axis=1)
            return lax.dot_general(keys, qs_masked[kv], _NT, preferred_element_type=F32)

        def values(kt):
            return vs_ref[pl.ds(pl.multiple_of(kt * tq, tq), tq), head_cols(kv)]

        return scores, values

    o_slc = _causal_flash([slc_chain(kv) for kv in kvs], qi, causal, HEAD_DIM, nl)

    win_p, win_l, win_v = [], [], []
    for kv in kvs:
        scores, values = [], []
        for back in range(WINDOW // tq, -1, -1):
            k0 = start - back * tq
            inside = k0 >= 0
            k0 = pl.multiple_of(jnp.maximum(k0, 0), tq)
            s = lax.dot_general(kw_ref[pl.ds(k0, tq), head_cols(kv)], qs[kv], _NT, preferred_element_type=F32)
            if back == 0:
                s = jnp.where(causal, s, NEG)
            elif back == WINDOW // tq:
                edge = q_local + jnp.where(inside, 0, tq)
                s = jnp.where(k_local > edge, s, NEG)
            else:
                s = s + jnp.where(inside, 0.0, NEG)
            scores.append(s)
            values.append(vw_ref[pl.ds(k0, tq), head_cols(kv)])
        s = jnp.concatenate(scores, axis=0)
        p = jnp.exp2(s - jnp.max(s, axis=0, keepdims=True))
        win_l.append(jnp.sum(p, axis=0, keepdims=True))
        win_p.append(p.astype(BF16))
        win_v.append(jnp.concatenate(values, axis=0))

    gates_t = jax.nn.sigmoid(gate_ref[...].astype(F32)).T
    for kv in kvs:
        acc_w = lax.dot_general(win_v[kv], win_p[kv], _TN, preferred_element_type=F32)
        o_win = acc_w * (1.0 / win_l[kv])

        gts = gates_t[kv * GATE_STRIDE:(kv + 1) * GATE_STRIDE]
        for g in range(NSA_GROUP):
            sl = slice(g * tq, (g + 1) * tq)
            o_t = (gts[g:g + 1, :] * o_cmp[kv][:, sl]
                   + gts[NSA_GROUP + g:NSA_GROUP + g + 1, :] * o_slc[kv][:, sl]
                   + gts[2 * NSA_GROUP + g:2 * NSA_GROUP + g + 1, :] * o_win[:, sl])
            head = kv * NSA_GROUP + g
            o_ref[:, head * HEAD_DIM:(head + 1) * HEAD_DIM] = o_t.T


def _nsa_attention(proj, cmp_kv, ovt, batch, *, tq=256):
    t = proj.shape[0]
    assert WINDOW % tq == 0 and tq % SLC_LEN == 0 and SEQ % tq == 0
    nq = SEQ // tq
    est = (4 * 2 * SEQ * NSA_KV_WIDTH * 2 + 2 * tq * NSA_WIDTH * (2 + 4)
           + N_NSA_KV * 10 * (WINDOW + tq) * NSA_GROUP * tq * 4)

    def kv_spec(col):
        return pl.BlockSpec((SEQ, NSA_KV_WIDTH), lambda b, i: (b, col // NSA_KV_WIDTH))

    def cmp_spec(which, kv):
        return pl.BlockSpec((1, N_CMP_PAD, HEAD_DIM), lambda b, i: (which, kv * batch + b, 0))

    return pl.pallas_call(
        functools.partial(_nsa_kernel, tq=tq),
        grid=(batch, nq),
        in_specs=[
            pl.BlockSpec((tq, NSA_WIDTH), lambda b, i: (b * nq + i, 0)),
            cmp_spec(0, 0), cmp_spec(1, 0), cmp_spec(0, 1), cmp_spec(1, 1),
            kv_spec(COL_KS), kv_spec(COL_VS), kv_spec(COL_KW), kv_spec(COL_VW),
            pl.BlockSpec((tq, HEAD_DIM), lambda b, i: (b * nq + i, COL_GATE // HEAD_DIM)),
            pl.BlockSpec((N_SEL, N_CMP_PAD), lambda b, i: (0, 0)),
        ],
        out_specs=pl.BlockSpec((tq, NSA_WIDTH), lambda b, i: (b * nq + i, 0)),
        out_shape=jax.ShapeDtypeStruct((t, NSA_WIDTH), F32),
        compiler_params=_params(("parallel", "arbitrary"), est),
        name="nsa_attn",
    )(proj, cmp_kv, cmp_kv, cmp_kv, cmp_kv, proj, proj, proj, proj, proj, ovt)


def _diff_kernel(q_ref, k_ref, v_ref, lq1_ref, lk1_ref, lq2_ref, lk2_ref, sub_ref, o_ref,
                 *, tq, lambda_init):
    qi = pl.program_id(2)
    wide = 2 * HEAD_DIM
    lam = (jnp.exp(jnp.sum(lq1_ref[...] * lk1_ref[...], axis=-1, keepdims=True))
           - jnp.exp(jnp.sum(lq2_ref[...] * lk2_ref[...], axis=-1, keepdims=True)) + lambda_init)
    q = q_ref[...]

    def chain(h):
        q1 = q[:, h * wide:h * wide + HEAD_DIM]
        q2 = q[:, h * wide + HEAD_DIM:(h + 1) * wide]

        def scores(kt):
            kk = k_ref[pl.ds(pl.multiple_of(kt * tq, tq), tq), h * wide:(h + 1) * wide]
            s1 = lax.dot_general(kk[:, :HEAD_DIM], q1, _NT, preferred_element_type=F32)
            s2 = lax.dot_general(kk[:, HEAD_DIM:], q2, _NT, preferred_element_type=F32)
            return jnp.concatenate([s1, s2], axis=1)

        def values(kt):
            return v_ref[pl.ds(pl.multiple_of(kt * tq, tq), tq), h * wide:(h + 1) * wide]

        return scores, values

    k_local = lax.broadcasted_iota(jnp.int32, (tq, 2 * tq), 0)
    q_local = lax.broadcasted_iota(jnp.int32, (1, 2 * tq), 1) & (tq - 1)
    outs = _causal_flash([chain(h) for h in range(DIFF_HEADS_PER_STEP)], qi, k_local <= q_local, wide, 2 * tq)
    for h, o_n in enumerate(outs):
        o_t = o_n[:, :tq] - lam * o_n[:, tq:]
        o = _rmsnorm(o_t.T, sub_ref[...]) * (1.0 - lambda_init)
        o_ref[:, h * wide:(h + 1) * wide] = o.astype(o_ref.dtype)


DIFF_HEADS_PER_STEP = 4


def _diff_attention(proj, lq1, lk1, lq2, lk2, subln, batch, lambda_init, *, tq=512):
    t = proj.shape[0]
    nq = SEQ // tq
    wide = 2 * HEAD_DIM
    step_cols = DIFF_HEADS_PER_STEP * wide
    assert COL_DQ % step_cols == 0 and COL_DK % step_cols == 0 and COL_DV % step_cols == 0
    est = 2 * 2 * SEQ * step_cols * 2 + 4 * tq * step_cols * 4 + DIFF_HEADS_PER_STEP * 16 * tq * 2 * tq * 4
    vec = pl.BlockSpec((1, HEAD_DIM), lambda b, h, i: (0, 0))
    return pl.pallas_call(
        functools.partial(_diff_kernel, tq=tq, lambda_init=lambda_init),
        grid=(batch, N_DIFF_HEADS // DIFF_HEADS_PER_STEP, nq),
        in_specs=[
            pl.BlockSpec((tq, step_cols), lambda b, h, i: (b * nq + i, COL_DQ // step_cols + h)),
            pl.BlockSpec((SEQ, step_cols), lambda b, h, i: (b, COL_DK // step_cols + h)),
            pl.BlockSpec((SEQ, step_cols), lambda b, h, i: (b, COL_DV // step_cols + h)),
            vec, vec, vec, vec,
            pl.BlockSpec((1, wide), lambda b, h, i: (0, 0)),
        ],
        out_specs=pl.BlockSpec((tq, step_cols), lambda b, h, i: (b * nq + i, h)),
        out_shape=jax.ShapeDtypeStruct((t, DIFF_WIDTH), BF16),
        compiler_params=_params(("parallel", "parallel", "arbitrary"), est),
        name="diff_attn",
    )(proj, proj, proj, lq1, lk1, lq2, lk2, subln)


def _oproj_kernel(yn_ref, g_ref, yd_ref, wo_ref, x_ref, o_ref):
    yn = _rmsnorm(yn_ref[...], g_ref[...]).astype(BF16)
    acc = jnp.dot(yn, wo_ref[:NSA_WIDTH, :], preferred_element_type=F32)
    acc = acc + jnp.dot(yd_ref[...], wo_ref[NSA_WIDTH:, :], preferred_element_type=F32)
    o_ref[...] = x_ref[...] + acc


def _out_proj(y_nsa, g, y_diff, w_o, x2, *, tm=512):
    t = x2.shape[0]
    est = (2 * (NSA_WIDTH + DIFF_WIDTH) * D_MODEL * 2 + 2 * tm * NSA_WIDTH * 4 + 2 * tm * DIFF_WIDTH * 2
           + 5 * tm * D_MODEL * 4)
    return pl.pallas_call(
        _oproj_kernel,
        grid=(t // tm,),
        in_specs=[
            pl.BlockSpec((tm, NSA_WIDTH), lambda i: (i, 0)),
            pl.BlockSpec((1, NSA_WIDTH), lambda i: (0, 0)),
            pl.BlockSpec((tm, DIFF_WIDTH), lambda i: (i, 0)),
            pl.BlockSpec((NSA_WIDTH + DIFF_WIDTH, D_MODEL), lambda i: (0, 0)),
            pl.BlockSpec((tm, D_MODEL), lambda i: (i, 0)),
        ],
        out_specs=pl.BlockSpec((tm, D_MODEL), lambda i: (i, 0)),
        out_shape=jax.ShapeDtypeStruct((t, D_MODEL), F32),
        compiler_params=_params(("parallel",), est),
        name="out_proj",
    )(y_nsa, g, y_diff, w_o, x2)


HALO = 8
FFN_CHUNK = 256
FFN_ROW_BLOCKS = 4


def _ffn_kernel(h_ref, halo_ref, g_ref, wu_ref, wg_ref, cwu_ref, cwg_ref, cbu_ref, cbg_ref, wd_ref,
                o_ref, xn_ref, raw_ref, act_ref, *, tm):
    i = pl.program_id(0)
    j = pl.program_id(1)

    @pl.when(j == 0)
    def _():
        keep = jnp.where((i * tm) % SEQ == 0, 0.0, 1.0)
        h = h_ref[...]
        xn_ref[0:HALO, :] = (_rmsnorm(halo_ref[...], g_ref[...]) * keep).astype(BF16)
        xn_ref[HALO:, :] = _rmsnorm(h, g_ref[...]).astype(BF16)
        o_ref[...] = h

    xn = xn_ref[...]

    def conv(h, cw_ref, cb_ref, cols):
        cw = cw_ref[:, cols]
        h3 = h.reshape(h.shape[0] // SUBLANES, SUBLANES, h.shape[1])
        sub = lax.broadcasted_iota(jnp.int32, h3.shape[1:], 0)
        out = cw[CONV_W - 1:CONV_W, :] * h3[1:] + cb_ref[:, cols]
        for back in range(1, CONV_W):
            rolled = pltpu.roll(h3, back, 1)
            shifted = jnp.where(sub < back, rolled[:-1], rolled[1:])
            out = out + cw[CONV_W - 1 - back:CONV_W - back, :] * shifted
        return out.reshape(h.shape[0] - HALO, h.shape[1])

    def gated(rows, cols):
        src = slice(rows.start, rows.stop + HALO)
        u = conv(raw_ref[slot, 0, src, cols], cwu_ref, cbu_ref, cols)
        gate = conv(raw_ref[slot, 1, src, cols], cwg_ref, cbg_ref, cols)
        half_gate = 0.5 * gate
        act_ref[rows, cols] = (half_gate * (1.0 + jnp.tanh(half_gate)) * u).astype(BF16)

    slot = j % 2
    chunks = [slice(lo, lo + FFN_CHUNK) for lo in range(0, wu_ref.shape[1], FFN_CHUNK)]
    row_blocks = [slice(lo, lo + tm // FFN_ROW_BLOCKS) for lo in range(0, tm, tm // FFN_ROW_BLOCKS)]
    for cols in chunks:
        raw_ref[slot, 0, :, cols] = jnp.dot(xn, wu_ref[:, cols], preferred_element_type=F32)
        raw_ref[slot, 1, :, cols] = jnp.dot(xn, wg_ref[:, cols], preferred_element_type=F32)
        if cols is not chunks[-1]:
            gated(slice(0, tm), cols)
    gated(row_blocks[0], chunks[-1])
    for n, rows in enumerate(row_blocks):
        if n + 1 < len(row_blocks):
            gated(row_blocks[n + 1], chunks[-1])
        o_ref[rows, :] += jnp.dot(act_ref[rows, :], wd_ref[...], preferred_element_type=F32)


def _conv_ffn(h1, g, w_up, conv_w, conv_b, w_down, *, tm=1024, tf=512):
    t = h1.shape[0]
    assert t % tm == 0 and SEQ % tm == 0 and D_FF % tf == 0
    nf = D_FF // tf
    est = (4 * tm * D_MODEL * 4 + 2 * 3 * D_MODEL * tf * 2 + (tm + HALO) * D_MODEL * 2
           + 8 * (tm + HALO) * tf * 4)
    return pl.pallas_call(
        functools.partial(_ffn_kernel, tm=tm),
        grid=(t // tm, nf),
        in_specs=[
            pl.BlockSpec((tm, D_MODEL), lambda i, j: (i, 0)),
            pl.BlockSpec((HALO, D_MODEL), lambda i, j: (jnp.maximum(i * (tm // HALO) - 1, 0), 0)),
            pl.BlockSpec((1, D_MODEL), lambda i, j: (0, 0)),
            pl.BlockSpec((D_MODEL, tf), lambda i, j: (0, j)),
            pl.BlockSpec((D_MODEL, tf), lambda i, j: (0, nf + j)),
            pl.BlockSpec((CONV_W, tf), lambda i, j: (0, j)),
            pl.BlockSpec((CONV_W, tf), lambda i, j: (0, nf + j)),
            pl.BlockSpec((1, tf), lambda i, j: (0, j)),
            pl.BlockSpec((1, tf), lambda i, j: (0, nf + j)),
            pl.BlockSpec((tf, D_MODEL), lambda i, j: (j, 0)),
        ],
        out_specs=pl.BlockSpec((tm, D_MODEL), lambda i, j: (i, 0), pipeline_mode=pl.Buffered(1)),
        out_shape=jax.ShapeDtypeStruct((t, D_MODEL), F32),
        scratch_shapes=[pltpu.VMEM((tm + HALO, D_MODEL), BF16), pltpu.VMEM((2, 2, tm + HALO, tf), F32),
                        pltpu.VMEM((tm, tf), BF16)],
        compiler_params=_params(("parallel", "arbitrary"), est),
        name="conv_ffn",
    )(h1, h1, g, w_up, w_up, conv_w, conv_w, conv_b, conv_b, w_down)


PLE_CHUNK = 512


def _ple_kernel(h_ref, gp_ref, wg_ref, p_ref, wp_ref, gf_ref, o_ref):
    hn = _rmsnorm(h_ref[...], gp_ref[...]).astype(BF16)
    pb = p_ref[...].astype(BF16)
    ssq = None
    for lo in range(0, D_MODEL, PLE_CHUNK):
        cols = slice(lo, lo + PLE_CHUNK)
        gate = jax.nn.sigmoid(jnp.dot(hn, wg_ref[:, cols], preferred_element_type=F32))
        emb = jnp.dot(pb, wp_ref[:, cols], preferred_element_type=F32)
        h3 = h_ref[:, cols] + gate * emb
        o_ref[:, cols] = h3
        part = jnp.sum(h3 * h3, axis=-1, keepdims=True)
        ssq = part if ssq is None else ssq + part
    o_ref[...] = o_ref[...] * lax.rsqrt(ssq * (1.0 / D_MODEL) + EPS) * gf_ref[...]


def _ple_out(h2, g_ple, w_gate, p2, w_proj, g_final, *, tm=256):
    t = h2.shape[0]
    est = D_MODEL * D_MODEL * 2 + 2 * PLE_DIM * D_MODEL * 2 + 4 * tm * D_MODEL * 4 + 6 * tm * D_MODEL * 4
    return pl.pallas_call(
        _ple_kernel,
        grid=(t // tm,),
        in_specs=[
            pl.BlockSpec((tm, D_MODEL), lambda i: (i, 0)),
            pl.BlockSpec((1, D_MODEL), lambda i: (0, 0)),
            pl.BlockSpec((D_MODEL, D_MODEL), lambda i: (0, 0), pipeline_mode=pl.Buffered(1)),
            pl.BlockSpec((tm, PLE_DIM), lambda i: (i, 0)),
            pl.BlockSpec((PLE_DIM, D_MODEL), lambda i: (0, 0)),
            pl.BlockSpec((1, D_MODEL), lambda i: (0, 0)),
        ],
        out_specs=pl.BlockSpec((tm, D_MODEL), lambda i: (i, 0)),
        out_shape=jax.ShapeDtypeStruct((t, D_MODEL), F32),
        compiler_params=_params(("parallel",), est),
        name="ple_out",
    )(h2, g_ple, w_gate, p2, w_proj, g_final)


def _split_w_in(w):
    w_a, gate, w_b = _cast_w_in(w)
    gate = gate[:, :N_GATES].reshape(D_MODEL, 3, N_NSA_KV, NSA_GROUP).transpose(0, 2, 1, 3)
    gate = jnp.pad(gate.reshape(D_MODEL, N_NSA_KV, 3 * NSA_GROUP), ((0, 0), (0, 0), (0, GATE_STRIDE - 3 * NSA_GROUP)))
    gate = jnp.pad(gate.reshape(D_MODEL, N_NSA_KV * GATE_STRIDE), ((0, 0), (0, HEAD_DIM - N_NSA_KV * GATE_STRIDE)))
    return w_a, gate, w_b


CAST_BLOCK = 512


def _cast_w_in_kernel(cur_ref, nxt_ref, wa_ref, wg_ref, wb_ref, *, n_a):
    s = pl.program_id(0)

    @pl.when(s < n_a)
    def _():
        wa_ref[...] = cur_ref[...].T.astype(BF16)

    @pl.when((s >= n_a) & (s < pl.num_programs(0) - 1))
    def _():
        rows = jnp.concatenate([cur_ref[N_GATES:, :], nxt_ref[0:N_GATES, :]], axis=0)
        wb_ref[...] = rows.T.astype(BF16)

    @pl.when(s == pl.num_programs(0) - 1)
    def _():
        wg_ref[...] = cur_ref[0:HEAD_DIM, :].T.astype(BF16)


def _cast_w_in(w):
    d, n = w.shape
    nsa_cols = NSA_WIDTH + 6 * NSA_KV_WIDTH
    diff_cols = n - nsa_cols - N_GATES
    assert nsa_cols % CAST_BLOCK == 0 and diff_cols % CAST_BLOCK == 0 and N_GATES % SUBLANES == 0
    n_a, n_b = nsa_cols // CAST_BLOCK, diff_cols // CAST_BLOCK
    spill_rows = 32
    assert N_GATES <= spill_rows and CAST_BLOCK % spill_rows == 0
    est = 2 * CAST_BLOCK * d * 4 + 2 * 2 * CAST_BLOCK * d * 2 + 6 * CAST_BLOCK * d * 4

    def in_block(s):
        return jnp.where(s == n_a + n_b, n_a, s)

    return pl.pallas_call(
        functools.partial(_cast_w_in_kernel, n_a=n_a),
        grid=(n_a + n_b + 1,),
        in_specs=[pl.BlockSpec((CAST_BLOCK, d), lambda s: (in_block(s), 0)),
                  pl.BlockSpec((spill_rows, d),
                               lambda s: (jnp.clip(s + 1, n_a + 1, n_a + n_b) * (CAST_BLOCK // spill_rows), 0))],
        out_specs=[pl.BlockSpec((d, CAST_BLOCK), lambda s: (0, jnp.minimum(s, n_a - 1))),
                   pl.BlockSpec((d, HEAD_DIM), lambda s: (0, 0)),
                   pl.BlockSpec((d, CAST_BLOCK), lambda s: (0, jnp.clip(s - n_a, 0, n_b - 1)))],
        out_shape=[jax.ShapeDtypeStruct((d, nsa_cols), BF16), jax.ShapeDtypeStruct((d, HEAD_DIM), BF16),
                   jax.ShapeDtypeStruct((d, diff_cols), BF16)],
        compiler_params=_params(("arbitrary",), est),
        name="cast_w_in",
    )(w.T, w.T)


def _rope_tables():
    inv = 1.0 / (ROPE_THETA ** (jnp.arange(0, ROPE_DIM, 2, dtype=F32) / ROPE_DIM))
    ang = jnp.arange(SEQ, dtype=F32)[:, None] * inv[None, :]
    cos, sin = jnp.cos(ang), jnp.sin(ang)
    rest = HEAD_DIM - ROPE_DIM
    cos_t = jnp.concatenate([cos, cos, jnp.ones((SEQ, rest), F32)], axis=1)
    sin_t = jnp.concatenate([-sin, sin, jnp.zeros((SEQ, rest), F32)], axis=1)
    return cos_t, sin_t


def _overlap_t():
    cmp_starts = np.arange(N_CMP) * CMP_STRIDE
    sel_starts = np.arange(N_SEL) * SLC_LEN
    ov = np.clip(np.minimum(cmp_starts[:, None] + CMP_LEN, sel_starts[None, :] + SLC_LEN)
                 - np.maximum(cmp_starts[:, None], sel_starts[None, :]), 0, None).astype(np.float32) / CMP_LEN
    ovt = np.zeros((N_SEL, N_CMP_PAD), np.float32)
    ovt[:, :N_CMP] = ov.T
    return jnp.asarray(ovt, BF16)


def kernel(x, p, attn_norm, w_in, cmp_k_pos, cmp_k_w1, cmp_k_w2, cmp_v_pos, cmp_v_w1, cmp_v_w2, nsa_out_norm, diff_lq1, diff_lk1, diff_lq2, diff_lk2, diff_subln, w_o, ffn_norm, w_up, conv_w, conv_b, w_down, ple_norm, w_ple_gate, w_ple_proj, final_norm):
    batch, seq, _ = x.shape
    assert seq == SEQ and p.shape[0] == 1
    t = batch * seq
    layer = 0
    lambda_init = 0.8 - 0.6 * math.exp(-0.3 * layer)
    x2 = x.reshape(t, D_MODEL)
    cos_t, sin_t = _rope_tables()

    proj, hkv, w_o_b, w_up_b, w_down_b, w_gate_b = _in_proj(
        x2, attn_norm[layer][None], *_split_w_in(w_in[layer]), cos_t, sin_t,
        later_weights=(w_o[layer], w_up[layer], w_down[layer], w_ple_gate[layer]))

    hkv = hkv.reshape(2, N_NSA_KV * t // HALF_BLOCK, HALF_BLOCK * HEAD_DIM)
    w1 = jnp.stack([cmp_k_w1[layer], cmp_v_w1[layer]]).astype(BF16)
    w2 = jnp.stack([cmp_k_w2[layer], cmp_v_w2[layer]]).astype(BF16)
    pos = jnp.stack([cmp_k_pos[layer], cmp_v_pos[layer]]).reshape(2, 1, CMP_LEN * HEAD_DIM)
    pos = jnp.broadcast_to(pos, (2, 8, CMP_LEN * HEAD_DIM)).astype(BF16)
    cmp_kv = _compress(hkv, w1, pos, w2)

    y_nsa = _nsa_attention(proj, cmp_kv, _overlap_t(), batch)
    y_diff = _diff_attention(proj, diff_lq1[layer][None], diff_lk1[layer][None], diff_lq2[layer][None],
                             diff_lk2[layer][None], diff_subln[layer][None], batch, lambda_init)
    h1 = _out_proj(y_nsa, nsa_out_norm[layer][None], y_diff, w_o_b, x2)
    h2 = _conv_ffn(h1, ffn_norm[layer][None], w_up_b, conv_w[layer], conv_b[layer][None], w_down_b)
    out = _ple_out(h2, ple_norm[layer][None], w_gate_b, p[layer].reshape(t, PLE_DIM),
                   w_ple_proj[layer].astype(BF16), final_norm[None])
    return out.reshape(batch, seq, D_MODEL)
```

```python
import functools
import math

import numpy as np
import jax
import jax.numpy as jnp
from jax import lax
from jax.experimental import pallas as pl
from jax.experimental.pallas import tpu as pltpu

D_MODEL = 2048
SEQ = 2048
HEAD_DIM = 128
ROPE_DIM = HEAD_DIM // 4
ROPE_THETA = 500000.0
N_NSA_HEADS = 8
N_NSA_KV = 2
NSA_GROUP = N_NSA_HEADS // N_NSA_KV
CMP_LEN = 32
CMP_STRIDE = 16
CMP_HIDDEN = 256
SLC_LEN = 64
SLC_TOP = 16
WINDOW = 512
N_DIFF_HEADS = 4
D_FF = 5632
CONV_W = 3
PLE_DIM = 256
EPS = 1e-6

NSA_WIDTH = N_NSA_HEADS * HEAD_DIM
NSA_KV_WIDTH = N_NSA_KV * HEAD_DIM
DIFF_WIDTH = N_DIFF_HEADS * 2 * HEAD_DIM
N_GATES = 3 * N_NSA_HEADS
N_CMP = (SEQ - CMP_LEN) // CMP_STRIDE + 1
N_CMP_PAD = SEQ // CMP_STRIDE
N_SEL = SEQ // SLC_LEN
GATE_STRIDE = 16

COL_NQ = 0
COL_DQ = COL_NQ + NSA_WIDTH
COL_DK = COL_DQ + DIFF_WIDTH
COL_DV = COL_DK + DIFF_WIDTH
COL_KS = COL_DV + DIFF_WIDTH
COL_KW = COL_KS + NSA_KV_WIDTH
COL_VS = COL_KW + NSA_KV_WIDTH
COL_VW = COL_VS + NSA_KV_WIDTH
COL_GATE = COL_VW + NSA_KV_WIDTH
PROJ_COLS = COL_GATE + HEAD_DIM

SUBLANES = 8
BF16_ROWS = 16
V7X_VMEM_REQUEST_CAP = 56 * 1024 * 1024
V7X_VMEM_REQUEST_FLOOR = 48 * 1024 * 1024
NEG = -1e30
QSCALE = HEAD_DIM ** -0.5 * math.log2(math.e)

_NT = (((1,), (1,)), ((), ()))
_TN = (((0,), (0,)), ((), ()))
BF16 = jnp.bfloat16
F32 = jnp.float32


def _params(semantics, vmem_estimate_bytes, flags=None):
    limit = min(max(int(vmem_estimate_bytes), V7X_VMEM_REQUEST_FLOOR), V7X_VMEM_REQUEST_CAP)
    return pltpu.CompilerParams(dimension_semantics=semantics, vmem_limit_bytes=limit, flags=flags)


def _rmsnorm(x, g):
    return x * lax.rsqrt(jnp.mean(x * x, axis=-1, keepdims=True) + EPS) * g


INPROJ_CHUNK = 1024
HALF_BLOCK = CMP_STRIDE

_HEADS_A = ([(COL_NQ + h * HEAD_DIM, True, True) for h in range(N_NSA_HEADS)]
            + [(("cmp", 0, h), True, False) for h in range(N_NSA_KV)]
            + [(("cmp", 1, h), False, False) for h in range(N_NSA_KV)]
            + [(COL_KS + h * HEAD_DIM, True, False) for h in range(N_NSA_KV)]
            + [(COL_VS + h * HEAD_DIM, False, False) for h in range(N_NSA_KV)]
            + [(COL_KW + h * HEAD_DIM, True, False) for h in range(N_NSA_KV)]
            + [(COL_VW + h * HEAD_DIM, False, False) for h in range(N_NSA_KV)])
_HEADS_B = ([(COL_DQ + h * HEAD_DIM, True, True) for h in range(2 * N_DIFF_HEADS)]
            + [(COL_DK + h * HEAD_DIM, True, False) for h in range(2 * N_DIFF_HEADS)]
            + [(COL_DV + h * HEAD_DIM, False, False) for h in range(2 * N_DIFF_HEADS)])
_HEADS_GATE = [(COL_GATE, False, False)]


def _inproj_kernel(*refs, n_cast):
    x_ref, g_ref, wa_ref, wg_ref, wb_ref, cos_ref, sin_ref = refs[:7]
    cast_in = refs[7:7 + n_cast]
    o_ref, hkv_ref = refs[7 + n_cast:9 + n_cast]
    cast_out = refs[9 + n_cast:9 + 2 * n_cast]
    stage_ref = refs[9 + 2 * n_cast]
    for src_ref, dst_ref in zip(cast_in, cast_out):
        dst_ref[...] = src_ref[...].astype(dst_ref.dtype)
    tm = x_ref.shape[0]
    xn = _rmsnorm(x_ref[...], g_ref[...]).astype(BF16)
    c = cos_ref[...]
    s = sin_ref[...]
    cq = c * QSCALE
    sq = s * QSCALE
    first_half = lax.broadcasted_iota(jnp.int32, c.shape, 1) < ROPE_DIM // 2
    n_staged = 0
    for w_ref, heads in ((wa_ref, _HEADS_A), (wg_ref, _HEADS_GATE), (wb_ref, _HEADS_B)):
        for lo in range(0, w_ref.shape[1], INPROJ_CHUNK):
            hi = min(lo + INPROJ_CHUNK, w_ref.shape[1])
            acc = jnp.dot(xn, w_ref[:, lo:hi], preferred_element_type=F32)
            for src in range(lo, hi, HEAD_DIM):
                dest, rope, is_query = heads[src // HEAD_DIM]
                a = acc[:, src - lo:src - lo + HEAD_DIM]
                if rope:
                    partner = jnp.where(first_half,
                                        pltpu.roll(a, HEAD_DIM - ROPE_DIM // 2, 1),
                                        pltpu.roll(a, ROPE_DIM // 2, 1))
                    a = a * cq + partner * sq if is_query else a * c + partner * s
                if isinstance(dest, tuple):
                    _, which, head = dest
                    stage = stage_ref.at[n_staged]
                    n_staged += 1
                    stage[...] = a
                    for r in range(HALF_BLOCK):
                        rows = stage[pl.ds(r, tm // HALF_BLOCK, stride=HALF_BLOCK), :]
                        hkv_ref[which, head, :, r * HEAD_DIM:(r + 1) * HEAD_DIM] = rows.astype(hkv_ref.dtype)
                else:
                    o_ref[:, dest:dest + HEAD_DIM] = a.astype(o_ref.dtype)


def _in_proj(x2, g, w, w_g, w_b, cos_t, sin_t, later_weights=(), *, tm=256):
    t = x2.shape[0]
    assert t % tm == 0 and SEQ % tm == 0 and tm % (HALF_BLOCK * BF16_ROWS) == 0
    n_steps = t // tm
    seq_tiles = SEQ // tm
    nsa_cols = NSA_WIDTH + 6 * NSA_KV_WIDTH
    n_w = nsa_cols + HEAD_DIM + w_b.shape[1]
    est = (D_MODEL * n_w * 2 + 2 * tm * D_MODEL * 4 + 2 * tm * n_w * 2 + tm * D_MODEL * 2
           + 3 * tm * INPROJ_CHUNK * 4)
    cast_specs = []
    for lw in later_weights:
        assert lw.shape[0] % (n_steps * BF16_ROWS) == 0
        cast_specs.append(pl.BlockSpec((lw.shape[0] // n_steps, lw.shape[1]), lambda i: (i, 0)))
        est += 2 * (lw.size // n_steps) * (4 + 2)
    resident = dict(pipeline_mode=pl.Buffered(1))
    half_cols = HALF_BLOCK * HEAD_DIM
    return pl.pallas_call(
        functools.partial(_inproj_kernel, n_cast=len(later_weights)),
        grid=(n_steps,),
        in_specs=[
            pl.BlockSpec((tm, D_MODEL), lambda i: (i, 0)),
            pl.BlockSpec((1, D_MODEL), lambda i: (0, 0)),
            pl.BlockSpec((D_MODEL, nsa_cols), lambda i: (0, 0), **resident),
            pl.BlockSpec(w_g.shape, lambda i: (0, 0), **resident),
            pl.BlockSpec(w_b.shape, lambda i: (0, 0), **resident),
            pl.BlockSpec((tm, HEAD_DIM), lambda i: (i % seq_tiles, 0)),
            pl.BlockSpec((tm, HEAD_DIM), lambda i: (i % seq_tiles, 0)),
        ] + cast_specs,
        out_specs=[
            pl.BlockSpec((tm, PROJ_COLS), lambda i: (i, 0)),
            pl.BlockSpec((2, N_NSA_KV, tm // HALF_BLOCK, half_cols), lambda i: (0, 0, i, 0)),
        ] + cast_specs,
        out_shape=[
            jax.ShapeDtypeStruct((t, PROJ_COLS), BF16),
            jax.ShapeDtypeStruct((2, N_NSA_KV, t // HALF_BLOCK, half_cols), BF16),
        ] + [jax.ShapeDtypeStruct(lw.shape, BF16) for lw in later_weights],
        scratch_shapes=[pltpu.VMEM((2 * N_NSA_KV, tm, HEAD_DIM), F32)],
        compiler_params=_params(("parallel",), est),
        name="in_proj",
    )(x2, g, w, w_g, w_b, cos_t, sin_t, *later_weights)


def _compress_kernel(h_ref, w1_ref, pos_ref, w2_ref, o_ref):
    half = CMP_LEN * HEAD_DIM // 2
    h = h_ref[0]
    top = jnp.dot(h, w1_ref[0, :half, :], preferred_element_type=F32)
    bot = jnp.dot(h, w1_ref[0, half:, :], preferred_element_type=F32)
    pos_bias = jnp.dot(pos_ref[0], w1_ref[0], preferred_element_type=F32)[0:1]
    pre = top + pltpu.roll(bot, bot.shape[0] - 1, 0) + pos_bias
    act = jax.nn.gelu(pre)
    o_ref[0] = jnp.dot(act.astype(BF16), w2_ref[0], preferred_element_type=F32).astype(o_ref.dtype)


def _compress(hkv, w1, pos, w2):
    rows = hkv.shape[1]
    kdim = CMP_LEN * HEAD_DIM
    est = 2 * (rows * kdim // 2 * 2 + kdim * CMP_HIDDEN * 2) + 6 * rows * CMP_HIDDEN * 4
    return pl.pallas_call(
        _compress_kernel,
        grid=(2,),
        in_specs=[
            pl.BlockSpec((1, rows, kdim // 2), lambda i: (i, 0, 0)),
            pl.BlockSpec((1, kdim, CMP_HIDDEN), lambda i: (i, 0, 0)),
            pl.BlockSpec((1, 8, kdim), lambda i: (i, 0, 0)),
            pl.BlockSpec((1, CMP_HIDDEN, HEAD_DIM), lambda i: (i, 0, 0)),
        ],
        out_specs=pl.BlockSpec((1, rows, HEAD_DIM), lambda i: (i, 0, 0)),
        out_shape=jax.ShapeDtypeStruct((2, rows, HEAD_DIM), BF16),
        compiler_params=_params(("parallel",), est),
        name="compress",
    )(hkv, w1, pos, w2)


def _causal_flash(chains, n_before, diag_mask, d, lanes):
    def tile_step(kt, carries, mask=None):
        stats = []
        probs = []
        for (score_fn, _), (m, l, _) in zip(chains, carries):
            s = score_fn(kt)
            if mask is not None:
                s = jnp.where(mask, s, NEG)
            m_new = jnp.maximum(m, jnp.max(s, axis=0, keepdims=True))
            alpha = jnp.exp2(m - m_new)
            p = jnp.exp2(s - m_new)
            stats.append((m_new, alpha * l + jnp.sum(p, axis=0, keepdims=True), alpha))
            probs.append(p.astype(BF16))
        out = []
        for (_, value_fn), (_, _, acc), (m_new, l_new, alpha), p in zip(chains, carries, stats, probs):
            pv = lax.dot_general(value_fn(kt), p, _TN, preferred_element_type=F32)
            out.append((m_new, l_new, alpha * acc + pv))
        return tuple(out)

    init = (jnp.full((1, lanes), NEG, F32), jnp.zeros((1, lanes), F32), jnp.zeros((d, lanes), F32))
    carries = lax.fori_loop(0, n_before, tile_step, (init,) * len(chains))
    return [acc * (1.0 / l) for _, l, acc in tile_step(n_before, carries, diag_mask)]


def _nsa_kernel(q_ref, kc0_ref, vc0_ref, kc1_ref, vc1_ref, ks_ref, vs_ref, kw_ref, vw_ref, gate_ref, ovt_ref,
                o_ref, *, tq):
    qi = pl.program_id(1)
    start = qi * tq
    nl = NSA_GROUP * tq
    kvs = range(N_NSA_KV)
    cmp_refs = ((kc0_ref, vc0_ref), (kc1_ref, vc1_ref))

    def head_cols(kv):
        return slice(kv * HEAD_DIM, (kv + 1) * HEAD_DIM)

    q_all = q_ref[...]
    qs = [jnp.concatenate([q_all[:, (kv * NSA_GROUP + g) * HEAD_DIM:(kv * NSA_GROUP + g + 1) * HEAD_DIM]
                           for g in range(NSA_GROUP)], axis=0) for kv in kvs]
    q_local = lax.broadcasted_iota(jnp.int32, (1, nl), 1) & (tq - 1)
    t_lane = start + q_local
    k_local = lax.broadcasted_iota(jnp.int32, (tq, nl), 0)
    causal = k_local <= q_local
    c_end = lax.broadcasted_iota(jnp.int32, (N_CMP_PAD, nl), 0) * CMP_STRIDE + (CMP_LEN - 1)
    cmask = c_end <= t_lane
    j_idx = lax.broadcasted_iota(jnp.int32, (N_SEL, tq), 0)
    t_q = start + lax.broadcasted_iota(jnp.int32, (N_SEL, tq), 1)
    cur = t_q // SLC_LEN
    forced = (j_idx == 0) | (j_idx == cur) | (j_idx == cur - 1)
    ovt = ovt_ref[...]

    o_cmp = []
    qs_masked = []
    for kv in kvs:
        kc_ref, vc_ref = cmp_refs[kv]
        s = lax.dot_general(kc_ref[0], qs[kv], _NT, preferred_element_type=F32)
        s = jnp.where(cmask, s, NEG)
        m = jnp.max(s, axis=0, keepdims=True)
        e = jnp.where(cmask, jnp.exp2(s - m), 0.0)
        l = jnp.sum(e, axis=0, keepdims=True)
        p_cmp = e * jnp.where(l > 0.0, 1.0 / l, 0.0)
        o_cmp.append(lax.dot_general(vc_ref[0], p_cmp.astype(BF16), _TN, preferred_element_type=F32))

        p_sum = p_cmp[:, 0:tq]
        for g in range(1, NSA_GROUP):
            p_sum = p_sum + p_cmp[:, g * tq:(g + 1) * tq]
        p_hi = p_sum.astype(BF16)
        p_lo = (p_sum - p_hi.astype(F32)).astype(BF16)
        p_slc = (jnp.dot(ovt, p_hi, preferred_element_type=F32)
                 + jnp.dot(ovt, p_lo, preferred_element_type=F32))
        score = jnp.where(forced, 1e4, jnp.where(j_idx > cur, -1e4, p_slc))
        rank = jnp.zeros((N_SEL, tq), jnp.int32)
        for i in range(N_SEL):
            row = score[i:i + 1, :]
            tie = jnp.where(j_idx > i, 1, 0)
            rank = rank + jnp.where(row > score, 1, jnp.where(row == score, tie, 0))
        bias = jnp.where((rank < SLC_TOP) & (j_idx <= cur), 0.0, NEG)
        bias_t = jnp.concatenate([bias, jnp.zeros((HEAD_DIM - N_SEL, tq), F32)], axis=0).T.astype(BF16)
        qs_masked.append(jnp.concatenate([qs[kv], jnp.concatenate([bias_t] * NSA_GROUP, axis=0)], axis=1))

    key_block = lax.broadcasted_iota(jnp.int32, (tq, HEAD_DIM), 0) // SLC_LEN
    block_lane = lax.broadcasted_iota(jnp.int32, (tq, HEAD_DIM), 1)

    def slc_chain(kv):
        def scores(kt):
            k0 = pl.multiple_of(kt * tq, tq)
            block_onehot = jnp.where(kt * (tq // SLC_LEN) + key_block == block_lane, 1.0, 0.0).astype(BF16)
            keys = jnp.concatenate([ks_ref[pl.ds(k0, tq), head_cols(kv)], block_onehot], axis=1)
            return lax.dot_general(keys, qs_masked[kv], _NT, preferred_element_type=F32)

        def values(kt):
            return vs_ref[pl.ds(pl.multiple_of(kt * tq, tq), tq), head_cols(kv)]

        return scores, values

    o_slc = _causal_flash([slc_chain(kv) for kv in kvs], qi, causal, HEAD_DIM, nl)

    win_p, win_l, win_v = [], [], []
    for kv in kvs:
        scores, values = [], []
        for back in range(WINDOW // tq, -1, -1):
            k0 = start - back * tq
            inside = k0 >= 0
            k0 = pl.multiple_of(jnp.maximum(k0, 0), tq)
            s = lax.dot_general(kw_ref[pl.ds(k0, tq), head_cols(kv)], qs[kv], _NT, preferred_element_type=F32)
            if back == 0:
                s = jnp.where(causal, s, NEG)
            elif back == WINDOW // tq:
                edge = q_local + jnp.where(inside, 0, tq)
                s = jnp.where(k_local > edge, s, NEG)
            else:
                s = s + jnp.where(inside, 0.0, NEG)
            scores.append(s)
            values.append(vw_ref[pl.ds(k0, tq), head_cols(kv)])
        s = jnp.concatenate(scores, axis=0)
        p = jnp.exp2(s - jnp.max(s, axis=0, keepdims=True))
        win_l.append(jnp.sum(p, axis=0, keepdims=True))
        win_p.append(p.astype(BF16))
        win_v.append(jnp.concatenate(values, axis=0))

    gates_t = jax.nn.sigmoid(gate_ref[...].astype(F32)).T
    for kv in kvs:
        acc_w = lax.dot_general(win_v[kv], win_p[kv], _TN, preferred_element_type=F32)
        o_win = acc_w * (1.0 / win_l[kv])

        gts = gates_t[kv * GATE_STRIDE:(kv + 1) * GATE_STRIDE]
        for g in range(NSA_GROUP):
            sl = slice(g * tq, (g + 1) * tq)
            o_t = (gts[g:g + 1, :] * o_cmp[kv][:, sl]
                   + gts[NSA_GROUP + g:NSA_GROUP + g + 1, :] * o_slc[kv][:, sl]
                   + gts[2 * NSA_GROUP + g:2 * NSA_GROUP + g + 1, :] * o_win[:, sl])
            head = kv * NSA_GROUP + g
            o_ref[:, head * HEAD_DIM:(head + 1) * HEAD_DIM] = o_t.T


def _nsa_attention(proj, cmp_kv, ovt, batch, *, tq=256):
    t = proj.shape[0]
    assert WINDOW % tq == 0 and tq % SLC_LEN == 0 and SEQ % tq == 0
    nq = SEQ // tq
    est = (4 * 2 * SEQ * NSA_KV_WIDTH * 2 + 2 * tq * NSA_WIDTH * (2 + 4)
           + N_NSA_KV * 10 * (WINDOW + tq) * NSA_GROUP * tq * 4)

    def kv_spec(col):
        return pl.BlockSpec((SEQ, NSA_KV_WIDTH), lambda b, i: (b, col // NSA_KV_WIDTH))

    def cmp_spec(which, kv):
        return pl.BlockSpec((1, N_CMP_PAD, HEAD_DIM), lambda b, i: (which, kv * batch + b, 0))

    return pl.pallas_call(
        functools.partial(_nsa_kernel, tq=tq),
        grid=(batch, nq),
        in_specs=[
            pl.BlockSpec((tq, NSA_WIDTH), lambda b, i: (b * nq + i, 0)),
            cmp_spec(0, 0), cmp_spec(1, 0), cmp_spec(0, 1), cmp_spec(1, 1),
            kv_spec(COL_KS), kv_spec(COL_VS), kv_spec(COL_KW), kv_spec(COL_VW),
            pl.BlockSpec((tq, HEAD_DIM), lambda b, i: (b * nq + i, COL_GATE // HEAD_DIM)),
            pl.BlockSpec((N_SEL, N_CMP_PAD), lambda b, i: (0, 0)),
        ],
        out_specs=pl.BlockSpec((tq, NSA_WIDTH), lambda b, i: (b * nq + i, 0)),
        out_shape=jax.ShapeDtypeStruct((t, NSA_WIDTH), F32),
        compiler_params=_params(("parallel", "arbitrary"), est),
        name="nsa_attn",
    )(proj, cmp_kv, cmp_kv, cmp_kv, cmp_kv, proj, proj, proj, proj, proj, ovt)


def _diff_kernel(q_ref, k_ref, v_ref, lq1_ref, lk1_ref, lq2_ref, lk2_ref, sub_ref, o_ref,
                 *, tq, lambda_init):
    qi = pl.program_id(2)
    wide = 2 * HEAD_DIM
    lam = (jnp.exp(jnp.sum(lq1_ref[...] * lk1_ref[...], axis=-1, keepdims=True))
           - jnp.exp(jnp.sum(lq2_ref[...] * lk2_ref[...], axis=-1, keepdims=True)) + lambda_init)
    q = q_ref[...]

    def chain(h):
        q1 = q[:, h * wide:h * wide + HEAD_DIM]
        q2 = q[:, h * wide + HEAD_DIM:(h + 1) * wide]

        def scores(kt):
            kk = k_ref[pl.ds(pl.multiple_of(kt * tq, tq), tq), h * wide:(h + 1) * wide]
            s1 = lax.dot_general(kk[:, :HEAD_DIM], q1, _NT, preferred_element_type=F32)
            s2 = lax.dot_general(kk[:, HEAD_DIM:], q2, _NT, preferred_element_type=F32)
            return jnp.concatenate([s1, s2], axis=1)

        def values(kt):
            return v_ref[pl.ds(pl.multiple_of(kt * tq, tq), tq), h * wide:(h + 1) * wide]

        return scores, values

    k_local = lax.broadcasted_iota(jnp.int32, (tq, 2 * tq), 0)
    q_local = lax.broadcasted_iota(jnp.int32, (1, 2 * tq), 1) & (tq - 1)
    outs = _causal_flash([chain(h) for h in range(DIFF_HEADS_PER_STEP)], qi, k_local <= q_local, wide, 2 * tq)
    for h, o_n in enumerate(outs):
        o_t = o_n[:, :tq] - lam * o_n[:, tq:]
        o = _rmsnorm(o_t.T, sub_ref[...]) * (1.0 - lambda_init)
        o_ref[:, h * wide:(h + 1) * wide] = o.astype(o_ref.dtype)


DIFF_HEADS_PER_STEP = 4


def _diff_attention(proj, lq1, lk1, lq2, lk2, subln, batch, lambda_init, *, tq=512):
    t = proj.shape[0]
    nq = SEQ // tq
    wide = 2 * HEAD_DIM
    step_cols = DIFF_HEADS_PER_STEP * wide
    assert COL_DQ % step_cols == 0 and COL_DK % step_cols == 0 and COL_DV % step_cols == 0
    est = 2 * 2 * SEQ * step_cols * 2 + 4 * tq * step_cols * 4 + DIFF_HEADS_PER_STEP * 16 * tq * 2 * tq * 4
    vec = pl.BlockSpec((1, HEAD_DIM), lambda b, h, i: (0, 0))
    return pl.pallas_call(
        functools.partial(_diff_kernel, tq=tq, lambda_init=lambda_init),
        grid=(batch, N_DIFF_HEADS // DIFF_HEADS_PER_STEP, nq),
        in_specs=[
            pl.BlockSpec((tq, step_cols), lambda b, h, i: (b * nq + i, COL_DQ // step_cols + h)),
            pl.BlockSpec((SEQ, step_cols), lambda b, h, i: (b, COL_DK // step_cols + h)),
            pl.BlockSpec((SEQ, step_cols), lambda b, h, i: (b, COL_DV // step_cols + h)),
            vec, vec, vec, vec,
            pl.BlockSpec((1, wide), lambda b, h, i: (0, 0)),
        ],
        out_specs=pl.BlockSpec((tq, step_cols), lambda b, h, i: (b * nq + i, h)),
        out_shape=jax.ShapeDtypeStruct((t, DIFF_WIDTH), BF16),
        compiler_params=_params(("parallel", "parallel", "arbitrary"), est),
        name="diff_attn",
    )(proj, proj, proj, lq1, lk1, lq2, lk2, subln)


def _oproj_kernel(yn_ref, g_ref, yd_ref, wo_ref, x_ref, o_ref):
    yn = _rmsnorm(yn_ref[...], g_ref[...]).astype(BF16)
    acc = jnp.dot(yn, wo_ref[:NSA_WIDTH, :], preferred_element_type=F32)
    acc = acc + jnp.dot(yd_ref[...], wo_ref[NSA_WIDTH:, :], preferred_element_type=F32)
    o_ref[...] = x_ref[...] + acc


def _out_proj(y_nsa, g, y_diff, w_o, x2, *, tm=512):
    t = x2.shape[0]
    est = (2 * (NSA_WIDTH + DIFF_WIDTH) * D_MODEL * 2 + 2 * tm * NSA_WIDTH * 4 + 2 * tm * DIFF_WIDTH * 2
           + 5 * tm * D_MODEL * 4)
    return pl.pallas_call(
        _oproj_kernel,
        grid=(t // tm,),
        in_specs=[
            pl.BlockSpec((tm, NSA_WIDTH), lambda i: (i, 0)),
            pl.BlockSpec((1, NSA_WIDTH), lambda i: (0, 0)),
            pl.BlockSpec((tm, DIFF_WIDTH), lambda i: (i, 0)),
            pl.BlockSpec((NSA_WIDTH + DIFF_WIDTH, D_MODEL), lambda i: (0, 0)),
            pl.BlockSpec((tm, D_MODEL), lambda i: (i, 0)),
        ],
        out_specs=pl.BlockSpec((tm, D_MODEL), lambda i: (i, 0)),
        out_shape=jax.ShapeDtypeStruct((t, D_MODEL), F32),
        compiler_params=_params(("parallel",), est),
        name="out_proj",
    )(y_nsa, g, y_diff, w_o, x2)


HALO = 8
FFN_CHUNK = 256
FFN_ROW_BLOCKS = 4


def _ffn_kernel(h_ref, halo_ref, g_ref, wu_ref, wg_ref, cwu_ref, cwg_ref, cbu_ref, cbg_ref, wd_ref,
                o_ref, xn_ref, raw_ref, act_ref, *, tm):
    i = pl.program_id(0)
    j = pl.program_id(1)

    @pl.when(j == 0)
    def _():
        keep = jnp.where((i * tm) % SEQ == 0, 0.0, 1.0)
        h = h_ref[...]
        xn_ref[0:HALO, :] = (_rmsnorm(halo_ref[...], g_ref[...]) * keep).astype(BF16)
        xn_ref[HALO:, :] = _rmsnorm(h, g_ref[...]).astype(BF16)
        o_ref[...] = h

    xn = xn_ref[...]

    def conv(h, cw_ref, cb_ref, cols):
        cw = cw_ref[:, cols]
        h3 = h.reshape(h.shape[0] // SUBLANES, SUBLANES, h.shape[1])
        sub = lax.broadcasted_iota(jnp.int32, h3.shape[1:], 0)
        out = cw[CONV_W - 1:CONV_W, :] * h3[1:] + cb_ref[:, cols]
        for back in range(1, CONV_W):
            rolled = pltpu.roll(h3, back, 1)
            shifted = jnp.where(sub < back, rolled[:-1], rolled[1:])
            out = out + cw[CONV_W - 1 - back:CONV_W - back, :] * shifted
        return out.reshape(h.shape[0] - HALO, h.shape[1])

    def gated(rows, cols):
        src = slice(rows.start, rows.stop + HALO)
        u = conv(raw_ref[slot, 0, src, cols], cwu_ref, cbu_ref, cols)
        gate = conv(raw_ref[slot, 1, src, cols], cwg_ref, cbg_ref, cols)
        half_gate = 0.5 * gate
        act_ref[rows, cols] = (half_gate * (1.0 + jnp.tanh(half_gate)) * u).astype(BF16)

    slot = j % 2
    chunks = [slice(lo, lo + FFN_CHUNK) for lo in range(0, wu_ref.shape[1], FFN_CHUNK)]
    row_blocks = [slice(lo, lo + tm // FFN_ROW_BLOCKS) for lo in range(0, tm, tm // FFN_ROW_BLOCKS)]
    for cols in chunks:
        raw_ref[slot, 0, :, cols] = jnp.dot(xn, wu_ref[:, cols], preferred_element_type=F32)
        raw_ref[slot, 1, :, cols] = jnp.dot(xn, wg_ref[:, cols], preferred_element_type=F32)
        if cols is not chunks[-1]:
            gated(slice(0, tm), cols)
    gated(row_blocks[0], chunks[-1])
    for n, rows in enumerate(row_blocks):
        if n + 1 < len(row_blocks):
            gated(row_blocks[n + 1], chunks[-1])
        o_ref[rows, :] += jnp.dot(act_ref[rows, :], wd_ref[...], preferred_element_type=F32)


def _conv_ffn(h1, g, w_up, conv_w, conv_b, w_down, *, tm=1024, tf=512):
    t = h1.shape[0]
    assert t % tm == 0 and SEQ % tm == 0 and D_FF % tf == 0
    nf = D_FF // tf
    est = (4 * tm * D_MODEL * 4 + 2 * 3 * D_MODEL * tf * 2 + (tm + HALO) * D_MODEL * 2
           + 8 * (tm + HALO) * tf * 4)
    return pl.pallas_call(
        functools.partial(_ffn_kernel, tm=tm),
        grid=(t // tm, nf),
        in_specs=[
            pl.BlockSpec((tm, D_MODEL), lambda i, j: (i, 0)),
            pl.BlockSpec((HALO, D_MODEL), lambda i, j: (jnp.maximum(i * (tm // HALO) - 1, 0), 0)),
            pl.BlockSpec((1, D_MODEL), lambda i, j: (0, 0)),
            pl.BlockSpec((D_MODEL, tf), lambda i, j: (0, j)),
            pl.BlockSpec((D_MODEL, tf), lambda i, j: (0, nf + j)),
            pl.BlockSpec((CONV_W, tf), lambda i, j: (0, j)),
            pl.BlockSpec((CONV_W, tf), lambda i, j: (0, nf + j)),
            pl.BlockSpec((1, tf), lambda i, j: (0, j)),
            pl.BlockSpec((1, tf), lambda i, j: (0, nf + j)),
            pl.BlockSpec((tf, D_MODEL), lambda i, j: (j, 0)),
        ],
        out_specs=pl.BlockSpec((tm, D_MODEL), lambda i, j: (i, 0), pipeline_mode=pl.Buffered(1)),
        out_shape=jax.ShapeDtypeStruct((t, D_MODEL), F32),
        scratch_shapes=[pltpu.VMEM((tm + HALO, D_MODEL), BF16), pltpu.VMEM((2, 2, tm + HALO, tf), F32),
                        pltpu.VMEM((tm, tf), BF16)],
        compiler_params=_params(("parallel", "arbitrary"), est),
        name="conv_ffn",
    )(h1, h1, g, w_up, w_up, conv_w, conv_w, conv_b, conv_b, w_down)


PLE_CHUNK = 512


def _ple_kernel(h_ref, gp_ref, wg_ref, p_ref, wp_ref, gf_ref, o_ref):
    hn = _rmsnorm(h_ref[...], gp_ref[...]).astype(BF16)
    pb = p_ref[...].astype(BF16)
    ssq = None
    for lo in range(0, D_MODEL, PLE_CHUNK):
        cols = slice(lo, lo + PLE_CHUNK)
        gate = jax.nn.sigmoid(jnp.dot(hn, wg_ref[:, cols], preferred_element_type=F32))
        emb = jnp.dot(pb, wp_ref[:, cols], preferred_element_type=F32)
        h3 = h_ref[:, cols] + gate * emb
        o_ref[:, cols] = h3
        part = jnp.sum(h3 * h3, axis=-1, keepdims=True)
        ssq = part if ssq is None else ssq + part
    o_ref[...] = o_ref[...] * lax.rsqrt(ssq * (1.0 / D_MODEL) + EPS) * gf_ref[...]


def _ple_out(h2, g_ple, w_gate, p2, w_proj, g_final, *, tm=256):
    t = h2.shape[0]
    est = D_MODEL * D_MODEL * 2 + 2 * PLE_DIM * D_MODEL * 2 + 4 * tm * D_MODEL * 4 + 6 * tm * D_MODEL * 4
    return pl.pallas_call(
        _ple_kernel,
        grid=(t // tm,),
        in_specs=[
            pl.BlockSpec((tm, D_MODEL), lambda i: (i, 0)),
            pl.BlockSpec((1, D_MODEL), lambda i: (0, 0)),
            pl.BlockSpec((D_MODEL, D_MODEL), lambda i: (0, 0), pipeline_mode=pl.Buffered(1)),
            pl.BlockSpec((tm, PLE_DIM), lambda i: (i, 0)),
            pl.BlockSpec((PLE_DIM, D_MODEL), lambda i: (0, 0)),
            pl.BlockSpec((1, D_MODEL), lambda i: (0, 0)),
        ],
        out_specs=pl.BlockSpec((tm, D_MODEL), lambda i: (i, 0)),
        out_shape=jax.ShapeDtypeStruct((t, D_MODEL), F32),
        compiler_params=_params(("parallel",), est),
        name="ple_out",
    )(h2, g_ple, w_gate, p2, w_proj, g_final)


def _split_w_in(w):
    w_a, gate, w_b = _cast_w_in(w)
    gate = gate[:, :N_GATES].reshape(D_MODEL, 3, N_NSA_KV, NSA_GROUP).transpose(0, 2, 1, 3)
    gate = jnp.pad(gate.reshape(D_MODEL, N_NSA_KV, 3 * NSA_GROUP), ((0, 0), (0, 0), (0, GATE_STRIDE - 3 * NSA_GROUP)))
    gate = jnp.pad(gate.reshape(D_MODEL, N_NSA_KV * GATE_STRIDE), ((0, 0), (0, HEAD_DIM - N_NSA_KV * GATE_STRIDE)))
    return w_a, gate, w_b


CAST_BLOCK = 512


def _cast_w_in_kernel(cur_ref, nxt_ref, wa_ref, wg_ref, wb_ref, *, n_a):
    s = pl.program_id(0)

    @pl.when(s < n_a)
    def _():
        wa_ref[...] = cur_ref[...].T.astype(BF16)

    @pl.when((s >= n_a) & (s < pl.num_programs(0) - 1))
    def _():
        rows = jnp.concatenate([cur_ref[N_GATES:, :], nxt_ref[0:N_GATES, :]], axis=0)
        wb_ref[...] = rows.T.astype(BF16)

    @pl.when(s == pl.num_programs(0) - 1)
    def _():
        wg_ref[...] = cur_ref[0:HEAD_DIM, :].T.astype(BF16)


def _cast_w_in(w):
    d, n = w.shape
    nsa_cols = NSA_WIDTH + 6 * NSA_KV_WIDTH
    diff_cols = n - nsa_cols - N_GATES
    assert nsa_cols % CAST_BLOCK == 0 and diff_cols % CAST_BLOCK == 0 and N_GATES % SUBLANES == 0
    n_a, n_b = nsa_cols // CAST_BLOCK, diff_cols // CAST_BLOCK
    spill_rows = 32
    assert N_GATES <= spill_rows and CAST_BLOCK % spill_rows == 0
    est = 2 * CAST_BLOCK * d * 4 + 2 * 2 * CAST_BLOCK * d * 2 + 6 * CAST_BLOCK * d * 4

    def in_block(s):
        return jnp.where(s == n_a + n_b, n_a, s)

    return pl.pallas_call(
        functools.partial(_cast_w_in_kernel, n_a=n_a),
        grid=(n_a + n_b + 1,),
        in_specs=[pl.BlockSpec((CAST_BLOCK, d), lambda s: (in_block(s), 0)),
                  pl.BlockSpec((spill_rows, d),
                               lambda s: (jnp.clip(s + 1, n_a + 1, n_a + n_b) * (CAST_BLOCK // spill_rows), 0))],
        out_specs=[pl.BlockSpec((d, CAST_BLOCK), lambda s: (0, jnp.minimum(s, n_a - 1))),
                   pl.BlockSpec((d, HEAD_DIM), lambda s: (0, 0)),
                   pl.BlockSpec((d, CAST_BLOCK), lambda s: (0, jnp.clip(s - n_a, 0, n_b - 1)))],
        out_shape=[jax.ShapeDtypeStruct((d, nsa_cols), BF16), jax.ShapeDtypeStruct((d, HEAD_DIM), BF16),
                   jax.ShapeDtypeStruct((d, diff_cols), BF16)],
        compiler_params=_params(("arbitrary",), est),
        name="cast_w_in",
    )(w.T, w.T)


def _rope_tables():
    inv = 1.0 / (ROPE_THETA ** (jnp.arange(0, ROPE_DIM, 2, dtype=F32) / ROPE_DIM))
    ang = jnp.arange(SEQ, dtype=F32)[:, None] * inv[None, :]
    cos, sin = jnp.cos(ang), jnp.sin(ang)
    rest = HEAD_DIM - ROPE_DIM
    cos_t = jnp.concatenate([cos, cos, jnp.ones((SEQ, rest), F32)], axis=1)
    sin_t = jnp.concatenate([-sin, sin, jnp.zeros((SEQ, rest), F32)], axis=1)
    return cos_t, sin_t


def _overlap_t():
    cmp_starts = np.arange(N_CMP) * CMP_STRIDE
    sel_starts = np.arange(N_SEL) * SLC_LEN
    ov = np.clip(np.minimum(cmp_starts[:, None] + CMP_LEN, sel_starts[None, :] + SLC_LEN)
                 - np.maximum(cmp_starts[:, None], sel_starts[None, :]), 0, None).astype(np.float32) / CMP_LEN
    ovt = np.zeros((N_SEL, N_CMP_PAD), np.float32)
    ovt[:, :N_CMP] = ov.T
    return jnp.asarray(ovt, BF16)


def kernel(x, p, attn_norm, w_in, cmp_k_pos, cmp_k_w1, cmp_k_w2, cmp_v_pos, cmp_v_w1, cmp_v_w2, nsa_out_norm, diff_lq1, diff_lk1, diff_lq2, diff_lk2, diff_subln, w_o, ffn_norm, w_up, conv_w, conv_b, w_down, ple_norm, w_ple_gate, w_ple_proj, final_norm):
    batch, seq, _ = x.shape
    assert seq == SEQ and p.shape[0] == 1
    t = batch * seq
    layer = 0
    lambda_init = 0.8 - 0.6 * math.exp(-0.3 * layer)
    x2 = x.reshape(t, D_MODEL)
    cos_t, sin_t = _rope_tables()

    proj, hkv, w_o_b, w_up_b, w_down_b, w_gate_b = _in_proj(
        x2, attn_norm[layer][None], *_split_w_in(w_in[layer]), cos_t, sin_t,
        later_weights=(w_o[layer], w_up[layer], w_down[layer], w_ple_gate[layer]))

    hkv = hkv.reshape(2, N_NSA_KV * t // HALF_BLOCK, HALF_BLOCK * HEAD_DIM)
    w1 = jnp.stack([cmp_k_w1[layer], cmp_v_w1[layer]]).astype(BF16)
    w2 = jnp.stack([cmp_k_w2[layer], cmp_v_w2[layer]]).astype(BF16)
    pos = jnp.stack([cmp_k_pos[layer], cmp_v_pos[layer]]).reshape(2, 1, CMP_LEN * HEAD_DIM)
    pos = jnp.broadcast_to(pos, (2, 8, CMP_LEN * HEAD_DIM)).astype(BF16)
    cmp_kv = _compress(hkv, w1, pos, w2)

    y_nsa = _nsa_attention(proj, cmp_kv, _overlap_t(), batch)
    y_diff = _diff_attention(proj, diff_lq1[layer][None], diff_lk1[layer][None], diff_lq2[layer][None],
                             diff_lk2[layer][None], diff_subln[layer][None], batch, lambda_init)
    h1 = _out_proj(y_nsa, nsa_out_norm[layer][None], y_diff, w_o_b, x2)
    h2 = _conv_ffn(h1, ffn_norm[layer][None], w_up_b, conv_w[layer], conv_b[layer][None], w_down_b)
    out = _ple_out(h2, ple_norm[layer][None], w_gate_b, p[layer].reshape(t, PLE_DIM),
                   w_ple_proj[layer].astype(BF16), final_norm[None])
    return out.reshape(batch, seq, D_MODEL)
```

```python
import functools
import math

import numpy as np
import jax
import jax.numpy as jnp
from jax import lax
from jax.experimental import pallas as pl
from jax.experimental.pallas import tpu as pltpu

D_MODEL = 2048
SEQ = 2048
HEAD_DIM = 128
ROPE_DIM = HEAD_DIM // 4
ROPE_THETA = 500000.0
N_NSA_HEADS = 8
N_NSA_KV = 2
NSA_GROUP = N_NSA_HEADS // N_NSA_KV
CMP_LEN = 32
CMP_STRIDE = 16
CMP_HIDDEN = 256
SLC_LEN = 64
SLC_TOP = 16
WINDOW = 512
N_DIFF_HEADS = 4
D_FF = 5632
CONV_W = 3
PLE_DIM = 256
EPS = 1e-6

NSA_WIDTH = N_NSA_HEADS * HEAD_DIM
NSA_KV_WIDTH = N_NSA_KV * HEAD_DIM
DIFF_WIDTH = N_DIFF_HEADS * 2 * HEAD_DIM
N_GATES = 3 * N_NSA_HEADS
N_CMP = (SEQ - CMP_LEN) // CMP_STRIDE + 1
N_CMP_PAD = SEQ // CMP_STRIDE
N_SEL = SEQ // SLC_LEN
GATE_STRIDE = 16

COL_NQ = 0
COL_DQ = COL_NQ + NSA_WIDTH
COL_DK = COL_DQ + DIFF_WIDTH
COL_DV = COL_DK + DIFF_WIDTH
COL_KS = COL_DV + DIFF_WIDTH
COL_KW = COL_KS + NSA_KV_WIDTH
COL_VS = COL_KW + NSA_KV_WIDTH
COL_VW = COL_VS + NSA_KV_WIDTH
COL_GATE = COL_VW + NSA_KV_WIDTH
PROJ_COLS = COL_GATE + HEAD_DIM

SUBLANES = 8
BF16_ROWS = 16
V7X_VMEM_REQUEST_CAP = 56 * 1024 * 1024
V7X_VMEM_REQUEST_FLOOR = 48 * 1024 * 1024
NEG = -1e30
QSCALE = HEAD_DIM ** -0.5 * math.log2(math.e)

_NT = (((1,), (1,)), ((), ()))
_TN = (((0,), (0,)), ((), ()))
BF16 = jnp.bfloat16
F32 = jnp.float32


def _params(semantics, vmem_estimate_bytes, flags=None):
    limit = min(max(int(vmem_estimate_bytes), V7X_VMEM_REQUEST_FLOOR), V7X_VMEM_REQUEST_CAP)
    return pltpu.CompilerParams(dimension_semantics=semantics, vmem_limit_bytes=limit, flags=flags)


def _rmsnorm(x, g):
    return x * lax.rsqrt(jnp.mean(x * x, axis=-1, keepdims=True) + EPS) * g


INPROJ_CHUNK = 1024
HALF_BLOCK = CMP_STRIDE

_HEADS_A = ([(COL_NQ + h * HEAD_DIM, True, True) for h in range(N_NSA_HEADS)]
            + [(("cmp", 0, h), True, False) for h in range(N_NSA_KV)]
            + [(("cmp", 1, h), False, False) for h in range(N_NSA_KV)]
            + [(COL_KS + h * HEAD_DIM, True, False) for h in range(N_NSA_KV)]
            + [(COL_VS + h * HEAD_DIM, False, False) for h in range(N_NSA_KV)]
            + [(COL_KW + h * HEAD_DIM, True, False) for h in range(N_NSA_KV)]
            + [(COL_VW + h * HEAD_DIM, False, False) for h in range(N_NSA_KV)])
_HEADS_B = ([(COL_DQ + h * HEAD_DIM, True, True) for h in range(2 * N_DIFF_HEADS)]
            + [(COL_DK + h * HEAD_DIM, True, False) for h in range(2 * N_DIFF_HEADS)]
            + [(COL_DV + h * HEAD_DIM, False, False) for h in range(2 * N_DIFF_HEADS)])
_HEADS_GATE = [(COL_GATE, False, False)]


def _inproj_kernel(*refs, n_cast):
    x_ref, g_ref, wa_ref, wg_ref, wb_ref, cos_ref, sin_ref = refs[:7]
    cast_in = refs[7:7 + n_cast]
    o_ref, hkv_ref = refs[7 + n_cast:9 + n_cast]
    cast_out = refs[9 + n_cast:9 + 2 * n_cast]
    stage_ref = refs[9 + 2 * n_cast]
    for src_ref, dst_ref in zip(cast_in, cast_out):
        dst_ref[...] = src_ref[...].astype(dst_ref.dtype)
    tm = x_ref.shape[0]
    xn = _rmsnorm(x_ref[...], g_ref[...]).astype(BF16)
    c = cos_ref[...]
    s = sin_ref[...]
    cq = c * QSCALE
    sq = s * QSCALE
    first_half = lax.broadcasted_iota(jnp.int32, c.shape, 1) < ROPE_DIM // 2
    n_staged = 0
    for w_ref, heads in ((wa_ref, _HEADS_A), (wg_ref, _HEADS_GATE), (wb_ref, _HEADS_B)):
        for lo in range(0, w_ref.shape[1], INPROJ_CHUNK):
            hi = min(lo + INPROJ_CHUNK, w_ref.shape[1])
            acc = jnp.dot(xn, w_ref[:, lo:hi], preferred_element_type=F32)
            for src in range(lo, hi, HEAD_DIM):
                dest, rope, is_query = heads[src // HEAD_DIM]
                a = acc[:, src - lo:src - lo + HEAD_DIM]
                if rope:
                    partner = jnp.where(first_half,
                                        pltpu.roll(a, HEAD_DIM - ROPE_DIM // 2, 1),
                                        pltpu.roll(a, ROPE_DIM // 2, 1))
                    a = a * cq + partner * sq if is_query else a * c + partner * s
                if isinstance(dest, tuple):
                    _, which, head = dest
                    stage = stage_ref.at[n_staged]
                    n_staged += 1
                    stage[...] = a
                    for r in range(HALF_BLOCK):
                        rows = stage[pl.ds(r, tm // HALF_BLOCK, stride=HALF_BLOCK), :]
                        hkv_ref[which, head, :, r * HEAD_DIM:(r + 1) * HEAD_DIM] = rows.astype(hkv_ref.dtype)
                else:
                    o_ref[:, dest:dest + HEAD_DIM] = a.astype(o_ref.dtype)


def _in_proj(x2, g, w, w_g, w_b, cos_t, sin_t, later_weights=(), *, tm=256):
    t = x2.shape[0]
    assert t % tm == 0 and SEQ % tm == 0 and tm % (HALF_BLOCK * BF16_ROWS) == 0
    n_steps = t // tm
    seq_tiles = SEQ // tm
    nsa_cols = NSA_WIDTH + 6 * NSA_KV_WIDTH
    n_w = nsa_cols + HEAD_DIM + w_b.shape[1]
    est = (D_MODEL * n_w * 2 + 2 * tm * D_MODEL * 4 + 2 * tm * n_w * 2 + tm * D_MODEL * 2
           + 3 * tm * INPROJ_CHUNK * 4)
    cast_specs = []
    for lw in later_weights:
        assert lw.shape[0] % (n_steps * BF16_ROWS) == 0
        cast_specs.append(pl.BlockSpec((lw.shape[0] // n_steps, lw.shape[1]), lambda i: (i, 0)))
        est += 2 * (lw.size // n_steps) * (4 + 2)
    resident = dict(pipeline_mode=pl.Buffered(1))
    half_cols = HALF_BLOCK * HEAD_DIM
    return pl.pallas_call(
        functools.partial(_inproj_kernel, n_cast=len(later_weights)),
        grid=(n_steps,),
        in_specs=[
            pl.BlockSpec((tm, D_MODEL), lambda i: (i, 0)),
            pl.BlockSpec((1, D_MODEL), lambda i: (0, 0)),
            pl.BlockSpec((D_MODEL, nsa_cols), lambda i: (0, 0), **resident),
            pl.BlockSpec(w_g.shape, lambda i: (0, 0), **resident),
            pl.BlockSpec(w_b.shape, lambda i: (0, 0), **resident),
            pl.BlockSpec((tm, HEAD_DIM), lambda i: (i % seq_tiles, 0)),
            pl.BlockSpec((tm, HEAD_DIM), lambda i: (i % seq_tiles, 0)),
        ] + cast_specs,
        out_specs=[
            pl.BlockSpec((tm, PROJ_COLS), lambda i: (i, 0)),
            pl.BlockSpec((2, N_NSA_KV, tm // HALF_BLOCK, half_cols), lambda i: (0, 0, i, 0)),
        ] + cast_specs,
        out_shape=[
            jax.ShapeDtypeStruct((t, PROJ_COLS), BF16),
            jax.ShapeDtypeStruct((2, N_NSA_KV, t // HALF_BLOCK, half_cols), BF16),
        ] + [jax.ShapeDtypeStruct(lw.shape, BF16) for lw in later_weights],
        scratch_shapes=[pltpu.VMEM((2 * N_NSA_KV, tm, HEAD_DIM), F32)],
        compiler_params=_params(("parallel",), est),
        name="in_proj",
    )(x2, g, w, w_g, w_b, cos_t, sin_t, *later_weights)


def _compress_kernel(h_ref, w1_ref, pos_ref, w2_ref, o_ref):
    half = CMP_LEN * HEAD_DIM // 2
    h = h_ref[0]
    top = jnp.dot(h, w1_ref[0, :half, :], preferred_element_type=F32)
    bot = jnp.dot(h, w1_ref[0, half:, :], preferred_element_type=F32)
    pos_bias = jnp.dot(pos_ref[0], w1_ref[0], preferred_element_type=F32)[0:1]
    pre = top + pltpu.roll(bot, bot.shape[0] - 1, 0) + pos_bias
    act = jax.nn.gelu(pre)
    o_ref[0] = jnp.dot(act.astype(BF16), w2_ref[0], preferred_element_type=F32).astype(o_ref.dtype)


def _compress(hkv, w1, pos, w2):
    rows = hkv.shape[1]
    kdim = CMP_LEN * HEAD_DIM
    est = 2 * (rows * kdim // 2 * 2 + kdim * CMP_HIDDEN * 2) + 6 * rows * CMP_HIDDEN * 4
    return pl.pallas_call(
        _compress_kernel,
        grid=(2,),
        in_specs=[
            pl.BlockSpec((1, rows, kdim // 2), lambda i: (i, 0, 0)),
            pl.BlockSpec((1, kdim, CMP_HIDDEN), lambda i: (i, 0, 0)),
            pl.BlockSpec((1, 8, kdim), lambda i: (i, 0, 0)),
            pl.BlockSpec((1, CMP_HIDDEN, HEAD_DIM), lambda i: (i, 0, 0)),
        ],
        out_specs=pl.BlockSpec((1, rows, HEAD_DIM), lambda i: (i, 0, 0)),
        out_shape=jax.ShapeDtypeStruct((2, rows, HEAD_DIM), BF16),
        compiler_params=_params(("parallel",), est),
        name="compress",
    )(hkv, w1, pos, w2)


def _causal_flash(chains, n_before, diag_mask, d, lanes):
    def tile_step(kt, carries, mask=None):
        stats = []
        probs = []
        for (score_fn, _), (m, l, _) in zip(chains, carries):
            s = score_fn(kt)
            if mask is not None:
                s = jnp.where(mask, s, NEG)
            m_new = jnp.maximum(m, jnp.max(s, axis=0, keepdims=True))
            alpha = jnp.exp2(m - m_new)
            p = jnp.exp2(s - m_new)
            stats.append((m_new, alpha * l + jnp.sum(p, axis=0, keepdims=True), alpha))
            probs.append(p.astype(BF16))
        out = []
        for (_, value_fn), (_, _, acc), (m_new, l_new, alpha), p in zip(chains, carries, stats, probs):
            pv = lax.dot_general(value_fn(kt), p, _TN, preferred_element_type=F32)
            out.append((m_new, l_new, alpha * acc + pv))
        return tuple(out)

    init = (jnp.full((1, lanes), NEG, F32), jnp.zeros((1, lanes), F32), jnp.zeros((d, lanes), F32))
    carries = lax.fori_loop(0, n_before, tile_step, (init,) * len(chains))
    return [acc * (1.0 / l) for _, l, acc in tile_step(n_before, carries, diag_mask)]


def _nsa_kernel(q_ref, kc0_ref, vc0_ref, kc1_ref, vc1_ref, ks_ref, vs_ref, kw_ref, vw_ref, gate_ref, ovt_ref,
                o_ref, *, tq):
    qi = pl.program_id(1)
    start = qi * tq
    nl = NSA_GROUP * tq
    kvs = range(N_NSA_KV)
    cmp_refs = ((kc0_ref, vc0_ref), (kc1_ref, vc1_ref))

    def head_cols(kv):
        return slice(kv * HEAD_DIM, (kv + 1) * HEAD_DIM)

    q_all = q_ref[...]
    qs = [jnp.concatenate([q_all[:, (kv * NSA_GROUP + g) * HEAD_DIM:(kv * NSA_GROUP + g + 1) * HEAD_DIM]
                           for g in range(NSA_GROUP)], axis=0) for kv in kvs]
    q_local = lax.broadcasted_iota(jnp.int32, (1, nl), 1) & (tq - 1)
    t_lane = start + q_local
    k_local = lax.broadcasted_iota(jnp.int32, (tq, nl), 0)
    causal = k_local <= q_local
    c_end = lax.broadcasted_iota(jnp.int32, (N_CMP_PAD, nl), 0) * CMP_STRIDE + (CMP_LEN - 1)
    cmask = c_end <= t_lane
    j_idx = lax.broadcasted_iota(jnp.int32, (N_SEL, tq), 0)
    t_q = start + lax.broadcasted_iota(jnp.int32, (N_SEL, tq), 1)
    cur = t_q // SLC_LEN
    forced = (j_idx == 0) | (j_idx == cur) | (j_idx == cur - 1)
    ovt = ovt_ref[...]

    o_cmp = []
    qs_masked = []
    for kv in kvs:
        kc_ref, vc_ref = cmp_refs[kv]
        s = lax.dot_general(kc_ref[0], qs[kv], _NT, preferred_element_type=F32)
        s = jnp.where(cmask, s, NEG)
        m = jnp.max(s, axis=0, keepdims=True)
        e = jnp.where(cmask, jnp.exp2(s - m), 0.0)
        l = jnp.sum(e, axis=0, keepdims=True)
        p_cmp = e * jnp.where(l > 0.0, 1.0 / l, 0.0)
        o_cmp.append(lax.dot_general(vc_ref[0], p_cmp.astype(BF16), _TN, preferred_element_type=F32))

        p_sum = p_cmp[:, 0:tq]
        for g in range(1, NSA_GROUP):
            p_sum = p_sum + p_cmp[:, g * tq:(g + 1) * tq]
        p_hi = p_sum.astype(BF16)
        p_lo = (p_sum - p_hi.astype(F32)).astype(BF16)
        p_slc = (jnp.dot(ovt, p_hi, preferred_element_type=F32)
                 + jnp.dot(ovt, p_lo, preferred_element_type=F32))
        score = jnp.where(forced, 1e4, jnp.where(j_idx > cur, -1e4, p_slc))
        rank = jnp.zeros((N_SEL, tq), jnp.int32)
        for i in range(N_SEL):
            row = score[i:i + 1, :]
            tie = jnp.where(j_idx > i, 1, 0)
            rank = rank + jnp.where(row > score, 1, jnp.where(row == score, tie, 0))
        bias = jnp.where((rank < SLC_TOP) & (j_idx <= cur), 0.0, NEG)
        bias_t = jnp.concatenate([bias, jnp.zeros((HEAD_DIM - N_SEL, tq), F32)], axis=0).T.astype(BF16)
        qs_masked.append(jnp.concatenate([qs[kv], jnp.concatenate([bias_t] * NSA_GROUP, axis=0)], axis=1))

    key_block = lax.broadcasted_iota(jnp.int32, (tq, HEAD_DIM), 0) // SLC_LEN
    block_lane = lax.broadcasted_iota(jnp.int32, (tq, HEAD_DIM), 1)

    def slc_chain(kv):
        def scores(kt):
            k0 = pl.multiple_of(kt * tq, tq)
            block_onehot = jnp.where(kt * (tq // SLC_LEN) + key_block == block_lane, 1.0, 0.0).astype(BF16)
            keys = jnp.concatenate([ks_ref[pl.ds(k0, tq), head_cols(kv)], block_onehot], axis=1)
            return lax.dot_general(keys, qs_masked[kv], _NT, preferred_element_type=F32)

        def values(kt):
            return vs_ref[pl.ds(pl.multiple_of(kt * tq, tq), tq), head_cols(kv)]

        return scores, values

    o_slc = _causal_flash([slc_chain(kv) for kv in kvs], qi, causal, HEAD_DIM, nl)

    win_p, win_l, win_v = [], [], []
    for kv in kvs:
        scores, values = [], []
        for back in range(WINDOW // tq, -1, -1):
            k0 = start - back * tq
            inside = k0 >= 0
            k0 = pl.multiple_of(jnp.maximum(k0, 0), tq)
            s = lax.dot_general(kw_ref[pl.ds(k0, tq), head_cols(kv)], qs[kv], _NT, preferred_element_type=F32)
            if back == 0:
                s = jnp.where(causal, s, NEG)
            elif back == WINDOW // tq:
                edge = q_local + jnp.where(inside, 0, tq)
                s = jnp.where(k_local > edge, s, NEG)
            else:
                s = s + jnp.where(inside, 0.0, NEG)
            scores.append(s)
            values.append(vw_ref[pl.ds(k0, tq), head_cols(kv)])
        s = jnp.concatenate(scores, axis=0)
        p = jnp.exp2(s - jnp.max(s, axis=0, keepdims=True))
        win_l.append(jnp.sum(p, axis=0, keepdims=True))
        win_p.append(p.astype(BF16))
        win_v.append(jnp.concatenate(values, axis=0))

    gates_t = jax.nn.sigmoid(gate_ref[...].astype(F32)).T
    for kv in kvs:
        acc_w = lax.dot_general(win_v[kv], win_p[kv], _TN, preferred_element_type=F32)
        o_win = acc_w * (1.0 / win_l[kv])

        gts = gates_t[kv * GATE_STRIDE:(kv + 1) * GATE_STRIDE]
        for g in range(NSA_GROUP):
            sl = slice(g * tq, (g + 1) * tq)
            o_t = (gts[g:g + 1, :] * o_cmp[kv][:, sl]
                   + gts[NSA_GROUP + g:NSA_GROUP + g + 1, :] * o_slc[kv][:, sl]
                   + gts[2 * NSA_GROUP + g:2 * NSA_GROUP + g + 1, :] * o_win[:, sl])
            head = kv * NSA_GROUP + g
            o_ref[:, head * HEAD_DIM:(head + 1) * HEAD_DIM] = o_t.T


def _nsa_attention(proj, cmp_kv, ovt, batch, *, tq=256):
    t = proj.shape[0]
    assert WINDOW % tq == 0 and tq % SLC_LEN == 0 and SEQ % tq == 0
    nq = SEQ // tq
    est = (4 * 2 * SEQ * NSA_KV_WIDTH * 2 + 2 * tq * NSA_WIDTH * (2 + 4)
           + N_NSA_KV * 10 * (WINDOW + tq) * NSA_GROUP * tq * 4)

    def kv_spec(col):
        return pl.BlockSpec((SEQ, NSA_KV_WIDTH), lambda b, i: (b, col // NSA_KV_WIDTH))

    def cmp_spec(which, kv):
        return pl.BlockSpec((1, N_CMP_PAD, HEAD_DIM), lambda b, i: (which, kv * batch + b, 0))

    return pl.pallas_call(
        functools.partial(_nsa_kernel, tq=tq),
        grid=(batch, nq),
        in_specs=[
            pl.BlockSpec((tq, NSA_WIDTH), lambda b, i: (b * nq + i, 0)),
            cmp_spec(0, 0), cmp_spec(1, 0), cmp_spec(0, 1), cmp_spec(1, 1),
            kv_spec(COL_KS), kv_spec(COL_VS), kv_spec(COL_KW), kv_spec(COL_VW),
            pl.BlockSpec((tq, HEAD_DIM), lambda b, i: (b * nq + i, COL_GATE // HEAD_DIM)),
            pl.BlockSpec((N_SEL, N_CMP_PAD), lambda b, i: (0, 0)),
        ],
        out_specs=pl.BlockSpec((tq, NSA_WIDTH), lambda b, i: (b * nq + i, 0)),
        out_shape=jax.ShapeDtypeStruct((t, NSA_WIDTH), F32),
        compiler_params=_params(("parallel", "arbitrary"), est),
        name="nsa_attn",
    )(proj, cmp_kv, cmp_kv, cmp_kv, cmp_kv, proj, proj, proj, proj, proj, ovt)


def _diff_kernel(q_ref, k_ref, v_ref, lq1_ref, lk1_ref, lq2_ref, lk2_ref, sub_ref, o_ref,
                 *, tq, lambda_init):
    qi = pl.program_id(2)
    wide = 2 * HEAD_DIM
    lam = (jnp.exp(jnp.sum(lq1_ref[...] * lk1_ref[...], axis=-1, keepdims=True))
           - jnp.exp(jnp.sum(lq2_ref[...] * lk2_ref[...], axis=-1, keepdims=True)) + lambda_init)
    q = q_ref[...]

    def chain(h):
        q1 = q[:, h * wide:h * wide + HEAD_DIM]
        q2 = q[:, h * wide + HEAD_DIM:(h + 1) * wide]

        def scores(kt):
            kk = k_ref[pl.ds(pl.multiple_of(kt * tq, tq), tq), h * wide:(h + 1) * wide]
            s1 = lax.dot_general(kk[:, :HEAD_DIM], q1, _NT, preferred_element_type=F32)
            s2 = lax.dot_general(kk[:, HEAD_DIM:], q2, _NT, preferred_element_type=F32)
            return jnp.concatenate([s1, s2], axis=1)

        def values(kt):
            return v_ref[pl.ds(pl.multiple_of(kt * tq, tq), tq), h * wide:(h + 1) * wide]

        return scores, values

    k_local = lax.broadcasted_iota(jnp.int32, (tq, 2 * tq), 0)
    q_local = lax.broadcasted_iota(jnp.int32, (1, 2 * tq), 1) & (tq - 1)
    outs = _causal_flash([chain(h) for h in range(DIFF_HEADS_PER_STEP)], qi, k_local <= q_local, wide, 2 * tq)
    for h, o_n in enumerate(outs):
        o_t = o_n[:, :tq] - lam * o_n[:, tq:]
        o = _rmsnorm(o_t.T, sub_ref[...]) * (1.0 - lambda_init)
        o_ref[:, h * wide:(h + 1) * wide] = o.astype(o_ref.dtype)


DIFF_HEADS_PER_STEP = 4


def _diff_attention(proj, lq1, lk1, lq2, lk2, subln, batch, lambda_init, *, tq=512):
    t = proj.shape[0]
    nq = SEQ // tq
    wide = 2 * HEAD_DIM
    step_cols = DIFF_HEADS_PER_STEP * wide
    assert COL_DQ % step_cols == 0 and COL_DK % step_cols == 0 and COL_DV % step_cols == 0
    est = 2 * 2 * SEQ * step_cols * 2 + 4 * tq * step_cols * 4 + DIFF_HEADS_PER_STEP * 16 * tq * 2 * tq * 4
    vec = pl.BlockSpec((1, HEAD_DIM), lambda b, h, i: (0, 0))
    return pl.pallas_call(
        functools.partial(_diff_kernel, tq=tq, lambda_init=lambda_init),
        grid=(batch, N_DIFF_HEADS // DIFF_HEADS_PER_STEP, nq),
        in_specs=[
            pl.BlockSpec((tq, step_cols), lambda b, h, i: (b * nq + i, COL_DQ // step_cols + h)),
            pl.BlockSpec((SEQ, step_cols), lambda b, h, i: (b, COL_DK // step_cols + h)),
            pl.BlockSpec((SEQ, step_cols), lambda b, h, i: (b, COL_DV // step_cols + h)),
            vec, vec, vec, vec,
            pl.BlockSpec((1, wide), lambda b, h, i: (0, 0)),
        ],
        out_specs=pl.BlockSpec((tq, step_cols), lambda b, h, i: (b * nq + i, h)),
        out_shape=jax.ShapeDtypeStruct((t, DIFF_WIDTH), BF16),
        compiler_params=_params(("parallel", "parallel", "arbitrary"), est),
        name="diff_attn",
    )(proj, proj, proj, lq1, lk1, lq2, lk2, subln)


def _oproj_kernel(yn_ref, g_ref, yd_ref, wo_ref, x_ref, o_ref):
    yn = _rmsnorm(yn_ref[...], g_ref[...]).astype(BF16)
    acc = jnp.dot(yn, wo_ref[:NSA_WIDTH, :], preferred_element_type=F32)
    acc = acc + jnp.dot(yd_ref[...], wo_ref[NSA_WIDTH:, :], preferred_element_type=F32)
    o_ref[...] = x_ref[...] + acc


def _out_proj(y_nsa, g, y_diff, w_o, x2, *, tm=512):
    t = x2.shape[0]
    est = (2 * (NSA_WIDTH + DIFF_WIDTH) * D_MODEL * 2 + 2 * tm * NSA_WIDTH * 4 + 2 * tm * DIFF_WIDTH * 2
           + 5 * tm * D_MODEL * 4)
    return pl.pallas_call(
        _oproj_kernel,
        grid=(t // tm,),
        in_specs=[
            pl.BlockSpec((tm, NSA_WIDTH), lambda i: (i, 0)),
            pl.BlockSpec((1, NSA_WIDTH), lambda i: (0, 0)),
            pl.BlockSpec((tm, DIFF_WIDTH), lambda i: (i, 0)),
            pl.BlockSpec((NSA_WIDTH + DIFF_WIDTH, D_MODEL), lambda i: (0, 0)),
            pl.BlockSpec((tm, D_MODEL), lambda i: (i, 0)),
        ],
        out_specs=pl.BlockSpec((tm, D_MODEL), lambda i: (i, 0)),
        out_shape=jax.ShapeDtypeStruct((t, D_MODEL), F32),
        compiler_params=_params(("parallel",), est),
        name="out_proj",
    )(y_nsa, g, y_diff, w_o, x2)


HALO = 8
FFN_CHUNK = 256
FFN_ROW_BLOCKS = 4


def _ffn_kernel(h_ref, halo_ref, g_ref, wu_ref, wg_ref, cwu_ref, cwg_ref, cbu_ref, cbg_ref, wd_ref,
                o_ref, xn_ref, raw_ref, act_ref, *, tm):
    i = pl.program_id(0)
    j = pl.program_id(1)

    @pl.when(j == 0)
    def _():
        keep = jnp.where((i * tm) % SEQ == 0, 0.0, 1.0)
        h = h_ref[...]
        xn_ref[0:HALO, :] = (_rmsnorm(halo_ref[...], g_ref[...]) * keep).astype(BF16)
        xn_ref[HALO:, :] = _rmsnorm(h, g_ref[...]).astype(BF16)
        o_ref[...] = h

    xn = xn_ref[...]

    def conv(h, cw_ref, cb_ref, cols):
        cw = cw_ref[:, cols]
        h3 = h.reshape(h.shape[0] // SUBLANES, SUBLANES, h.shape[1])
        sub = lax.broadcasted_iota(jnp.int32, h3.shape[1:], 0)
        out = cw[CONV_W - 1:CONV_W, :] * h3[1:] + cb_ref[:, cols]
        for back in range(1, CONV_W):
            rolled = pltpu.roll(h3, back, 1)
            shifted = jnp.where(sub < back, rolled[:-1], rolled[1:])
            out = out + cw[CONV_W - 1 - back:CONV_W - back, :] * shifted
        return out.reshape(h.shape[0] - HALO, h.shape[1])

    def gated(rows, cols):
        src = slice(rows.start, rows.stop + HALO)
        u = conv(raw_ref[slot, 0, src, cols], cwu_ref, cbu_ref, cols)
        gate = conv(raw_ref[slot, 1, src, cols], cwg_ref, cbg_ref, cols)
        half_gate = 0.5 * gate
        act_ref[rows, cols] = (half_gate * (1.0 + jnp.tanh(half_gate)) * u).astype(BF16)

    slot = j % 2
    chunks = [slice(lo, lo + FFN_CHUNK) for lo in range(0, wu_ref.shape[1], FFN_CHUNK)]
    row_blocks = [slice(lo, lo + tm // FFN_ROW_BLOCKS) for lo in range(0, tm, tm // FFN_ROW_BLOCKS)]
    for cols in chunks:
        raw_ref[slot, 0, :, cols] = jnp.dot(xn, wu_ref[:, cols], preferred_element_type=F32)
        raw_ref[slot, 1, :, cols] = jnp.dot(xn, wg_ref[:, cols], preferred_element_type=F32)
        if cols is not chunks[-1]:
            gated(slice(0, tm), cols)
    gated(row_blocks[0], chunks[-1])
    for n, rows in enumerate(row_blocks):
        if n + 1 < len(row_blocks):
            gated(row_blocks[n + 1], chunks[-1])
        o_ref[rows, :] += jnp.dot(act_ref[rows, :], wd_ref[...], preferred_element_type=F32)


def _conv_ffn(h1, g, w_up, conv_w, conv_b, w_down, *, tm=1024, tf=512):
    t = h1.shape[0]
    assert t % tm == 0 and SEQ % tm == 0 and D_FF % tf == 0
    nf = D_FF // tf
    est = (4 * tm * D_MODEL * 4 + 2 * 3 * D_MODEL * tf * 2 + (tm + HALO) * D_MODEL * 2
           + 8 * (tm + HALO) * tf * 4)
    return pl.pallas_call(
        functools.partial(_ffn_kernel, tm=tm),
        grid=(t // tm, nf),
        in_specs=[
            pl.BlockSpec((tm, D_MODEL), lambda i, j: (i, 0)),
            pl.BlockSpec((HALO, D_MODEL), lambda i, j: (jnp.maximum(i * (tm // HALO) - 1, 0), 0)),
            pl.BlockSpec((1, D_MODEL), lambda i, j: (0, 0)),
            pl.BlockSpec((D_MODEL, tf), lambda i, j: (0, j)),
            pl.BlockSpec((D_MODEL, tf), lambda i, j: (0, nf + j)),
            pl.BlockSpec((CONV_W, tf), lambda i, j: (0, j)),
            pl.BlockSpec((CONV_W, tf), lambda i, j: (0, nf + j)),
            pl.BlockSpec((1, tf), lambda i, j: (0, j)),
            pl.BlockSpec((1, tf), lambda i, j: (0, nf + j)),
            pl.BlockSpec((tf, D_MODEL), lambda i, j: (j, 0)),
        ],
        out_specs=pl.BlockSpec((tm, D_MODEL), lambda i, j: (i, 0), pipeline_mode=pl.Buffered(1)),
        out_shape=jax.ShapeDtypeStruct((t, D_MODEL), F32),
        scratch_shapes=[pltpu.VMEM((tm + HALO, D_MODEL), BF16), pltpu.VMEM((2, 2, tm + HALO, tf), F32),
                        pltpu.VMEM((tm, tf), BF16)],
        compiler_params=_params(("parallel", "arbitrary"), est),
        name="conv_ffn",
    )(h1, h1, g, w_up, w_up, conv_w, conv_w, conv_b, conv_b, w_down)


PLE_CHUNK = 512
PLE_ROWS = 256


def _ple_kernel(h_ref, gp_ref, wg_ref, p_ref, wp_ref, gf_ref, o_ref):
    for lo_r in range(0, h_ref.shape[0], PLE_ROWS):
        rows = slice(lo_r, lo_r + PLE_ROWS)
        hn = _rmsnorm(h_ref[rows, :], gp_ref[...]).astype(BF16)
        pb = p_ref[rows, :].astype(BF16)
        ssq = None
        for lo in range(0, D_MODEL, PLE_CHUNK):
            cols = slice(lo, lo + PLE_CHUNK)
            gate = jax.nn.sigmoid(jnp.dot(hn, wg_ref[:, cols], preferred_element_type=F32))
            emb = jnp.dot(pb, wp_ref[:, cols], preferred_element_type=F32)
            h3 = h_ref[rows, cols] + gate * emb
            o_ref[rows, cols] = h3
            part = jnp.sum(h3 * h3, axis=-1, keepdims=True)
            ssq = part if ssq is None else ssq + part
        o_ref[rows, :] = o_ref[rows, :] * lax.rsqrt(ssq * (1.0 / D_MODEL) + EPS) * gf_ref[...]


def _ple_out(h2, g_ple, w_gate, p2, w_proj, g_final, *, tm=512):
    t = h2.shape[0]
    est = D_MODEL * D_MODEL * 2 + 2 * PLE_DIM * D_MODEL * 2 + 4 * tm * D_MODEL * 4 + 6 * tm * D_MODEL * 4
    return pl.pallas_call(
        _ple_kernel,
        grid=(t // tm,),
        in_specs=[
            pl.BlockSpec((tm, D_MODEL), lambda i: (i, 0)),
            pl.BlockSpec((1, D_MODEL), lambda i: (0, 0)),
            pl.BlockSpec((D_MODEL, D_MODEL), lambda i: (0, 0), pipeline_mode=pl.Buffered(1)),
            pl.BlockSpec((tm, PLE_DIM), lambda i: (i, 0)),
            pl.BlockSpec((PLE_DIM, D_MODEL), lambda i: (0, 0)),
            pl.BlockSpec((1, D_MODEL), lambda i: (0, 0)),
        ],
        out_specs=pl.BlockSpec((tm, D_MODEL), lambda i: (i, 0)),
        out_shape=jax.ShapeDtypeStruct((t, D_MODEL), F32),
        compiler_params=_params(("parallel",), est),
        name="ple_out",
    )(h2, g_ple, w_gate, p2, w_proj, g_final)


def _split_w_in(w):
    w_a, gate, w_b = _cast_w_in(w)
    gate = gate[:, :N_GATES].reshape(D_MODEL, 3, N_NSA_KV, NSA_GROUP).transpose(0, 2, 1, 3)
    gate = jnp.pad(gate.reshape(D_MODEL, N_NSA_KV, 3 * NSA_GROUP), ((0, 0), (0, 0), (0, GATE_STRIDE - 3 * NSA_GROUP)))
    gate = jnp.pad(gate.reshape(D_MODEL, N_NSA_KV * GATE_STRIDE), ((0, 0), (0, HEAD_DIM - N_NSA_KV * GATE_STRIDE)))
    return w_a, gate, w_b


CAST_BLOCK = 512


def _cast_w_in_kernel(cur_ref, nxt_ref, wa_ref, wg_ref, wb_ref, *, n_a):
    s = pl.program_id(0)

    @pl.when(s < n_a)
    def _():
        wa_ref[...] = cur_ref[...].T.astype(BF16)

    @pl.when((s >= n_a) & (s < pl.num_programs(0) - 1))
    def _():
        rows = jnp.concatenate([cur_ref[N_GATES:, :], nxt_ref[0:N_GATES, :]], axis=0)
        wb_ref[...] = rows.T.astype(BF16)

    @pl.when(s == pl.num_programs(0) - 1)
    def _():
        wg_ref[...] = cur_ref[0:HEAD_DIM, :].T.astype(BF16)


def _cast_w_in(w):
    d, n = w.shape
    nsa_cols = NSA_WIDTH + 6 * NSA_KV_WIDTH
    diff_cols = n - nsa_cols - N_GATES
    assert nsa_cols % CAST_BLOCK == 0 and diff_cols % CAST_BLOCK == 0 and N_GATES % SUBLANES == 0
    n_a, n_b = nsa_cols // CAST_BLOCK, diff_cols // CAST_BLOCK
    spill_rows = 32
    assert N_GATES <= spill_rows and CAST_BLOCK % spill_rows == 0
    est = 2 * CAST_BLOCK * d * 4 + 2 * 2 * CAST_BLOCK * d * 2 + 6 * CAST_BLOCK * d * 4

    def in_block(s):
        return jnp.where(s == n_a + n_b, n_a, s)

    return pl.pallas_call(
        functools.partial(_cast_w_in_kernel, n_a=n_a),
        grid=(n_a + n_b + 1,),
        in_specs=[pl.BlockSpec((CAST_BLOCK, d), lambda s: (in_block(s), 0)),
                  pl.BlockSpec((spill_rows, d),
                               lambda s: (jnp.clip(s + 1, n_a + 1, n_a + n_b) * (CAST_BLOCK // spill_rows), 0))],
        out_specs=[pl.BlockSpec((d, CAST_BLOCK), lambda s: (0, jnp.minimum(s, n_a - 1))),
                   pl.BlockSpec((d, HEAD_DIM), lambda s: (0, 0)),
                   pl.BlockSpec((d, CAST_BLOCK), lambda s: (0, jnp.clip(s - n_a, 0, n_b - 1)))],
        out_shape=[jax.ShapeDtypeStruct((d, nsa_cols), BF16), jax.ShapeDtypeStruct((d, HEAD_DIM), BF16),
                   jax.ShapeDtypeStruct((d, diff_cols), BF16)],
        compiler_params=_params(("arbitrary",), est),
        name="cast_w_in",
    )(w.T, w.T)


def _rope_tables():
    inv = 1.0 / (ROPE_THETA ** (jnp.arange(0, ROPE_DIM, 2, dtype=F32) / ROPE_DIM))
    ang = jnp.arange(SEQ, dtype=F32)[:, None] * inv[None, :]
    cos, sin = jnp.cos(ang), jnp.sin(ang)
    rest = HEAD_DIM - ROPE_DIM
    cos_t = jnp.concatenate([cos, cos, jnp.ones((SEQ, rest), F32)], axis=1)
    sin_t = jnp.concatenate([-sin, sin, jnp.zeros((SEQ, rest), F32)], axis=1)
    return cos_t, sin_t


def _overlap_t():
    cmp_starts = np.arange(N_CMP) * CMP_STRIDE
    sel_starts = np.arange(N_SEL) * SLC_LEN
    ov = np.clip(np.minimum(cmp_starts[:, None] + CMP_LEN, sel_starts[None, :] + SLC_LEN)
                 - np.maximum(cmp_starts[:, None], sel_starts[None, :]), 0, None).astype(np.float32) / CMP_LEN
    ovt = np.zeros((N_SEL, N_CMP_PAD), np.float32)
    ovt[:, :N_CMP] = ov.T
    return jnp.asarray(ovt, BF16)


def kernel(x, p, attn_norm, w_in, cmp_k_pos, cmp_k_w1, cmp_k_w2, cmp_v_pos, cmp_v_w1, cmp_v_w2, nsa_out_norm, diff_lq1, diff_lk1, diff_lq2, diff_lk2, diff_subln, w_o, ffn_norm, w_up, conv_w, conv_b, w_down, ple_norm, w_ple_gate, w_ple_proj, final_norm):
    batch, seq, _ = x.shape
    assert seq == SEQ and p.shape[0] == 1
    t = batch * seq
    layer = 0
    lambda_init = 0.8 - 0.6 * math.exp(-0.3 * layer)
    x2 = x.reshape(t, D_MODEL)
    cos_t, sin_t = _rope_tables()

    proj, hkv, w_o_b, w_up_b, w_down_b, w_gate_b = _in_proj(
        x2, attn_norm[layer][None], *_split_w_in(w_in[layer]), cos_t, sin_t,
        later_weights=(w_o[layer], w_up[layer], w_down[layer], w_ple_gate[layer]))

    hkv = hkv.reshape(2, N_NSA_KV * t // HALF_BLOCK, HALF_BLOCK * HEAD_DIM)
    w1 = jnp.stack([cmp_k_w1[layer], cmp_v_w1[layer]]).astype(BF16)
    w2 = jnp.stack([cmp_k_w2[layer], cmp_v_w2[layer]]).astype(BF16)
    pos = jnp.stack([cmp_k_pos[layer], cmp_v_pos[layer]]).reshape(2, 1, CMP_LEN * HEAD_DIM)
    pos = jnp.broadcast_to(pos, (2, 8, CMP_LEN * HEAD_DIM)).astype(BF16)
    cmp_kv = _compress(hkv, w1, pos, w2)

    y_nsa = _nsa_attention(proj, cmp_kv, _overlap_t(), batch)
    y_diff = _diff_attention(proj, diff_lq1[layer][None], diff_lk1[layer][None], diff_lq2[layer][None],
                             diff_lk2[layer][None], diff_subln[layer][None], batch, lambda_init)
    h1 = _out_proj(y_nsa, nsa_out_norm[layer][None], y_diff, w_o_b, x2)
    h2 = _conv_ffn(h1, ffn_norm[layer][None], w_up_b, conv_w[layer], conv_b[layer][None], w_down_b)
    out = _ple_out(h2, ple_norm[layer][None], w_gate_b, p[layer].reshape(t, PLE_DIM),
                   w_ple_proj[layer].astype(BF16), final_norm[None])
    return out.reshape(batch, seq, D_MODEL)
```

```python
import functools
import math

import numpy as np
import jax
import jax.numpy as jnp
from jax import lax
from jax.experimental import pallas as pl
from jax.experimental.pallas import tpu as pltpu

D_MODEL = 2048
SEQ = 2048
HEAD_DIM = 128
ROPE_DIM = HEAD_DIM // 4
ROPE_THETA = 500000.0
N_NSA_HEADS = 8
N_NSA_KV = 2
NSA_GROUP = N_NSA_HEADS // N_NSA_KV
CMP_LEN = 32
CMP_STRIDE = 16
CMP_HIDDEN = 256
SLC_LEN = 64
SLC_TOP = 16
WINDOW = 512
N_DIFF_HEADS = 4
D_FF = 5632
CONV_W = 3
PLE_DIM = 256
EPS = 1e-6

NSA_WIDTH = N_NSA_HEADS * HEAD_DIM
NSA_KV_WIDTH = N_NSA_KV * HEAD_DIM
DIFF_WIDTH = N_DIFF_HEADS * 2 * HEAD_DIM
N_GATES = 3 * N_NSA_HEADS
N_CMP = (SEQ - CMP_LEN) // CMP_STRIDE + 1
N_CMP_PAD = SEQ // CMP_STRIDE
N_SEL = SEQ // SLC_LEN
GATE_STRIDE = 16

COL_NQ = 0
COL_DQ = COL_NQ + NSA_WIDTH
COL_DK = COL_DQ + DIFF_WIDTH
COL_DV = COL_DK + DIFF_WIDTH
COL_KS = COL_DV + DIFF_WIDTH
COL_KW = COL_KS + NSA_KV_WIDTH
COL_VS = COL_KW + NSA_KV_WIDTH
COL_VW = COL_VS + NSA_KV_WIDTH
COL_GATE = COL_VW + NSA_KV_WIDTH
PROJ_COLS = COL_GATE + HEAD_DIM

SUBLANES = 8
BF16_ROWS = 16
V7X_VMEM_REQUEST_CAP = 56 * 1024 * 1024
V7X_VMEM_REQUEST_FLOOR = 48 * 1024 * 1024
NEG = -1e30
QSCALE = HEAD_DIM ** -0.5 * math.log2(math.e)

_NT = (((1,), (1,)), ((), ()))
_TN = (((0,), (0,)), ((), ()))
BF16 = jnp.bfloat16
F32 = jnp.float32


def _params(semantics, vmem_estimate_bytes, flags=None):
    limit = min(max(int(vmem_estimate_bytes), V7X_VMEM_REQUEST_FLOOR), V7X_VMEM_REQUEST_CAP)
    return pltpu.CompilerParams(dimension_semantics=semantics, vmem_limit_bytes=limit, flags=flags)


def _rmsnorm(x, g):
    return x * lax.rsqrt(jnp.mean(x * x, axis=-1, keepdims=True) + EPS) * g


INPROJ_CHUNK = 1024
HALF_BLOCK = CMP_STRIDE

_HEADS_A = ([(COL_NQ + h * HEAD_DIM, True, True) for h in range(N_NSA_HEADS)]
            + [(("cmp", 0, h), True, False) for h in range(N_NSA_KV)]
            + [(("cmp", 1, h), False, False) for h in range(N_NSA_KV)]
            + [(COL_KS + h * HEAD_DIM, True, False) for h in range(N_NSA_KV)]
            + [(COL_VS + h * HEAD_DIM, False, False) for h in range(N_NSA_KV)]
            + [(COL_KW + h * HEAD_DIM, True, False) for h in range(N_NSA_KV)]
            + [(COL_VW + h * HEAD_DIM, False, False) for h in range(N_NSA_KV)])
_HEADS_B = ([(COL_DQ + h * HEAD_DIM, True, True) for h in range(2 * N_DIFF_HEADS)]
            + [(COL_DK + h * HEAD_DIM, True, False) for h in range(2 * N_DIFF_HEADS)]
            + [(COL_DV + h * HEAD_DIM, False, False) for h in range(2 * N_DIFF_HEADS)])
_HEADS_GATE = [(COL_GATE, False, False)]


def _inproj_kernel(*refs, n_cast):
    x_ref, g_ref, wa_ref, wg_ref, wb_ref, cos_ref, sin_ref = refs[:7]
    cast_in = refs[7:7 + n_cast]
    o_ref, hkv_ref = refs[7 + n_cast:9 + n_cast]
    cast_out = refs[9 + n_cast:9 + 2 * n_cast]
    stage_ref = refs[9 + 2 * n_cast]
    for src_ref, dst_ref in zip(cast_in, cast_out):
        dst_ref[...] = src_ref[...].astype(dst_ref.dtype)
    tm = x_ref.shape[0]
    xn = _rmsnorm(x_ref[...], g_ref[...]).astype(BF16)
    c = cos_ref[...]
    s = sin_ref[...]
    cq = c * QSCALE
    sq = s * QSCALE
    first_half = lax.broadcasted_iota(jnp.int32, c.shape, 1) < ROPE_DIM // 2
    n_staged = 0
    for w_ref, heads in ((wa_ref, _HEADS_A), (wg_ref, _HEADS_GATE), (wb_ref, _HEADS_B)):
        for lo in range(0, w_ref.shape[1], INPROJ_CHUNK):
            hi = min(lo + INPROJ_CHUNK, w_ref.shape[1])
            acc = jnp.dot(xn, w_ref[:, lo:hi], preferred_element_type=F32)
            for src in range(lo, hi, HEAD_DIM):
                dest, rope, is_query = heads[src // HEAD_DIM]
                a = acc[:, src - lo:src - lo + HEAD_DIM]
                if rope:
                    partner = jnp.where(first_half,
                                        pltpu.roll(a, HEAD_DIM - ROPE_DIM // 2, 1),
                                        pltpu.roll(a, ROPE_DIM // 2, 1))
                    a = a * cq + partner * sq if is_query else a * c + partner * s
                if isinstance(dest, tuple):
                    _, which, head = dest
                    stage = stage_ref.at[n_staged]
                    n_staged += 1
                    stage[...] = a
                    for r in range(HALF_BLOCK):
                        rows = stage[pl.ds(r, tm // HALF_BLOCK, stride=HALF_BLOCK), :]
                        hkv_ref[which, head, :, r * HEAD_DIM:(r + 1) * HEAD_DIM] = rows.astype(hkv_ref.dtype)
                else:
                    o_ref[:, dest:dest + HEAD_DIM] = a.astype(o_ref.dtype)


def _in_proj(x2, g, w, w_g, w_b, cos_t, sin_t, later_weights=(), *, tm=256):
    t = x2.shape[0]
    assert t % tm == 0 and SEQ % tm == 0 and tm % (HALF_BLOCK * BF16_ROWS) == 0
    n_steps = t // tm
    seq_tiles = SEQ // tm
    nsa_cols = NSA_WIDTH + 6 * NSA_KV_WIDTH
    n_w = nsa_cols + HEAD_DIM + w_b.shape[1]
    est = (D_MODEL * n_w * 2 + 2 * tm * D_MODEL * 4 + 2 * tm * n_w * 2 + tm * D_MODEL * 2
           + 3 * tm * INPROJ_CHUNK * 4)
    cast_specs = []
    for lw in later_weights:
        assert lw.shape[0] % (n_steps * BF16_ROWS) == 0
        cast_specs.append(pl.BlockSpec((lw.shape[0] // n_steps, lw.shape[1]), lambda i: (i, 0)))
        est += 2 * (lw.size // n_steps) * (4 + 2)
    resident = dict(pipeline_mode=pl.Buffered(1))
    half_cols = HALF_BLOCK * HEAD_DIM
    return pl.pallas_call(
        functools.partial(_inproj_kernel, n_cast=len(later_weights)),
        grid=(n_steps,),
        in_specs=[
            pl.BlockSpec((tm, D_MODEL), lambda i: (i, 0)),
            pl.BlockSpec((1, D_MODEL), lambda i: (0, 0)),
            pl.BlockSpec((D_MODEL, nsa_cols), lambda i: (0, 0), **resident),
            pl.BlockSpec(w_g.shape, lambda i: (0, 0), **resident),
            pl.BlockSpec(w_b.shape, lambda i: (0, 0), **resident),
            pl.BlockSpec((tm, HEAD_DIM), lambda i: (i % seq_tiles, 0)),
            pl.BlockSpec((tm, HEAD_DIM), lambda i: (i % seq_tiles, 0)),
        ] + cast_specs,
        out_specs=[
            pl.BlockSpec((tm, PROJ_COLS), lambda i: (i, 0)),
            pl.BlockSpec((2, N_NSA_KV, tm // HALF_BLOCK, half_cols), lambda i: (0, 0, i, 0)),
        ] + cast_specs,
        out_shape=[
            jax.ShapeDtypeStruct((t, PROJ_COLS), BF16),
            jax.ShapeDtypeStruct((2, N_NSA_KV, t // HALF_BLOCK, half_cols), BF16),
        ] + [jax.ShapeDtypeStruct(lw.shape, BF16) for lw in later_weights],
        scratch_shapes=[pltpu.VMEM((2 * N_NSA_KV, tm, HEAD_DIM), F32)],
        compiler_params=_params(("parallel",), est),
        name="in_proj",
    )(x2, g, w, w_g, w_b, cos_t, sin_t, *later_weights)


def _compress_kernel(h_ref, w1_ref, pos_ref, w2_ref, o_ref):
    half = CMP_LEN * HEAD_DIM // 2
    h = h_ref[0]
    top = jnp.dot(h, w1_ref[0, :half, :], preferred_element_type=F32)
    bot = jnp.dot(h, w1_ref[0, half:, :], preferred_element_type=F32)
    pos_bias = jnp.dot(pos_ref[0], w1_ref[0], preferred_element_type=F32)[0:1]
    pre = top + pltpu.roll(bot, bot.shape[0] - 1, 0) + pos_bias
    act = jax.nn.gelu(pre)
    o_ref[0] = jnp.dot(act.astype(BF16), w2_ref[0], preferred_element_type=F32).astype(o_ref.dtype)


def _compress(hkv, w1, pos, w2):
    rows = hkv.shape[1]
    kdim = CMP_LEN * HEAD_DIM
    est = 2 * (rows * kdim // 2 * 2 + kdim * CMP_HIDDEN * 2) + 6 * rows * CMP_HIDDEN * 4
    return pl.pallas_call(
        _compress_kernel,
        grid=(2,),
        in_specs=[
            pl.BlockSpec((1, rows, kdim // 2), lambda i: (i, 0, 0)),
            pl.BlockSpec((1, kdim, CMP_HIDDEN), lambda i: (i, 0, 0)),
            pl.BlockSpec((1, 8, kdim), lambda i: (i, 0, 0)),
            pl.BlockSpec((1, CMP_HIDDEN, HEAD_DIM), lambda i: (i, 0, 0)),
        ],
        out_specs=pl.BlockSpec((1, rows, HEAD_DIM), lambda i: (i, 0, 0)),
        out_shape=jax.ShapeDtypeStruct((2, rows, HEAD_DIM), BF16),
        compiler_params=_params(("parallel",), est),
        name="compress",
    )(hkv, w1, pos, w2)


def _causal_flash(chains, n_before, diag_mask, d, lanes):
    def tile_step(kt, carries, mask=None):
        stats = []
        probs = []
        for (score_fn, _), (m, l, _) in zip(chains, carries):
            s = score_fn(kt)
            if mask is not None:
                s = jnp.where(mask, s, NEG)
            m_new = jnp.maximum(m, jnp.max(s, axis=0, keepdims=True))
            alpha = jnp.exp2(m - m_new)
            p = jnp.exp2(s - m_new)
            stats.append((m_new, alpha * l + jnp.sum(p, axis=0, keepdims=True), alpha))
            probs.append(p.astype(BF16))
        out = []
        for (_, value_fn), (_, _, acc), (m_new, l_new, alpha), p in zip(chains, carries, stats, probs):
            pv = lax.dot_general(value_fn(kt), p, _TN, preferred_element_type=F32)
            out.append((m_new, l_new, alpha * acc + pv))
        return tuple(out)

    init = (jnp.full((1, lanes), NEG, F32), jnp.zeros((1, lanes), F32), jnp.zeros((d, lanes), F32))
    carries = lax.fori_loop(0, n_before, tile_step, (init,) * len(chains))
    return [acc * (1.0 / l) for _, l, acc in tile_step(n_before, carries, diag_mask)]


def _nsa_kernel(q_ref, kc0_ref, vc0_ref, kc1_ref, vc1_ref, ks_ref, vs_ref, kw_ref, vw_ref, gate_ref, ovt_ref,
                o_ref, *, tq):
    qi = pl.program_id(1)
    start = qi * tq
    nl = NSA_GROUP * tq
    kvs = range(N_NSA_KV)
    cmp_refs = ((kc0_ref, vc0_ref), (kc1_ref, vc1_ref))

    def head_cols(kv):
        return slice(kv * HEAD_DIM, (kv + 1) * HEAD_DIM)

    q_all = q_ref[...]
    qs = [jnp.concatenate([q_all[:, (kv * NSA_GROUP + g) * HEAD_DIM:(kv * NSA_GROUP + g + 1) * HEAD_DIM]
                           for g in range(NSA_GROUP)], axis=0) for kv in kvs]
    q_local = lax.broadcasted_iota(jnp.int32, (1, nl), 1) & (tq - 1)
    t_lane = start + q_local
    k_local = lax.broadcasted_iota(jnp.int32, (tq, nl), 0)
    causal = k_local <= q_local
    c_end = lax.broadcasted_iota(jnp.int32, (N_CMP_PAD, nl), 0) * CMP_STRIDE + (CMP_LEN - 1)
    cmask = c_end <= t_lane
    j_idx = lax.broadcasted_iota(jnp.int32, (N_SEL, tq), 0)
    t_q = start + lax.broadcasted_iota(jnp.int32, (N_SEL, tq), 1)
    cur = t_q // SLC_LEN
    forced = (j_idx == 0) | (j_idx == cur) | (j_idx == cur - 1)
    ovt = ovt_ref[...]

    o_cmp = []
    qs_masked = []
    for kv in kvs:
        kc_ref, vc_ref = cmp_refs[kv]
        s = lax.dot_general(kc_ref[0], qs[kv], _NT, preferred_element_type=F32)
        s = jnp.where(cmask, s, NEG)
        m = jnp.max(s, axis=0, keepdims=True)
        e = jnp.where(cmask, jnp.exp2(s - m), 0.0)
        l = jnp.sum(e, axis=0, keepdims=True)
        p_cmp = e * jnp.where(l > 0.0, 1.0 / l, 0.0)
        o_cmp.append(lax.dot_general(vc_ref[0], p_cmp.astype(BF16), _TN, preferred_element_type=F32))

        p_sum = p_cmp[:, 0:tq]
        for g in range(1, NSA_GROUP):
            p_sum = p_sum + p_cmp[:, g * tq:(g + 1) * tq]
        p_hi = p_sum.astype(BF16)
        p_lo = (p_sum - p_hi.astype(F32)).astype(BF16)
        p_slc = (jnp.dot(ovt, p_hi, preferred_element_type=F32)
                 + jnp.dot(ovt, p_lo, preferred_element_type=F32))
        score = jnp.where(forced, 1e4, jnp.where(j_idx > cur, -1e4, p_slc))
        rank = jnp.zeros((N_SEL, tq), jnp.int32)
        for i in range(N_SEL):
            row = score[i:i + 1, :]
            tie = jnp.where(j_idx > i, 1, 0)
            rank = rank + jnp.where(row > score, 1, jnp.where(row == score, tie, 0))
        bias = jnp.where((rank < SLC_TOP) & (j_idx <= cur), 0.0, NEG)
        bias_t = jnp.concatenate([bias, jnp.zeros((HEAD_DIM - N_SEL, tq), F32)], axis=0).T.astype(BF16)
        qs_masked.append(jnp.concatenate([qs[kv], jnp.concatenate([bias_t] * NSA_GROUP, axis=0)], axis=1))

    key_block = lax.broadcasted_iota(jnp.int32, (tq, HEAD_DIM), 0) // SLC_LEN
    block_lane = lax.broadcasted_iota(jnp.int32, (tq, HEAD_DIM), 1)

    def slc_chain(kv):
        def scores(kt):
            k0 = pl.multiple_of(kt * tq, tq)
            block_onehot = jnp.where(kt * (tq // SLC_LEN) + key_block == block_lane, 1.0, 0.0).astype(BF16)
            keys = jnp.concatenate([ks_ref[pl.ds(k0, tq), head_cols(kv)], block_onehot], axis=1)
            return lax.dot_general(keys, qs_masked[kv], _NT, preferred_element_type=F32)

        def values(kt):
            return vs_ref[pl.ds(pl.multiple_of(kt * tq, tq), tq), head_cols(kv)]

        return scores, values

    o_slc = _causal_flash([slc_chain(kv) for kv in kvs], qi, causal, HEAD_DIM, nl)

    win_p, win_l, win_v = [], [], []
    for kv in kvs:
        scores, values = [], []
        for back in range(WINDOW // tq, -1, -1):
            k0 = start - back * tq
            inside = k0 >= 0
            k0 = pl.multiple_of(jnp.maximum(k0, 0), tq)
            s = lax.dot_general(kw_ref[pl.ds(k0, tq), head_cols(kv)], qs[kv], _NT, preferred_element_type=F32)
            if back == 0:
                s = jnp.where(causal, s, NEG)
            elif back == WINDOW // tq:
                edge = q_local + jnp.where(inside, 0, tq)
                s = jnp.where(k_local > edge, s, NEG)
            else:
                s = s + jnp.where(inside, 0.0, NEG)
            scores.append(s)
            values.append(vw_ref[pl.ds(k0, tq), head_cols(kv)])
        s = jnp.concatenate(scores, axis=0)
        p = jnp.exp2(s - jnp.max(s, axis=0, keepdims=True))
        win_l.append(jnp.sum(p, axis=0, keepdims=True))
        win_p.append(p.astype(BF16))
        win_v.append(jnp.concatenate(values, axis=0))

    gates_t = jax.nn.sigmoid(gate_ref[...].astype(F32)).T
    for kv in kvs:
        acc_w = lax.dot_general(win_v[kv], win_p[kv], _TN, preferred_element_type=F32)
        o_win = acc_w * (1.0 / win_l[kv])

        gts = gates_t[kv * GATE_STRIDE:(kv + 1) * GATE_STRIDE]
        for g in range(NSA_GROUP):
            sl = slice(g * tq, (g + 1) * tq)
            o_t = (gts[g:g + 1, :] * o_cmp[kv][:, sl]
                   + gts[NSA_GROUP + g:NSA_GROUP + g + 1, :] * o_slc[kv][:, sl]
                   + gts[2 * NSA_GROUP + g:2 * NSA_GROUP + g + 1, :] * o_win[:, sl])
            head = kv * NSA_GROUP + g
            o_ref[:, head * HEAD_DIM:(head + 1) * HEAD_DIM] = o_t.T


def _nsa_attention(proj, cmp_kv, ovt, batch, *, tq=256):
    t = proj.shape[0]
    assert WINDOW % tq == 0 and tq % SLC_LEN == 0 and SEQ % tq == 0
    nq = SEQ // tq
    est = (4 * 2 * SEQ * NSA_KV_WIDTH * 2 + 2 * tq * NSA_WIDTH * (2 + 4)
           + N_NSA_KV * 10 * (WINDOW + tq) * NSA_GROUP * tq * 4)

    def kv_spec(col):
        return pl.BlockSpec((SEQ, NSA_KV_WIDTH), lambda b, i: (b, col // NSA_KV_WIDTH))

    def cmp_spec(which, kv):
        return pl.BlockSpec((1, N_CMP_PAD, HEAD_DIM), lambda b, i: (which, kv * batch + b, 0))

    return pl.pallas_call(
        functools.partial(_nsa_kernel, tq=tq),
        grid=(batch, nq),
        in_specs=[
            pl.BlockSpec((tq, NSA_WIDTH), lambda b, i: (b * nq + i, 0)),
            cmp_spec(0, 0), cmp_spec(1, 0), cmp_spec(0, 1), cmp_spec(1, 1),
            kv_spec(COL_KS), kv_spec(COL_VS), kv_spec(COL_KW), kv_spec(COL_VW),
            pl.BlockSpec((tq, HEAD_DIM), lambda b, i: (b * nq + i, COL_GATE // HEAD_DIM)),
            pl.BlockSpec((N_SEL, N_CMP_PAD), lambda b, i: (0, 0)),
        ],
        out_specs=pl.BlockSpec((tq, NSA_WIDTH), lambda b, i: (b * nq + i, 0)),
        out_shape=jax.ShapeDtypeStruct((t, NSA_WIDTH), F32),
        compiler_params=_params(("parallel", "arbitrary"), est),
        name="nsa_attn",
    )(proj, cmp_kv, cmp_kv, cmp_kv, cmp_kv, proj, proj, proj, proj, proj, ovt)


def _diff_kernel(q_ref, k_ref, v_ref, lq1_ref, lk1_ref, lq2_ref, lk2_ref, sub_ref, o_ref,
                 *, tq, lambda_init):
    qi = pl.program_id(2)
    wide = 2 * HEAD_DIM
    lam = (jnp.exp(jnp.sum(lq1_ref[...] * lk1_ref[...], axis=-1, keepdims=True))
           - jnp.exp(jnp.sum(lq2_ref[...] * lk2_ref[...], axis=-1, keepdims=True)) + lambda_init)
    q = q_ref[...]

    def chain(h):
        q1 = q[:, h * wide:h * wide + HEAD_DIM]
        q2 = q[:, h * wide + HEAD_DIM:(h + 1) * wide]

        def scores(kt):
            kk = k_ref[pl.ds(pl.multiple_of(kt * tq, tq), tq), h * wide:(h + 1) * wide]
            s1 = lax.dot_general(kk[:, :HEAD_DIM], q1, _NT, preferred_element_type=F32)
            s2 = lax.dot_general(kk[:, HEAD_DIM:], q2, _NT, preferred_element_type=F32)
            return jnp.concatenate([s1, s2], axis=1)

        def values(kt):
            return v_ref[pl.ds(pl.multiple_of(kt * tq, tq), tq), h * wide:(h + 1) * wide]

        return scores, values

    k_local = lax.broadcasted_iota(jnp.int32, (tq, 2 * tq), 0)
    q_local = lax.broadcasted_iota(jnp.int32, (1, 2 * tq), 1) & (tq - 1)
    outs = _causal_flash([chain(h) for h in range(DIFF_HEADS_PER_STEP)], qi, k_local <= q_local, wide, 2 * tq)
    for h, o_n in enumerate(outs):
        o_t = o_n[:, :tq] - lam * o_n[:, tq:]
        o = _rmsnorm(o_t.T, sub_ref[...]) * (1.0 - lambda_init)
        o_ref[:, h * wide:(h + 1) * wide] = o.astype(o_ref.dtype)


DIFF_HEADS_PER_STEP = 4


def _diff_attention(proj, lq1, lk1, lq2, lk2, subln, batch, lambda_init, *, tq=512):
    t = proj.shape[0]
    nq = SEQ // tq
    wide = 2 * HEAD_DIM
    step_cols = DIFF_HEADS_PER_STEP * wide
    assert COL_DQ % step_cols == 0 and COL_DK % step_cols == 0 and COL_DV % step_cols == 0
    est = 2 * 2 * SEQ * step_cols * 2 + 4 * tq * step_cols * 4 + DIFF_HEADS_PER_STEP * 16 * tq * 2 * tq * 4
    vec = pl.BlockSpec((1, HEAD_DIM), lambda b, h, i: (0, 0))
    return pl.pallas_call(
        functools.partial(_diff_kernel, tq=tq, lambda_init=lambda_init),
        grid=(batch, N_DIFF_HEADS // DIFF_HEADS_PER_STEP, nq),
        in_specs=[
            pl.BlockSpec((tq, step_cols), lambda b, h, i: (b * nq + i, COL_DQ // step_cols + h)),
            pl.BlockSpec((SEQ, step_cols), lambda b, h, i: (b, COL_DK // step_cols + h)),
            pl.BlockSpec((SEQ, step_cols), lambda b, h, i: (b, COL_DV // step_cols + h)),
            vec, vec, vec, vec,
            pl.BlockSpec((1, wide), lambda b, h, i: (0, 0)),
        ],
        out_specs=pl.BlockSpec((tq, step_cols), lambda b, h, i: (b * nq + i, h)),
        out_shape=jax.ShapeDtypeStruct((t, DIFF_WIDTH), BF16),
        compiler_params=_params(("parallel", "parallel", "arbitrary"), est),
        name="diff_attn",
    )(proj, proj, proj, lq1, lk1, lq2, lk2, subln)


def _oproj_kernel(yn_ref, g_ref, yd_ref, wo_ref, x_ref, o_ref):
    yn = _rmsnorm(yn_ref[...], g_ref[...]).astype(BF16)
    acc = jnp.dot(yn, wo_ref[:NSA_WIDTH, :], preferred_element_type=F32)
    acc = acc + jnp.dot(yd_ref[...], wo_ref[NSA_WIDTH:, :], preferred_element_type=F32)
    o_ref[...] = x_ref[...] + acc


def _out_proj(y_nsa, g, y_diff, w_o, x2, *, tm=512):
    t = x2.shape[0]
    est = (2 * (NSA_WIDTH + DIFF_WIDTH) * D_MODEL * 2 + 2 * tm * NSA_WIDTH * 4 + 2 * tm * DIFF_WIDTH * 2
           + 5 * tm * D_MODEL * 4)
    return pl.pallas_call(
        _oproj_kernel,
        grid=(t // tm,),
        in_specs=[
            pl.BlockSpec((tm, NSA_WIDTH), lambda i: (i, 0)),
            pl.BlockSpec((1, NSA_WIDTH), lambda i: (0, 0)),
            pl.BlockSpec((tm, DIFF_WIDTH), lambda i: (i, 0)),
            pl.BlockSpec((NSA_WIDTH + DIFF_WIDTH, D_MODEL), lambda i: (0, 0)),
            pl.BlockSpec((tm, D_MODEL), lambda i: (i, 0)),
        ],
        out_specs=pl.BlockSpec((tm, D_MODEL), lambda i: (i, 0)),
        out_shape=jax.ShapeDtypeStruct((t, D_MODEL), F32),
        compiler_params=_params(("parallel",), est),
        name="out_proj",
    )(y_nsa, g, y_diff, w_o, x2)


HALO = 8
FFN_CHUNK = 256
FFN_ROW_BLOCKS = 4


def _ffn_kernel(h_ref, halo_ref, g_ref, wu_ref, wg_ref, cwu_ref, cwg_ref, cbu_ref, cbg_ref, wd_ref,
                o_ref, xn_ref, raw_ref, act_ref, *, tm):
    i = pl.program_id(0)
    j = pl.program_id(1)

    @pl.when(j == 0)
    def _():
        keep = jnp.where((i * tm) % SEQ == 0, 0.0, 1.0)
        h = h_ref[...]
        xn_ref[0:HALO, :] = (_rmsnorm(halo_ref[...], g_ref[...]) * keep).astype(BF16)
        xn_ref[HALO:, :] = _rmsnorm(h, g_ref[...]).astype(BF16)
        o_ref[...] = h

    xn = xn_ref[...]

    def conv(h, cw_ref, cb_ref, cols):
        cw = cw_ref[:, cols]
        h3 = h.reshape(h.shape[0] // SUBLANES, SUBLANES, h.shape[1])
        sub = lax.broadcasted_iota(jnp.int32, h3.shape[1:], 0)
        out = cw[CONV_W - 1:CONV_W, :] * h3[1:] + cb_ref[:, cols]
        for back in range(1, CONV_W):
            rolled = pltpu.roll(h3, back, 1)
            shifted = jnp.where(sub < back, rolled[:-1], rolled[1:])
            out = out + cw[CONV_W - 1 - back:CONV_W - back, :] * shifted
        return out.reshape(h.shape[0] - HALO, h.shape[1])

    def gated(rows, cols):
        src = slice(rows.start, rows.stop + HALO)
        u = conv(raw_ref[slot, 0, src, cols], cwu_ref, cbu_ref, cols)
        gate = conv(raw_ref[slot, 1, src, cols], cwg_ref, cbg_ref, cols)
        half_gate = 0.5 * gate
        act_ref[rows, cols] = (half_gate * (1.0 + jnp.tanh(half_gate)) * u).astype(BF16)

    slot = j % 2
    chunks = [slice(lo, lo + FFN_CHUNK) for lo in range(0, wu_ref.shape[1], FFN_CHUNK)]
    row_blocks = [slice(lo, lo + tm // FFN_ROW_BLOCKS) for lo in range(0, tm, tm // FFN_ROW_BLOCKS)]
    for cols in chunks:
        raw_ref[slot, 0, :, cols] = jnp.dot(xn, wu_ref[:, cols], preferred_element_type=F32)
        raw_ref[slot, 1, :, cols] = jnp.dot(xn, wg_ref[:, cols], preferred_element_type=F32)
        if cols is not chunks[-1]:
            gated(slice(0, tm), cols)
    gated(row_blocks[0], chunks[-1])
    for n, rows in enumerate(row_blocks):
        if n + 1 < len(row_blocks):
            gated(row_blocks[n + 1], chunks[-1])
        o_ref[rows, :] += jnp.dot(act_ref[rows, :], wd_ref[...], preferred_element_type=F32)


def _conv_ffn(h1, g, w_up, conv_w, conv_b, w_down, *, tm=1024, tf=512):
    t = h1.shape[0]
    assert t % tm == 0 and SEQ % tm == 0 and D_FF % tf == 0
    nf = D_FF // tf
    est = (4 * tm * D_MODEL * 4 + 2 * 3 * D_MODEL * tf * 2 + (tm + HALO) * D_MODEL * 2
           + 8 * (tm + HALO) * tf * 4)
    return pl.pallas_call(
        functools.partial(_ffn_kernel, tm=tm),
        grid=(t // tm, nf),
        in_specs=[
            pl.BlockSpec((tm, D_MODEL), lambda i, j: (i, 0)),
            pl.BlockSpec((HALO, D_MODEL), lambda i, j: (jnp.maximum(i * (tm // HALO) - 1, 0), 0)),
            pl.BlockSpec((1, D_MODEL), lambda i, j: (0, 0)),
            pl.BlockSpec((D_MODEL, tf), lambda i, j: (0, j)),
            pl.BlockSpec((D_MODEL, tf), lambda i, j: (0, nf + j)),
            pl.BlockSpec((CONV_W, tf), lambda i, j: (0, j)),
            pl.BlockSpec((CONV_W, tf), lambda i, j: (0, nf + j)),
            pl.BlockSpec((1, tf), lambda i, j: (0, j)),
            pl.BlockSpec((1, tf), lambda i, j: (0, nf + j)),
            pl.BlockSpec((tf, D_MODEL), lambda i, j: (j, 0)),
        ],
        out_specs=pl.BlockSpec((tm, D_MODEL), lambda i, j: (i, 0), pipeline_mode=pl.Buffered(1)),
        out_shape=jax.ShapeDtypeStruct((t, D_MODEL), F32),
        scratch_shapes=[pltpu.VMEM((tm + HALO, D_MODEL), BF16), pltpu.VMEM((2, 2, tm + HALO, tf), F32),
                        pltpu.VMEM((tm, tf), BF16)],
        compiler_params=_params(("parallel", "arbitrary"), est),
        name="conv_ffn",
    )(h1, h1, g, w_up, w_up, conv_w, conv_w, conv_b, conv_b, w_down)


PLE_CHUNK = 512


def _ple_kernel(h_ref, gp_ref, wg_ref, p_ref, wp_ref, gf_ref, o_ref):
    hn = _rmsnorm(h_ref[...], gp_ref[...]).astype(BF16)
    pb = p_ref[...].astype(BF16)
    ssq = None
    for lo in range(0, D_MODEL, PLE_CHUNK):
        cols = slice(lo, lo + PLE_CHUNK)
        gate = jax.nn.sigmoid(jnp.dot(hn, wg_ref[:, cols], preferred_element_type=F32))
        emb = jnp.dot(pb, wp_ref[:, cols], preferred_element_type=F32)
        h3 = h_ref[:, cols] + gate * emb
        o_ref[:, cols] = h3
        part = jnp.sum(h3 * h3, axis=-1, keepdims=True)
        ssq = part if ssq is None else ssq + part
    o_ref[...] = o_ref[...] * lax.rsqrt(ssq * (1.0 / D_MODEL) + EPS) * gf_ref[...]


def _ple_out(h2, g_ple, w_gate, p2, w_proj, g_final, *, tm=256):
    t = h2.shape[0]
    est = D_MODEL * D_MODEL * 2 + 2 * PLE_DIM * D_MODEL * 2 + 4 * tm * D_MODEL * 4 + 6 * tm * D_MODEL * 4
    return pl.pallas_call(
        _ple_kernel,
        grid=(t // tm,),
        in_specs=[
            pl.BlockSpec((tm, D_MODEL), lambda i: (i, 0)),
            pl.BlockSpec((1, D_MODEL), lambda i: (0, 0)),
            pl.BlockSpec((D_MODEL, D_MODEL), lambda i: (0, 0), pipeline_mode=pl.Buffered(1)),
            pl.BlockSpec((tm, PLE_DIM), lambda i: (i, 0)),
            pl.BlockSpec((PLE_DIM, D_MODEL), lambda i: (0, 0)),
            pl.BlockSpec((1, D_MODEL), lambda i: (0, 0)),
        ],
        out_specs=pl.BlockSpec((tm, D_MODEL), lambda i: (i, 0)),
        out_shape=jax.ShapeDtypeStruct((t, D_MODEL), F32),
        compiler_params=_params(("parallel",), est),
        name="ple_out",
    )(h2, g_ple, w_gate, p2, w_proj, g_final)


def _split_w_in(w):
    w_a, gate, w_b = _cast_w_in(w)
    gate = gate[:, :N_GATES].reshape(D_MODEL, 3, N_NSA_KV, NSA_GROUP).transpose(0, 2, 1, 3)
    gate = jnp.pad(gate.reshape(D_MODEL, N_NSA_KV, 3 * NSA_GROUP), ((0, 0), (0, 0), (0, GATE_STRIDE - 3 * NSA_GROUP)))
    gate = jnp.pad(gate.reshape(D_MODEL, N_NSA_KV * GATE_STRIDE), ((0, 0), (0, HEAD_DIM - N_NSA_KV * GATE_STRIDE)))
    return w_a, gate, w_b


CAST_BLOCK = 512


def _cast_w_in_kernel(cur_ref, nxt_ref, wa_ref, wg_ref, wb_ref, *, n_a):
    s = pl.program_id(0)

    @pl.when(s < n_a)
    def _():
        wa_ref[...] = cur_ref[...].T.astype(BF16)

    @pl.when((s >= n_a) & (s < pl.num_programs(0) - 1))
    def _():
        rows = jnp.concatenate([cur_ref[N_GATES:, :], nxt_ref[0:N_GATES, :]], axis=0)
        wb_ref[...] = rows.T.astype(BF16)

    @pl.when(s == pl.num_programs(0) - 1)
    def _():
        wg_ref[...] = cur_ref[0:HEAD_DIM, :].T.astype(BF16)


def _cast_w_in(w):
    d, n = w.shape
    nsa_cols = NSA_WIDTH + 6 * NSA_KV_WIDTH
    diff_cols = n - nsa_cols - N_GATES
    assert nsa_cols % CAST_BLOCK == 0 and diff_cols % CAST_BLOCK == 0 and N_GATES % SUBLANES == 0
    n_a, n_b = nsa_cols // CAST_BLOCK, diff_cols // CAST_BLOCK
    spill_rows = 32
    assert N_GATES <= spill_rows and CAST_BLOCK % spill_rows == 0
    est = 2 * CAST_BLOCK * d * 4 + 2 * 2 * CAST_BLOCK * d * 2 + 6 * CAST_BLOCK * d * 4

    def in_block(s):
        return jnp.where(s == n_a + n_b, n_a, s)

    return pl.pallas_call(
        functools.partial(_cast_w_in_kernel, n_a=n_a),
        grid=(n_a + n_b + 1,),
        in_specs=[pl.BlockSpec((CAST_BLOCK, d), lambda s: (in_block(s), 0)),
                  pl.BlockSpec((spill_rows, d),
                               lambda s: (jnp.clip(s + 1, n_a + 1, n_a + n_b) * (CAST_BLOCK // spill_rows), 0))],
        out_specs=[pl.BlockSpec((d, CAST_BLOCK), lambda s: (0, jnp.minimum(s, n_a - 1))),
                   pl.BlockSpec((d, HEAD_DIM), lambda s: (0, 0)),
                   pl.BlockSpec((d, CAST_BLOCK), lambda s: (0, jnp.clip(s - n_a, 0, n_b - 1)))],
        out_shape=[jax.ShapeDtypeStruct((d, nsa_cols), BF16), jax.ShapeDtypeStruct((d, HEAD_DIM), BF16),
                   jax.ShapeDtypeStruct((d, diff_cols), BF16)],
        compiler_params=_params(("arbitrary",), est),
        name="cast_w_in",
    )(w.T, w.T)


def _rope_tables():
    f32 = np.float32
    inv = (f32(1.0) / (f32(ROPE_THETA) ** (np.arange(0, ROPE_DIM, 2, dtype=f32) / f32(ROPE_DIM)))).astype(f32)
    ang = np.arange(SEQ, dtype=f32)[:, None] * inv[None, :]
    cos, sin = np.cos(ang).astype(f32), np.sin(ang).astype(f32)
    rest = HEAD_DIM - ROPE_DIM
    cos_t = np.concatenate([cos, cos, np.ones((SEQ, rest), f32)], axis=1)
    sin_t = np.concatenate([-sin, sin, np.zeros((SEQ, rest), f32)], axis=1)
    return jnp.asarray(cos_t), jnp.asarray(sin_t)


def _overlap_t():
    cmp_starts = np.arange(N_CMP) * CMP_STRIDE
    sel_starts = np.arange(N_SEL) * SLC_LEN
    ov = np.clip(np.minimum(cmp_starts[:, None] + CMP_LEN, sel_starts[None, :] + SLC_LEN)
                 - np.maximum(cmp_starts[:, None], sel_starts[None, :]), 0, None).astype(np.float32) / CMP_LEN
    ovt = np.zeros((N_SEL, N_CMP_PAD), np.float32)
    ovt[:, :N_CMP] = ov.T
    return jnp.asarray(ovt, BF16)


def kernel(x, p, attn_norm, w_in, cmp_k_pos, cmp_k_w1, cmp_k_w2, cmp_v_pos, cmp_v_w1, cmp_v_w2, nsa_out_norm, diff_lq1, diff_lk1, diff_lq2, diff_lk2, diff_subln, w_o, ffn_norm, w_up, conv_w, conv_b, w_down, ple_norm, w_ple_gate, w_ple_proj, final_norm):
    batch, seq, _ = x.shape
    assert seq == SEQ and p.shape[0] == 1
    t = batch * seq
    layer = 0
    lambda_init = 0.8 - 0.6 * math.exp(-0.3 * layer)
    x2 = x.reshape(t, D_MODEL)
    cos_t, sin_t = _rope_tables()

    proj, hkv, w_o_b, w_up_b, w_down_b, w_gate_b = _in_proj(
        x2, attn_norm[layer][None], *_split_w_in(w_in[layer]), cos_t, sin_t,
        later_weights=(w_o[layer], w_up[layer], w_down[layer], w_ple_gate[layer]))

    hkv = hkv.reshape(2, N_NSA_KV * t // HALF_BLOCK, HALF_BLOCK * HEAD_DIM)
    w1 = jnp.stack([cmp_k_w1[layer], cmp_v_w1[layer]]).astype(BF16)
    w2 = jnp.stack([cmp_k_w2[layer], cmp_v_w2[layer]]).astype(BF16)
    pos = jnp.stack([cmp_k_pos[layer], cmp_v_pos[layer]]).reshape(2, 1, CMP_LEN * HEAD_DIM)
    pos = jnp.broadcast_to(pos, (2, 8, CMP_LEN * HEAD_DIM)).astype(BF16)
    cmp_kv = _compress(hkv, w1, pos, w2)

    y_nsa = _nsa_attention(proj, cmp_kv, _overlap_t(), batch)
    y_diff = _diff_attention(proj, diff_lq1[layer][None], diff_lk1[layer][None], diff_lq2[layer][None],
                             diff_lk2[layer][None], diff_subln[layer][None], batch, lambda_init)
    h1 = _out_proj(y_nsa, nsa_out_norm[layer][None], y_diff, w_o_b, x2)
    h2 = _conv_ffn(h1, ffn_norm[layer][None], w_up_b, conv_w[layer], conv_b[layer][None], w_down_b)
    out = _ple_out(h2, ple_norm[layer][None], w_gate_b, p[layer].reshape(t, PLE_DIM),
                   w_ple_proj[layer].astype(BF16), final_norm[None])
    return out.reshape(batch, seq, D_MODEL)
```
